```python
import jax, jax.numpy as jnp
from jax import lax
import numpy as np

D_MODEL = 1024
BATCH = 16
SEQ = 256
DEPTH = 2
DEC_BATCH = 4
DEC_SEQ = 4096
PAST_LEN = 512

GRID_W = 64
SC_DIM = 256
SC_WIDTH = 3
N_Q_HEADS = 8
N_KV_HEADS = 2
N_GROUP = N_Q_HEADS // N_KV_HEADS
HEAD_DIM = 64
ATT_DIM = N_Q_HEADS * HEAD_DIM
KV_DIM = N_KV_HEADS * HEAD_DIM
WINDOW = 128
BLOCK = 128
N_DN_HEADS = 4
DN_HEAD_DIM = 64
DN_DIM = N_DN_HEADS * DN_HEAD_DIM
DN_CONV = 3
CHUNK = 64
MIX_DIM = SC_DIM + ATT_DIM + DN_DIM
IN_SIZES = (SC_DIM, SC_DIM, SC_DIM, ATT_DIM, KV_DIM, KV_DIM, 3 * DN_DIM, DN_DIM, 2 * N_DN_HEADS, 2 * N_DN_HEADS)
IN_DIM = sum(IN_SIZES)
D_FF = -(-8 * D_MODEL // (3 * 256)) * 256
ROPE_BASE = 10000.0
EPS = 1e-6
NEG = -1e30

kernel_name = 'hybrid_diffusion_trunk_step'

F32 = jnp.float32


def _rms(x, g):
    xf = x.astype(F32)
    y = xf * lax.rsqrt(jnp.mean(xf * xf, axis=-1, keepdims=True) + EPS)
    return (y * g.astype(F32)).astype(x.dtype)


def _l2n(x):
    xf = x.astype(F32)
    return xf * lax.rsqrt(jnp.sum(xf * xf, axis=-1, keepdims=True) + EPS)


def _dwconv(x, w):
    p = w.shape[0] // 2
    return lax.conv_general_dilated(x, w[:, None, :].astype(x.dtype), (1,), [(p, p)],
                                    dimension_numbers=('NWC', 'WIO', 'NWC'),
                                    feature_group_count=x.shape[-1])


def _rope_part(x, pos):
    h = x.shape[-1] // 2
    inv = 1.0 / (ROPE_BASE ** (jnp.arange(h, dtype=F32) / h))
    ang = pos.astype(F32)[:, None] * inv
    cos = jnp.cos(ang)[None, :, None, :]
    sin = jnp.sin(ang)[None, :, None, :]
    x1, x2 = x[..., :h], x[..., h:]
    return jnp.concatenate([x1 * cos - x2 * sin, x2 * cos + x1 * sin], axis=-1)


def _axial_rope(x):
    n = x.shape[1]
    n_rows = n // GRID_W
    row = jnp.repeat(jnp.arange(n_rows), GRID_W)
    col = jnp.tile(jnp.arange(GRID_W), n_rows)
    xf = x.astype(F32)
    half = HEAD_DIM // 2
    out = jnp.concatenate([_rope_part(xf[..., :half], row), _rope_part(xf[..., half:], col)], axis=-1)
    return out.astype(x.dtype)


def _sink_attend(qb, sink, kv_sets):
    scale = HEAD_DIM ** -0.5
    logits = []
    for k, v, m in kv_sets:
        s = jnp.einsum('bqkgd,bskd->bkgqs', qb, k).astype(F32) * scale
        if m is not None:
            s = jnp.where(m, s, NEG)
        logits.append(s)
    bsz, nq = qb.shape[0], qb.shape[1]
    sink_l = jnp.broadcast_to(sink.astype(F32).reshape(N_KV_HEADS, N_GROUP)[None, :, :, None, None],
                              (bsz, N_KV_HEADS, N_GROUP, nq, 1))
    p = jax.nn.softmax(jnp.concatenate([sink_l] + logits, axis=-1), axis=-1)
    out = None
    off = 1
    for (k, v, m), s in zip(kv_sets, logits):
        ns = s.shape[-1]
        o = jnp.einsum('bkgqs,bskd->bqkgd', p[..., off:off + ns].astype(v.dtype), v)
        out = o if out is None else out + o
        off += ns
    return out


def _context_attention(q, k, v, sink):
    bsz, n = q.shape[0], q.shape[1]
    def blk(i):
        qb = lax.dynamic_slice_in_dim(q, i * BLOCK, BLOCK, axis=1)
        return _sink_attend(qb, sink, [(k, v, None)])
    o = lax.map(blk, jnp.arange(n // BLOCK))
    return jnp.moveaxis(o, 0, 1).reshape(bsz, n, ATT_DIM)


def _latent_attention(q, k, v, k_ctx, v_ctx, sink):
    bsz, n = q.shape[0], q.shape[1]
    pad = ((0, 0), (BLOCK, BLOCK), (0, 0), (0, 0))
    kp = jnp.pad(k, pad)
    vp = jnp.pad(v, pad)
    qoff = jnp.arange(BLOCK)
    koff = jnp.arange(3 * BLOCK) - BLOCK
    def blk(i):
        start = i * BLOCK
        qb = lax.dynamic_slice_in_dim(q, start, BLOCK, axis=1)
        kb = lax.dynamic_slice_in_dim(kp, start, 3 * BLOCK, axis=1)
        vb = lax.dynamic_slice_in_dim(vp, start, 3 * BLOCK, axis=1)
        qpos = start + qoff
        kpos = start + koff
        m = (jnp.abs(qpos[:, None] - kpos[None, :]) <= WINDOW) & (kpos >= 0)[None, :] & (kpos < n)[None, :]
        return _sink_attend(qb, sink, [(k_ctx, v_ctx, None), (kb, vb, m)])
    o = lax.map(blk, jnp.arange(n // BLOCK))
    return jnp.moveaxis(o, 0, 1).reshape(bsz, n, ATT_DIM)


def _gated_delta_chunked(q, k, v, g, beta, s0):
    b, n, h, dk = q.shape
    dv = v.shape[-1]
    nc = n // CHUNK
    def chunks(t):
        return jnp.swapaxes(t, 1, 2).reshape((b, h, nc, CHUNK) + t.shape[3:])
    qc, kc, vc, gc, bc = (chunks(t) for t in (q, k, v, g, beta))
    gc = jnp.cumsum(gc, axis=-1)
    idx = jnp.arange(CHUNK)
    incl = idx[:, None] >= idx[None, :]
    strict = idx[:, None] > idx[None, :]
    dec_incl = jnp.exp(jnp.where(incl, gc[..., :, None] - gc[..., None, :], NEG))
    dec_strict = jnp.where(strict, dec_incl, 0.0)
    kb = kc * bc[..., None]
    lmat = jnp.einsum('bhnid,bhnjd->bhnij', kb, kc) * dec_strict
    eye = jnp.eye(CHUNK, dtype=F32)
    rhs = jnp.concatenate([vc * bc[..., None], kb * jnp.exp(gc)[..., None]], axis=-1)
    sol = lax.linalg.triangular_solve(lmat + eye, rhs, left_side=True, lower=True, unit_diagonal=True)
    u, w = sol[..., :dv], sol[..., dv:]
    a_in = jnp.einsum('bhnid,bhnjd->bhnij', qc, kc) * dec_incl
    def step(s, inp):
        q_t, k_t, u_t, w_t, g_t, a_t = inp
        v_new = u_t - jnp.einsum('bhck,bhkv->bhcv', w_t, s)
        o = (jnp.einsum('bhck,bhkv->bhcv', q_t * jnp.exp(g_t)[..., None], s)
             + jnp.einsum('bhij,bhjv->bhiv', a_t, v_new))
        g_last = g_t[..., -1]
        s = (s * jnp.exp(g_last)[..., None, None]
             + jnp.einsum('bhck,bhcv->bhkv', k_t * jnp.exp(g_last[..., None] - g_t)[..., None], v_new))
        return s, o
    xs = tuple(jnp.moveaxis(t, 2, 0) for t in (qc, kc, u, w, gc, a_in))
    s_fin, o = lax.scan(step, s0, xs)
    o = jnp.swapaxes(jnp.moveaxis(o, 0, 2).reshape(b, h, n, dv), 1, 2)
    return o, s_fin


def _delta_mixer(qkv_in, z, a, bg, conv_w, a_log, dt_bias, norm_g, s0):
    bsz, n, _ = qkv_in.shape
    qkv = jax.nn.silu(_dwconv(qkv_in, conv_w))
    q, k, v = jnp.split(qkv, 3, axis=-1)
    shp = (bsz, n, N_DN_HEADS, DN_HEAD_DIM)
    q = _l2n(q.reshape(shp)) * (DN_HEAD_DIM ** -0.5)
    k = _l2n(k.reshape(shp))
    v = v.reshape(shp).astype(F32)
    a = a.reshape(bsz, n, 2, N_DN_HEADS).astype(F32)
    bg = bg.reshape(bsz, n, 2, N_DN_HEADS).astype(F32)
    gdec = -jnp.exp(a_log.astype(F32)) * jax.nn.softplus(a + dt_bias.astype(F32))
    beta = jax.nn.sigmoid(bg)
    s0 = s0.astype(F32)
    o_f, s_f = _gated_delta_chunked(q, k, v, gdec[:, :, 0], beta[:, :, 0], s0[:, 0])
    fl = lambda t: jnp.flip(t, axis=1)
    o_b, s_b = _gated_delta_chunked(fl(q), fl(k), fl(v), fl(gdec[:, :, 1]), fl(beta[:, :, 1]), s0[:, 1])
    o = o_f + fl(o_b)
    o = _rms(o, norm_g) * jax.nn.silu(z.reshape(shp).astype(F32))
    return o.reshape(bsz, n, DN_DIM).astype(z.dtype), jnp.stack([s_f, s_b], axis=1)


def _layer(x, cvec, p, ctx):
    bsz, n, _ = x.shape
    mod = (jax.nn.silu(cvec) @ p['ada_w'] + p['ada_b'])[:, None, :]
    sh1, sc1, g1, sh2, sc2, g2 = jnp.split(mod, 6, axis=-1)
    h = _rms(x, p['norm1_g']) * (1 + sc1) + sh1
    splits = np.cumsum(IN_SIZES)[:-1].tolist()
    sc_b, sc_c, sc_h, q, k, v, dn_qkv, dn_z, dn_a, dn_b = jnp.split(h @ p['w_in'], splits, axis=-1)
    y_sc = sc_b * _dwconv(sc_c * sc_h, p['sc_conv_w'])
    q = _rms(q.reshape(bsz, n, N_Q_HEADS, HEAD_DIM), p['q_norm_g'])
    k = _rms(k.reshape(bsz, n, N_KV_HEADS, HEAD_DIM), p['k_norm_g'])
    v = v.reshape(bsz, n, N_KV_HEADS, HEAD_DIM)
    gshape = (bsz, n, N_KV_HEADS, N_GROUP, HEAD_DIM)
    if ctx is None:
        y_att = _context_attention(q.reshape(gshape), k, v, p['attn_sink'])
        s0 = jnp.zeros((bsz, 2, N_DN_HEADS, DN_HEAD_DIM, DN_HEAD_DIM), F32)
    else:
        k_ctx, v_ctx, s0 = ctx
        y_att = _latent_attention(_axial_rope(q).reshape(gshape), _axial_rope(k), v,
                                  k_ctx.astype(k.dtype), v_ctx.astype(v.dtype), p['attn_sink'])
    y_dn, s_out = _delta_mixer(dn_qkv, dn_z, dn_a, dn_b, p['dn_conv_w'], p['dn_A_log'],
                               p['dn_dt_bias'], p['dn_norm_g'], s0)
    y = jnp.concatenate([y_sc, y_att.astype(y_sc.dtype), y_dn], axis=-1) @ p['w_out']
    x = x + g1 * y
    h2 = _rms(x, p['norm2_g']) * (1 + sc2) + sh2
    x = x + g2 * ((jax.nn.silu(h2 @ p['w_gate']) * (h2 @ p['w_up'])) @ p['w_down'])
    if ctx is None:
        return x, (k, v, s_out.astype(x.dtype))
    return x, None


def setup_inputs(seed: int = 0) -> dict:
    key = jax.random.key(seed)
    ks = jax.random.split(key, 32)
    nrm = lambda i, shape, s: jax.random.normal(ks[i], shape, F32) * s
    a_raw = jax.random.uniform(ks[20], (DEPTH, 2, N_DN_HEADS), F32, 1.0, 16.0)
    dt = jnp.exp(jax.random.uniform(ks[21], (DEPTH, 2, N_DN_HEADS), F32, float(np.log(1e-3)), float(np.log(1e-1))))
    return {
        'x_prompt': nrm(0, (BATCH, SEQ, D_MODEL), 1.0),
        'x_sample': nrm(1, (DEC_BATCH, DEC_SEQ, D_MODEL), 1.0),
        'cache_k': nrm(2, (DEC_BATCH, DEPTH, PAST_LEN, N_KV_HEADS, HEAD_DIM), 1.0),
        'cache_v': nrm(3, (DEC_BATCH, DEPTH, PAST_LEN, N_KV_HEADS, HEAD_DIM), 1.0),
        'state_delta': nrm(4, (DEC_BATCH, DEPTH, 2, N_DN_HEADS, DN_HEAD_DIM, DN_HEAD_DIM), 0.1),
        'c': nrm(5, (DEC_BATCH, D_MODEL), 1.0),
        'c_ctx': nrm(6, (D_MODEL,), 1.0),
        'w_in': nrm(7, (DEPTH, D_MODEL, IN_DIM), D_MODEL ** -0.5),
        'w_out': nrm(8, (DEPTH, MIX_DIM, D_MODEL), MIX_DIM ** -0.5),
        'ada_w': nrm(9, (DEPTH, D_MODEL, 6 * D_MODEL), 0.5 * D_MODEL ** -0.5),
        'ada_b': nrm(10, (DEPTH, 6 * D_MODEL), 0.01),
        'norm1_g': 1.0 + nrm(11, (DEPTH, D_MODEL), 0.05),
        'norm2_g': 1.0 + nrm(12, (DEPTH, D_MODEL), 0.05),
        'sc_conv_w': nrm(13, (DEPTH, SC_WIDTH, SC_DIM), SC_WIDTH ** -0.5),
        'dn_conv_w': nrm(14, (DEPTH, DN_CONV, 3 * DN_DIM), DN_CONV ** -0.5),
        'q_norm_g': 1.0 + nrm(15, (DEPTH, HEAD_DIM), 0.05),
        'k_norm_g': 1.0 + nrm(16, (DEPTH, HEAD_DIM), 0.05),
        'attn_sink': nrm(17, (DEPTH, N_Q_HEADS), 0.5),
        'dn_A_log': jnp.log(a_raw),
        'dn_dt_bias': dt + jnp.log(-jnp.expm1(-dt)),
        'dn_norm_g': 1.0 + nrm(18, (DEPTH, DN_HEAD_DIM), 0.05),
        'w_gate': nrm(22, (DEPTH, D_MODEL, D_FF), D_MODEL ** -0.5),
        'w_up': nrm(23, (DEPTH, D_MODEL, D_FF), D_MODEL ** -0.5),
        'w_down': nrm(24, (DEPTH, D_FF, D_MODEL), D_FF ** -0.5),
    }


def reference(x_prompt, x_sample, cache_k, cache_v, state_delta, c, c_ctx, w_in, w_out, ada_w, ada_b,
              norm1_g, norm2_g, sc_conv_w, dn_conv_w, q_norm_g, k_norm_g, attn_sink, dn_A_log,
              dn_dt_bias, dn_norm_g, w_gate, w_up, w_down):
    xp, xs = x_prompt, x_sample
    new_k, new_v, new_s = [], [], []
    for l in range(DEPTH):
        p = {'w_in': w_in[l], 'w_out': w_out[l], 'ada_w': ada_w[l], 'ada_b': ada_b[l],
             'norm1_g': norm1_g[l], 'norm2_g': norm2_g[l], 'sc_conv_w': sc_conv_w[l],
             'dn_conv_w': dn_conv_w[l], 'q_norm_g': q_norm_g[l], 'k_norm_g': k_norm_g[l],
             'attn_sink': attn_sink[l], 'dn_A_log': dn_A_log[l], 'dn_dt_bias': dn_dt_bias[l],
             'dn_norm_g': dn_norm_g[l], 'w_gate': w_gate[l], 'w_up': w_up[l], 'w_down': w_down[l]}
        xp, (kl, vl, sl) = _layer(xp, c_ctx[None, :], p, None)
        new_k.append(kl)
        new_v.append(vl)
        new_s.append(sl)
        xs, _ = _layer(xs, c, p, (cache_k[:, l], cache_v[:, l], state_delta[:, l]))
    return (xp, xs, jnp.stack(new_k, axis=1), jnp.stack(new_v, axis=1), jnp.stack(new_s, axis=1))
```

```python
import functools

import numpy as np
import jax
import jax.numpy as jnp
from jax import lax
from jax.experimental import pallas as pl
from jax.experimental.pallas import tpu as pltpu

F32 = jnp.float32
BF16 = jnp.bfloat16

D_MODEL = 1024
BATCH = 16
SEQ = 256
DEPTH = 2
DEC_BATCH = 4
DEC_SEQ = 4096
PAST_LEN = 512
GRID_W = 64
SC_DIM = 256
N_Q_HEADS = 8
N_KV_HEADS = 2
N_GROUP = N_Q_HEADS // N_KV_HEADS
HEAD_DIM = 64
ATT_DIM = N_Q_HEADS * HEAD_DIM
KV_DIM = N_KV_HEADS * HEAD_DIM
WINDOW = 128
N_DN_HEADS = 4
DN_HEAD_DIM = 64
DN_DIM = N_DN_HEADS * DN_HEAD_DIM
CHUNK = 64
MIX_DIM = SC_DIM + ATT_DIM + DN_DIM
D_FF = -(-8 * D_MODEL // (3 * 256)) * 256
ROPE_BASE = 10000.0
EPS = 1e-6
NEG = -1e30

LANES = 128
V7X_VMEM_BYTES = 64 * 1024 * 1024
VMEM_LIMIT = (V7X_VMEM_BYTES * 3) // 4

OFF_SC, OFF_Q, OFF_KV, OFF_DQ, OFF_DZ, OFF_AB = 0, 768, 1280, 1536, 2304, 2560
IN_COLS = OFF_AB + 2 * LANES
N_PAIRS = N_DN_HEADS // 2
MOD_ROWS = 8
TM = 256


def _sigmoid(x):
    return 1.0 / (1.0 + jnp.exp(-x))


def _silu(x):
    return x * _sigmoid(x)


def _softplus(x):
    return jnp.maximum(x, 0.0) + jnp.log1p(jnp.exp(-jnp.abs(x)))


def _dot(a, b):
    return jnp.dot(a, b, preferred_element_type=F32)


def _dot_nt(a, b):
    return lax.dot_general(a, b, (((1,), (1,)), ((), ())), preferred_element_type=F32)


def _dot_tn(a, b):
    return lax.dot_general(a, b, (((0,), (0,)), ((), ())), preferred_element_type=F32)


def _dot_f32(a, b):
    return jnp.dot(a, b, precision=lax.Precision.HIGHEST, preferred_element_type=F32)


def _lo_mask(shape):
    return lax.broadcasted_iota(jnp.int32, shape, 1) % LANES < HEAD_DIM


def _half_sums(xx, lo):
    s_lo = jnp.sum(jnp.where(lo, xx, 0.0), axis=-1, keepdims=True)
    s_hi = jnp.sum(jnp.where(lo, 0.0, xx), axis=-1, keepdims=True)
    return jnp.where(lo, s_lo, s_hi)


def _head_rms(x, g_row):
    lo = _lo_mask(x.shape)
    ms = _half_sums(x * x, lo) * (1.0 / HEAD_DIM)
    return x * lax.rsqrt(ms + EPS) * g_row


def _head_l2n(x):
    lo = _lo_mask(x.shape)
    return x * lax.rsqrt(_half_sums(x * x, lo) + EPS)


def _rope(x, cos, sin):
    n = x.shape[1]
    lane = lax.broadcasted_iota(jnp.int32, x.shape, 1)
    swapped = jnp.where((lane & 16) == 0, pltpu.roll(x, n - 16, axis=1), pltpu.roll(x, 16, axis=1))
    return x * cos + swapped * sin


def _dup_half(x, half):
    lo = _lo_mask(x.shape)
    r = pltpu.roll(x, HEAD_DIM, axis=1)
    return jnp.where(lo, x, r) if half == 0 else jnp.where(lo, r, x)


def _mod_kernel(c_ref, w_ref, b_ref, o_ref):
    s = _silu(c_ref[...])
    o_ref[...] = _dot(s.astype(BF16), w_ref[...].astype(BF16)) + b_ref[...]


def _modulation(cvecs, ada_w, ada_b):
    nblk = 6
    return pl.pallas_call(
        _mod_kernel,
        out_shape=jax.ShapeDtypeStruct((DEPTH, MOD_ROWS, 6 * D_MODEL), F32),
        grid=(DEPTH, nblk),
        in_specs=[pl.BlockSpec((MOD_ROWS, D_MODEL), lambda l, j: (0, 0)),
                  pl.BlockSpec((None, D_MODEL, D_MODEL), lambda l, j: (l, 0, j)),
                  pl.BlockSpec((None, 1, D_MODEL), lambda l, j: (l, 0, j))],
        out_specs=pl.BlockSpec((None, MOD_ROWS, D_MODEL), lambda l, j: (l, 0, j)),
        compiler_params=pltpu.CompilerParams(vmem_limit_bytes=VMEM_LIMIT),
        name="adaln_mod",
    )(cvecs, ada_w, ada_b.reshape(DEPTH, 1, 6 * D_MODEL))


def _mod_row_map(layer, tiles_per_batch):
    if tiles_per_batch is None:
        return lambda i: (layer, 0, 0, 0)
    return lambda i: (layer, 1 + i // tiles_per_batch, 0, 0)


def _in_kernel(x_ref, mod_ref, g_ref, w_ref, sc_ref, q_ref, kv_ref, dq_ref, dz_ref, ab_ref):
    x = x_ref[...]
    ms = jnp.mean(x * x, axis=-1, keepdims=True)
    h = (x * lax.rsqrt(ms + EPS) * g_ref[...]) * (1.0 + mod_ref[1:2, :]) + mod_ref[0:1, :]
    hb = h.astype(BF16)
    for ref, a, b in ((sc_ref, OFF_SC, OFF_Q), (q_ref, OFF_Q, OFF_KV), (kv_ref, OFF_KV, OFF_DQ),
                      (dq_ref, OFF_DQ, OFF_DZ), (dz_ref, OFF_DZ, OFF_AB), (ab_ref, OFF_AB, IN_COLS)):
        ref[...] = _dot(hb, w_ref[:, a:b])


def _in_projection(x, mod, norm1_g, w_in_b, layer, tiles_per_batch):
    t = x.shape[0]
    widths = (OFF_Q - OFF_SC, OFF_KV - OFF_Q, OFF_DQ - OFF_KV, OFF_DZ - OFF_DQ, OFF_AB - OFF_DZ, IN_COLS - OFF_AB)
    return pl.pallas_call(
        _in_kernel,
        out_shape=[jax.ShapeDtypeStruct((t, w), F32) for w in widths],
        grid=(t // TM,),
        in_specs=[pl.BlockSpec((TM, D_MODEL), lambda i: (i, 0)),
                  pl.BlockSpec((None, None, 6, D_MODEL), _mod_row_map(layer, tiles_per_batch)),
                  pl.BlockSpec((None, 1, D_MODEL), lambda i: (layer, 0, 0)),
                  pl.BlockSpec((None, D_MODEL, IN_COLS), lambda i: (layer, 0, 0),
                               pipeline_mode=pl.Buffered(1))],
        out_specs=[pl.BlockSpec((TM, w), lambda i: (i, 0)) for w in widths],
        compiler_params=pltpu.CompilerParams(vmem_limit_bytes=VMEM_LIMIT),
        name="in_proj",
    )(x, mod, norm1_g, w_in_b)


def _prep_q(q, qg, cos, sin):
    out = []
    for g in range(ATT_DIM // LANES):
        x = _head_rms(q[:, g * LANES:(g + 1) * LANES], qg)
        if cos is not None:
            x = _rope(x, cos, sin)
        out.append(x * (HEAD_DIM ** -0.5))
    return out


def _stack_group(qa, qb):
    lo = _lo_mask(qa.shape)
    parts = [jnp.where(lo, qa, 0.0), jnp.where(lo, 0.0, qa), jnp.where(lo, qb, 0.0), jnp.where(lo, 0.0, qb)]
    return jnp.concatenate(parts, axis=0).astype(BF16)


def _sink_column(sink_ref, layer, kvh, rows):
    return jnp.concatenate([jnp.full((rows, 1), sink_ref[layer, kvh * N_GROUP + g], F32)
                            for g in range(N_GROUP)], axis=0)


def _unstack_group(o4, rows):
    lo = _lo_mask((rows, LANES))
    return (jnp.where(lo, o4[0:rows], o4[rows:2 * rows]),
            jnp.where(lo, o4[2 * rows:3 * rows], o4[3 * rows:4 * rows]))


def _attn_ctx_kernel(sink_ref, q_ref, kv_ref, qg_ref, kg_ref, o_ref, kn_ref, *, layer):
    kv = kv_ref[...]
    kn = _head_rms(kv[:, :LANES], kg_ref[...])
    kn_ref[...] = kn
    v2 = kv[:, LANES:]
    qs = _prep_q(q_ref[...], qg_ref[...], None, None)
    for kvh in range(N_KV_HEADS):
        kd = _dup_half(kn, kvh).astype(BF16)
        vd = _dup_half(v2, kvh).astype(BF16)
        q4 = _stack_group(qs[2 * kvh], qs[2 * kvh + 1])
        s = _dot_nt(q4, kd)
        sink = _sink_column(sink_ref, layer, kvh, SEQ)
        m = jnp.maximum(jnp.max(s, axis=-1, keepdims=True), sink)
        e = jnp.exp(s - m)
        den = jnp.sum(e, axis=-1, keepdims=True) + jnp.exp(sink - m)
        o4 = _dot(e.astype(BF16), vd) / den
        oa, ob = _unstack_group(o4, SEQ)
        o_ref[:, (2 * kvh) * LANES:(2 * kvh + 1) * LANES] = oa
        o_ref[:, (2 * kvh + 1) * LANES:(2 * kvh + 2) * LANES] = ob


def _attention_ctx(q, kv, q_norm_g, k_norm_g, attn_sink, layer):
    t = q.shape[0]
    return pl.pallas_call(
        functools.partial(_attn_ctx_kernel, layer=layer),
        out_shape=[jax.ShapeDtypeStruct((t, ATT_DIM), F32), jax.ShapeDtypeStruct((t, KV_DIM), F32)],
        grid=(t // SEQ,),
        in_specs=[pl.BlockSpec(memory_space=pltpu.SMEM),
                  pl.BlockSpec((SEQ, ATT_DIM), lambda b: (b, 0)),
                  pl.BlockSpec((SEQ, 2 * KV_DIM), lambda b: (b, 0)),
                  pl.BlockSpec((None, 1, LANES), lambda b: (layer, 0, 0)),
                  pl.BlockSpec((None, 1, LANES), lambda b: (layer, 0, 0))],
        out_specs=[pl.BlockSpec((SEQ, ATT_DIM), lambda b: (b, 0)),
                   pl.BlockSpec((SEQ, KV_DIM), lambda b: (b, 0))],
        compiler_params=pltpu.CompilerParams(vmem_limit_bytes=VMEM_LIMIT),
        name="attn_ctx",
    )(attn_sink, q, kv, q_norm_g, k_norm_g)


QBLK = 128
WIN_KEYS = QBLK + 2 * WINDOW
PREP_ROWS = 512


def _attn_lat_kernel(sink_ref, q_ref, kv_ref, kc_ref, vc_ref, qg_ref, kg_ref, cos_ref, sin_ref, o_ref,
                     kd_s, vd_s, kcd_s, vcd_s, *, layer):
    i = pl.program_id(1)
    n = DEC_SEQ

    @pl.when(i == 0)
    def _():
        zpad = jnp.zeros((WINDOW, LANES), BF16)
        for kvh in range(N_KV_HEADS):
            for s in (kd_s, vd_s):
                s[kvh, 0:WINDOW, :] = zpad
                s[kvh, WINDOW + n:2 * WINDOW + n, :] = zpad
            kcd_s[kvh] = _dup_half(kc_ref[...], kvh).astype(BF16)
            vcd_s[kvh] = _dup_half(vc_ref[...], kvh).astype(BF16)

        def prep(r, carry):
            r0 = pl.multiple_of(r * PREP_ROWS, PREP_ROWS)
            kv = kv_ref[pl.ds(r0, PREP_ROWS), :]
            kn = _head_rms(kv[:, :LANES], kg_ref[...])
            kr = _rope(kn, cos_ref[pl.ds(r0, PREP_ROWS), :], sin_ref[pl.ds(r0, PREP_ROWS), :])
            v2 = kv[:, LANES:]
            for kvh in range(N_KV_HEADS):
                kd_s[kvh, pl.ds(WINDOW + r0, PREP_ROWS), :] = _dup_half(kr, kvh).astype(BF16)
                vd_s[kvh, pl.ds(WINDOW + r0, PREP_ROWS), :] = _dup_half(v2, kvh).astype(BF16)
            return carry

        lax.fori_loop(0, n // PREP_ROWS, prep, 0)

    start = pl.multiple_of(i * QBLK, QBLK)
    qs = _prep_q(q_ref[...], qg_ref[...], cos_ref[pl.ds(start, QBLK), :], sin_ref[pl.ds(start, QBLK), :])
    rows = N_GROUP * QBLK
    rr = lax.broadcasted_iota(jnp.int32, (rows, WIN_KEYS), 0) % QBLK
    jj = lax.broadcasted_iota(jnp.int32, (rows, WIN_KEYS), 1)
    kpos = jj + (start - WINDOW)
    valid = (jj >= rr) & (jj <= rr + 2 * WINDOW) & (kpos >= 0) & (kpos < n)
    for kvh in range(N_KV_HEADS):
        q4 = _stack_group(qs[2 * kvh], qs[2 * kvh + 1])
        kw = kd_s[kvh, pl.ds(start, WIN_KEYS), :]
        vw = vd_s[kvh, pl.ds(start, WIN_KEYS), :]
        s_c = _dot_nt(q4, kcd_s[kvh])
        s_w = jnp.where(valid, _dot_nt(q4, kw), NEG)
        sink = _sink_column(sink_ref, layer, kvh, QBLK)
        m = jnp.maximum(jnp.maximum(jnp.max(s_c, axis=-1, keepdims=True),
                                    jnp.max(s_w, axis=-1, keepdims=True)), sink)
        e_c = jnp.exp(s_c - m)
        e_w = jnp.exp(s_w - m)
        den = (jnp.sum(e_c, axis=-1, keepdims=True) + jnp.sum(e_w, axis=-1, keepdims=True)
               + jnp.exp(sink - m))
        o4 = (_dot(e_c.astype(BF16), vcd_s[kvh]) + _dot(e_w.astype(BF16), vw)) / den
        oa, ob = _unstack_group(o4, QBLK)
        o_ref[:, (2 * kvh) * LANES:(2 * kvh + 1) * LANES] = oa
        o_ref[:, (2 * kvh + 1) * LANES:(2 * kvh + 2) * LANES] = ob


def _attention_lat(q, kv, cache_k, cache_v, q_norm_g, k_norm_g, attn_sink, cos_t, sin_t, layer):
    t = q.shape[0]
    n = DEC_SEQ
    nblk = n // QBLK
    return pl.pallas_call(
        functools.partial(_attn_lat_kernel, layer=layer),
        out_shape=jax.ShapeDtypeStruct((t, ATT_DIM), F32),
        grid=(DEC_BATCH, nblk),
        in_specs=[pl.BlockSpec(memory_space=pltpu.SMEM),
                  pl.BlockSpec((QBLK, ATT_DIM), lambda b, i: (b * nblk + i, 0)),
                  pl.BlockSpec((n, 2 * KV_DIM), lambda b, i: (b, 0)),
                  pl.BlockSpec((None, None, PAST_LEN, KV_DIM), lambda b, i: (b, layer, 0, 0)),
                  pl.BlockSpec((None, None, PAST_LEN, KV_DIM), lambda b, i: (b, layer, 0, 0)),
                  pl.BlockSpec((None, 1, LANES), lambda b, i: (layer, 0, 0)),
                  pl.BlockSpec((None, 1, LANES), lambda b, i: (layer, 0, 0)),
                  pl.BlockSpec((n, LANES), lambda b, i: (0, 0)),
                  pl.BlockSpec((n, LANES), lambda b, i: (0, 0))],
        out_specs=pl.BlockSpec((QBLK, ATT_DIM), lambda b, i: (b * nblk + i, 0)),
        scratch_shapes=[pltpu.VMEM((N_KV_HEADS, n + 2 * WINDOW, LANES), BF16),
                        pltpu.VMEM((N_KV_HEADS, n + 2 * WINDOW, LANES), BF16),
                        pltpu.VMEM((N_KV_HEADS, PAST_LEN, LANES), BF16),
                        pltpu.VMEM((N_KV_HEADS, PAST_LEN, LANES), BF16)],
        compiler_params=pltpu.CompilerParams(vmem_limit_bytes=VMEM_LIMIT),
        name="attn_lat",
    )(attn_sink, q, kv, cache_k, cache_v, q_norm_g, k_norm_g, cos_t, sin_t)


def _dn_conv(x, w_ref, n):
    row = lax.broadcasted_iota(jnp.int32, x.shape, 0)
    prev = jnp.where(row == 0, 0.0, pltpu.roll(x, 1, axis=0))
    nxt = jnp.where(row == n - 1, 0.0, pltpu.roll(x, n - 1, axis=0))
    return _silu(prev * w_ref[0:1, :] + x * w_ref[1:2, :] + nxt * w_ref[2:3, :])


def _dn_kernel(q_ref, k_ref, v_ref, z_ref, ab_ref, wq_ref, wk_ref, wv_ref, pa_ref, pdt_ref, ng_ref, s0_ref,
               o_ref, sout_ref, qs, ks, vs, gs, bs, st, *, n):
    nc = n // CHUNK
    c64 = (CHUNK, CHUNK)
    qs[...] = _head_l2n(_dn_conv(q_ref[...], wq_ref, n)) * (DN_HEAD_DIM ** -0.5)
    ks[...] = _head_l2n(_dn_conv(k_ref[...], wk_ref, n))
    vs[...] = _dn_conv(v_ref[...], wv_ref, n)
    ab = ab_ref[...]
    gs[...] = -jnp.exp(pa_ref[...]) * _softplus(ab + pdt_ref[...])
    bs[...] = _sigmoid(ab)
    o_ref[...] = jnp.zeros_like(o_ref)
    zero = jnp.zeros(c64, F32)
    for d in range(2):
        st[d] = jnp.concatenate([jnp.concatenate([s0_ref[d, 0], zero], axis=1),
                                 jnp.concatenate([zero, s0_ref[d, 1]], axis=1)], axis=0)

    ri = lax.broadcasted_iota(jnp.int32, c64, 0)
    ci = lax.broadcasted_iota(jnp.int32, c64, 1)
    lo = _lo_mask((CHUNK, LANES))
    r2 = lax.broadcasted_iota(jnp.int32, (2 * HEAD_DIM, 2 * HEAD_DIM), 0) // HEAD_DIM
    c2 = lax.broadcasted_iota(jnp.int32, (2 * HEAD_DIM, 2 * HEAD_DIM), 1) // HEAD_DIM
    bd_mask = r2 == c2
    eye = (ri == ci).astype(F32)
    incl = (ri >= ci, ri <= ci)
    strict = (ri > ci, ri < ci)
    cum = tuple(m.astype(F32) for m in incl)
    after = tuple(m.astype(F32) for m in (ri < ci, ri > ci))
    cum_after = tuple(jnp.concatenate([cum[d], after[d]], axis=0) for d in range(2))

    def level_mask(d, k):
        rk, ck = ri >> k, ci >> k
        if d == 0:
            return (rk - ck == 1) & ((rk & 1) == 1)
        return (ck - rk == 1) & ((ck & 1) == 1)

    def solve(l, d):
        t = eye - jnp.where(level_mask(d, 0), l, 0.0)
        for k in range(1, 6):
            lb = jnp.where(level_mask(d, k), l, 0.0).astype(BF16)
            tb = t.astype(BF16)
            t = t - _dot(_dot(tb, lb).astype(BF16), tb)
        return t

    def chunk_step(i, carry):
        for d in range(2):
            c = i if d == 0 else nc - 1 - i
            rows = pl.ds(pl.multiple_of(c * CHUNK, CHUNK), CHUNK)
            qc, kc, vc = qs[rows, :], ks[rows, :], vs[rows, :]
            gx, bx = gs[rows, :], bs[rows, :]
            kcb = kc.astype(BF16)
            gb = [jnp.broadcast_to(gx[:, 2 * d + j:2 * d + j + 1], (CHUNK, LANES)) for j in range(2)]
            bb = [jnp.broadcast_to(bx[:, 4 + 2 * d + j:4 + 2 * d + j + 1], (CHUNK, LANES)) for j in range(2)]
            g2 = jnp.where(lo, gb[0], gb[1])
            b2 = jnp.where(lo, bb[0], bb[1])
            gcr = _dot_f32(cum_after[d], g2)
            e_gc = jnp.exp(gcr[:CHUNK])
            e_rest = jnp.exp(gcr[CHUNK:])
            e_last = e_gc[CHUNK - 1:CHUNK, :] if d == 0 else e_gc[0:1, :]
            rhs_u = vc * b2
            rhs_w = kc * b2 * e_gc
            rhs = jnp.concatenate([rhs_u, rhs_w], axis=1)
            rhs_b = rhs.astype(BF16)
            sol, amat = [], []
            for j in range(2):
                half = lo if j == 0 else jnp.logical_not(lo)
                gram = _dot_nt(jnp.where(half, kc, 0.0).astype(BF16), kcb)
                qk = _dot_nt(jnp.where(half, qc, 0.0).astype(BF16), kcb)
                diff = _dot_f32(cum[d], jnp.where(strict[d], gb[j][:, :CHUNK], 0.0))
                dec = jnp.where(incl[d], jnp.exp(diff), 0.0)
                lmat = jnp.where(strict[d], gram * bb[j][:, :CHUNK] * dec, 0.0)
                t = solve(lmat, d)
                sol.append(rhs + _dot((t - eye).astype(BF16), rhs_b))
                amat.append((qk * dec).astype(BF16))
            u2 = jnp.where(lo, sol[0][:, :LANES], sol[1][:, :LANES])
            w2 = jnp.where(lo, sol[0][:, LANES:], sol[1][:, LANES:])
            s = st[d]
            sb = s.astype(BF16)
            v_new = u2 - _dot(w2.astype(BF16), sb)
            vnb = v_new.astype(BF16)
            o2 = _dot((qc * e_gc).astype(BF16), sb) + jnp.where(lo, _dot(amat[0], vnb), _dot(amat[1], vnb))
            kd = (kc * e_rest).astype(BF16)
            st[d] = s * e_last + jnp.where(bd_mask, _dot_tn(kd, vnb), 0.0)
            o_ref[rows, :] += o2
        return carry

    lax.fori_loop(0, nc, chunk_step, 0)

    for d in range(2):
        s = st[d]
        sout_ref[d, 0] = s[:HEAD_DIM, :HEAD_DIM]
        sout_ref[d, 1] = s[HEAD_DIM:, HEAD_DIM:]
    o_ref[...] = _head_rms(o_ref[...], ng_ref[...]) * _silu(z_ref[...])


def _delta_net(dq, dz, dab, dn_conv_w, pa, pdt, dn_norm_g, s0, layer, s0_layer, n):
    t = dq.shape[0]
    bsz = t // n
    conv_spec = lambda off: pl.BlockSpec((None, 3, LANES), lambda b, p: (layer, 0, off + p))
    col_spec = lambda off: pl.BlockSpec((n, LANES), lambda b, p: (b, off + p))
    return pl.pallas_call(
        functools.partial(_dn_kernel, n=n),
        out_shape=[jax.ShapeDtypeStruct((t, DN_DIM), F32),
                   jax.ShapeDtypeStruct((bsz, 2, N_DN_HEADS, DN_HEAD_DIM, DN_HEAD_DIM), F32)],
        grid=(bsz, N_PAIRS),
        in_specs=[col_spec(0), col_spec(N_PAIRS), col_spec(2 * N_PAIRS), col_spec(0), col_spec(0),
                  conv_spec(0), conv_spec(N_PAIRS), conv_spec(2 * N_PAIRS),
                  pl.BlockSpec((None, None, 1, LANES), lambda b, p: (layer, p, 0, 0)),
                  pl.BlockSpec((None, None, 1, LANES), lambda b, p: (layer, p, 0, 0)),
                  pl.BlockSpec((None, 1, LANES), lambda b, p: (layer, 0, 0)),
                  pl.BlockSpec((None, None, 2, 2, DN_HEAD_DIM, DN_HEAD_DIM),
                               lambda b, p: (b, s0_layer, 0, p, 0, 0))],
        out_specs=[col_spec(0),
                   pl.BlockSpec((None, 2, 2, DN_HEAD_DIM, DN_HEAD_DIM), lambda b, p: (b, 0, p, 0, 0))],
        scratch_shapes=[pltpu.VMEM((n, LANES), F32) for _ in range(5)]
                       + [pltpu.VMEM((2, 2 * DN_HEAD_DIM, 2 * DN_HEAD_DIM), F32)],
        compiler_params=pltpu.CompilerParams(vmem_limit_bytes=VMEM_LIMIT),
        name="delta_net",
    )(dq, dq, dq, dz, dab, dn_conv_w, dn_conv_w, dn_conv_w, pa, pdt, dn_norm_g, s0)


def _out_kernel(x_ref, sc_ref, scp_ref, scn_ref, att_ref, dn_ref, mod_ref, cw_ref, wo_ref, n2_ref,
                wg_ref, wu_ref, wd_ref, o_ref, *, tiles_per_seq):
    t = pl.program_id(0) % tiles_per_seq
    sc = sc_ref[...]
    prod = sc[:, SC_DIM:2 * SC_DIM] * sc[:, 2 * SC_DIM:]
    pv = scp_ref[7:8, :]
    nx = scn_ref[0:1, :]
    prev_row = jnp.where(t > 0, pv[:, SC_DIM:2 * SC_DIM] * pv[:, 2 * SC_DIM:], 0.0)
    next_row = jnp.where(t < tiles_per_seq - 1, nx[:, SC_DIM:2 * SC_DIM] * nx[:, 2 * SC_DIM:], 0.0)
    row = lax.broadcasted_iota(jnp.int32, prod.shape, 0)
    p_prev = jnp.where(row == 0, prev_row, pltpu.roll(prod, 1, axis=0))
    p_next = jnp.where(row == TM - 1, next_row, pltpu.roll(prod, TM - 1, axis=0))
    y_sc = sc[:, :SC_DIM] * (p_prev * cw_ref[0:1, :] + prod * cw_ref[1:2, :] + p_next * cw_ref[2:3, :])
    y = (_dot(y_sc.astype(BF16), wo_ref[0:SC_DIM, :])
         + _dot(att_ref[...].astype(BF16), wo_ref[SC_DIM:SC_DIM + ATT_DIM, :])
         + _dot(dn_ref[...].astype(BF16), wo_ref[SC_DIM + ATT_DIM:, :]))
    x1 = x_ref[...] + mod_ref[2:3, :] * y
    ms = jnp.mean(x1 * x1, axis=-1, keepdims=True)
    h2 = (x1 * lax.rsqrt(ms + EPS) * n2_ref[...]) * (1.0 + mod_ref[4:5, :]) + mod_ref[3:4, :]
    hb = h2.astype(BF16)
    act = _silu(_dot(hb, wg_ref[...])) * _dot(hb, wu_ref[...])
    o_ref[...] = x1 + mod_ref[5:6, :] * _dot(act.astype(BF16), wd_ref[...])


def _output_stage(x, sc, att, dn, mod, sc_conv_w, w_out_b, norm2_g, w_gate_b, w_up_b, w_down_b, layer,
                  tiles_per_batch, tiles_per_seq):
    t = x.shape[0]
    halo = TM // 8
    last = t // 8 - 1
    resident = lambda shape: pl.BlockSpec((None,) + shape, lambda i: (layer, 0, 0), pipeline_mode=pl.Buffered(1))
    return pl.pallas_call(
        functools.partial(_out_kernel, tiles_per_seq=tiles_per_seq),
        out_shape=jax.ShapeDtypeStruct((t, D_MODEL), F32),
        grid=(t // TM,),
        in_specs=[pl.BlockSpec((TM, D_MODEL), lambda i: (i, 0)),
                  pl.BlockSpec((TM, 3 * SC_DIM), lambda i: (i, 0)),
                  pl.BlockSpec((8, 3 * SC_DIM), lambda i: (jnp.maximum(i * halo - 1, 0), 0)),
                  pl.BlockSpec((8, 3 * SC_DIM), lambda i: (jnp.minimum((i + 1) * halo, last), 0)),
                  pl.BlockSpec((TM, ATT_DIM), lambda i: (i, 0)),
                  pl.BlockSpec((TM, DN_DIM), lambda i: (i, 0)),
                  pl.BlockSpec((None, None, 6, D_MODEL), _mod_row_map(layer, tiles_per_batch)),
                  pl.BlockSpec((None, 3, SC_DIM), lambda i: (layer, 0, 0)),
                  resident((MIX_DIM, D_MODEL)),
                  pl.BlockSpec((None, 1, D_MODEL), lambda i: (layer, 0, 0)),
                  resident((D_MODEL, D_FF)), resident((D_MODEL, D_FF)), resident((D_FF, D_MODEL))],
        out_specs=pl.BlockSpec((TM, D_MODEL), lambda i: (i, 0)),
        compiler_params=pltpu.CompilerParams(vmem_limit_bytes=VMEM_LIMIT),
        name="out_stage",
    )(x, sc, sc, sc, att, dn, mod, sc_conv_w, w_out_b, norm2_g, w_gate_b, w_up_b, w_down_b)


def _rope_tables():
    pos = jnp.arange(DEC_SEQ)
    half = HEAD_DIM // 4
    inv = 1.0 / (ROPE_BASE ** (jnp.arange(half, dtype=F32) / half))
    ang_r = (pos // GRID_W).astype(F32)[:, None] * inv
    ang_c = (pos % GRID_W).astype(F32)[:, None] * inv
    cos = jnp.concatenate([jnp.cos(ang_r)] * 2 + [jnp.cos(ang_c)] * 2, axis=-1)
    sin = jnp.concatenate([-jnp.sin(ang_r), jnp.sin(ang_r), -jnp.sin(ang_c), jnp.sin(ang_c)], axis=-1)
    return jnp.tile(cos, (1, 2)), jnp.tile(sin, (1, 2))


def _pack_w_in(w_in):
    a0 = OFF_AB
    b0 = OFF_AB + 2 * N_DN_HEADS
    blocks = [w_in[:, :, :OFF_AB]]
    for p in range(N_PAIRS):
        idx = [a0 + d * N_DN_HEADS + 2 * p + j for d in range(2) for j in range(2)]
        idx += [b0 + d * N_DN_HEADS + 2 * p + j for d in range(2) for j in range(2)]
        blocks.append(jnp.take(w_in, jnp.asarray(idx), axis=2))
        blocks.append(jnp.zeros((DEPTH, D_MODEL, LANES - len(idx)), w_in.dtype))
    return jnp.concatenate(blocks, axis=2).astype(BF16)


def _pair_lanes(p):
    x = p.reshape(DEPTH, 2, N_PAIRS, 2).transpose(0, 2, 1, 3).reshape(DEPTH, N_PAIRS, 1, 4)
    return jnp.pad(x, ((0, 0), (0, 0), (0, 0), (0, LANES - 4)))


def kernel(x_prompt, x_sample, cache_k, cache_v, state_delta, c, c_ctx, w_in, w_out, ada_w, ada_b,
           norm1_g, norm2_g, sc_conv_w, dn_conv_w, q_norm_g, k_norm_g, attn_sink, dn_A_log,
           dn_dt_bias, dn_norm_g, w_gate, w_up, w_down):
    w_in_b = _pack_w_in(w_in)
    w_out_b, w_gate_b, w_up_b, w_down_b = (w.astype(BF16) for w in (w_out, w_gate, w_up, w_down))
    n1 = norm1_g.reshape(DEPTH, 1, D_MODEL)
    n2 = norm2_g.reshape(DEPTH, 1, D_MODEL)
    qg = jnp.tile(q_norm_g, (1, 2)).reshape(DEPTH, 1, LANES)
    kg = jnp.tile(k_norm_g, (1, 2)).reshape(DEPTH, 1, LANES)
    ng = jnp.tile(dn_norm_g, (1, 2)).reshape(DEPTH, 1, LANES)
    pa, pdt = _pair_lanes(dn_A_log), _pair_lanes(dn_dt_bias)
    cos_t, sin_t = _rope_tables()
    ck = cache_k.reshape(DEC_BATCH, DEPTH, PAST_LEN, KV_DIM)
    cv = cache_v.reshape(DEC_BATCH, DEPTH, PAST_LEN, KV_DIM)
    zero_state = jnp.zeros((BATCH, 1, 2, N_DN_HEADS, DN_HEAD_DIM, DN_HEAD_DIM), F32)

    cvecs = jnp.concatenate([c_ctx[None, :], c, jnp.zeros((MOD_ROWS - 1 - DEC_BATCH, D_MODEL), F32)], axis=0)
    mod = _modulation(cvecs, ada_w, ada_b).reshape(DEPTH, MOD_ROWS, 6, D_MODEL)

    xp = x_prompt.reshape(BATCH * SEQ, D_MODEL)
    xs = x_sample.reshape(DEC_BATCH * DEC_SEQ, D_MODEL)
    new_k, new_v, new_s = [], [], []
    for l in range(DEPTH):
        out_args = (mod, sc_conv_w, w_out_b, n2, w_gate_b, w_up_b, w_down_b, l)
        sc, q, kv, dq, dz, dab = _in_projection(xp, mod, n1, w_in_b, l, None)
        att, kn = _attention_ctx(q, kv, qg, kg, attn_sink, l)
        dn, s_out = _delta_net(dq, dz, dab, dn_conv_w, pa, pdt, ng, zero_state, l, 0, SEQ)
        xp = _output_stage(xp, sc, att, dn, *out_args, None, SEQ // TM)
        new_k.append(kn.reshape(BATCH, SEQ, N_KV_HEADS, HEAD_DIM))
        new_v.append(kv[:, KV_DIM:].reshape(BATCH, SEQ, N_KV_HEADS, HEAD_DIM))
        new_s.append(s_out)
        sc, q, kv, dq, dz, dab = _in_projection(xs, mod, n1, w_in_b, l, DEC_SEQ // TM)
        att = _attention_lat(q, kv, ck, cv, qg, kg, attn_sink, cos_t, sin_t, l)
        dn, _ = _delta_net(dq, dz, dab, dn_conv_w, pa, pdt, ng, state_delta, l, l, DEC_SEQ)
        xs = _output_stage(xs, sc, att, dn, *out_args, DEC_SEQ // TM, DEC_SEQ // TM)
    return (xp.reshape(BATCH, SEQ, D_MODEL), xs.reshape(DEC_BATCH, DEC_SEQ, D_MODEL),
            jnp.stack(new_k, axis=1), jnp.stack(new_v, axis=1), jnp.stack(new_s, axis=1))
```

```python
import functools

import numpy as np
import jax
import jax.numpy as jnp
from jax import lax
from jax.experimental import pallas as pl
from jax.experimental.pallas import tpu as pltpu

F32 = jnp.float32
BF16 = jnp.bfloat16

D_MODEL = 1024
BATCH = 16
SEQ = 256
DEPTH = 2
DEC_BATCH = 4
DEC_SEQ = 4096
PAST_LEN = 512
GRID_W = 64
SC_DIM = 256
N_Q_HEADS = 8
N_KV_HEADS = 2
N_GROUP = N_Q_HEADS // N_KV_HEADS
HEAD_DIM = 64
ATT_DIM = N_Q_HEADS * HEAD_DIM
KV_DIM = N_KV_HEADS * HEAD_DIM
WINDOW = 128
N_DN_HEADS = 4
DN_HEAD_DIM = 64
DN_DIM = N_DN_HEADS * DN_HEAD_DIM
CHUNK = 64
MIX_DIM = SC_DIM + ATT_DIM + DN_DIM
D_FF = -(-8 * D_MODEL // (3 * 256)) * 256
ROPE_BASE = 10000.0
EPS = 1e-6
NEG = -1e30

LANES = 128
V7X_VMEM_BYTES = 64 * 1024 * 1024
VMEM_LIMIT = (V7X_VMEM_BYTES * 3) // 4

OFF_SC, OFF_Q, OFF_KV, OFF_DQ, OFF_DZ, OFF_AB = 0, 768, 1280, 1536, 2304, 2560
IN_COLS = OFF_AB + 2 * LANES
N_PAIRS = N_DN_HEADS // 2
MOD_ROWS = 8
TM = 256


def _sigmoid(x):
    return 1.0 / (1.0 + jnp.exp(-x))


def _silu(x):
    return x * _sigmoid(x)


def _softplus(x):
    return jnp.maximum(x, 0.0) + jnp.log1p(jnp.exp(-jnp.abs(x)))


def _dot(a, b):
    return jnp.dot(a, b, preferred_element_type=F32)


def _dot_nt(a, b):
    return lax.dot_general(a, b, (((1,), (1,)), ((), ())), preferred_element_type=F32)


def _dot_tn(a, b):
    return lax.dot_general(a, b, (((0,), (0,)), ((), ())), preferred_element_type=F32)


def _lo_mask(shape):
    return lax.broadcasted_iota(jnp.int32, shape, 1) % LANES < HEAD_DIM


def _half_sums(xx, lo):
    s_lo = jnp.sum(jnp.where(lo, xx, 0.0), axis=-1, keepdims=True)
    s_hi = jnp.sum(jnp.where(lo, 0.0, xx), axis=-1, keepdims=True)
    return jnp.where(lo, s_lo, s_hi)


def _head_rms(x, g_row):
    lo = _lo_mask(x.shape)
    ms = _half_sums(x * x, lo) * (1.0 / HEAD_DIM)
    return x * lax.rsqrt(ms + EPS) * g_row


def _head_l2n(x):
    lo = _lo_mask(x.shape)
    return x * lax.rsqrt(_half_sums(x * x, lo) + EPS)


def _rope(x, cos, sin):
    n = x.shape[1]
    lane = lax.broadcasted_iota(jnp.int32, x.shape, 1)
    swapped = jnp.where((lane & 16) == 0, pltpu.roll(x, n - 16, axis=1), pltpu.roll(x, 16, axis=1))
    return x * cos + swapped * sin


def _dup_half(x, half):
    lo = _lo_mask(x.shape)
    r = pltpu.roll(x, HEAD_DIM, axis=1)
    return jnp.where(lo, x, r) if half == 0 else jnp.where(lo, r, x)


def _mod_kernel(c_ref, w_ref, b_ref, o_ref):
    s = _silu(c_ref[...])
    o_ref[...] = _dot(s.astype(BF16), w_ref[...].astype(BF16)) + b_ref[...]


def _modulation(cvecs, ada_w, ada_b):
    nblk = 6
    return pl.pallas_call(
        _mod_kernel,
        out_shape=jax.ShapeDtypeStruct((DEPTH, MOD_ROWS, 6 * D_MODEL), F32),
        grid=(DEPTH, nblk),
        in_specs=[pl.BlockSpec((MOD_ROWS, D_MODEL), lambda l, j: (0, 0)),
                  pl.BlockSpec((None, D_MODEL, D_MODEL), lambda l, j: (l, 0, j)),
                  pl.BlockSpec((None, 1, D_MODEL), lambda l, j: (l, 0, j))],
        out_specs=pl.BlockSpec((None, MOD_ROWS, D_MODEL), lambda l, j: (l, 0, j)),
        compiler_params=pltpu.CompilerParams(vmem_limit_bytes=VMEM_LIMIT),
        name="adaln_mod",
    )(cvecs, ada_w, ada_b.reshape(DEPTH, 1, 6 * D_MODEL))


def _mod_row_map(layer, tiles_per_batch):
    if tiles_per_batch is None:
        return lambda i: (layer, 0, 0, 0)
    return lambda i: (layer, 1 + i // tiles_per_batch, 0, 0)


def _in_kernel(x_ref, mod_ref, g_ref, w_ref, sc_ref, q_ref, kv_ref, dq_ref, dz_ref, ab_ref):
    x = x_ref[...]
    ms = jnp.mean(x * x, axis=-1, keepdims=True)
    h = (x * lax.rsqrt(ms + EPS) * g_ref[...]) * (1.0 + mod_ref[1:2, :]) + mod_ref[0:1, :]
    hb = h.astype(BF16)
    for ref, a, b in ((sc_ref, OFF_SC, OFF_Q), (q_ref, OFF_Q, OFF_KV), (kv_ref, OFF_KV, OFF_DQ),
                      (dq_ref, OFF_DQ, OFF_DZ), (dz_ref, OFF_DZ, OFF_AB), (ab_ref, OFF_AB, IN_COLS)):
        ref[...] = _dot(hb, w_ref[:, a:b])


def _in_projection(x, mod, norm1_g, w_in_b, layer, tiles_per_batch):
    t = x.shape[0]
    widths = (OFF_Q - OFF_SC, OFF_KV - OFF_Q, OFF_DQ - OFF_KV, OFF_DZ - OFF_DQ, OFF_AB - OFF_DZ, IN_COLS - OFF_AB)
    return pl.pallas_call(
        _in_kernel,
        out_shape=[jax.ShapeDtypeStruct((t, w), F32) for w in widths],
        grid=(t // TM,),
        in_specs=[pl.BlockSpec((TM, D_MODEL), lambda i: (i, 0)),
                  pl.BlockSpec((None, None, 6, D_MODEL), _mod_row_map(layer, tiles_per_batch)),
                  pl.BlockSpec((None, 1, D_MODEL), lambda i: (layer, 0, 0)),
                  pl.BlockSpec((None, D_MODEL, IN_COLS), lambda i: (layer, 0, 0),
                               pipeline_mode=pl.Buffered(1))],
        out_specs=[pl.BlockSpec((TM, w), lambda i: (i, 0)) for w in widths],
        compiler_params=pltpu.CompilerParams(vmem_limit_bytes=VMEM_LIMIT),
        name="in_proj",
    )(x, mod, norm1_g, w_in_b)


def _prep_q(q, qg, cos, sin):
    out = []
    for g in range(ATT_DIM // LANES):
        x = _head_rms(q[:, g * LANES:(g + 1) * LANES], qg)
        if cos is not None:
            x = _rope(x, cos, sin)
        out.append(x * (HEAD_DIM ** -0.5))
    return out


def _stack_group(qa, qb):
    lo = _lo_mask(qa.shape)
    parts = [jnp.where(lo, qa, 0.0), jnp.where(lo, 0.0, qa), jnp.where(lo, qb, 0.0), jnp.where(lo, 0.0, qb)]
    return jnp.concatenate(parts, axis=0).astype(BF16)


def _sink_column(sink_ref, layer, kvh, rows):
    return jnp.concatenate([jnp.full((rows, 1), sink_ref[layer, kvh * N_GROUP + g], F32)
                            for g in range(N_GROUP)], axis=0)


def _unstack_group(o4, rows):
    lo = _lo_mask((rows, LANES))
    return (jnp.where(lo, o4[0:rows], o4[rows:2 * rows]),
            jnp.where(lo, o4[2 * rows:3 * rows], o4[3 * rows:4 * rows]))


def _attn_ctx_kernel(sink_ref, q_ref, kv_ref, qg_ref, kg_ref, o_ref, kn_ref, *, layer):
    kv = kv_ref[...]
    kn = _head_rms(kv[:, :LANES], kg_ref[...])
    kn_ref[...] = kn
    v2 = kv[:, LANES:]
    qs = _prep_q(q_ref[...], qg_ref[...], None, None)
    for kvh in range(N_KV_HEADS):
        kd = _dup_half(kn, kvh).astype(BF16)
        vd = _dup_half(v2, kvh).astype(BF16)
        q4 = _stack_group(qs[2 * kvh], qs[2 * kvh + 1])
        s = _dot_nt(q4, kd)
        sink = _sink_column(sink_ref, layer, kvh, SEQ)
        m = jnp.maximum(jnp.max(s, axis=-1, keepdims=True), sink)
        e = jnp.exp(s - m)
        den = jnp.sum(e, axis=-1, keepdims=True) + jnp.exp(sink - m)
        o4 = _dot(e.astype(BF16), vd) / den
        oa, ob = _unstack_group(o4, SEQ)
        o_ref[:, (2 * kvh) * LANES:(2 * kvh + 1) * LANES] = oa
        o_ref[:, (2 * kvh + 1) * LANES:(2 * kvh + 2) * LANES] = ob


def _attention_ctx(q, kv, q_norm_g, k_norm_g, attn_sink, layer):
    t = q.shape[0]
    return pl.pallas_call(
        functools.partial(_attn_ctx_kernel, layer=layer),
        out_shape=[jax.ShapeDtypeStruct((t, ATT_DIM), F32), jax.ShapeDtypeStruct((t, KV_DIM), F32)],
        grid=(t // SEQ,),
        in_specs=[pl.BlockSpec(memory_space=pltpu.SMEM),
                  pl.BlockSpec((SEQ, ATT_DIM), lambda b: (b, 0)),
                  pl.BlockSpec((SEQ, 2 * KV_DIM), lambda b: (b, 0)),
                  pl.BlockSpec((None, 1, LANES), lambda b: (layer, 0, 0)),
                  pl.BlockSpec((None, 1, LANES), lambda b: (layer, 0, 0))],
        out_specs=[pl.BlockSpec((SEQ, ATT_DIM), lambda b: (b, 0)),
                   pl.BlockSpec((SEQ, KV_DIM), lambda b: (b, 0))],
        compiler_params=pltpu.CompilerParams(vmem_limit_bytes=VMEM_LIMIT),
        name="attn_ctx",
    )(attn_sink, q, kv, q_norm_g, k_norm_g)


QBLK = 128
WIN_KEYS = QBLK + 2 * WINDOW
PREP_ROWS = 512


def _attn_lat_kernel(sink_ref, q_ref, kv_ref, kc_ref, vc_ref, qg_ref, kg_ref, cos_ref, sin_ref, o_ref,
                     kd_s, vd_s, kcd_s, vcd_s, *, layer):
    i = pl.program_id(1)
    n = DEC_SEQ

    @pl.when(i == 0)
    def _():
        zpad = jnp.zeros((WINDOW, LANES), BF16)
        for kvh in range(N_KV_HEADS):
            for s in (kd_s, vd_s):
                s[kvh, 0:WINDOW, :] = zpad
                s[kvh, WINDOW + n:2 * WINDOW + n, :] = zpad
            kcd_s[kvh] = _dup_half(kc_ref[...], kvh).astype(BF16)
            vcd_s[kvh] = _dup_half(vc_ref[...], kvh).astype(BF16)

        def prep(r, carry):
            r0 = pl.multiple_of(r * PREP_ROWS, PREP_ROWS)
            kv = kv_ref[pl.ds(r0, PREP_ROWS), :]
            kn = _head_rms(kv[:, :LANES], kg_ref[...])
            kr = _rope(kn, cos_ref[pl.ds(r0, PREP_ROWS), :], sin_ref[pl.ds(r0, PREP_ROWS), :])
            v2 = kv[:, LANES:]
            for kvh in range(N_KV_HEADS):
                kd_s[kvh, pl.ds(WINDOW + r0, PREP_ROWS), :] = _dup_half(kr, kvh).astype(BF16)
                vd_s[kvh, pl.ds(WINDOW + r0, PREP_ROWS), :] = _dup_half(v2, kvh).astype(BF16)
            return carry

        lax.fori_loop(0, n // PREP_ROWS, prep, 0)

    start = pl.multiple_of(i * QBLK, QBLK)
    qs = _prep_q(q_ref[...], qg_ref[...], cos_ref[pl.ds(start, QBLK), :], sin_ref[pl.ds(start, QBLK), :])
    rows = N_GROUP * QBLK
    rr = lax.broadcasted_iota(jnp.int32, (rows, WIN_KEYS), 0) % QBLK
    jj = lax.broadcasted_iota(jnp.int32, (rows, WIN_KEYS), 1)
    kpos = jj + (start - WINDOW)
    valid = (jj >= rr) & (jj <= rr + 2 * WINDOW) & (kpos >= 0) & (kpos < n)
    for kvh in range(N_KV_HEADS):
        q4 = _stack_group(qs[2 * kvh], qs[2 * kvh + 1])
        kw = kd_s[kvh, pl.ds(start, WIN_KEYS), :]
        vw = vd_s[kvh, pl.ds(start, WIN_KEYS), :]
        s_c = _dot_nt(q4, kcd_s[kvh])
        s_w = jnp.where(valid, _dot_nt(q4, kw), NEG)
        sink = _sink_column(sink_ref, layer, kvh, QBLK)
        m = jnp.maximum(jnp.maximum(jnp.max(s_c, axis=-1, keepdims=True),
                                    jnp.max(s_w, axis=-1, keepdims=True)), sink)
        e_c = jnp.exp(s_c - m)
        e_w = jnp.exp(s_w - m)
        den = (jnp.sum(e_c, axis=-1, keepdims=True) + jnp.sum(e_w, axis=-1, keepdims=True)
               + jnp.exp(sink - m))
        o4 = (_dot(e_c.astype(BF16), vcd_s[kvh]) + _dot(e_w.astype(BF16), vw)) / den
        oa, ob = _unstack_group(o4, QBLK)
        o_ref[:, (2 * kvh) * LANES:(2 * kvh + 1) * LANES] = oa
        o_ref[:, (2 * kvh + 1) * LANES:(2 * kvh + 2) * LANES] = ob


def _attention_lat(q, kv, cache_k, cache_v, q_norm_g, k_norm_g, attn_sink, cos_t, sin_t, layer):
    t = q.shape[0]
    n = DEC_SEQ
    nblk = n // QBLK
    return pl.pallas_call(
        functools.partial(_attn_lat_kernel, layer=layer),
        out_shape=jax.ShapeDtypeStruct((t, ATT_DIM), F32),
        grid=(DEC_BATCH, nblk),
        in_specs=[pl.BlockSpec(memory_space=pltpu.SMEM),
                  pl.BlockSpec((QBLK, ATT_DIM), lambda b, i: (b * nblk + i, 0)),
                  pl.BlockSpec((n, 2 * KV_DIM), lambda b, i: (b, 0)),
                  pl.BlockSpec((None, None, PAST_LEN, KV_DIM), lambda b, i: (b, layer, 0, 0)),
                  pl.BlockSpec((None, None, PAST_LEN, KV_DIM), lambda b, i: (b, layer, 0, 0)),
                  pl.BlockSpec((None, 1, LANES), lambda b, i: (layer, 0, 0)),
                  pl.BlockSpec((None, 1, LANES), lambda b, i: (layer, 0, 0)),
                  pl.BlockSpec((n, LANES), lambda b, i: (0, 0)),
                  pl.BlockSpec((n, LANES), lambda b, i: (0, 0))],
        out_specs=pl.BlockSpec((QBLK, ATT_DIM), lambda b, i: (b * nblk + i, 0)),
        scratch_shapes=[pltpu.VMEM((N_KV_HEADS, n + 2 * WINDOW, LANES), BF16),
                        pltpu.VMEM((N_KV_HEADS, n + 2 * WINDOW, LANES), BF16),
                        pltpu.VMEM((N_KV_HEADS, PAST_LEN, LANES), BF16),
                        pltpu.VMEM((N_KV_HEADS, PAST_LEN, LANES), BF16)],
        compiler_params=pltpu.CompilerParams(vmem_limit_bytes=VMEM_LIMIT),
        name="attn_lat",
    )(attn_sink, q, kv, cache_k, cache_v, q_norm_g, k_norm_g, cos_t, sin_t)


def _dn_conv(x, w_ref, n):
    row = lax.broadcasted_iota(jnp.int32, x.shape, 0)
    prev = jnp.where(row == 0, 0.0, pltpu.roll(x, 1, axis=0))
    nxt = jnp.where(row == n - 1, 0.0, pltpu.roll(x, n - 1, axis=0))
    return _silu(prev * w_ref[0:1, :] + x * w_ref[1:2, :] + nxt * w_ref[2:3, :])


N_UNITS = 4
UW = N_UNITS * CHUNK
N_LEVELS = 6
M_INCL, M_STRICT, M_BD = N_LEVELS, N_LEVELS + 1, N_LEVELS + 2


def _split3(x):
    hi = x.astype(BF16)
    r = x - hi.astype(F32)
    mid = r.astype(BF16)
    lo = (r - mid.astype(F32)).astype(BF16)
    return hi, mid, lo


def _dot_exact01(a01, x):
    hi, mid, lo = _split3(x)
    return _dot(a01, hi) + _dot(a01, mid) + _dot(a01, lo)


def _rows_to_lanes(m):
    return m[0:CHUNK] + m[CHUNK:2 * CHUNK] + m[2 * CHUNK:3 * CHUNK] + m[3 * CHUNK:]


def _dn_kernel(q_ref, k_ref, v_ref, z_ref, ab_ref, wq_ref, wk_ref, wv_ref, pa_ref, pdt_ref, ng_ref, s0_ref,
               o_ref, sout_ref, qs, ks, vs, gbs, u_s, w_s, a_s, qe_s, kd_s, el_s, st, msk, cum_s, lhs_s, *, n):
    nc = n // CHUNK
    qs[...] = _head_l2n(_dn_conv(q_ref[...], wq_ref, n)) * (DN_HEAD_DIM ** -0.5)
    ks[...] = _head_l2n(_dn_conv(k_ref[...], wk_ref, n))
    vs[...] = _dn_conv(v_ref[...], wv_ref, n)
    ab = ab_ref[...]
    lane = lax.broadcasted_iota(jnp.int32, ab.shape, 1)
    gbs[...] = jnp.where(lane < N_UNITS, -jnp.exp(pa_ref[...]) * _softplus(ab + pdt_ref[...]), _sigmoid(ab))
    o_ref[...] = jnp.zeros_like(o_ref)

    rr = lax.broadcasted_iota(jnp.int32, (UW, UW), 0)
    cc = lax.broadcasted_iota(jnp.int32, (UW, UW), 1)
    ir, ic = rr % CHUNK, cc % CHUNK
    bd = (rr // CHUNK) == (cc // CHUNK)
    fwd = rr < 2 * CHUNK
    bwd = rr >= 2 * CHUNK
    as_f32 = lambda m: jnp.where(m, 1.0, 0.0).astype(F32)
    for k in range(N_LEVELS):
        rk, ck = ir >> k, ic >> k
        lower = (rk - ck == 1) & ((rk & 1) == 1)
        upper = (ck - rk == 1) & ((ck & 1) == 1)
        msk[k] = as_f32(bd & ((fwd & lower) | (bwd & upper)))
    incl = as_f32(bd & ((fwd & (ir >= ic)) | (bwd & (ir <= ic))))
    msk[M_INCL] = incl
    msk[M_STRICT] = as_f32(bd & ((fwd & (ir > ic)) | (bwd & (ir < ic))))
    msk[M_BD] = as_f32(bd)
    cum_s[...] = incl.astype(BF16)
    r4 = lax.broadcasted_iota(jnp.int32, (UW, CHUNK), 0)
    t4 = lax.broadcasted_iota(jnp.int32, (UW, CHUNK), 1)
    i4, blk = r4 % CHUNK, r4 // CHUNK
    lhs = (((blk == 0) & (t4 <= i4)) | ((blk == 1) & (t4 >= i4))
           | ((blk == 2) & (t4 > i4)) | ((blk == 3) & (t4 < i4)))
    lhs_s[...] = as_f32(lhs).astype(BF16)

    zero = jnp.zeros((CHUNK, CHUNK), F32)
    st[...] = jnp.concatenate(
        [jnp.concatenate([s0_ref[u // 2, u % 2] if v == u else zero for v in range(N_UNITS)], axis=1)
         for u in range(N_UNITS)], axis=0)

    unit_of_lane = lax.broadcasted_iota(jnp.int32, (CHUNK, UW), 1) // CHUNK
    eye = as_f32(rr == cc)

    def lane_pack(cols):
        return jnp.where(unit_of_lane == 0, cols[0],
                         jnp.where(unit_of_lane == 1, cols[1], jnp.where(unit_of_lane == 2, cols[2], cols[3])))

    def bd_stack(x):
        return (jnp.concatenate([x] * N_UNITS, axis=0) * msk[M_BD]).astype(BF16)

    def chunk_prep(c, carry):
        rows = pl.ds(pl.multiple_of(c * CHUNK, CHUNK), CHUNK)
        qc, kc, vc, gx = qs[rows, :], ks[rows, :], vs[rows, :], gbs[rows, :]
        gcols = [jnp.broadcast_to(gx[:, u:u + 1], (CHUNK, UW)) for u in range(N_UNITS)]
        bcols = [jnp.broadcast_to(gx[:, N_UNITS + u:N_UNITS + u + 1], (CHUNK, UW)) for u in range(N_UNITS)]
        g4, b4 = lane_pack(gcols), lane_pack(bcols)
        res = _dot_exact01(lhs_s[...], g4)
        gc4 = jnp.concatenate([res[0:CHUNK, :LANES], res[CHUNK:2 * CHUNK, LANES:]], axis=1)
        rest4 = jnp.concatenate([res[2 * CHUNK:3 * CHUNK, :LANES], res[3 * CHUNK:, LANES:]], axis=1)
        e_gc = jnp.exp(gc4)
        e_rest = jnp.exp(rest4)
        diff = _dot_exact01(cum_s[...], jnp.concatenate(gcols, axis=0) * msk[M_STRICT])
        dec = jnp.exp(diff) * msk[M_INCL]
        lo = _lo_mask(kc.shape)
        k2 = jnp.concatenate([jnp.where(lo, kc, 0.0), jnp.where(lo, 0.0, kc)], axis=0).astype(BF16)
        q2 = jnp.concatenate([jnp.where(lo, qc, 0.0), jnp.where(lo, 0.0, qc)], axis=0).astype(BF16)
        gram = _dot_nt(k2, k2)
        qk = _dot_nt(q2, k2)
        tile = lambda m: jnp.concatenate([jnp.concatenate([m, m], axis=1)] * 2, axis=0)
        lmat = tile(gram) * jnp.concatenate(bcols, axis=0) * dec * msk[M_STRICT]
        a_s[rows, :] = _rows_to_lanes(tile(qk) * dec).astype(BF16)
        t = eye - lmat * msk[0]
        for k in range(1, N_LEVELS):
            lb = (lmat * msk[k]).astype(BF16)
            tb = t.astype(BF16)
            t = t - _dot(_dot(tb, lb).astype(BF16), tb)
        tm = _rows_to_lanes(t - eye).astype(BF16)
        k4 = jnp.concatenate([kc, kc], axis=1)
        ru = jnp.concatenate([vc, vc], axis=1) * b4
        rw = k4 * b4 * e_gc
        u_s[rows, :] = ru + _dot(tm, bd_stack(ru))
        w_s[rows, :] = (rw + _dot(tm, bd_stack(rw))).astype(BF16)
        qe_s[rows, :] = (jnp.concatenate([qc, qc], axis=1) * e_gc).astype(BF16)
        kd_s[rows, :] = (k4 * e_rest).astype(BF16)
        el = jnp.concatenate([e_gc[CHUNK - 1:CHUNK, :LANES], e_gc[0:1, LANES:]], axis=1)
        el_s[c] = jnp.broadcast_to(el, (8, UW))
        return carry

    lax.fori_loop(0, nc, chunk_prep, 0)

    def scan_step(i, carry):
        cf, cb = i, nc - 1 - i
        rf = pl.ds(pl.multiple_of(cf * CHUNK, CHUNK), CHUNK)
        rb = pl.ds(pl.multiple_of(cb * CHUNK, CHUNK), CHUNK)
        mix = lambda ref: jnp.concatenate([ref[rf, :LANES], ref[rb, LANES:]], axis=1)
        el = jnp.concatenate([el_s[cf][0:1, :LANES], el_s[cb][0:1, LANES:]], axis=1)
        s = st[...]
        sb = s.astype(BF16)
        v_new = mix(u_s) - _dot(mix(w_s), sb)
        o4 = _dot(mix(qe_s), sb) + _dot(mix(a_s), bd_stack(v_new))
        st[...] = s * el + _dot_tn(mix(kd_s), v_new.astype(BF16)) * msk[M_BD]
        o_ref[rf, :] += o4[:, :LANES]
        o_ref[rb, :] += o4[:, LANES:]
        return carry

    lax.fori_loop(0, nc, scan_step, 0)

    s = st[...]
    for u in range(N_UNITS):
        sout_ref[u // 2, u % 2] = s[u * CHUNK:(u + 1) * CHUNK, u * CHUNK:(u + 1) * CHUNK]
    o_ref[...] = _head_rms(o_ref[...], ng_ref[...]) * _silu(z_ref[...])


def _delta_net(dq, dz, dab, dn_conv_w, pa, pdt, dn_norm_g, s0, layer, s0_layer, n):
    t = dq.shape[0]
    bsz = t // n
    nc = n // CHUNK
    conv_spec = lambda off: pl.BlockSpec((None, 3, LANES), lambda b, p: (layer, 0, off + p))
    col_spec = lambda off: pl.BlockSpec((n, LANES), lambda b, p: (b, off + p), pipeline_mode=pl.Buffered(1))
    return pl.pallas_call(
        functools.partial(_dn_kernel, n=n),
        out_shape=[jax.ShapeDtypeStruct((t, DN_DIM), F32),
                   jax.ShapeDtypeStruct((bsz, 2, N_DN_HEADS, DN_HEAD_DIM, DN_HEAD_DIM), F32)],
        grid=(bsz, N_PAIRS),
        in_specs=[col_spec(0), col_spec(N_PAIRS), col_spec(2 * N_PAIRS), col_spec(0), col_spec(0),
                  conv_spec(0), conv_spec(N_PAIRS), conv_spec(2 * N_PAIRS),
                  pl.BlockSpec((None, None, 1, LANES), lambda b, p: (layer, p, 0, 0)),
                  pl.BlockSpec((None, None, 1, LANES), lambda b, p: (layer, p, 0, 0)),
                  pl.BlockSpec((None, 1, LANES), lambda b, p: (layer, 0, 0)),
                  pl.BlockSpec((None, None, 2, 2, DN_HEAD_DIM, DN_HEAD_DIM),
                               lambda b, p: (b, s0_layer, 0, p, 0, 0))],
        out_specs=[pl.BlockSpec((n, LANES), lambda b, p: (b, p)),
                   pl.BlockSpec((None, 2, 2, DN_HEAD_DIM, DN_HEAD_DIM), lambda b, p: (b, 0, p, 0, 0))],
        scratch_shapes=[pltpu.VMEM((n, LANES), F32) for _ in range(4)]
                       + [pltpu.VMEM((n, UW), F32)]
                       + [pltpu.VMEM((n, UW), BF16) for _ in range(4)]
                       + [pltpu.VMEM((nc, 8, UW), F32),
                          pltpu.VMEM((UW, UW), F32),
                          pltpu.VMEM((N_LEVELS + 3, UW, UW), F32),
                          pltpu.VMEM((UW, UW), BF16),
                          pltpu.VMEM((UW, CHUNK), BF16)],
        compiler_params=pltpu.CompilerParams(vmem_limit_bytes=(V7X_VMEM_BYTES * 7) // 8),
        name="delta_net",
    )(dq, dq, dq, dz, dab, dn_conv_w, dn_conv_w, dn_conv_w, pa, pdt, dn_norm_g, s0)


def _out_kernel(x_ref, sc_ref, scp_ref, scn_ref, att_ref, dn_ref, mod_ref, cw_ref, wo_ref, n2_ref,
                wg_ref, wu_ref, wd_ref, o_ref, *, tiles_per_seq):
    t = pl.program_id(0) % tiles_per_seq
    sc = sc_ref[...]
    prod = sc[:, SC_DIM:2 * SC_DIM] * sc[:, 2 * SC_DIM:]
    pv = scp_ref[7:8, :]
    nx = scn_ref[0:1, :]
    prev_row = jnp.where(t > 0, pv[:, SC_DIM:2 * SC_DIM] * pv[:, 2 * SC_DIM:], 0.0)
    next_row = jnp.where(t < tiles_per_seq - 1, nx[:, SC_DIM:2 * SC_DIM] * nx[:, 2 * SC_DIM:], 0.0)
    row = lax.broadcasted_iota(jnp.int32, prod.shape, 0)
    p_prev = jnp.where(row == 0, prev_row, pltpu.roll(prod, 1, axis=0))
    p_next = jnp.where(row == TM - 1, next_row, pltpu.roll(prod, TM - 1, axis=0))
    y_sc = sc[:, :SC_DIM] * (p_prev * cw_ref[0:1, :] + prod * cw_ref[1:2, :] + p_next * cw_ref[2:3, :])
    y = (_dot(y_sc.astype(BF16), wo_ref[0:SC_DIM, :])
         + _dot(att_ref[...].astype(BF16), wo_ref[SC_DIM:SC_DIM + ATT_DIM, :])
         + _dot(dn_ref[...].astype(BF16), wo_ref[SC_DIM + ATT_DIM:, :]))
    x1 = x_ref[...] + mod_ref[2:3, :] * y
    ms = jnp.mean(x1 * x1, axis=-1, keepdims=True)
    h2 = (x1 * lax.rsqrt(ms + EPS) * n2_ref[...]) * (1.0 + mod_ref[4:5, :]) + mod_ref[3:4, :]
    hb = h2.astype(BF16)
    act = _silu(_dot(hb, wg_ref[...])) * _dot(hb, wu_ref[...])
    o_ref[...] = x1 + mod_ref[5:6, :] * _dot(act.astype(BF16), wd_ref[...])


def _output_stage(x, sc, att, dn, mod, sc_conv_w, w_out_b, norm2_g, w_gate_b, w_up_b, w_down_b, layer,
                  tiles_per_batch, tiles_per_seq):
    t = x.shape[0]
    halo = TM // 8
    last = t // 8 - 1
    resident = lambda shape: pl.BlockSpec((None,) + shape, lambda i: (layer, 0, 0), pipeline_mode=pl.Buffered(1))
    return pl.pallas_call(
        functools.partial(_out_kernel, tiles_per_seq=tiles_per_seq),
        out_shape=jax.ShapeDtypeStruct((t, D_MODEL), F32),
        grid=(t // TM,),
        in_specs=[pl.BlockSpec((TM, D_MODEL), lambda i: (i, 0)),
                  pl.BlockSpec((TM, 3 * SC_DIM), lambda i: (i, 0)),
                  pl.BlockSpec((8, 3 * SC_DIM), lambda i: (jnp.maximum(i * halo - 1, 0), 0)),
                  pl.BlockSpec((8, 3 * SC_DIM), lambda i: (jnp.minimum((i + 1) * halo, last), 0)),
                  pl.BlockSpec((TM, ATT_DIM), lambda i: (i, 0)),
                  pl.BlockSpec((TM, DN_DIM), lambda i: (i, 0)),
                  pl.BlockSpec((None, None, 6, D_MODEL), _mod_row_map(layer, tiles_per_batch)),
                  pl.BlockSpec((None, 3, SC_DIM), lambda i: (layer, 0, 0)),
                  resident((MIX_DIM, D_MODEL)),
                  pl.BlockSpec((None, 1, D_MODEL), lambda i: (layer, 0, 0)),
                  resident((D_MODEL, D_FF)), resident((D_MODEL, D_FF)), resident((D_FF, D_MODEL))],
        out_specs=pl.BlockSpec((TM, D_MODEL), lambda i: (i, 0)),
        compiler_params=pltpu.CompilerParams(vmem_limit_bytes=VMEM_LIMIT),
        name="out_stage",
    )(x, sc, sc, sc, att, dn, mod, sc_conv_w, w_out_b, norm2_g, w_gate_b, w_up_b, w_down_b)


def _rope_tables():
    pos = jnp.arange(DEC_SEQ)
    half = HEAD_DIM // 4
    inv = 1.0 / (ROPE_BASE ** (jnp.arange(half, dtype=F32) / half))
    ang_r = (pos // GRID_W).astype(F32)[:, None] * inv
    ang_c = (pos % GRID_W).astype(F32)[:, None] * inv
    cos = jnp.concatenate([jnp.cos(ang_r)] * 2 + [jnp.cos(ang_c)] * 2, axis=-1)
    sin = jnp.concatenate([-jnp.sin(ang_r), jnp.sin(ang_r), -jnp.sin(ang_c), jnp.sin(ang_c)], axis=-1)
    return jnp.tile(cos, (1, 2)), jnp.tile(sin, (1, 2))


def _pack_w_in(w_in):
    a0 = OFF_AB
    b0 = OFF_AB + 2 * N_DN_HEADS
    blocks = [w_in[:, :, :OFF_AB]]
    for p in range(N_PAIRS):
        idx = [a0 + d * N_DN_HEADS + 2 * p + j for d in range(2) for j in range(2)]
        idx += [b0 + d * N_DN_HEADS + 2 * p + j for d in range(2) for j in range(2)]
        blocks.append(jnp.take(w_in, jnp.asarray(idx), axis=2))
        blocks.append(jnp.zeros((DEPTH, D_MODEL, LANES - len(idx)), w_in.dtype))
    return jnp.concatenate(blocks, axis=2).astype(BF16)


def _pair_lanes(p):
    x = p.reshape(DEPTH, 2, N_PAIRS, 2).transpose(0, 2, 1, 3).reshape(DEPTH, N_PAIRS, 1, 4)
    return jnp.pad(x, ((0, 0), (0, 0), (0, 0), (0, LANES - 4)))


def kernel(x_prompt, x_sample, cache_k, cache_v, state_delta, c, c_ctx, w_in, w_out, ada_w, ada_b,
           norm1_g, norm2_g, sc_conv_w, dn_conv_w, q_norm_g, k_norm_g, attn_sink, dn_A_log,
           dn_dt_bias, dn_norm_g, w_gate, w_up, w_down):
    w_in_b = _pack_w_in(w_in)
    w_out_b, w_gate_b, w_up_b, w_down_b = (w.astype(BF16) for w in (w_out, w_gate, w_up, w_down))
    n1 = norm1_g.reshape(DEPTH, 1, D_MODEL)
    n2 = norm2_g.reshape(DEPTH, 1, D_MODEL)
    qg = jnp.tile(q_norm_g, (1, 2)).reshape(DEPTH, 1, LANES)
    kg = jnp.tile(k_norm_g, (1, 2)).reshape(DEPTH, 1, LANES)
    ng = jnp.tile(dn_norm_g, (1, 2)).reshape(DEPTH, 1, LANES)
    pa, pdt = _pair_lanes(dn_A_log), _pair_lanes(dn_dt_bias)
    cos_t, sin_t = _rope_tables()
    ck = cache_k.reshape(DEC_BATCH, DEPTH, PAST_LEN, KV_DIM)
    cv = cache_v.reshape(DEC_BATCH, DEPTH, PAST_LEN, KV_DIM)
    zero_state = jnp.zeros((BATCH, 1, 2, N_DN_HEADS, DN_HEAD_DIM, DN_HEAD_DIM), F32)

    cvecs = jnp.concatenate([c_ctx[None, :], c, jnp.zeros((MOD_ROWS - 1 - DEC_BATCH, D_MODEL), F32)], axis=0)
    mod = _modulation(cvecs, ada_w, ada_b).reshape(DEPTH, MOD_ROWS, 6, D_MODEL)

    xp = x_prompt.reshape(BATCH * SEQ, D_MODEL)
    xs = x_sample.reshape(DEC_BATCH * DEC_SEQ, D_MODEL)
    new_k, new_v, new_s = [], [], []
    for l in range(DEPTH):
        out_args = (mod, sc_conv_w, w_out_b, n2, w_gate_b, w_up_b, w_down_b, l)
        sc, q, kv, dq, dz, dab = _in_projection(xp, mod, n1, w_in_b, l, None)
        att, kn = _attention_ctx(q, kv, qg, kg, attn_sink, l)
        dn, s_out = _delta_net(dq, dz, dab, dn_conv_w, pa, pdt, ng, zero_state, l, 0, SEQ)
        xp = _output_stage(xp, sc, att, dn, *out_args, None, SEQ // TM)
        new_k.append(kn.reshape(BATCH, SEQ, N_KV_HEADS, HEAD_DIM))
        new_v.append(kv[:, KV_DIM:].reshape(BATCH, SEQ, N_KV_HEADS, HEAD_DIM))
        new_s.append(s_out)
        sc, q, kv, dq, dz, dab = _in_projection(xs, mod, n1, w_in_b, l, DEC_SEQ // TM)
        att = _attention_lat(q, kv, ck, cv, qg, kg, attn_sink, cos_t, sin_t, l)
        dn, _ = _delta_net(dq, dz, dab, dn_conv_w, pa, pdt, ng, state_delta, l, l, DEC_SEQ)
        xs = _output_stage(xs, sc, att, dn, *out_args, DEC_SEQ // TM, DEC_SEQ // TM)
    return (xp.reshape(BATCH, SEQ, D_MODEL), xs.reshape(DEC_BATCH, DEC_SEQ, D_MODEL),
            jnp.stack(new_k, axis=1), jnp.stack(new_v, axis=1), jnp.stack(new_s, axis=1))
```

```python
import functools

import numpy as np
import jax
import jax.numpy as jnp
from jax import lax
from jax.experimental import pallas as pl
from jax.experimental.pallas import tpu as pltpu

F32 = jnp.float32
BF16 = jnp.bfloat16

D_MODEL = 1024
BATCH = 16
SEQ = 256
DEPTH = 2
DEC_BATCH = 4
DEC_SEQ = 4096
PAST_LEN = 512
GRID_W = 64
SC_DIM = 256
N_Q_HEADS = 8
N_KV_HEADS = 2
N_GROUP = N_Q_HEADS // N_KV_HEADS
HEAD_DIM = 64
ATT_DIM = N_Q_HEADS * HEAD_DIM
KV_DIM = N_KV_HEADS * HEAD_DIM
WINDOW = 128
N_DN_HEADS = 4
DN_HEAD_DIM = 64
DN_DIM = N_DN_HEADS * DN_HEAD_DIM
CHUNK = 64
MIX_DIM = SC_DIM + ATT_DIM + DN_DIM
D_FF = -(-8 * D_MODEL // (3 * 256)) * 256
ROPE_BASE = 10000.0
EPS = 1e-6
NEG = -1e30

LANES = 128
V7X_VMEM_BYTES = 64 * 1024 * 1024
VMEM_LIMIT = (V7X_VMEM_BYTES * 3) // 4

OFF_SC, OFF_Q, OFF_KV, OFF_DQ, OFF_DZ, OFF_AB = 0, 768, 1280, 1536, 2304, 2560
IN_COLS = OFF_AB + 2 * LANES
N_PAIRS = N_DN_HEADS // 2
MOD_ROWS = 8
TM = 256


def _sigmoid(x):
    return 1.0 / (1.0 + jnp.exp(-x))


def _silu(x):
    return x * _sigmoid(x)


def _softplus(x):
    return jnp.maximum(x, 0.0) + jnp.log1p(jnp.exp(-jnp.abs(x)))


def _dot(a, b):
    return jnp.dot(a, b, preferred_element_type=F32)


def _dot_nt(a, b):
    return lax.dot_general(a, b, (((1,), (1,)), ((), ())), preferred_element_type=F32)


def _dot_tn(a, b):
    return lax.dot_general(a, b, (((0,), (0,)), ((), ())), preferred_element_type=F32)


def _lo_mask(shape):
    return lax.broadcasted_iota(jnp.int32, shape, 1) % LANES < HEAD_DIM


def _half_sums(xx, lo):
    s_lo = jnp.sum(jnp.where(lo, xx, 0.0), axis=-1, keepdims=True)
    s_hi = jnp.sum(jnp.where(lo, 0.0, xx), axis=-1, keepdims=True)
    return jnp.where(lo, s_lo, s_hi)


def _head_rms(x, g_row):
    lo = _lo_mask(x.shape)
    ms = _half_sums(x * x, lo) * (1.0 / HEAD_DIM)
    return x * lax.rsqrt(ms + EPS) * g_row


def _head_l2n(x):
    lo = _lo_mask(x.shape)
    return x * lax.rsqrt(_half_sums(x * x, lo) + EPS)


def _rope(x, cos, sin):
    n = x.shape[1]
    lane = lax.broadcasted_iota(jnp.int32, x.shape, 1)
    swapped = jnp.where((lane & 16) == 0, pltpu.roll(x, n - 16, axis=1), pltpu.roll(x, 16, axis=1))
    return x * cos + swapped * sin


def _dup_half(x, half):
    lo = _lo_mask(x.shape)
    r = pltpu.roll(x, HEAD_DIM, axis=1)
    return jnp.where(lo, x, r) if half == 0 else jnp.where(lo, r, x)


def _mod_kernel(c_ref, w_ref, b_ref, o_ref):
    s = _silu(c_ref[...])
    o_ref[...] = _dot(s.astype(BF16), w_ref[...].astype(BF16)) + b_ref[...]


def _modulation(cvecs, ada_w, ada_b):
    nblk = 6
    return pl.pallas_call(
        _mod_kernel,
        out_shape=jax.ShapeDtypeStruct((DEPTH, MOD_ROWS, 6 * D_MODEL), F32),
        grid=(DEPTH, nblk),
        in_specs=[pl.BlockSpec((MOD_ROWS, D_MODEL), lambda l, j: (0, 0)),
                  pl.BlockSpec((None, D_MODEL, D_MODEL), lambda l, j: (l, 0, j)),
                  pl.BlockSpec((None, 1, D_MODEL), lambda l, j: (l, 0, j))],
        out_specs=pl.BlockSpec((None, MOD_ROWS, D_MODEL), lambda l, j: (l, 0, j)),
        compiler_params=pltpu.CompilerParams(vmem_limit_bytes=VMEM_LIMIT),
        name="adaln_mod",
    )(cvecs, ada_w, ada_b.reshape(DEPTH, 1, 6 * D_MODEL))


def _mod_row_map(layer, tiles_per_batch):
    if tiles_per_batch is None:
        return lambda i: (layer, 0, 0, 0)
    return lambda i: (layer, 1 + i // tiles_per_batch, 0, 0)


def _in_kernel(x_ref, mod_ref, g_ref, w_ref, sc_ref, q_ref, kv_ref, dq_ref, dz_ref, ab_ref):
    x = x_ref[...]
    ms = jnp.mean(x * x, axis=-1, keepdims=True)
    h = (x * lax.rsqrt(ms + EPS) * g_ref[...]) * (1.0 + mod_ref[1:2, :]) + mod_ref[0:1, :]
    hb = h.astype(BF16)
    for ref, a, b in ((sc_ref, OFF_SC, OFF_Q), (q_ref, OFF_Q, OFF_KV), (kv_ref, OFF_KV, OFF_DQ),
                      (dq_ref, OFF_DQ, OFF_DZ), (dz_ref, OFF_DZ, OFF_AB), (ab_ref, OFF_AB, IN_COLS)):
        ref[...] = _dot(hb, w_ref[:, a:b])


def _in_projection(x, mod, norm1_g, w_in_b, layer, tiles_per_batch):
    t = x.shape[0]
    widths = (OFF_Q - OFF_SC, OFF_KV - OFF_Q, OFF_DQ - OFF_KV, OFF_DZ - OFF_DQ, OFF_AB - OFF_DZ, IN_COLS - OFF_AB)
    return pl.pallas_call(
        _in_kernel,
        out_shape=[jax.ShapeDtypeStruct((t, w), F32) for w in widths],
        grid=(t // TM,),
        in_specs=[pl.BlockSpec((TM, D_MODEL), lambda i: (i, 0)),
                  pl.BlockSpec((None, None, 6, D_MODEL), _mod_row_map(layer, tiles_per_batch)),
                  pl.BlockSpec((None, 1, D_MODEL), lambda i: (layer, 0, 0)),
                  pl.BlockSpec((None, D_MODEL, IN_COLS), lambda i: (layer, 0, 0),
                               pipeline_mode=pl.Buffered(1))],
        out_specs=[pl.BlockSpec((TM, w), lambda i: (i, 0)) for w in widths],
        compiler_params=pltpu.CompilerParams(vmem_limit_bytes=VMEM_LIMIT),
        name="in_proj",
    )(x, mod, norm1_g, w_in_b)


def _prep_q(q, qg, cos, sin):
    out = []
    for g in range(ATT_DIM // LANES):
        x = _head_rms(q[:, g * LANES:(g + 1) * LANES], qg)
        if cos is not None:
            x = _rope(x, cos, sin)
        out.append(x * (HEAD_DIM ** -0.5))
    return out


def _stack_group(qa, qb):
    lo = _lo_mask(qa.shape)
    parts = [jnp.where(lo, qa, 0.0), jnp.where(lo, 0.0, qa), jnp.where(lo, qb, 0.0), jnp.where(lo, 0.0, qb)]
    return jnp.concatenate(parts, axis=0).astype(BF16)


def _sink_column(sink_ref, layer, kvh, rows):
    return jnp.concatenate([jnp.full((rows, 1), sink_ref[layer, kvh * N_GROUP + g], F32)
                            for g in range(N_GROUP)], axis=0)


def _unstack_group(o4, rows):
    lo = _lo_mask((rows, LANES))
    return (jnp.where(lo, o4[0:rows], o4[rows:2 * rows]),
            jnp.where(lo, o4[2 * rows:3 * rows], o4[3 * rows:4 * rows]))


def _attn_ctx_kernel(sink_ref, q_ref, kv_ref, qg_ref, kg_ref, o_ref, kn_ref, *, layer):
    kv = kv_ref[...]
    kn = _head_rms(kv[:, :LANES], kg_ref[...])
    kn_ref[...] = kn
    v2 = kv[:, LANES:]
    qs = _prep_q(q_ref[...], qg_ref[...], None, None)
    for kvh in range(N_KV_HEADS):
        kd = _dup_half(kn, kvh).astype(BF16)
        vd = _dup_half(v2, kvh).astype(BF16)
        q4 = _stack_group(qs[2 * kvh], qs[2 * kvh + 1])
        s = _dot_nt(q4, kd)
        sink = _sink_column(sink_ref, layer, kvh, SEQ)
        m = jnp.maximum(jnp.max(s, axis=-1, keepdims=True), sink)
        e = jnp.exp(s - m)
        den = jnp.sum(e, axis=-1, keepdims=True) + jnp.exp(sink - m)
        o4 = _dot(e.astype(BF16), vd) / den
        oa, ob = _unstack_group(o4, SEQ)
        o_ref[:, (2 * kvh) * LANES:(2 * kvh + 1) * LANES] = oa
        o_ref[:, (2 * kvh + 1) * LANES:(2 * kvh + 2) * LANES] = ob


def _attention_ctx(q, kv, q_norm_g, k_norm_g, attn_sink, layer):
    t = q.shape[0]
    return pl.pallas_call(
        functools.partial(_attn_ctx_kernel, layer=layer),
        out_shape=[jax.ShapeDtypeStruct((t, ATT_DIM), F32), jax.ShapeDtypeStruct((t, KV_DIM), F32)],
        grid=(t // SEQ,),
        in_specs=[pl.BlockSpec(memory_space=pltpu.SMEM),
                  pl.BlockSpec((SEQ, ATT_DIM), lambda b: (b, 0)),
                  pl.BlockSpec((SEQ, 2 * KV_DIM), lambda b: (b, 0)),
                  pl.BlockSpec((None, 1, LANES), lambda b: (layer, 0, 0)),
                  pl.BlockSpec((None, 1, LANES), lambda b: (layer, 0, 0))],
        out_specs=[pl.BlockSpec((SEQ, ATT_DIM), lambda b: (b, 0)),
                   pl.BlockSpec((SEQ, KV_DIM), lambda b: (b, 0))],
        compiler_params=pltpu.CompilerParams(vmem_limit_bytes=VMEM_LIMIT),
        name="attn_ctx",
    )(attn_sink, q, kv, q_norm_g, k_norm_g)


QBLK = 128
WIN_KEYS = QBLK + 2 * WINDOW
PREP_ROWS = 512


def _attn_lat_kernel(sink_ref, q_ref, kv_ref, kc_ref, vc_ref, qg_ref, kg_ref, cos_ref, sin_ref, o_ref,
                     kd_s, vd_s, kcd_s, vcd_s, *, layer):
    i = pl.program_id(1)
    n = DEC_SEQ

    @pl.when(i == 0)
    def _():
        zpad = jnp.zeros((WINDOW, LANES), BF16)
        for kvh in range(N_KV_HEADS):
            for s in (kd_s, vd_s):
                s[kvh, 0:WINDOW, :] = zpad
                s[kvh, WINDOW + n:2 * WINDOW + n, :] = zpad
            kcd_s[kvh] = _dup_half(kc_ref[...], kvh).astype(BF16)
            vcd_s[kvh] = _dup_half(vc_ref[...], kvh).astype(BF16)

        def prep(r, carry):
            r0 = pl.multiple_of(r * PREP_ROWS, PREP_ROWS)
            kv = kv_ref[pl.ds(r0, PREP_ROWS), :]
            kn = _head_rms(kv[:, :LANES], kg_ref[...])
            kr = _rope(kn, cos_ref[pl.ds(r0, PREP_ROWS), :], sin_ref[pl.ds(r0, PREP_ROWS), :])
            v2 = kv[:, LANES:]
            for kvh in range(N_KV_HEADS):
                kd_s[kvh, pl.ds(WINDOW + r0, PREP_ROWS), :] = _dup_half(kr, kvh).astype(BF16)
                vd_s[kvh, pl.ds(WINDOW + r0, PREP_ROWS), :] = _dup_half(v2, kvh).astype(BF16)
            return carry

        lax.fori_loop(0, n // PREP_ROWS, prep, 0)

    start = pl.multiple_of(i * QBLK, QBLK)
    qs = _prep_q(q_ref[...], qg_ref[...], cos_ref[pl.ds(start, QBLK), :], sin_ref[pl.ds(start, QBLK), :])
    rows = N_GROUP * QBLK
    rr = lax.broadcasted_iota(jnp.int32, (rows, WIN_KEYS), 0) % QBLK
    jj = lax.broadcasted_iota(jnp.int32, (rows, WIN_KEYS), 1)
    kpos = jj + (start - WINDOW)
    valid = (jj >= rr) & (jj <= rr + 2 * WINDOW) & (kpos >= 0) & (kpos < n)
    for kvh in range(N_KV_HEADS):
        q4 = _stack_group(qs[2 * kvh], qs[2 * kvh + 1])
        kw = kd_s[kvh, pl.ds(start, WIN_KEYS), :]
        vw = vd_s[kvh, pl.ds(start, WIN_KEYS), :]
        s_c = _dot_nt(q4, kcd_s[kvh])
        s_w = jnp.where(valid, _dot_nt(q4, kw), NEG)
        sink = _sink_column(sink_ref, layer, kvh, QBLK)
        m = jnp.maximum(jnp.maximum(jnp.max(s_c, axis=-1, keepdims=True),
                                    jnp.max(s_w, axis=-1, keepdims=True)), sink)
        e_c = jnp.exp(s_c - m)
        e_w = jnp.exp(s_w - m)
        den = (jnp.sum(e_c, axis=-1, keepdims=True) + jnp.sum(e_w, axis=-1, keepdims=True)
               + jnp.exp(sink - m))
        o4 = (_dot(e_c.astype(BF16), vcd_s[kvh]) + _dot(e_w.astype(BF16), vw)) / den
        oa, ob = _unstack_group(o4, QBLK)
        o_ref[:, (2 * kvh) * LANES:(2 * kvh + 1) * LANES] = oa
        o_ref[:, (2 * kvh + 1) * LANES:(2 * kvh + 2) * LANES] = ob


def _attention_lat(q, kv, cache_k, cache_v, q_norm_g, k_norm_g, attn_sink, cos_t, sin_t, layer):
    t = q.shape[0]
    n = DEC_SEQ
    nblk = n // QBLK
    return pl.pallas_call(
        functools.partial(_attn_lat_kernel, layer=layer),
        out_shape=jax.ShapeDtypeStruct((t, ATT_DIM), F32),
        grid=(DEC_BATCH, nblk),
        in_specs=[pl.BlockSpec(memory_space=pltpu.SMEM),
                  pl.BlockSpec((QBLK, ATT_DIM), lambda b, i: (b * nblk + i, 0)),
                  pl.BlockSpec((n, 2 * KV_DIM), lambda b, i: (b, 0)),
                  pl.BlockSpec((None, None, PAST_LEN, KV_DIM), lambda b, i: (b, layer, 0, 0)),
                  pl.BlockSpec((None, None, PAST_LEN, KV_DIM), lambda b, i: (b, layer, 0, 0)),
                  pl.BlockSpec((None, 1, LANES), lambda b, i: (layer, 0, 0)),
                  pl.BlockSpec((None, 1, LANES), lambda b, i: (layer, 0, 0)),
                  pl.BlockSpec((n, LANES), lambda b, i: (0, 0)),
                  pl.BlockSpec((n, LANES), lambda b, i: (0, 0))],
        out_specs=pl.BlockSpec((QBLK, ATT_DIM), lambda b, i: (b * nblk + i, 0)),
        scratch_shapes=[pltpu.VMEM((N_KV_HEADS, n + 2 * WINDOW, LANES), BF16),
                        pltpu.VMEM((N_KV_HEADS, n + 2 * WINDOW, LANES), BF16),
                        pltpu.VMEM((N_KV_HEADS, PAST_LEN, LANES), BF16),
                        pltpu.VMEM((N_KV_HEADS, PAST_LEN, LANES), BF16)],
        compiler_params=pltpu.CompilerParams(vmem_limit_bytes=VMEM_LIMIT),
        name="attn_lat",
    )(attn_sink, q, kv, cache_k, cache_v, q_norm_g, k_norm_g, cos_t, sin_t)


DN_ROWS = 128


def _dn_conv(x_ref, w_ref, r0, n):
    x = x_ref[pl.ds(r0, DN_ROWS), :]
    above = jnp.where(r0 > 0, x_ref[pl.ds(jnp.maximum(r0 - 1, 0), 1), :], 0.0)
    below = jnp.where(r0 + DN_ROWS < n, x_ref[pl.ds(jnp.minimum(r0 + DN_ROWS, n - 1), 1), :], 0.0)
    row = lax.broadcasted_iota(jnp.int32, x.shape, 0)
    prev = jnp.where(row == 0, above, pltpu.roll(x, 1, axis=0))
    nxt = jnp.where(row == DN_ROWS - 1, below, pltpu.roll(x, DN_ROWS - 1, axis=0))
    return _silu(prev * w_ref[0:1, :] + x * w_ref[1:2, :] + nxt * w_ref[2:3, :])


N_UNITS = 4
UW = N_UNITS * CHUNK
N_LEVELS = 6
M_INCL, M_STRICT, M_EYE = N_LEVELS, N_LEVELS + 1, N_LEVELS + 2
PREP_BLOCK = 8


def _split3(x):
    hi = x.astype(BF16)
    r = x - hi.astype(F32)
    mid = r.astype(BF16)
    lo = (r - mid.astype(F32)).astype(BF16)
    return hi, mid, lo


def _dn_kernel(q_ref, k_ref, v_ref, z_ref, ab_ref, wq_ref, wk_ref, wv_ref, pa_ref, pdt_ref, ng_ref, s0_ref,
               o_ref, sout_ref, qs, ks, vs, gbs, pfs, u_s, w_s, a_s, qe_s, kd_s, el_s, st, mc, bd_s, bdb_s, *, n):
    nc = n // CHUNK
    pblk = min(PREP_BLOCK, nc)
    def prep_rows(b, carry):
        r0 = pl.multiple_of(b * DN_ROWS, DN_ROWS)
        rows = pl.ds(r0, DN_ROWS)
        qs[rows, :] = _head_l2n(_dn_conv(q_ref, wq_ref, r0, n)) * (DN_HEAD_DIM ** -0.5)
        ks[rows, :] = _head_l2n(_dn_conv(k_ref, wk_ref, r0, n))
        vs[rows, :] = _dn_conv(v_ref, wv_ref, r0, n)
        ab = ab_ref[rows, :]
        lane = lax.broadcasted_iota(jnp.int32, ab.shape, 1)
        gb = jnp.where(lane < N_UNITS, -jnp.exp(pa_ref[...]) * _softplus(ab + pdt_ref[...]), _sigmoid(ab))
        gbs[rows, :] = gb
        row_in_chunk = lax.broadcasted_iota(jnp.int32, ab.shape, 0) % CHUNK
        pref = gb
        for sh in (1, 2, 4, 8, 16, 32):
            pref = pref + jnp.where(row_in_chunk >= sh, pltpu.roll(pref, sh, axis=0), 0.0)
        pfs[rows, :] = pref
        o_ref[rows, :] = jnp.zeros((DN_ROWS, LANES), F32)
        return carry

    lax.fori_loop(0, n // DN_ROWS, prep_rows, 0)

    as_f32 = lambda m: jnp.where(m, 1.0, 0.0).astype(F32)
    ii = lax.broadcasted_iota(jnp.int32, (CHUNK, UW), 0)
    ll = lax.broadcasted_iota(jnp.int32, (CHUNK, UW), 1)
    unit_of_lane, jj = ll // CHUNK, ll % CHUNK
    fwd, bwd = unit_of_lane < 2, unit_of_lane >= 2
    for k in range(N_LEVELS):
        rk, ck = ii >> k, jj >> k
        lower = (rk - ck == 1) & ((rk & 1) == 1)
        upper = (ck - rk == 1) & ((ck & 1) == 1)
        mc[k] = as_f32((fwd & lower) | (bwd & upper))
    mc[M_INCL] = as_f32((fwd & (ii >= jj)) | (bwd & (ii <= jj)))
    mc[M_STRICT] = as_f32((fwd & (ii > jj)) | (bwd & (ii < jj)))
    mc[M_EYE] = as_f32(ii == jj)
    rr = lax.broadcasted_iota(jnp.int32, (UW, UW), 0)
    cc = lax.broadcasted_iota(jnp.int32, (UW, UW), 1)
    bd = as_f32((rr // CHUNK) == (cc // CHUNK))
    bd_s[...] = bd
    bdb_s[...] = bd.astype(BF16)

    zero = jnp.zeros((CHUNK, CHUNK), F32)
    st[...] = jnp.concatenate(
        [jnp.concatenate([s0_ref[u // 2, u % 2] if v == u else zero for v in range(N_UNITS)], axis=1)
         for u in range(N_UNITS)], axis=0)

    lo = _lo_mask((CHUNK, LANES))
    lane_c = lax.broadcasted_iota(jnp.int32, (CHUNK, LANES), 1)

    def lane_pack(m, off):
        cols = [jnp.broadcast_to(m[:, off + u:off + u + 1], (CHUNK, UW)) for u in range(N_UNITS)]
        return jnp.where(unit_of_lane == 0, cols[0],
                         jnp.where(unit_of_lane == 1, cols[1], jnp.where(unit_of_lane == 2, cols[2], cols[3])))

    def block_diag(xb):
        return jnp.concatenate([xb] * N_UNITS, axis=0) * bdb_s[...]

    dup = lambda x: jnp.concatenate([x, x], axis=1)
    split_heads = lambda x: jnp.concatenate([jnp.where(lo, x, 0.0), jnp.where(lo, 0.0, x)], axis=0).astype(BF16)

    def head_blocks(m):
        return dup(jnp.where(lo, m[0:CHUNK], m[CHUNK:]))

    def chunk_prep(blk, carry):
        cs = [blk * pblk + s for s in range(pblk)]
        rows = [pl.ds(pl.multiple_of(c * CHUNK, CHUNK), CHUNK) for c in cs]
        each = lambda f, *lists: [f(*xs) for xs in zip(*lists)]

        gx = [gbs[r, :] for r in rows]
        pc = [pfs[r, :] for r in rows]
        tot = [p[CHUNK - 1:CHUNK, :] for p in pc]
        gcm = each(lambda g, p, t: jnp.where(lane_c < 2, p, t - p + g), gx, pc, tot)
        rsm = each(lambda g, p, t: jnp.where(lane_c < 2, t - p, p - g), gx, pc, tot)
        e_gc = each(lambda m: jnp.exp(lane_pack(m, 0)), gcm)
        e_rest = each(lambda m: jnp.exp(lane_pack(m, 0)), rsm)
        b4 = each(lambda g: lane_pack(g, N_UNITS), gx)
        pieces = each(lambda g: _split3(lane_pack(g, 0) * mc[M_STRICT]), gx)
        cum = mc[M_INCL].astype(BF16)
        diff = each(lambda ps: _dot(cum, block_diag(ps[0])) + _dot(cum, block_diag(ps[1]))
                    + _dot(cum, block_diag(ps[2])), pieces)
        dec = each(lambda d: jnp.exp(d) * mc[M_INCL], diff)
        k2 = [split_heads(ks[r, :]) for r in rows]
        q2 = [split_heads(qs[r, :]) for r in rows]
        gram = each(lambda k: head_blocks(_dot_nt(k, k)), k2)
        qk = each(lambda q, k: head_blocks(_dot_nt(q, k)), q2, k2)
        lmat = each(lambda g, b, d: g * b * d * mc[M_STRICT], gram, b4, dec)
        for r, m, d in zip(rows, qk, dec):
            a_s[r, :] = (m * d).astype(BF16)
        t = each(lambda l: mc[M_EYE] - l * mc[0], lmat)
        for k in range(1, N_LEVELS):
            lb = each(lambda l: block_diag((l * mc[k]).astype(BF16)), lmat)
            tb = each(lambda x: x.astype(BF16), t)
            x = each(lambda a, b: _dot(a, b).astype(BF16), tb, lb)
            t = each(lambda t0, a, b: t0 - _dot(a, block_diag(b)), t, x, tb)
        tm = each(lambda x: (x - mc[M_EYE]).astype(BF16), t)
        for i, (c, r) in enumerate(zip(cs, rows)):
            k4 = dup(ks[r, :])
            ru = dup(vs[r, :]) * b4[i]
            rw = k4 * b4[i] * e_gc[i]
            u_s[r, :] = ru + _dot(tm[i], block_diag(ru.astype(BF16)))
            w_s[r, :] = (rw + _dot(tm[i], block_diag(rw.astype(BF16)))).astype(BF16)
            qe_s[r, :] = (dup(qs[r, :]) * e_gc[i]).astype(BF16)
            kd_s[r, :] = (k4 * e_rest[i]).astype(BF16)
            el = jnp.concatenate([e_gc[i][CHUNK - 1:CHUNK, :LANES], e_gc[i][0:1, LANES:]], axis=1)
            el_s[c] = jnp.broadcast_to(el, (8, UW))
        return carry

    lax.fori_loop(0, nc // pblk, chunk_prep, 0)

    def scan_step(i, carry):
        cf, cb = i, nc - 1 - i
        rf = pl.ds(pl.multiple_of(cf * CHUNK, CHUNK), CHUNK)
        rb = pl.ds(pl.multiple_of(cb * CHUNK, CHUNK), CHUNK)
        mix = lambda ref: jnp.concatenate([ref[rf, :LANES], ref[rb, LANES:]], axis=1)
        el = jnp.concatenate([el_s[cf][0:1, :LANES], el_s[cb][0:1, LANES:]], axis=1)
        s = st[...]
        sb = s.astype(BF16)
        v_new = (mix(u_s) - _dot(mix(w_s), sb)).astype(BF16)
        o4 = _dot(mix(qe_s), sb) + _dot(mix(a_s), block_diag(v_new))
        st[...] = s * el + _dot_tn(mix(kd_s), v_new) * bd_s[...]
        o_ref[rf, :] += o4[:, :LANES]
        o_ref[rb, :] += o4[:, LANES:]
        return carry

    lax.fori_loop(0, nc, scan_step, 0)

    s = st[...]
    for u in range(N_UNITS):
        sout_ref[u // 2, u % 2] = s[u * CHUNK:(u + 1) * CHUNK, u * CHUNK:(u + 1) * CHUNK]

    def finish_rows(b, carry):
        rows = pl.ds(pl.multiple_of(b * DN_ROWS, DN_ROWS), DN_ROWS)
        o_ref[rows, :] = _head_rms(o_ref[rows, :], ng_ref[...]) * _silu(z_ref[rows, :])
        return carry

    lax.fori_loop(0, n // DN_ROWS, finish_rows, 0)


def _delta_net(dq, dz, dab, dn_conv_w, pa, pdt, dn_norm_g, s0, layer, s0_layer, n):
    t = dq.shape[0]
    bsz = t // n
    nc = n // CHUNK
    conv_spec = lambda off: pl.BlockSpec((None, 3, LANES), lambda b, p: (layer, 0, off + p))
    col_spec = lambda off: pl.BlockSpec((n, LANES), lambda b, p: (b, off + p), pipeline_mode=pl.Buffered(1))
    return pl.pallas_call(
        functools.partial(_dn_kernel, n=n),
        out_shape=[jax.ShapeDtypeStruct((t, DN_DIM), F32),
                   jax.ShapeDtypeStruct((bsz, 2, N_DN_HEADS, DN_HEAD_DIM, DN_HEAD_DIM), F32)],
        grid=(bsz, N_PAIRS),
        in_specs=[col_spec(0), col_spec(N_PAIRS), col_spec(2 * N_PAIRS), col_spec(0), col_spec(0),
                  conv_spec(0), conv_spec(N_PAIRS), conv_spec(2 * N_PAIRS),
                  pl.BlockSpec((None, None, 1, LANES), lambda b, p: (layer, p, 0, 0)),
                  pl.BlockSpec((None, None, 1, LANES), lambda b, p: (layer, p, 0, 0)),
                  pl.BlockSpec((None, 1, LANES), lambda b, p: (layer, 0, 0)),
                  pl.BlockSpec((None, None, 2, 2, DN_HEAD_DIM, DN_HEAD_DIM),
                               lambda b, p: (b, s0_layer, 0, p, 0, 0))],
        out_specs=[pl.BlockSpec((n, LANES), lambda b, p: (b, p)),
                   pl.BlockSpec((None, 2, 2, DN_HEAD_DIM, DN_HEAD_DIM), lambda b, p: (b, 0, p, 0, 0))],
        scratch_shapes=[pltpu.VMEM((n, LANES), F32) for _ in range(5)]
                       + [pltpu.VMEM((n, UW), F32)]
                       + [pltpu.VMEM((n, UW), BF16) for _ in range(4)]
                       + [pltpu.VMEM((nc, 8, UW), F32),
                          pltpu.VMEM((UW, UW), F32),
                          pltpu.VMEM((N_LEVELS + 3, CHUNK, UW), F32),
                          pltpu.VMEM((UW, UW), F32),
                          pltpu.VMEM((UW, UW), BF16)],
        compiler_params=pltpu.CompilerParams(vmem_limit_bytes=(V7X_VMEM_BYTES * 7) // 8),
        name="delta_net",
    )(dq, dq, dq, dz, dab, dn_conv_w, dn_conv_w, dn_conv_w, pa, pdt, dn_norm_g, s0)


def _out_kernel(x_ref, sc_ref, scp_ref, scn_ref, att_ref, dn_ref, mod_ref, cw_ref, wo_ref, n2_ref,
                wg_ref, wu_ref, wd_ref, o_ref, *, tiles_per_seq):
    t = pl.program_id(0) % tiles_per_seq
    sc = sc_ref[...]
    prod = sc[:, SC_DIM:2 * SC_DIM] * sc[:, 2 * SC_DIM:]
    pv = scp_ref[7:8, :]
    nx = scn_ref[0:1, :]
    prev_row = jnp.where(t > 0, pv[:, SC_DIM:2 * SC_DIM] * pv[:, 2 * SC_DIM:], 0.0)
    next_row = jnp.where(t < tiles_per_seq - 1, nx[:, SC_DIM:2 * SC_DIM] * nx[:, 2 * SC_DIM:], 0.0)
    row = lax.broadcasted_iota(jnp.int32, prod.shape, 0)
    p_prev = jnp.where(row == 0, prev_row, pltpu.roll(prod, 1, axis=0))
    p_next = jnp.where(row == TM - 1, next_row, pltpu.roll(prod, TM - 1, axis=0))
    y_sc = sc[:, :SC_DIM] * (p_prev * cw_ref[0:1, :] + prod * cw_ref[1:2, :] + p_next * cw_ref[2:3, :])
    y = (_dot(y_sc.astype(BF16), wo_ref[0:SC_DIM, :])
         + _dot(att_ref[...].astype(BF16), wo_ref[SC_DIM:SC_DIM + ATT_DIM, :])
         + _dot(dn_ref[...].astype(BF16), wo_ref[SC_DIM + ATT_DIM:, :]))
    x1 = x_ref[...] + mod_ref[2:3, :] * y
    ms = jnp.mean(x1 * x1, axis=-1, keepdims=True)
    h2 = (x1 * lax.rsqrt(ms + EPS) * n2_ref[...]) * (1.0 + mod_ref[4:5, :]) + mod_ref[3:4, :]
    hb = h2.astype(BF16)
    act = _silu(_dot(hb, wg_ref[...])) * _dot(hb, wu_ref[...])
    o_ref[...] = x1 + mod_ref[5:6, :] * _dot(act.astype(BF16), wd_ref[...])


def _output_stage(x, sc, att, dn, mod, sc_conv_w, w_out_b, norm2_g, w_gate_b, w_up_b, w_down_b, layer,
                  tiles_per_batch, tiles_per_seq):
    t = x.shape[0]
    halo = TM // 8
    last = t // 8 - 1
    resident = lambda shape: pl.BlockSpec((None,) + shape, lambda i: (layer, 0, 0), pipeline_mode=pl.Buffered(1))
    return pl.pallas_call(
        functools.partial(_out_kernel, tiles_per_seq=tiles_per_seq),
        out_shape=jax.ShapeDtypeStruct((t, D_MODEL), F32),
        grid=(t // TM,),
        in_specs=[pl.BlockSpec((TM, D_MODEL), lambda i: (i, 0)),
                  pl.BlockSpec((TM, 3 * SC_DIM), lambda i: (i, 0)),
                  pl.BlockSpec((8, 3 * SC_DIM), lambda i: (jnp.maximum(i * halo - 1, 0), 0)),
                  pl.BlockSpec((8, 3 * SC_DIM), lambda i: (jnp.minimum((i + 1) * halo, last), 0)),
                  pl.BlockSpec((TM, ATT_DIM), lambda i: (i, 0)),
                  pl.BlockSpec((TM, DN_DIM), lambda i: (i, 0)),
                  pl.BlockSpec((None, None, 6, D_MODEL), _mod_row_map(layer, tiles_per_batch)),
                  pl.BlockSpec((None, 3, SC_DIM), lambda i: (layer, 0, 0)),
                  resident((MIX_DIM, D_MODEL)),
                  pl.BlockSpec((None, 1, D_MODEL), lambda i: (layer, 0, 0)),
                  resident((D_MODEL, D_FF)), resident((D_MODEL, D_FF)), resident((D_FF, D_MODEL))],
        out_specs=pl.BlockSpec((TM, D_MODEL), lambda i: (i, 0)),
        compiler_params=pltpu.CompilerParams(vmem_limit_bytes=VMEM_LIMIT),
        name="out_stage",
    )(x, sc, sc, sc, att, dn, mod, sc_conv_w, w_out_b, norm2_g, w_gate_b, w_up_b, w_down_b)


def _rope_tables():
    pos = jnp.arange(DEC_SEQ)
    half = HEAD_DIM // 4
    inv = 1.0 / (ROPE_BASE ** (jnp.arange(half, dtype=F32) / half))
    ang_r = (pos // GRID_W).astype(F32)[:, None] * inv
    ang_c = (pos % GRID_W).astype(F32)[:, None] * inv
    cos = jnp.concatenate([jnp.cos(ang_r)] * 2 + [jnp.cos(ang_c)] * 2, axis=-1)
    sin = jnp.concatenate([-jnp.sin(ang_r), jnp.sin(ang_r), -jnp.sin(ang_c), jnp.sin(ang_c)], axis=-1)
    return jnp.tile(cos, (1, 2)), jnp.tile(sin, (1, 2))


def _pack_w_in(w_in):
    a0 = OFF_AB
    b0 = OFF_AB + 2 * N_DN_HEADS
    blocks = [w_in[:, :, :OFF_AB]]
    for p in range(N_PAIRS):
        idx = [a0 + d * N_DN_HEADS + 2 * p + j for d in range(2) for j in range(2)]
        idx += [b0 + d * N_DN_HEADS + 2 * p + j for d in range(2) for j in range(2)]
        blocks.append(jnp.take(w_in, jnp.asarray(idx), axis=2))
        blocks.append(jnp.zeros((DEPTH, D_MODEL, LANES - len(idx)), w_in.dtype))
    return jnp.concatenate(blocks, axis=2).astype(BF16)


def _pair_lanes(p):
    x = p.reshape(DEPTH, 2, N_PAIRS, 2).transpose(0, 2, 1, 3).reshape(DEPTH, N_PAIRS, 1, 4)
    return jnp.pad(x, ((0, 0), (0, 0), (0, 0), (0, LANES - 4)))


def kernel(x_prompt, x_sample, cache_k, cache_v, state_delta, c, c_ctx, w_in, w_out, ada_w, ada_b,
           norm1_g, norm2_g, sc_conv_w, dn_conv_w, q_norm_g, k_norm_g, attn_sink, dn_A_log,
           dn_dt_bias, dn_norm_g, w_gate, w_up, w_down):
    w_in_b = _pack_w_in(w_in)
    w_out_b, w_gate_b, w_up_b, w_down_b = (w.astype(BF16) for w in (w_out, w_gate, w_up, w_down))
    n1 = norm1_g.reshape(DEPTH, 1, D_MODEL)
    n2 = norm2_g.reshape(DEPTH, 1, D_MODEL)
    qg = jnp.tile(q_norm_g, (1, 2)).reshape(DEPTH, 1, LANES)
    kg = jnp.tile(k_norm_g, (1, 2)).reshape(DEPTH, 1, LANES)
    ng = jnp.tile(dn_norm_g, (1, 2)).reshape(DEPTH, 1, LANES)
    pa, pdt = _pair_lanes(dn_A_log), _pair_lanes(dn_dt_bias)
    cos_t, sin_t = _rope_tables()
    ck = cache_k.reshape(DEC_BATCH, DEPTH, PAST_LEN, KV_DIM)
    cv = cache_v.reshape(DEC_BATCH, DEPTH, PAST_LEN, KV_DIM)
    zero_state = jnp.zeros((BATCH, 1, 2, N_DN_HEADS, DN_HEAD_DIM, DN_HEAD_DIM), F32)

    cvecs = jnp.concatenate([c_ctx[None, :], c, jnp.zeros((MOD_ROWS - 1 - DEC_BATCH, D_MODEL), F32)], axis=0)
    mod = _modulation(cvecs, ada_w, ada_b).reshape(DEPTH, MOD_ROWS, 6, D_MODEL)

    xp = x_prompt.reshape(BATCH * SEQ, D_MODEL)
    xs = x_sample.reshape(DEC_BATCH * DEC_SEQ, D_MODEL)
    new_k, new_v, new_s = [], [], []
    for l in range(DEPTH):
        out_args = (mod, sc_conv_w, w_out_b, n2, w_gate_b, w_up_b, w_down_b, l)
        sc, q, kv, dq, dz, dab = _in_projection(xp, mod, n1, w_in_b, l, None)
        att, kn = _attention_ctx(q, kv, qg, kg, attn_sink, l)
        dn, s_out = _delta_net(dq, dz, dab, dn_conv_w, pa, pdt, ng, zero_state, l, 0, SEQ)
        xp = _output_stage(xp, sc, att, dn, *out_args, None, SEQ // TM)
        new_k.append(kn.reshape(BATCH, SEQ, N_KV_HEADS, HEAD_DIM))
        new_v.append(kv[:, KV_DIM:].reshape(BATCH, SEQ, N_KV_HEADS, HEAD_DIM))
        new_s.append(s_out)
        sc, q, kv, dq, dz, dab = _in_projection(xs, mod, n1, w_in_b, l, DEC_SEQ // TM)
        att = _attention_lat(q, kv, ck, cv, qg, kg, attn_sink, cos_t, sin_t, l)
        dn, _ = _delta_net(dq, dz, dab, dn_conv_w, pa, pdt, ng, state_delta, l, l, DEC_SEQ)
        xs = _output_stage(xs, sc, att, dn, *out_args, DEC_SEQ // TM, DEC_SEQ // TM)
    return (xp.reshape(BATCH, SEQ, D_MODEL), xs.reshape(DEC_BATCH, DEC_SEQ, D_MODEL),
            jnp.stack(new_k, axis=1), jnp.stack(new_v, axis=1), jnp.stack(new_s, axis=1))
```

```python
import functools

import numpy as np
import jax
import jax.numpy as jnp
from jax import lax
from jax.experimental import pallas as pl
from jax.experimental.pallas import tpu as pltpu

F32 = jnp.float32
BF16 = jnp.bfloat16

D_MODEL = 1024
BATCH = 16
SEQ = 256
DEPTH = 2
DEC_BATCH = 4
DEC_SEQ = 4096
PAST_LEN = 512
GRID_W = 64
SC_DIM = 256
N_Q_HEADS = 8
N_KV_HEADS = 2
N_GROUP = N_Q_HEADS // N_KV_HEADS
HEAD_DIM = 64
ATT_DIM = N_Q_HEADS * HEAD_DIM
KV_DIM = N_KV_HEADS * HEAD_DIM
WINDOW = 128
N_DN_HEADS = 4
DN_HEAD_DIM = 64
DN_DIM = N_DN_HEADS * DN_HEAD_DIM
CHUNK = 64
MIX_DIM = SC_DIM + ATT_DIM + DN_DIM
D_FF = -(-8 * D_MODEL // (3 * 256)) * 256
ROPE_BASE = 10000.0
EPS = 1e-6
NEG = -1e30

LANES = 128
V7X_VMEM_BYTES = 64 * 1024 * 1024
VMEM_LIMIT = (V7X_VMEM_BYTES * 3) // 4

OFF_SC, OFF_Q, OFF_KV, OFF_DQ, OFF_DZ, OFF_AB = 0, 768, 1280, 1536, 2304, 2560
IN_COLS = OFF_AB + 2 * LANES
N_PAIRS = N_DN_HEADS // 2
MOD_ROWS = 8
TM = 256


def _sigmoid(x):
    return 1.0 / (1.0 + jnp.exp(-x))


def _silu(x):
    return x * _sigmoid(x)


def _softplus(x):
    return jnp.maximum(x, 0.0) + jnp.log1p(jnp.exp(-jnp.abs(x)))


def _dot(a, b):
    return jnp.dot(a, b, preferred_element_type=F32)


def _dot_nt(a, b):
    return lax.dot_general(a, b, (((1,), (1,)), ((), ())), preferred_element_type=F32)


def _dot_tn(a, b):
    return lax.dot_general(a, b, (((0,), (0,)), ((), ())), preferred_element_type=F32)


def _lo_mask(shape):
    return lax.broadcasted_iota(jnp.int32, shape, 1) % LANES < HEAD_DIM


def _half_sums(xx, lo):
    s_lo = jnp.sum(jnp.where(lo, xx, 0.0), axis=-1, keepdims=True)
    s_hi = jnp.sum(jnp.where(lo, 0.0, xx), axis=-1, keepdims=True)
    return jnp.where(lo, s_lo, s_hi)


def _head_rms(x, g_row):
    lo = _lo_mask(x.shape)
    ms = _half_sums(x * x, lo) * (1.0 / HEAD_DIM)
    return x * lax.rsqrt(ms + EPS) * g_row


def _head_l2n(x):
    lo = _lo_mask(x.shape)
    return x * lax.rsqrt(_half_sums(x * x, lo) + EPS)


def _rope(x, cos, sin):
    n = x.shape[1]
    lane = lax.broadcasted_iota(jnp.int32, x.shape, 1)
    swapped = jnp.where((lane & 16) == 0, pltpu.roll(x, n - 16, axis=1), pltpu.roll(x, 16, axis=1))
    return x * cos + swapped * sin


def _dup_half(x, half):
    lo = _lo_mask(x.shape)
    r = pltpu.roll(x, HEAD_DIM, axis=1)
    return jnp.where(lo, x, r) if half == 0 else jnp.where(lo, r, x)


def _mod_kernel(c_ref, w_ref, b_ref, o_ref):
    s = _silu(c_ref[...])
    o_ref[...] = _dot(s.astype(BF16), w_ref[...].astype(BF16)) + b_ref[...]


def _modulation(cvecs, ada_w, ada_b):
    nblk = 6
    return pl.pallas_call(
        _mod_kernel,
        out_shape=jax.ShapeDtypeStruct((DEPTH, MOD_ROWS, 6 * D_MODEL), F32),
        grid=(DEPTH, nblk),
        in_specs=[pl.BlockSpec((MOD_ROWS, D_MODEL), lambda l, j: (0, 0)),
                  pl.BlockSpec((None, D_MODEL, D_MODEL), lambda l, j: (l, 0, j)),
                  pl.BlockSpec((None, 1, D_MODEL), lambda l, j: (l, 0, j))],
        out_specs=pl.BlockSpec((None, MOD_ROWS, D_MODEL), lambda l, j: (l, 0, j)),
        compiler_params=pltpu.CompilerParams(vmem_limit_bytes=VMEM_LIMIT),
        name="adaln_mod",
    )(cvecs, ada_w, ada_b.reshape(DEPTH, 1, 6 * D_MODEL))


def _mod_row_map(layer, tiles_per_batch):
    if tiles_per_batch is None:
        return lambda i: (layer, 0, 0, 0)
    return lambda i: (layer, 1 + i // tiles_per_batch, 0, 0)


def _in_kernel(x_ref, mod_ref, g_ref, w_ref, wab_ref, sc_ref, q_ref, kv_ref, dq_ref, dz_ref, ab_ref):
    x = x_ref[...]
    ms = jnp.mean(x * x, axis=-1, keepdims=True)
    h = (x * lax.rsqrt(ms + EPS) * g_ref[...]) * (1.0 + mod_ref[1:2, :]) + mod_ref[0:1, :]
    hb = h.astype(BF16)
    for ref, a, b in ((sc_ref, OFF_SC, OFF_Q), (q_ref, OFF_Q, OFF_KV), (kv_ref, OFF_KV, OFF_DQ),
                      (dq_ref, OFF_DQ, OFF_DZ), (dz_ref, OFF_DZ, OFF_AB)):
        ref[...] = _dot(hb, w_ref[:, a:b])
    ab_ref[...] = _dot(hb, wab_ref[...])


def _in_projection(x, mod, norm1_g, w_main_b, w_ab_b, layer, tiles_per_batch):
    t = x.shape[0]
    widths = (OFF_Q - OFF_SC, OFF_KV - OFF_Q, OFF_DQ - OFF_KV, OFF_DZ - OFF_DQ, OFF_AB - OFF_DZ, IN_COLS - OFF_AB)
    return pl.pallas_call(
        _in_kernel,
        out_shape=[jax.ShapeDtypeStruct((t, w), F32) for w in widths],
        grid=(t // TM,),
        in_specs=[pl.BlockSpec((TM, D_MODEL), lambda i: (i, 0)),
                  pl.BlockSpec((None, None, 6, D_MODEL), _mod_row_map(layer, tiles_per_batch)),
                  pl.BlockSpec((None, 1, D_MODEL), lambda i: (layer, 0, 0)),
                  pl.BlockSpec((None, D_MODEL, OFF_AB), lambda i: (layer, 0, 0),
                               pipeline_mode=pl.Buffered(1)),
                  pl.BlockSpec((None, D_MODEL, IN_COLS - OFF_AB), lambda i: (layer, 0, 0),
                               pipeline_mode=pl.Buffered(1))],
        out_specs=[pl.BlockSpec((TM, w), lambda i: (i, 0)) for w in widths],
        compiler_params=pltpu.CompilerParams(vmem_limit_bytes=VMEM_LIMIT),
        name="in_proj",
    )(x, mod, norm1_g, w_main_b, w_ab_b)


def _prep_q(q, qg, cos, sin):
    out = []
    for g in range(ATT_DIM // LANES):
        x = _head_rms(q[:, g * LANES:(g + 1) * LANES], qg)
        if cos is not None:
            x = _rope(x, cos, sin)
        out.append(x * (HEAD_DIM ** -0.5))
    return out


def _head_logits(qa, half, key_sets):
    lo = _lo_mask(qa.shape)
    qh = jnp.where(lo if half == 0 else jnp.logical_not(lo), qa, 0.0).astype(BF16)
    logits = []
    for k, _, mask in key_sets:
        s = _dot_nt(qh, k)
        logits.append(s if mask is None else jnp.where(mask, s, NEG))
    return logits


def _head_softmax_pv(logits, key_sets, sink):
    m = jnp.maximum(jnp.max(jnp.concatenate(logits, axis=1), axis=-1, keepdims=True), sink)
    acc = None
    for s, (_, v1, _) in zip(logits, key_sets):
        o = _dot(jnp.exp(s - m).astype(BF16), v1)
        acc = o if acc is None else acc + o
    return acc[:, :LANES] / (acc[:, LANES:] + jnp.exp(sink - m))


def _attend_heads(qs, key_sets_of, sink_ref, layer, o_ref):
    lo = _lo_mask(qs[0].shape)
    heads = [(pair, half) for pair in range(len(qs)) for half in range(2)]
    keys = {kvh: key_sets_of(kvh) for kvh in range(N_KV_HEADS)}
    kvh_of = lambda pair: (2 * pair) // N_GROUP
    logits = _head_logits(qs[0], 0, keys[0])
    done = []
    for idx, (pair, half) in enumerate(heads):
        nxt = None
        if idx + 1 < len(heads):
            npair, nhalf = heads[idx + 1]
            nxt = _head_logits(qs[npair], nhalf, keys[kvh_of(npair)])
        done.append(_head_softmax_pv(logits, keys[kvh_of(pair)], sink_ref[layer, 2 * pair + half]))
        logits = nxt
        if half == 1:
            o_ref[:, pair * LANES:(pair + 1) * LANES] = jnp.where(lo, done[-2], done[-1])


def _values_with_ones(v2, kvh):
    return jnp.concatenate([_dup_half(v2, kvh), jnp.ones(v2.shape, F32)], axis=1).astype(BF16)


def _attn_ctx_kernel(sink_ref, q_ref, kv_ref, qg_ref, kg_ref, o_ref, kn_ref, *, layer):
    kv = kv_ref[...]
    kn = _head_rms(kv[:, :LANES], kg_ref[...])
    kn_ref[...] = kn
    v2 = kv[:, LANES:]
    qs = _prep_q(q_ref[...], qg_ref[...], None, None)
    key_sets_of = lambda kvh: [(_dup_half(kn, kvh).astype(BF16), _values_with_ones(v2, kvh), None)]
    _attend_heads(qs, key_sets_of, sink_ref, layer, o_ref)


def _attention_ctx(q, kv, q_norm_g, k_norm_g, attn_sink, layer):
    t = q.shape[0]
    return pl.pallas_call(
        functools.partial(_attn_ctx_kernel, layer=layer),
        out_shape=[jax.ShapeDtypeStruct((t, ATT_DIM), F32), jax.ShapeDtypeStruct((t, KV_DIM), F32)],
        grid=(t // SEQ,),
        in_specs=[pl.BlockSpec(memory_space=pltpu.SMEM),
                  pl.BlockSpec((SEQ, ATT_DIM), lambda b: (b, 0)),
                  pl.BlockSpec((SEQ, 2 * KV_DIM), lambda b: (b, 0)),
                  pl.BlockSpec((None, 1, LANES), lambda b: (layer, 0, 0)),
                  pl.BlockSpec((None, 1, LANES), lambda b: (layer, 0, 0))],
        out_specs=[pl.BlockSpec((SEQ, ATT_DIM), lambda b: (b, 0)),
                   pl.BlockSpec((SEQ, KV_DIM), lambda b: (b, 0))],
        compiler_params=pltpu.CompilerParams(vmem_limit_bytes=VMEM_LIMIT),
        name="attn_ctx",
    )(attn_sink, q, kv, q_norm_g, k_norm_g)


QBLK = 128
WIN_KEYS = QBLK + 2 * WINDOW
PREP_ROWS = 512


def _attn_lat_kernel(sink_ref, q_ref, kv_ref, kc_ref, vc_ref, qg_ref, kg_ref, cos_ref, sin_ref, o_ref,
                     kd_s, vd_s, kcd_s, vcd_s, *, layer):
    i = pl.program_id(1)
    n = DEC_SEQ

    @pl.when(i == 0)
    def _():
        for kvh in range(N_KV_HEADS):
            for s in (kd_s, vd_s):
                zpad = jnp.zeros((WINDOW, s.shape[-1]), BF16)
                s[kvh, 0:WINDOW, :] = zpad
                s[kvh, WINDOW + n:2 * WINDOW + n, :] = zpad
            kcd_s[kvh] = _dup_half(kc_ref[...], kvh).astype(BF16)
            vcd_s[kvh] = _values_with_ones(vc_ref[...], kvh)

        def prep(r, carry):
            r0 = pl.multiple_of(r * PREP_ROWS, PREP_ROWS)
            kv = kv_ref[pl.ds(r0, PREP_ROWS), :]
            kn = _head_rms(kv[:, :LANES], kg_ref[...])
            kr = _rope(kn, cos_ref[pl.ds(r0, PREP_ROWS), :], sin_ref[pl.ds(r0, PREP_ROWS), :])
            v2 = kv[:, LANES:]
            for kvh in range(N_KV_HEADS):
                kd_s[kvh, pl.ds(WINDOW + r0, PREP_ROWS), :] = _dup_half(kr, kvh).astype(BF16)
                vd_s[kvh, pl.ds(WINDOW + r0, PREP_ROWS), :] = _values_with_ones(v2, kvh)
            return carry

        lax.fori_loop(0, n // PREP_ROWS, prep, 0)

    start = pl.multiple_of(i * QBLK, QBLK)
    qs = _prep_q(q_ref[...], qg_ref[...], cos_ref[pl.ds(start, QBLK), :], sin_ref[pl.ds(start, QBLK), :])
    rr = lax.broadcasted_iota(jnp.int32, (QBLK, WIN_KEYS), 0)
    jj = lax.broadcasted_iota(jnp.int32, (QBLK, WIN_KEYS), 1)
    kpos = jj + (start - WINDOW)
    valid = (jj >= rr) & (jj <= rr + 2 * WINDOW) & (kpos >= 0) & (kpos < n)
    key_sets_of = lambda kvh: [
        (kcd_s[kvh], vcd_s[kvh], None),
        (kd_s[kvh, pl.ds(start, WIN_KEYS), :], vd_s[kvh, pl.ds(start, WIN_KEYS), :], valid)]
    _attend_heads(qs, key_sets_of, sink_ref, layer, o_ref)


def _attention_lat(q, kv, cache_k, cache_v, q_norm_g, k_norm_g, attn_sink, cos_t, sin_t, layer):
    t = q.shape[0]
    n = DEC_SEQ
    nblk = n // QBLK
    return pl.pallas_call(
        functools.partial(_attn_lat_kernel, layer=layer),
        out_shape=jax.ShapeDtypeStruct((t, ATT_DIM), F32),
        grid=(DEC_BATCH, nblk),
        in_specs=[pl.BlockSpec(memory_space=pltpu.SMEM),
                  pl.BlockSpec((QBLK, ATT_DIM), lambda b, i: (b * nblk + i, 0)),
                  pl.BlockSpec((n, 2 * KV_DIM), lambda b, i: (b, 0)),
                  pl.BlockSpec((None, None, PAST_LEN, KV_DIM), lambda b, i: (b, layer, 0, 0)),
                  pl.BlockSpec((None, None, PAST_LEN, KV_DIM), lambda b, i: (b, layer, 0, 0)),
                  pl.BlockSpec((None, 1, LANES), lambda b, i: (layer, 0, 0)),
                  pl.BlockSpec((None, 1, LANES), lambda b, i: (layer, 0, 0)),
                  pl.BlockSpec((n, LANES), lambda b, i: (0, 0)),
                  pl.BlockSpec((n, LANES), lambda b, i: (0, 0))],
        out_specs=pl.BlockSpec((QBLK, ATT_DIM), lambda b, i: (b * nblk + i, 0)),
        scratch_shapes=[pltpu.VMEM((N_KV_HEADS, n + 2 * WINDOW, LANES), BF16),
                        pltpu.VMEM((N_KV_HEADS, n + 2 * WINDOW, 2 * LANES), BF16),
                        pltpu.VMEM((N_KV_HEADS, PAST_LEN, LANES), BF16),
                        pltpu.VMEM((N_KV_HEADS, PAST_LEN, 2 * LANES), BF16)],
        compiler_params=pltpu.CompilerParams(vmem_limit_bytes=VMEM_LIMIT),
        name="attn_lat",
    )(attn_sink, q, kv, cache_k, cache_v, q_norm_g, k_norm_g, cos_t, sin_t)


DN_ROWS = 128
DN_OUT_ROWS = 256


def _dn_conv(x_ref, w_ref, r0, n):
    x = x_ref[pl.ds(r0, DN_ROWS), :]
    above = jnp.where(r0 > 0, x_ref[pl.ds(jnp.maximum(r0 - 1, 0), 1), :], 0.0)
    below = jnp.where(r0 + DN_ROWS < n, x_ref[pl.ds(jnp.minimum(r0 + DN_ROWS, n - 1), 1), :], 0.0)
    row = lax.broadcasted_iota(jnp.int32, x.shape, 0)
    prev = jnp.where(row == 0, above, pltpu.roll(x, 1, axis=0))
    nxt = jnp.where(row == DN_ROWS - 1, below, pltpu.roll(x, DN_ROWS - 1, axis=0))
    return _silu(prev * w_ref[0:1, :] + x * w_ref[1:2, :] + nxt * w_ref[2:3, :])


N_UNITS = 4
UW = N_UNITS * CHUNK
N_LEVELS = 6
M_INCL, M_STRICT, M_EYE = N_LEVELS, N_LEVELS + 1, N_LEVELS + 2
PREP_BLOCK = 8


def _split3(x):
    hi = x.astype(BF16)
    r = x - hi.astype(F32)
    mid = r.astype(BF16)
    lo = (r - mid.astype(F32)).astype(BF16)
    return hi, mid, lo


def _dn_kernel(q_ref, k_ref, v_ref, z_ref, ab_ref, wq_ref, wk_ref, wv_ref, pa_ref, pdt_ref, ng_ref, s0_ref,
               o_ref, sout_ref, qs, ks, vs, gbs, pfs, r_s, au_s, m_s, qa_s, sh_s, el_s, st, mc, *, n):
    nc = n // CHUNK
    pblk = min(PREP_BLOCK, nc)
    def prep_rows(b, carry):
        r0 = pl.multiple_of(b * DN_ROWS, DN_ROWS)
        rows = pl.ds(r0, DN_ROWS)
        qs[rows, :] = _head_l2n(_dn_conv(q_ref, wq_ref, r0, n)) * (DN_HEAD_DIM ** -0.5)
        ks[rows, :] = _head_l2n(_dn_conv(k_ref, wk_ref, r0, n))
        vs[rows, :] = _dn_conv(v_ref, wv_ref, r0, n)
        ab = ab_ref[rows, :]
        lane = lax.broadcasted_iota(jnp.int32, ab.shape, 1)
        gb = jnp.where(lane < N_UNITS, -jnp.exp(pa_ref[...]) * _softplus(ab + pdt_ref[...]), _sigmoid(ab))
        gbs[rows, :] = gb
        row_in_chunk = lax.broadcasted_iota(jnp.int32, ab.shape, 0) % CHUNK
        pref = gb
        for sh in (1, 2, 4, 8, 16, 32):
            pref = pref + jnp.where(row_in_chunk >= sh, pltpu.roll(pref, sh, axis=0), 0.0)
        pfs[rows, :] = pref
        return carry

    lax.fori_loop(0, n // DN_ROWS, prep_rows, 0)

    as_f32 = lambda m: jnp.where(m, 1.0, 0.0).astype(F32)
    ii = lax.broadcasted_iota(jnp.int32, (CHUNK, UW), 0)
    ll = lax.broadcasted_iota(jnp.int32, (CHUNK, UW), 1)
    unit_of_lane, jj = ll // CHUNK, ll % CHUNK
    fwd, bwd = unit_of_lane < 2, unit_of_lane >= 2
    for k in range(N_LEVELS):
        rk, ck = ii >> k, jj >> k
        lower = (rk - ck == 1) & ((rk & 1) == 1)
        upper = (ck - rk == 1) & ((ck & 1) == 1)
        mc[k] = as_f32((fwd & lower) | (bwd & upper))
    mc[M_INCL] = as_f32((fwd & (ii >= jj)) | (bwd & (ii <= jj)))
    mc[M_STRICT] = as_f32((fwd & (ii > jj)) | (bwd & (ii < jj)))
    mc[M_EYE] = as_f32(ii == jj)
    st[...] = jnp.concatenate([s0_ref[u // 2, u % 2] for u in range(N_UNITS)], axis=1)

    lo = _lo_mask((CHUNK, LANES))
    lane_c = lax.broadcasted_iota(jnp.int32, (CHUNK, LANES), 1)

    def lane_pack(m, off):
        cols = [jnp.broadcast_to(m[:, off + u:off + u + 1], (CHUNK, LANES)) for u in range(N_UNITS)]
        return jnp.concatenate([jnp.where(lo, cols[0], cols[1]), jnp.where(lo, cols[2], cols[3])], axis=1)

    def unit_dot(a, xb):
        zero = jnp.zeros((CHUNK, LANES), BF16)
        halves = []
        for h in range(2):
            x = xb[:, h * LANES:(h + 1) * LANES]
            bd = jnp.concatenate([jnp.where(lo, x, zero), jnp.where(lo, zero, x)], axis=0)
            halves.append(_dot(a[:, h * LANES:(h + 1) * LANES], bd))
        return jnp.concatenate(halves, axis=1)

    def rows_to_lanes(m):
        return jnp.concatenate([jnp.where(lo, m[0:CHUNK, :LANES], m[CHUNK:2 * CHUNK, :LANES]),
                                jnp.where(lo, m[2 * CHUNK:3 * CHUNK, LANES:], m[3 * CHUNK:, LANES:])], axis=1)

    dup = lambda x: jnp.concatenate([x, x], axis=1)
    split_heads = lambda x: jnp.concatenate([jnp.where(lo, x, 0.0), jnp.where(lo, 0.0, x)], axis=0).astype(BF16)

    def head_blocks(m):
        return dup(jnp.where(lo, m[0:CHUNK], m[CHUNK:]))

    def chunk_prep(blk, carry):
        cs = [blk * pblk + s for s in range(pblk)]
        rows = [pl.ds(pl.multiple_of(c * CHUNK, CHUNK), CHUNK) for c in cs]
        each = lambda f, *lists: [f(*xs) for xs in zip(*lists)]

        gx = [gbs[r, :] for r in rows]
        pc = [pfs[r, :] for r in rows]
        tot = [p[CHUNK - 1:CHUNK, :] for p in pc]
        gcm = each(lambda g, p, t: jnp.where(lane_c < 2, p, t - p + g), gx, pc, tot)
        rsm = each(lambda g, p, t: jnp.where(lane_c < 2, t - p, p - g), gx, pc, tot)
        e_gc = each(lambda m: jnp.exp(lane_pack(m, 0)), gcm)
        e_rest = each(lambda m: jnp.exp(lane_pack(m, 0)), rsm)
        b4 = each(lambda g: lane_pack(g, N_UNITS), gx)
        pieces = each(lambda g: _split3(lane_pack(g, 0) * mc[M_STRICT]), gx)
        cum = mc[M_INCL].astype(BF16)
        diff = each(lambda ps: unit_dot(cum, ps[0]) + unit_dot(cum, ps[1]) + unit_dot(cum, ps[2]), pieces)
        dec = each(lambda d: jnp.exp(d) * mc[M_INCL], diff)
        k2 = [split_heads(ks[r, :]) for r in rows]
        q2 = [split_heads(qs[r, :]) for r in rows]
        gram = each(lambda k: head_blocks(_dot_nt(k, k)), k2)
        qk = each(lambda q, k: head_blocks(_dot_nt(q, k)), q2, k2)
        lmat = each(lambda g, b, d: g * b * d * mc[M_STRICT], gram, b4, dec)
        amat = each(lambda m, d: (m * d).astype(BF16), qk, dec)
        t = each(lambda l: mc[M_EYE] - l * mc[0], lmat)
        for k in range(1, N_LEVELS):
            lb = each(lambda l: (l * mc[k]).astype(BF16), lmat)
            tb = each(lambda x: x.astype(BF16), t)
            x = each(lambda a, b: unit_dot(a, b).astype(BF16), tb, lb)
            t = each(lambda t0, a, b: t0 - unit_dot(a, b), t, x, tb)
        tm = each(lambda x: (x - mc[M_EYE]).astype(BF16), t)
        k4 = [dup(ks[r, :]) for r in rows]
        ru = each(lambda r, b: dup(vs[r, :]) * b, rows, b4)
        rw = each(lambda k, b, e: k * b * e, k4, b4, e_gc)
        ub = each(lambda x, tmi: (x + unit_dot(tmi, x.astype(BF16))).astype(BF16), ru, tm)
        wb = each(lambda x, tmi: (x + unit_dot(tmi, x.astype(BF16))).astype(BF16), rw, tm)
        kd = each(lambda k, e: (k * e).astype(BF16), k4, e_rest)
        mmat = each(lambda a, b: rows_to_lanes(_dot_tn(a, b)), kd, wb)
        rmat = each(lambda a, b: rows_to_lanes(_dot_tn(a, b)), kd, ub)
        aw = each(unit_dot, amat, wb)
        au = each(unit_dot, amat, ub)
        for i, (c, r) in enumerate(zip(cs, rows)):
            m_s[r, :] = mmat[i].astype(BF16)
            r_s[r, :] = rmat[i]
            qa_s[r, :] = (dup(qs[r, :]) * e_gc[i] - aw[i]).astype(BF16)
            au_s[r, :] = au[i]
            el = jnp.concatenate([e_gc[i][CHUNK - 1:CHUNK, :LANES], e_gc[i][0:1, LANES:]], axis=1)
            el_s[c] = jnp.broadcast_to(el, (8, UW))
        return carry

    lax.fori_loop(0, nc // pblk, chunk_prep, 0)

    def scan_step(i, carry):
        rf = pl.ds(pl.multiple_of(i * CHUNK, CHUNK), CHUNK)
        rb = pl.ds(pl.multiple_of((nc - 1 - i) * CHUNK, CHUNK), CHUNK)
        mixed = lambda ref: jnp.concatenate([ref[rf, :LANES], ref[rb, LANES:]], axis=1)
        el = jnp.concatenate([el_s[i][0:1, :LANES], el_s[nc - 1 - i][0:1, LANES:]], axis=1)
        s = st[...]
        sb = s.astype(BF16)
        st[...] = s * el - unit_dot(mixed(m_s), sb) + mixed(r_s)
        sh_s[rf, :LANES] = sb[:, :LANES]
        sh_s[rb, LANES:] = sb[:, LANES:]
        return carry

    lax.fori_loop(0, nc, scan_step, 0)

    s = st[...]
    for u in range(N_UNITS):
        sout_ref[u // 2, u % 2] = s[:, u * CHUNK:(u + 1) * CHUNK]

    def finish_rows(b, carry):
        r0 = b * DN_OUT_ROWS
        parts = []
        for c in range(DN_OUT_ROWS // CHUNK):
            rows = pl.ds(pl.multiple_of(r0 + c * CHUNK, CHUNK), CHUNK)
            parts.append(unit_dot(qa_s[rows, :], sh_s[rows, :]) + au_s[rows, :])
        o = jnp.concatenate([p[:, :LANES] + p[:, LANES:] for p in parts], axis=0)
        rows = pl.ds(pl.multiple_of(r0, DN_OUT_ROWS), DN_OUT_ROWS)
        o_ref[rows, :] = _head_rms(o, ng_ref[...]) * _silu(z_ref[rows, :])
        return carry

    lax.fori_loop(0, n // DN_OUT_ROWS, finish_rows, 0)


def _delta_net(dq, dz, dab, dn_conv_w, pa, pdt, dn_norm_g, s0, layer, s0_layer, n):
    t = dq.shape[0]
    bsz = t // n
    nc = n // CHUNK
    conv_spec = lambda off: pl.BlockSpec((None, 3, LANES), lambda b, p: (layer, 0, off + p))
    col_spec = lambda off: pl.BlockSpec((n, LANES), lambda b, p: (b, off + p), pipeline_mode=pl.Buffered(1))
    return pl.pallas_call(
        functools.partial(_dn_kernel, n=n),
        out_shape=[jax.ShapeDtypeStruct((t, DN_DIM), F32),
                   jax.ShapeDtypeStruct((bsz, 2, N_DN_HEADS, DN_HEAD_DIM, DN_HEAD_DIM), F32)],
        grid=(bsz, N_PAIRS),
        in_specs=[col_spec(0), col_spec(N_PAIRS), col_spec(2 * N_PAIRS), col_spec(0), col_spec(0),
                  conv_spec(0), conv_spec(N_PAIRS), conv_spec(2 * N_PAIRS),
                  pl.BlockSpec((None, None, 1, LANES), lambda b, p: (layer, p, 0, 0)),
                  pl.BlockSpec((None, None, 1, LANES), lambda b, p: (layer, p, 0, 0)),
                  pl.BlockSpec((None, 1, LANES), lambda b, p: (layer, 0, 0)),
                  pl.BlockSpec((None, None, 2, 2, DN_HEAD_DIM, DN_HEAD_DIM),
                               lambda b, p: (b, s0_layer, 0, p, 0, 0))],
        out_specs=[pl.BlockSpec((n, LANES), lambda b, p: (b, p)),
                   pl.BlockSpec((None, 2, 2, DN_HEAD_DIM, DN_HEAD_DIM), lambda b, p: (b, 0, p, 0, 0))],
        scratch_shapes=[pltpu.VMEM((n, LANES), F32) for _ in range(5)]
                       + [pltpu.VMEM((n, UW), F32) for _ in range(2)]
                       + [pltpu.VMEM((n, UW), BF16) for _ in range(3)]
                       + [pltpu.VMEM((nc, 8, UW), F32),
                          pltpu.VMEM((CHUNK, UW), F32),
                          pltpu.VMEM((N_LEVELS + 3, CHUNK, UW), F32)],
        compiler_params=pltpu.CompilerParams(vmem_limit_bytes=(V7X_VMEM_BYTES * 7) // 8),
        name="delta_net",
    )(dq, dq, dq, dz, dab, dn_conv_w, dn_conv_w, dn_conv_w, pa, pdt, dn_norm_g, s0)


def _out_kernel(x_ref, sc_ref, scp_ref, scn_ref, att_ref, dn_ref, mod_ref, cw_ref, wo_ref, n2_ref,
                wg_ref, wu_ref, wd_ref, o_ref, *, tiles_per_seq):
    t = pl.program_id(0) % tiles_per_seq
    sc = sc_ref[...]
    prod = sc[:, SC_DIM:2 * SC_DIM] * sc[:, 2 * SC_DIM:]
    pv = scp_ref[7:8, :]
    nx = scn_ref[0:1, :]
    prev_row = jnp.where(t > 0, pv[:, SC_DIM:2 * SC_DIM] * pv[:, 2 * SC_DIM:], 0.0)
    next_row = jnp.where(t < tiles_per_seq - 1, nx[:, SC_DIM:2 * SC_DIM] * nx[:, 2 * SC_DIM:], 0.0)
    row = lax.broadcasted_iota(jnp.int32, prod.shape, 0)
    p_prev = jnp.where(row == 0, prev_row, pltpu.roll(prod, 1, axis=0))
    p_next = jnp.where(row == TM - 1, next_row, pltpu.roll(prod, TM - 1, axis=0))
    y_sc = sc[:, :SC_DIM] * (p_prev * cw_ref[0:1, :] + prod * cw_ref[1:2, :] + p_next * cw_ref[2:3, :])
    y = (_dot(y_sc.astype(BF16), wo_ref[0:SC_DIM, :])
         + _dot(att_ref[...].astype(BF16), wo_ref[SC_DIM:SC_DIM + ATT_DIM, :])
         + _dot(dn_ref[...].astype(BF16), wo_ref[SC_DIM + ATT_DIM:, :]))
    x1 = x_ref[...] + mod_ref[2:3, :] * y
    ms = jnp.mean(x1 * x1, axis=-1, keepdims=True)
    h2 = (x1 * lax.rsqrt(ms + EPS) * n2_ref[...]) * (1.0 + mod_ref[4:5, :]) + mod_ref[3:4, :]
    hb = h2.astype(BF16)
    act = _silu(_dot(hb, wg_ref[...])) * _dot(hb, wu_ref[...])
    o_ref[...] = x1 + mod_ref[5:6, :] * _dot(act.astype(BF16), wd_ref[...])


def _output_stage(x, sc, att, dn, mod, sc_conv_w, w_out_b, norm2_g, w_gate_b, w_up_b, w_down_b, layer,
                  tiles_per_batch, tiles_per_seq):
    t = x.shape[0]
    halo = TM // 8
    last = t // 8 - 1
    resident = lambda shape: pl.BlockSpec((None,) + shape, lambda i: (layer, 0, 0), pipeline_mode=pl.Buffered(1))
    return pl.pallas_call(
        functools.partial(_out_kernel, tiles_per_seq=tiles_per_seq),
        out_shape=jax.ShapeDtypeStruct((t, D_MODEL), F32),
        grid=(t // TM,),
        in_specs=[pl.BlockSpec((TM, D_MODEL), lambda i: (i, 0)),
                  pl.BlockSpec((TM, 3 * SC_DIM), lambda i: (i, 0)),
                  pl.BlockSpec((8, 3 * SC_DIM), lambda i: (jnp.maximum(i * halo - 1, 0), 0)),
                  pl.BlockSpec((8, 3 * SC_DIM), lambda i: (jnp.minimum((i + 1) * halo, last), 0)),
                  pl.BlockSpec((TM, ATT_DIM), lambda i: (i, 0)),
                  pl.BlockSpec((TM, DN_DIM), lambda i: (i, 0)),
                  pl.BlockSpec((None, None, 6, D_MODEL), _mod_row_map(layer, tiles_per_batch)),
                  pl.BlockSpec((None, 3, SC_DIM), lambda i: (layer, 0, 0)),
                  resident((MIX_DIM, D_MODEL)),
                  pl.BlockSpec((None, 1, D_MODEL), lambda i: (layer, 0, 0)),
                  resident((D_MODEL, D_FF)), resident((D_MODEL, D_FF)), resident((D_FF, D_MODEL))],
        out_specs=pl.BlockSpec((TM, D_MODEL), lambda i: (i, 0)),
        compiler_params=pltpu.CompilerParams(vmem_limit_bytes=VMEM_LIMIT),
        name="out_stage",
    )(x, sc, sc, sc, att, dn, mod, sc_conv_w, w_out_b, norm2_g, w_gate_b, w_up_b, w_down_b)


def _rope_tables():
    pos = jnp.arange(DEC_SEQ)
    half = HEAD_DIM // 4
    inv = 1.0 / (ROPE_BASE ** (jnp.arange(half, dtype=F32) / half))
    ang_r = (pos // GRID_W).astype(F32)[:, None] * inv
    ang_c = (pos % GRID_W).astype(F32)[:, None] * inv
    cos = jnp.concatenate([jnp.cos(ang_r)] * 2 + [jnp.cos(ang_c)] * 2, axis=-1)
    sin = jnp.concatenate([-jnp.sin(ang_r), jnp.sin(ang_r), -jnp.sin(ang_c), jnp.sin(ang_c)], axis=-1)
    return jnp.tile(cos, (1, 2)), jnp.tile(sin, (1, 2))


def _pack_w_in(w_in):
    a0 = OFF_AB
    b0 = OFF_AB + 2 * N_DN_HEADS
    blocks = []
    for p in range(N_PAIRS):
        idx = [a0 + d * N_DN_HEADS + 2 * p + j for d in range(2) for j in range(2)]
        idx += [b0 + d * N_DN_HEADS + 2 * p + j for d in range(2) for j in range(2)]
        blocks += [w_in[:, :, c:c + 1] for c in idx]
        blocks.append(jnp.zeros((DEPTH, D_MODEL, LANES - len(idx)), w_in.dtype))
    return w_in[:, :, :OFF_AB].astype(BF16), jnp.concatenate(blocks, axis=2).astype(BF16)


def _pair_lanes(p):
    x = p.reshape(DEPTH, 2, N_PAIRS, 2).transpose(0, 2, 1, 3).reshape(DEPTH, N_PAIRS, 1, 4)
    return jnp.pad(x, ((0, 0), (0, 0), (0, 0), (0, LANES - 4)))


def kernel(x_prompt, x_sample, cache_k, cache_v, state_delta, c, c_ctx, w_in, w_out, ada_w, ada_b,
           norm1_g, norm2_g, sc_conv_w, dn_conv_w, q_norm_g, k_norm_g, attn_sink, dn_A_log,
           dn_dt_bias, dn_norm_g, w_gate, w_up, w_down):
    w_main_b, w_ab_b = _pack_w_in(w_in)
    w_out_b, w_gate_b, w_up_b, w_down_b = (w.astype(BF16) for w in (w_out, w_gate, w_up, w_down))
    n1 = norm1_g.reshape(DEPTH, 1, D_MODEL)
    n2 = norm2_g.reshape(DEPTH, 1, D_MODEL)
    qg = jnp.tile(q_norm_g, (1, 2)).reshape(DEPTH, 1, LANES)
    kg = jnp.tile(k_norm_g, (1, 2)).reshape(DEPTH, 1, LANES)
    ng = jnp.tile(dn_norm_g, (1, 2)).reshape(DEPTH, 1, LANES)
    pa, pdt = _pair_lanes(dn_A_log), _pair_lanes(dn_dt_bias)
    cos_t, sin_t = _rope_tables()
    ck = cache_k.reshape(DEC_BATCH, DEPTH, PAST_LEN, KV_DIM)
    cv = cache_v.reshape(DEC_BATCH, DEPTH, PAST_LEN, KV_DIM)
    zero_state = jnp.zeros((BATCH, 1, 2, N_DN_HEADS, DN_HEAD_DIM, DN_HEAD_DIM), F32)

    cvecs = jnp.concatenate([c_ctx[None, :], c, jnp.zeros((MOD_ROWS - 1 - DEC_BATCH, D_MODEL), F32)], axis=0)
    mod = _modulation(cvecs, ada_w, ada_b).reshape(DEPTH, MOD_ROWS, 6, D_MODEL)

    xp = x_prompt.reshape(BATCH * SEQ, D_MODEL)
    xs = x_sample.reshape(DEC_BATCH * DEC_SEQ, D_MODEL)
    new_k, new_v, new_s = [], [], []
    for l in range(DEPTH):
        out_args = (mod, sc_conv_w, w_out_b, n2, w_gate_b, w_up_b, w_down_b, l)
        sc, q, kv, dq, dz, dab = _in_projection(xp, mod, n1, w_main_b, w_ab_b, l, None)
        att, kn = _attention_ctx(q, kv, qg, kg, attn_sink, l)
        dn, s_out = _delta_net(dq, dz, dab, dn_conv_w, pa, pdt, ng, zero_state, l, 0, SEQ)
        xp = _output_stage(xp, sc, att, dn, *out_args, None, SEQ // TM)
        new_k.append(kn.reshape(BATCH, SEQ, N_KV_HEADS, HEAD_DIM))
        new_v.append(kv[:, KV_DIM:].reshape(BATCH, SEQ, N_KV_HEADS, HEAD_DIM))
        new_s.append(s_out)
        sc, q, kv, dq, dz, dab = _in_projection(xs, mod, n1, w_main_b, w_ab_b, l, DEC_SEQ // TM)
        att = _attention_lat(q, kv, ck, cv, qg, kg, attn_sink, cos_t, sin_t, l)
        dn, _ = _delta_net(dq, dz, dab, dn_conv_w, pa, pdt, ng, state_delta, l, l, DEC_SEQ)
        xs = _output_stage(xs, sc, att, dn, *out_args, DEC_SEQ // TM, DEC_SEQ // TM)
    return (xp.reshape(BATCH, SEQ, D_MODEL), xs.reshape(DEC_BATCH, DEC_SEQ, D_MODEL),
            jnp.stack(new_k, axis=1), jnp.stack(new_v, axis=1), jnp.stack(new_s, axis=1))
```

```python
import functools

import numpy as np
import jax
import jax.numpy as jnp
from jax import lax
from jax.experimental import pallas as pl
from jax.experimental.pallas import tpu as pltpu

F32 = jnp.float32
BF16 = jnp.bfloat16

D_MODEL = 1024
BATCH = 16
SEQ = 256
DEPTH = 2
DEC_BATCH = 4
DEC_SEQ = 4096
PAST_LEN = 512
GRID_W = 64
SC_DIM = 256
N_Q_HEADS = 8
N_KV_HEADS = 2
N_GROUP = N_Q_HEADS // N_KV_HEADS
HEAD_DIM = 64
ATT_DIM = N_Q_HEADS * HEAD_DIM
KV_DIM = N_KV_HEADS * HEAD_DIM
WINDOW = 128
N_DN_HEADS = 4
DN_HEAD_DIM = 64
DN_DIM = N_DN_HEADS * DN_HEAD_DIM
CHUNK = 64
MIX_DIM = SC_DIM + ATT_DIM + DN_DIM
D_FF = -(-8 * D_MODEL // (3 * 256)) * 256
ROPE_BASE = 10000.0
EPS = 1e-6
NEG = -1e30

LANES = 128
V7X_VMEM_BYTES = 64 * 1024 * 1024
VMEM_LIMIT = (V7X_VMEM_BYTES * 3) // 4

OFF_SC, OFF_Q, OFF_KV, OFF_DQ, OFF_DZ, OFF_AB = 0, 768, 1280, 1536, 2304, 2560
IN_COLS = OFF_AB + 2 * LANES
N_PAIRS = N_DN_HEADS // 2
MOD_ROWS = 8
TM = 256


def _sigmoid(x):
    return 1.0 / (1.0 + jnp.exp(-x))


def _silu(x):
    return x * _sigmoid(x)


def _softplus(x):
    return jnp.maximum(x, 0.0) + jnp.log1p(jnp.exp(-jnp.abs(x)))


def _dot(a, b):
    return jnp.dot(a, b, preferred_element_type=F32)


def _dot_nt(a, b):
    return lax.dot_general(a, b, (((1,), (1,)), ((), ())), preferred_element_type=F32)


def _lo_mask(shape):
    return lax.broadcasted_iota(jnp.int32, shape, 1) % LANES < HEAD_DIM


def _half_sums(xx, lo):
    s_lo = jnp.sum(jnp.where(lo, xx, 0.0), axis=-1, keepdims=True)
    s_hi = jnp.sum(jnp.where(lo, 0.0, xx), axis=-1, keepdims=True)
    return jnp.where(lo, s_lo, s_hi)


def _head_rms(x, g_row):
    lo = _lo_mask(x.shape)
    ms = _half_sums(x * x, lo) * (1.0 / HEAD_DIM)
    return x * lax.rsqrt(ms + EPS) * g_row


def _head_l2n(x):
    lo = _lo_mask(x.shape)
    return x * lax.rsqrt(_half_sums(x * x, lo) + EPS)


def _rope(x, cos, sin):
    n = x.shape[1]
    lane = lax.broadcasted_iota(jnp.int32, x.shape, 1)
    swapped = jnp.where((lane & 16) == 0, pltpu.roll(x, n - 16, axis=1), pltpu.roll(x, 16, axis=1))
    return x * cos + swapped * sin


def _dup_half(x, half):
    lo = _lo_mask(x.shape)
    r = pltpu.roll(x, HEAD_DIM, axis=1)
    return jnp.where(lo, x, r) if half == 0 else jnp.where(lo, r, x)


def _mod_kernel(c_ref, w_ref, b_ref, o_ref):
    s = _silu(c_ref[...])
    o_ref[...] = _dot(s.astype(BF16), w_ref[...].astype(BF16)) + b_ref[...]


def _modulation(cvecs, ada_w, ada_b):
    nblk = 6
    return pl.pallas_call(
        _mod_kernel,
        out_shape=jax.ShapeDtypeStruct((DEPTH, MOD_ROWS, 6 * D_MODEL), F32),
        grid=(DEPTH, nblk),
        in_specs=[pl.BlockSpec((MOD_ROWS, D_MODEL), lambda l, j: (0, 0)),
                  pl.BlockSpec((None, D_MODEL, D_MODEL), lambda l, j: (l, 0, j)),
                  pl.BlockSpec((None, 1, D_MODEL), lambda l, j: (l, 0, j))],
        out_specs=pl.BlockSpec((None, MOD_ROWS, D_MODEL), lambda l, j: (l, 0, j)),
        compiler_params=pltpu.CompilerParams(vmem_limit_bytes=VMEM_LIMIT),
        name="adaln_mod",
    )(cvecs, ada_w, ada_b.reshape(DEPTH, 1, 6 * D_MODEL))


def _mod_row_map(layer, tiles_per_batch):
    if tiles_per_batch is None:
        return lambda i: (layer, 0, 0, 0)
    return lambda i: (layer, 1 + i // tiles_per_batch, 0, 0)


def _in_kernel(x_ref, mod_ref, g_ref, w_ref, wab_ref, sc_ref, q_ref, kv_ref, dq_ref, dz_ref, ab_ref):
    x = x_ref[...]
    ms = jnp.mean(x * x, axis=-1, keepdims=True)
    h = (x * lax.rsqrt(ms + EPS) * g_ref[...]) * (1.0 + mod_ref[1:2, :]) + mod_ref[0:1, :]
    hb = h.astype(BF16)
    for ref, a, b in ((sc_ref, OFF_SC, OFF_Q), (q_ref, OFF_Q, OFF_KV), (kv_ref, OFF_KV, OFF_DQ),
                      (dq_ref, OFF_DQ, OFF_DZ), (dz_ref, OFF_DZ, OFF_AB)):
        ref[...] = _dot(hb, w_ref[:, a:b])
    ab_ref[...] = _dot(hb, wab_ref[...])


def _in_projection(x, mod, norm1_g, w_main_b, w_ab_b, layer, tiles_per_batch):
    t = x.shape[0]
    widths = (OFF_Q - OFF_SC, OFF_KV - OFF_Q, OFF_DQ - OFF_KV, OFF_DZ - OFF_DQ, OFF_AB - OFF_DZ, IN_COLS - OFF_AB)
    return pl.pallas_call(
        _in_kernel,
        out_shape=[jax.ShapeDtypeStruct((t, w), F32) for w in widths],
        grid=(t // TM,),
        in_specs=[pl.BlockSpec((TM, D_MODEL), lambda i: (i, 0)),
                  pl.BlockSpec((None, None, 6, D_MODEL), _mod_row_map(layer, tiles_per_batch)),
                  pl.BlockSpec((None, 1, D_MODEL), lambda i: (layer, 0, 0)),
                  pl.BlockSpec((None, D_MODEL, OFF_AB), lambda i: (layer, 0, 0),
                               pipeline_mode=pl.Buffered(1)),
                  pl.BlockSpec((None, D_MODEL, IN_COLS - OFF_AB), lambda i: (layer, 0, 0),
                               pipeline_mode=pl.Buffered(1))],
        out_specs=[pl.BlockSpec((TM, w), lambda i: (i, 0)) for w in widths],
        compiler_params=pltpu.CompilerParams(vmem_limit_bytes=VMEM_LIMIT),
        name="in_proj",
    )(x, mod, norm1_g, w_main_b, w_ab_b)


def _prep_q(q, qg, cos, sin):
    out = []
    for g in range(ATT_DIM // LANES):
        x = _head_rms(q[:, g * LANES:(g + 1) * LANES], qg)
        if cos is not None:
            x = _rope(x, cos, sin)
        out.append(x * (HEAD_DIM ** -0.5))
    return out


def _head_logits(qa, half, key_sets):
    lo = _lo_mask(qa.shape)
    qh = jnp.where(lo if half == 0 else jnp.logical_not(lo), qa, 0.0).astype(BF16)
    logits = []
    for k, _, mask in key_sets:
        s = _dot_nt(qh, k)
        logits.append(s if mask is None else jnp.where(mask, s, NEG))
    return logits


def _head_softmax_pv(logits, key_sets, sink):
    m = jnp.maximum(jnp.max(jnp.concatenate(logits, axis=1), axis=-1, keepdims=True), sink)
    acc = None
    for s, (_, v1, _) in zip(logits, key_sets):
        o = _dot(jnp.exp(s - m).astype(BF16), v1)
        acc = o if acc is None else acc + o
    return acc[:, :LANES] / (acc[:, LANES:] + jnp.exp(sink - m))


def _attend_heads(qs, key_sets_of, sink_ref, layer, o_ref):
    lo = _lo_mask(qs[0].shape)
    heads = [(pair, half) for pair in range(len(qs)) for half in range(2)]
    keys = {kvh: key_sets_of(kvh) for kvh in range(N_KV_HEADS)}
    kvh_of = lambda pair: (2 * pair) // N_GROUP
    logits = _head_logits(qs[0], 0, keys[0])
    done = []
    for idx, (pair, half) in enumerate(heads):
        nxt = None
        if idx + 1 < len(heads):
            npair, nhalf = heads[idx + 1]
            nxt = _head_logits(qs[npair], nhalf, keys[kvh_of(npair)])
        done.append(_head_softmax_pv(logits, keys[kvh_of(pair)], sink_ref[layer, 2 * pair + half]))
        logits = nxt
        if half == 1:
            o_ref[:, pair * LANES:(pair + 1) * LANES] = jnp.where(lo, done[-2], done[-1])


def _values_with_ones(v2, kvh):
    return jnp.concatenate([_dup_half(v2, kvh), jnp.ones(v2.shape, F32)], axis=1).astype(BF16)


def _attn_ctx_kernel(sink_ref, q_ref, kv_ref, qg_ref, kg_ref, o_ref, kn_ref, *, layer):
    kv = kv_ref[...]
    kn = _head_rms(kv[:, :LANES], kg_ref[...])
    kn_ref[...] = kn
    v2 = kv[:, LANES:]
    qs = _prep_q(q_ref[...], qg_ref[...], None, None)
    key_sets_of = lambda kvh: [(_dup_half(kn, kvh).astype(BF16), _values_with_ones(v2, kvh), None)]
    _attend_heads(qs, key_sets_of, sink_ref, layer, o_ref)


def _attention_ctx(q, kv, q_norm_g, k_norm_g, attn_sink, layer):
    t = q.shape[0]
    return pl.pallas_call(
        functools.partial(_attn_ctx_kernel, layer=layer),
        out_shape=[jax.ShapeDtypeStruct((t, ATT_DIM), F32), jax.ShapeDtypeStruct((t, KV_DIM), F32)],
        grid=(t // SEQ,),
        in_specs=[pl.BlockSpec(memory_space=pltpu.SMEM),
                  pl.BlockSpec((SEQ, ATT_DIM), lambda b: (b, 0)),
                  pl.BlockSpec((SEQ, 2 * KV_DIM), lambda b: (b, 0)),
                  pl.BlockSpec((None, 1, LANES), lambda b: (layer, 0, 0)),
                  pl.BlockSpec((None, 1, LANES), lambda b: (layer, 0, 0))],
        out_specs=[pl.BlockSpec((SEQ, ATT_DIM), lambda b: (b, 0)),
                   pl.BlockSpec((SEQ, KV_DIM), lambda b: (b, 0))],
        compiler_params=pltpu.CompilerParams(vmem_limit_bytes=VMEM_LIMIT),
        name="attn_ctx",
    )(attn_sink, q, kv, q_norm_g, k_norm_g)


QBLK = 128
WIN_KEYS = QBLK + 2 * WINDOW
PREP_ROWS = 512


def _attn_lat_kernel(sink_ref, q_ref, kv_ref, kc_ref, vc_ref, qg_ref, kg_ref, cos_ref, sin_ref, o_ref,
                     kd_s, vd_s, kcd_s, vcd_s, *, layer):
    i = pl.program_id(1)
    n = DEC_SEQ

    @pl.when(i == 0)
    def _():
        for kvh in range(N_KV_HEADS):
            for s in (kd_s, vd_s):
                zpad = jnp.zeros((WINDOW, s.shape[-1]), BF16)
                s[kvh, 0:WINDOW, :] = zpad
                s[kvh, WINDOW + n:2 * WINDOW + n, :] = zpad
            kcd_s[kvh] = _dup_half(kc_ref[...], kvh).astype(BF16)
            vcd_s[kvh] = _values_with_ones(vc_ref[...], kvh)

        def prep(r, carry):
            r0 = pl.multiple_of(r * PREP_ROWS, PREP_ROWS)
            kv = kv_ref[pl.ds(r0, PREP_ROWS), :]
            kn = _head_rms(kv[:, :LANES], kg_ref[...])
            kr = _rope(kn, cos_ref[pl.ds(r0, PREP_ROWS), :], sin_ref[pl.ds(r0, PREP_ROWS), :])
            v2 = kv[:, LANES:]
            for kvh in range(N_KV_HEADS):
                kd_s[kvh, pl.ds(WINDOW + r0, PREP_ROWS), :] = _dup_half(kr, kvh).astype(BF16)
                vd_s[kvh, pl.ds(WINDOW + r0, PREP_ROWS), :] = _values_with_ones(v2, kvh)
            return carry

        lax.fori_loop(0, n // PREP_ROWS, prep, 0)

    start = pl.multiple_of(i * QBLK, QBLK)
    qs = _prep_q(q_ref[...], qg_ref[...], cos_ref[pl.ds(start, QBLK), :], sin_ref[pl.ds(start, QBLK), :])
    rr = lax.broadcasted_iota(jnp.int32, (QBLK, WIN_KEYS), 0)
    jj = lax.broadcasted_iota(jnp.int32, (QBLK, WIN_KEYS), 1)
    kpos = jj + (start - WINDOW)
    valid = (jj >= rr) & (jj <= rr + 2 * WINDOW) & (kpos >= 0) & (kpos < n)
    key_sets_of = lambda kvh: [
        (kcd_s[kvh], vcd_s[kvh], None),
        (kd_s[kvh, pl.ds(start, WIN_KEYS), :], vd_s[kvh, pl.ds(start, WIN_KEYS), :], valid)]
    _attend_heads(qs, key_sets_of, sink_ref, layer, o_ref)


def _attention_lat(q, kv, cache_k, cache_v, q_norm_g, k_norm_g, attn_sink, cos_t, sin_t, layer):
    t = q.shape[0]
    n = DEC_SEQ
    nblk = n // QBLK
    return pl.pallas_call(
        functools.partial(_attn_lat_kernel, layer=layer),
        out_shape=jax.ShapeDtypeStruct((t, ATT_DIM), F32),
        grid=(DEC_BATCH, nblk),
        in_specs=[pl.BlockSpec(memory_space=pltpu.SMEM),
                  pl.BlockSpec((QBLK, ATT_DIM), lambda b, i: (b * nblk + i, 0)),
                  pl.BlockSpec((n, 2 * KV_DIM), lambda b, i: (b, 0)),
                  pl.BlockSpec((None, None, PAST_LEN, KV_DIM), lambda b, i: (b, layer, 0, 0)),
                  pl.BlockSpec((None, None, PAST_LEN, KV_DIM), lambda b, i: (b, layer, 0, 0)),
                  pl.BlockSpec((None, 1, LANES), lambda b, i: (layer, 0, 0)),
                  pl.BlockSpec((None, 1, LANES), lambda b, i: (layer, 0, 0)),
                  pl.BlockSpec((n, LANES), lambda b, i: (0, 0)),
                  pl.BlockSpec((n, LANES), lambda b, i: (0, 0))],
        out_specs=pl.BlockSpec((QBLK, ATT_DIM), lambda b, i: (b * nblk + i, 0)),
        scratch_shapes=[pltpu.VMEM((N_KV_HEADS, n + 2 * WINDOW, LANES), BF16),
                        pltpu.VMEM((N_KV_HEADS, n + 2 * WINDOW, 2 * LANES), BF16),
                        pltpu.VMEM((N_KV_HEADS, PAST_LEN, LANES), BF16),
                        pltpu.VMEM((N_KV_HEADS, PAST_LEN, 2 * LANES), BF16)],
        compiler_params=pltpu.CompilerParams(vmem_limit_bytes=VMEM_LIMIT),
        name="attn_lat",
    )(attn_sink, q, kv, cache_k, cache_v, q_norm_g, k_norm_g, cos_t, sin_t)


DN_ROWS = 128
DN_OUT_ROWS = 256
DN_CTX_SEQS = 4


def _dn_conv(x_ref, w_ref, r0, n):
    x = x_ref[pl.ds(r0, DN_ROWS), :]
    total = x_ref.shape[0]
    above = jnp.where(r0 % n > 0, x_ref[pl.ds(jnp.maximum(r0 - 1, 0), 1), :], 0.0)
    below = jnp.where(r0 % n + DN_ROWS < n, x_ref[pl.ds(jnp.minimum(r0 + DN_ROWS, total - 1), 1), :], 0.0)
    row = lax.broadcasted_iota(jnp.int32, x.shape, 0)
    prev = jnp.where(row == 0, above, pltpu.roll(x, 1, axis=0))
    nxt = jnp.where(row == DN_ROWS - 1, below, pltpu.roll(x, DN_ROWS - 1, axis=0))
    return _silu(prev * w_ref[0:1, :] + x * w_ref[1:2, :] + nxt * w_ref[2:3, :])


N_UNITS = 4
UW = N_UNITS * CHUNK
N_LEVELS = 6
M_INCL, M_STRICT, M_EYE = N_LEVELS, N_LEVELS + 1, N_LEVELS + 2
PREP_BLOCK = 8


def _dn_kernel(q_ref, k_ref, v_ref, z_ref, ab_ref, wq_ref, wk_ref, wv_ref, pa_ref, pdt_ref, ng_ref, s0_ref,
               o_ref, sout_ref, qs, ks, vs, gbs, pfs, r_s, au_s, m_s, qa_s, sh_s, el_s, st, mc, ex_s, *, n):
    nc = n // CHUNK
    nseq = q_ref.shape[0] // n
    pblk = min(PREP_BLOCK, nseq * nc)

    def prep_rows(b, carry):
        r0 = pl.multiple_of(b * DN_ROWS, DN_ROWS)
        rows = pl.ds(r0, DN_ROWS)
        qs[rows, :] = _head_l2n(_dn_conv(q_ref, wq_ref, r0, n)) * (DN_HEAD_DIM ** -0.5)
        ks[rows, :] = _head_l2n(_dn_conv(k_ref, wk_ref, r0, n))
        vs[rows, :] = _dn_conv(v_ref, wv_ref, r0, n)
        ab = ab_ref[rows, :]
        lane = lax.broadcasted_iota(jnp.int32, ab.shape, 1)
        gb = jnp.where(lane < N_UNITS, -jnp.exp(pa_ref[...]) * _softplus(ab + pdt_ref[...]), _sigmoid(ab))
        gbs[rows, :] = gb
        row_in_chunk = lax.broadcasted_iota(jnp.int32, ab.shape, 0) % CHUNK
        pref = gb
        for sh in (1, 2, 4, 8, 16, 32):
            pref = pref + jnp.where(row_in_chunk >= sh, pltpu.roll(pref, sh, axis=0), 0.0)
        pfs[rows, :] = pref
        return carry

    lax.fori_loop(0, nseq * n // DN_ROWS, prep_rows, 0)

    @pl.when((pl.program_id(0) == 0) & (pl.program_id(1) == 0))
    def _():
        as_f32 = lambda m: jnp.where(m, 1.0, 0.0).astype(F32)
        ii = lax.broadcasted_iota(jnp.int32, (CHUNK, UW), 0)
        ll = lax.broadcasted_iota(jnp.int32, (CHUNK, UW), 1)
        unit_of_lane, jj = ll // CHUNK, ll % CHUNK
        fwd, bwd = unit_of_lane < 2, unit_of_lane >= 2
        for k in range(N_LEVELS):
            rk, ck = ii >> k, jj >> k
            lower = (rk - ck == 1) & ((rk & 1) == 1)
            upper = (ck - rk == 1) & ((ck & 1) == 1)
            mc[k] = as_f32((fwd & lower) | (bwd & upper))
        mc[M_INCL] = as_f32((fwd & (ii >= jj)) | (bwd & (ii <= jj)))
        mc[M_STRICT] = as_f32((fwd & (ii > jj)) | (bwd & (ii < jj)))
        mc[M_EYE] = as_f32(ii == jj)
        ec = lax.broadcasted_iota(jnp.int32, (LANES, 2 * UW), 0)
        el_ = lax.broadcasted_iota(jnp.int32, (LANES, 2 * UW), 1)
        ex_s[...] = as_f32(ec == (el_ // UW) * N_UNITS + (el_ % UW) // CHUNK).astype(BF16)

    for q in range(nseq):
        st[q] = jnp.concatenate([s0_ref[q, u // 2, u % 2] for u in range(N_UNITS)], axis=1)
    lo = _lo_mask((CHUNK, LANES))
    lane_c = lax.broadcasted_iota(jnp.int32, (CHUNK, LANES), 1)

    def spread_columns(m):
        hi = m.astype(BF16)
        r1 = m - hi.astype(F32)
        mid = r1.astype(BF16)
        low = (r1 - mid.astype(F32)).astype(BF16)
        e = ex_s[...]
        return _dot(hi, e) + _dot(mid, e) + _dot(low, e)

    def pair_block_diag(x):
        zero = jnp.zeros((CHUNK, LANES), BF16)
        return jnp.concatenate([jnp.where(lo, x, zero), jnp.where(lo, zero, x)], axis=0)

    def unit_dot(a, xb):
        return jnp.concatenate([_dot(a[:, h * LANES:(h + 1) * LANES], pair_block_diag(xb[:, h * LANES:(h + 1) * LANES]))
                                for h in range(2)], axis=1)

    def unit_rows(m):
        t = m.T
        return jnp.concatenate([t[u:u + 1, :] for u in range(N_UNITS)], axis=1)

    dup = lambda x: jnp.concatenate([x, x], axis=1)

    def chunk_prep(blk, carry):
        cs = [blk * pblk + s for s in range(pblk)]
        rows = [pl.ds(pl.multiple_of(c * CHUNK, CHUNK), CHUNK) for c in cs]
        each = lambda f, *lists: [f(*xs) for xs in zip(*lists)]

        gx = [gbs[r, :] for r in rows]
        pc = [pfs[r, :] for r in rows]
        tot = [p[CHUNK - 1:CHUNK, :] for p in pc]
        gcm = each(lambda g, p, t: jnp.where(lane_c < 2, p, t - p + g), gx, pc, tot)
        rsm = each(lambda g, p, t: jnp.where(lane_c < 2, t - p, p - g), gx, pc, tot)
        spread = each(lambda m, g: spread_columns(jnp.where(lane_c < N_UNITS, m, g)), gcm, gx)
        gc4 = [x[:, :UW] for x in spread]
        b4 = [x[:, UW:] for x in spread]
        e_gc = each(jnp.exp, gc4)
        dec = each(lambda c, m: jnp.exp(jnp.where(mc[M_INCL] > 0.0, c - unit_rows(m), NEG)), gc4, gcm)
        kt = [ks[r, :].T for r in rows]
        kt_pair = each(lambda t: jnp.concatenate([t[0:CHUNK], t[CHUNK:]], axis=1), kt)
        kt_bd = each(lambda t: pair_block_diag(t.astype(BF16)), kt_pair)
        gram = each(lambda r, b: dup(_dot(ks[r, :].astype(BF16), b)), rows, kt_bd)
        qk = each(lambda r, b: dup(_dot(qs[r, :].astype(BF16), b)), rows, kt_bd)
        lmat = each(lambda g, b, d: g * b * d * mc[M_STRICT], gram, b4, dec)
        amat = each(lambda m, d: (m * d).astype(BF16), qk, dec)
        t = each(lambda l: mc[M_EYE] - l * mc[0], lmat)
        for k in range(1, N_LEVELS):
            lb = each(lambda l: (l * mc[k]).astype(BF16), lmat)
            tb = each(lambda x: x.astype(BF16), t)
            x = each(lambda a, b: unit_dot(a, b).astype(BF16), tb, lb)
            t = each(lambda t0, a, b: t0 - unit_dot(a, b), t, x, tb)
        tm = each(lambda x: (x - mc[M_EYE]).astype(BF16), t)
        k4 = [dup(ks[r, :]) for r in rows]
        ru = each(lambda r, b: dup(vs[r, :]) * b, rows, b4)
        rw = each(lambda k, b, e: k * b * e, k4, b4, e_gc)
        ub = each(lambda x, tmi: (x + unit_dot(tmi, x.astype(BF16))).astype(BF16), ru, tm)
        wb = each(lambda x, tmi: (x + unit_dot(tmi, x.astype(BF16))).astype(BF16), rw, tm)
        kdt = each(lambda t, m: (dup(t) * jnp.exp(unit_rows(m))).astype(BF16), kt_pair, rsm)
        mmat = each(unit_dot, kdt, wb)
        rmat = each(unit_dot, kdt, ub)
        aw = each(unit_dot, amat, wb)
        au = each(unit_dot, amat, ub)
        for i, (c, r) in enumerate(zip(cs, rows)):
            m_s[r, :] = mmat[i].astype(BF16)
            r_s[r, :] = rmat[i]
            qa_s[r, :] = (dup(qs[r, :]) * e_gc[i] - aw[i]).astype(BF16)
            au_s[r, :] = au[i]
            el = jnp.concatenate([e_gc[i][CHUNK - 1:CHUNK, :LANES], e_gc[i][0:1, LANES:]], axis=1)
            el_s[c] = jnp.broadcast_to(el, (8, UW))
        return carry

    lax.fori_loop(0, nseq * nc // pblk, chunk_prep, 0)

    def scan_step(i, carry):
        cf = [q * nc + i for q in range(nseq)]
        cb = [q * nc + nc - 1 - i for q in range(nseq)]
        rf = [pl.ds(pl.multiple_of(c * CHUNK, CHUNK), CHUNK) for c in cf]
        rb = [pl.ds(pl.multiple_of(c * CHUNK, CHUNK), CHUNK) for c in cb]
        mixed = lambda ref, q: jnp.concatenate([ref[rf[q], :LANES], ref[rb[q], LANES:]], axis=1)
        s = [st[q] for q in range(nseq)]
        sb = [x.astype(BF16) for x in s]
        ms = [unit_dot(mixed(m_s, q), sb[q]) for q in range(nseq)]
        for q in range(nseq):
            el = jnp.concatenate([el_s[cf[q]][0:1, :LANES], el_s[cb[q]][0:1, LANES:]], axis=1)
            st[q] = s[q] * el - ms[q] + mixed(r_s, q)
            sh_s[rf[q], :LANES] = sb[q][:, :LANES]
            sh_s[rb[q], LANES:] = sb[q][:, LANES:]
        return carry

    lax.fori_loop(0, nc, scan_step, 0)

    for q in range(nseq):
        s = st[q]
        for u in range(N_UNITS):
            sout_ref[q, u // 2, u % 2] = s[:, u * CHUNK:(u + 1) * CHUNK]

    def finish_rows(b, carry):
        r0 = b * DN_OUT_ROWS
        parts = []
        for c in range(DN_OUT_ROWS // CHUNK):
            rows = pl.ds(pl.multiple_of(r0 + c * CHUNK, CHUNK), CHUNK)
            parts.append(unit_dot(qa_s[rows, :], sh_s[rows, :]) + au_s[rows, :])
        o = jnp.concatenate([p[:, :LANES] + p[:, LANES:] for p in parts], axis=0)
        rows = pl.ds(pl.multiple_of(r0, DN_OUT_ROWS), DN_OUT_ROWS)
        o_ref[rows, :] = _head_rms(o, ng_ref[...]) * _silu(z_ref[rows, :])
        return carry

    lax.fori_loop(0, nseq * n // DN_OUT_ROWS, finish_rows, 0)


def _delta_net(dq, dz, dab, dn_conv_w, pa, pdt, dn_norm_g, s0, layer, s0_layer, n, nseq):
    t = dq.shape[0]
    bsz = t // n
    rows = nseq * n
    nc = n // CHUNK
    conv_spec = lambda off: pl.BlockSpec((None, 3, LANES), lambda b, p: (layer, 0, off + p))
    col_spec = lambda off, bufs=2: pl.BlockSpec((rows, LANES), lambda b, p: (b, off + p),
                                                pipeline_mode=pl.Buffered(bufs))
    state_block = (nseq, 2, 2, DN_HEAD_DIM, DN_HEAD_DIM)
    return pl.pallas_call(
        functools.partial(_dn_kernel, n=n),
        out_shape=[jax.ShapeDtypeStruct((t, DN_DIM), F32),
                   jax.ShapeDtypeStruct((bsz, 2, N_DN_HEADS, DN_HEAD_DIM, DN_HEAD_DIM), F32)],
        grid=(bsz // nseq, N_PAIRS),
        in_specs=[col_spec(0), col_spec(N_PAIRS), col_spec(2 * N_PAIRS), col_spec(0, 1), col_spec(0),
                  conv_spec(0), conv_spec(N_PAIRS), conv_spec(2 * N_PAIRS),
                  pl.BlockSpec((None, None, 1, LANES), lambda b, p: (layer, p, 0, 0)),
                  pl.BlockSpec((None, None, 1, LANES), lambda b, p: (layer, p, 0, 0)),
                  pl.BlockSpec((None, 1, LANES), lambda b, p: (layer, 0, 0)),
                  pl.BlockSpec((nseq, None) + state_block[1:], lambda b, p: (b, s0_layer, 0, p, 0, 0))],
        out_specs=[pl.BlockSpec((rows, LANES), lambda b, p: (b, p)),
                   pl.BlockSpec(state_block, lambda b, p: (b, 0, p, 0, 0))],
        scratch_shapes=[pltpu.VMEM((rows, LANES), F32) for _ in range(5)]
                       + [pltpu.VMEM((rows, UW), F32) for _ in range(2)]
                       + [pltpu.VMEM((rows, UW), BF16) for _ in range(3)]
                       + [pltpu.VMEM((nseq * nc, 8, UW), F32),
                          pltpu.VMEM((nseq, CHUNK, UW), F32),
                          pltpu.VMEM((N_LEVELS + 3, CHUNK, UW), F32),
                          pltpu.VMEM((LANES, 2 * UW), BF16)],
        compiler_params=pltpu.CompilerParams(vmem_limit_bytes=(V7X_VMEM_BYTES * 7) // 8),
        name="delta_net",
    )(dq, dq, dq, dz, dab, dn_conv_w, dn_conv_w, dn_conv_w, pa, pdt, dn_norm_g, s0)


def _out_kernel(x_ref, sc_ref, scp_ref, scn_ref, att_ref, dn_ref, mod_ref, cw_ref, wo_ref, n2_ref,
                wg_ref, wu_ref, wd_ref, o_ref, *, tiles_per_seq):
    t = pl.program_id(0) % tiles_per_seq
    sc = sc_ref[...]
    prod = sc[:, SC_DIM:2 * SC_DIM] * sc[:, 2 * SC_DIM:]
    pv = scp_ref[7:8, :]
    nx = scn_ref[0:1, :]
    prev_row = jnp.where(t > 0, pv[:, SC_DIM:2 * SC_DIM] * pv[:, 2 * SC_DIM:], 0.0)
    next_row = jnp.where(t < tiles_per_seq - 1, nx[:, SC_DIM:2 * SC_DIM] * nx[:, 2 * SC_DIM:], 0.0)
    row = lax.broadcasted_iota(jnp.int32, prod.shape, 0)
    p_prev = jnp.where(row == 0, prev_row, pltpu.roll(prod, 1, axis=0))
    p_next = jnp.where(row == TM - 1, next_row, pltpu.roll(prod, TM - 1, axis=0))
    y_sc = sc[:, :SC_DIM] * (p_prev * cw_ref[0:1, :] + prod * cw_ref[1:2, :] + p_next * cw_ref[2:3, :])
    y = (_dot(y_sc.astype(BF16), wo_ref[0:SC_DIM, :])
         + _dot(att_ref[...].astype(BF16), wo_ref[SC_DIM:SC_DIM + ATT_DIM, :])
         + _dot(dn_ref[...].astype(BF16), wo_ref[SC_DIM + ATT_DIM:, :]))
    x1 = x_ref[...] + mod_ref[2:3, :] * y
    ms = jnp.mean(x1 * x1, axis=-1, keepdims=True)
    h2 = (x1 * lax.rsqrt(ms + EPS) * n2_ref[...]) * (1.0 + mod_ref[4:5, :]) + mod_ref[3:4, :]
    hb = h2.astype(BF16)
    act = _silu(_dot(hb, wg_ref[...])) * _dot(hb, wu_ref[...])
    o_ref[...] = x1 + mod_ref[5:6, :] * _dot(act.astype(BF16), wd_ref[...])


def _output_stage(x, sc, att, dn, mod, sc_conv_w, w_out_b, norm2_g, w_gate_b, w_up_b, w_down_b, layer,
                  tiles_per_batch, tiles_per_seq):
    t = x.shape[0]
    halo = TM // 8
    last = t // 8 - 1
    resident = lambda shape: pl.BlockSpec((None,) + shape, lambda i: (layer, 0, 0), pipeline_mode=pl.Buffered(1))
    return pl.pallas_call(
        functools.partial(_out_kernel, tiles_per_seq=tiles_per_seq),
        out_shape=jax.ShapeDtypeStruct((t, D_MODEL), F32),
        grid=(t // TM,),
        in_specs=[pl.BlockSpec((TM, D_MODEL), lambda i: (i, 0)),
                  pl.BlockSpec((TM, 3 * SC_DIM), lambda i: (i, 0)),
                  pl.BlockSpec((8, 3 * SC_DIM), lambda i: (jnp.maximum(i * halo - 1, 0), 0)),
                  pl.BlockSpec((8, 3 * SC_DIM), lambda i: (jnp.minimum((i + 1) * halo, last), 0)),
                  pl.BlockSpec((TM, ATT_DIM), lambda i: (i, 0)),
                  pl.BlockSpec((TM, DN_DIM), lambda i: (i, 0)),
                  pl.BlockSpec((None, None, 6, D_MODEL), _mod_row_map(layer, tiles_per_batch)),
                  pl.BlockSpec((None, 3, SC_DIM), lambda i: (layer, 0, 0)),
                  resident((MIX_DIM, D_MODEL)),
                  pl.BlockSpec((None, 1, D_MODEL), lambda i: (layer, 0, 0)),
                  resident((D_MODEL, D_FF)), resident((D_MODEL, D_FF)), resident((D_FF, D_MODEL))],
        out_specs=pl.BlockSpec((TM, D_MODEL), lambda i: (i, 0)),
        compiler_params=pltpu.CompilerParams(vmem_limit_bytes=VMEM_LIMIT),
        name="out_stage",
    )(x, sc, sc, sc, att, dn, mod, sc_conv_w, w_out_b, norm2_g, w_gate_b, w_up_b, w_down_b)


def _rope_tables():
    pos = jnp.arange(DEC_SEQ)
    half = HEAD_DIM // 4
    inv = 1.0 / (ROPE_BASE ** (jnp.arange(half, dtype=F32) / half))
    ang_r = (pos // GRID_W).astype(F32)[:, None] * inv
    ang_c = (pos % GRID_W).astype(F32)[:, None] * inv
    cos = jnp.concatenate([jnp.cos(ang_r)] * 2 + [jnp.cos(ang_c)] * 2, axis=-1)
    sin = jnp.concatenate([-jnp.sin(ang_r), jnp.sin(ang_r), -jnp.sin(ang_c), jnp.sin(ang_c)], axis=-1)
    return jnp.tile(cos, (1, 2)), jnp.tile(sin, (1, 2))


def _pack_w_in(w_in):
    a0 = OFF_AB
    b0 = OFF_AB + 2 * N_DN_HEADS
    blocks = []
    for p in range(N_PAIRS):
        idx = [a0 + d * N_DN_HEADS + 2 * p + j for d in range(2) for j in range(2)]
        idx += [b0 + d * N_DN_HEADS + 2 * p + j for d in range(2) for j in range(2)]
        blocks += [w_in[:, :, c:c + 1] for c in idx]
        blocks.append(jnp.zeros((DEPTH, D_MODEL, LANES - len(idx)), w_in.dtype))
    return w_in[:, :, :OFF_AB].astype(BF16), jnp.concatenate(blocks, axis=2).astype(BF16)


def _pair_lanes(p):
    x = p.reshape(DEPTH, 2, N_PAIRS, 2).transpose(0, 2, 1, 3).reshape(DEPTH, N_PAIRS, 1, 4)
    return jnp.pad(x, ((0, 0), (0, 0), (0, 0), (0, LANES - 4)))


def kernel(x_prompt, x_sample, cache_k, cache_v, state_delta, c, c_ctx, w_in, w_out, ada_w, ada_b,
           norm1_g, norm2_g, sc_conv_w, dn_conv_w, q_norm_g, k_norm_g, attn_sink, dn_A_log,
           dn_dt_bias, dn_norm_g, w_gate, w_up, w_down):
    w_main_b, w_ab_b = _pack_w_in(w_in)
    w_out_b, w_gate_b, w_up_b, w_down_b = (w.astype(BF16) for w in (w_out, w_gate, w_up, w_down))
    n1 = norm1_g.reshape(DEPTH, 1, D_MODEL)
    n2 = norm2_g.reshape(DEPTH, 1, D_MODEL)
    qg = jnp.tile(q_norm_g, (1, 2)).reshape(DEPTH, 1, LANES)
    kg = jnp.tile(k_norm_g, (1, 2)).reshape(DEPTH, 1, LANES)
    ng = jnp.tile(dn_norm_g, (1, 2)).reshape(DEPTH, 1, LANES)
    pa, pdt = _pair_lanes(dn_A_log), _pair_lanes(dn_dt_bias)
    cos_t, sin_t = _rope_tables()
    ck = cache_k.reshape(DEC_BATCH, DEPTH, PAST_LEN, KV_DIM)
    cv = cache_v.reshape(DEC_BATCH, DEPTH, PAST_LEN, KV_DIM)
    zero_state = jnp.zeros((BATCH, 1, 2, N_DN_HEADS, DN_HEAD_DIM, DN_HEAD_DIM), F32)

    cvecs = jnp.concatenate([c_ctx[None, :], c, jnp.zeros((MOD_ROWS - 1 - DEC_BATCH, D_MODEL), F32)], axis=0)
    mod = _modulation(cvecs, ada_w, ada_b).reshape(DEPTH, MOD_ROWS, 6, D_MODEL)

    xp = x_prompt.reshape(BATCH * SEQ, D_MODEL)
    xs = x_sample.reshape(DEC_BATCH * DEC_SEQ, D_MODEL)
    new_k, new_v, new_s = [], [], []
    for l in range(DEPTH):
        out_args = (mod, sc_conv_w, w_out_b, n2, w_gate_b, w_up_b, w_down_b, l)
        sc, q, kv, dq, dz, dab = _in_projection(xp, mod, n1, w_main_b, w_ab_b, l, None)
        att, kn = _attention_ctx(q, kv, qg, kg, attn_sink, l)
        dn, s_out = _delta_net(dq, dz, dab, dn_conv_w, pa, pdt, ng, zero_state, l, 0, SEQ, DN_CTX_SEQS)
        xp = _output_stage(xp, sc, att, dn, *out_args, None, SEQ // TM)
        new_k.append(kn.reshape(BATCH, SEQ, N_KV_HEADS, HEAD_DIM))
        new_v.append(kv[:, KV_DIM:].reshape(BATCH, SEQ, N_KV_HEADS, HEAD_DIM))
        new_s.append(s_out)
        sc, q, kv, dq, dz, dab = _in_projection(xs, mod, n1, w_main_b, w_ab_b, l, DEC_SEQ // TM)
        att = _attention_lat(q, kv, ck, cv, qg, kg, attn_sink, cos_t, sin_t, l)
        dn, _ = _delta_net(dq, dz, dab, dn_conv_w, pa, pdt, ng, state_delta, l, l, DEC_SEQ, 1)
        xs = _output_stage(xs, sc, att, dn, *out_args, DEC_SEQ // TM, DEC_SEQ // TM)
    return (xp.reshape(BATCH, SEQ, D_MODEL), xs.reshape(DEC_BATCH, DEC_SEQ, D_MODEL),
            jnp.stack(new_k, axis=1), jnp.stack(new_v, axis=1), jnp.stack(new_s, axis=1))
```

```python
import functools

import numpy as np
import jax
import jax.numpy as jnp
from jax import lax
from jax.experimental import pallas as pl
from jax.experimental.pallas import tpu as pltpu

F32 = jnp.float32
BF16 = jnp.bfloat16

D_MODEL = 1024
BATCH = 16
SEQ = 256
DEPTH = 2
DEC_BATCH = 4
DEC_SEQ = 4096
PAST_LEN = 512
GRID_W = 64
SC_DIM = 256
N_Q_HEADS = 8
N_KV_HEADS = 2
N_GROUP = N_Q_HEADS // N_KV_HEADS
HEAD_DIM = 64
ATT_DIM = N_Q_HEADS * HEAD_DIM
KV_DIM = N_KV_HEADS * HEAD_DIM
WINDOW = 128
N_DN_HEADS = 4
DN_HEAD_DIM = 64
DN_DIM = N_DN_HEADS * DN_HEAD_DIM
CHUNK = 64
MIX_DIM = SC_DIM + ATT_DIM + DN_DIM
D_FF = -(-8 * D_MODEL // (3 * 256)) * 256
ROPE_BASE = 10000.0
EPS = 1e-6
NEG = -1e30
LOG2E = 1.4426950408889634

LANES = 128
V7X_VMEM_BYTES = 64 * 1024 * 1024
VMEM_LIMIT = (V7X_VMEM_BYTES * 3) // 4

OFF_SC, OFF_Q, OFF_KV, OFF_DQ, OFF_DZ, OFF_AB = 0, 768, 1280, 1536, 2304, 2560
IN_COLS = OFF_AB + 2 * LANES
N_PAIRS = N_DN_HEADS // 2
MOD_ROWS = 8
TM = 256
TM_IN = 512


def _sigmoid(x):
    return 0.5 * jnp.tanh(0.5 * x) + 0.5


def _silu(x):
    return x * _sigmoid(x)


def _softplus(x):
    return jnp.maximum(x, 0.0) + jnp.log1p(jnp.exp(-jnp.abs(x)))


def _dot(a, b):
    return jnp.dot(a, b, preferred_element_type=F32)


def _dot_nt(a, b):
    return lax.dot_general(a, b, (((1,), (1,)), ((), ())), preferred_element_type=F32)


def _lo_mask(shape):
    return lax.broadcasted_iota(jnp.int32, shape, 1) % LANES < HEAD_DIM


def _half_sums(xx, lo):
    s_lo = jnp.sum(jnp.where(lo, xx, 0.0), axis=-1, keepdims=True)
    s_hi = jnp.sum(jnp.where(lo, 0.0, xx), axis=-1, keepdims=True)
    return jnp.where(lo, s_lo, s_hi)


def _head_rms(x, g_row):
    lo = _lo_mask(x.shape)
    ms = _half_sums(x * x, lo) * (1.0 / HEAD_DIM)
    return x * lax.rsqrt(ms + EPS) * g_row


def _head_l2n(x):
    lo = _lo_mask(x.shape)
    return x * lax.rsqrt(_half_sums(x * x, lo) + EPS)


def _rope(x, cos, sin):
    n = x.shape[1]
    lane = lax.broadcasted_iota(jnp.int32, x.shape, 1)
    swapped = jnp.where((lane & 16) == 0, pltpu.roll(x, n - 16, axis=1), pltpu.roll(x, 16, axis=1))
    return x * cos + swapped * sin


def _dup_half(x, half):
    lo = _lo_mask(x.shape)
    r = pltpu.roll(x, HEAD_DIM, axis=1)
    return jnp.where(lo, x, r) if half == 0 else jnp.where(lo, r, x)


def _mod_kernel(c_ref, w_ref, b_ref, o_ref):
    s = _silu(c_ref[...])
    o_ref[...] = _dot(s.astype(BF16), w_ref[...].astype(BF16)) + b_ref[...]


def _modulation(cvecs, ada_w, ada_b):
    nblk = 6
    return pl.pallas_call(
        _mod_kernel,
        out_shape=jax.ShapeDtypeStruct((DEPTH, MOD_ROWS, 6 * D_MODEL), F32),
        grid=(DEPTH, nblk),
        in_specs=[pl.BlockSpec((MOD_ROWS, D_MODEL), lambda l, j: (0, 0)),
                  pl.BlockSpec((None, D_MODEL, D_MODEL), lambda l, j: (l, 0, j)),
                  pl.BlockSpec((None, 1, D_MODEL), lambda l, j: (l, 0, j))],
        out_specs=pl.BlockSpec((None, MOD_ROWS, D_MODEL), lambda l, j: (l, 0, j)),
        compiler_params=pltpu.CompilerParams(vmem_limit_bytes=VMEM_LIMIT),
        name="adaln_mod",
    )(cvecs, ada_w, ada_b.reshape(DEPTH, 1, 6 * D_MODEL))


def _mod_row_map(layer, tiles_per_batch):
    if tiles_per_batch is None:
        return lambda i: (layer, 0, 0, 0)
    return lambda i: (layer, 1 + i // tiles_per_batch, 0, 0)


def _prep_q(q, qg, cos, sin):
    out = []
    for g in range(ATT_DIM // LANES):
        x = _head_rms(q[:, g * LANES:(g + 1) * LANES], qg)
        if cos is not None:
            x = _rope(x, cos, sin)
        out.append((x * (LOG2E * HEAD_DIM ** -0.5)).astype(BF16))
    return jnp.concatenate(out, axis=1)


def _in_kernel(x_ref, mod_ref, g_ref, w_ref, wab_ref, qg_ref, cos_ref, sin_ref,
               sc_ref, q_ref, kv_ref, dq_ref, dz_ref, ab_ref, *, rotary):
    x = x_ref[...]
    ms = jnp.mean(x * x, axis=-1, keepdims=True)
    h = (x * lax.rsqrt(ms + EPS) * g_ref[...]) * (1.0 + mod_ref[1:2, :]) + mod_ref[0:1, :]
    hb = h.astype(BF16)
    q = _dot(hb, w_ref[:, OFF_Q:OFF_KV])
    q_ref[...] = _prep_q(q, qg_ref[...], cos_ref[...] if rotary else None, sin_ref[...] if rotary else None)
    for ref, a, b in ((sc_ref, OFF_SC, OFF_Q), (kv_ref, OFF_KV, OFF_DQ), (dq_ref, OFF_DQ, OFF_DZ),
                      (dz_ref, OFF_DZ, OFF_AB)):
        ref[...] = _dot(hb, w_ref[:, a:b])
    ab_ref[...] = _dot(hb, wab_ref[...])


def _in_projection(x, mod, norm1_g, w_main_b, w_ab_b, q_norm_g, cos_t, sin_t, layer, tiles_per_batch):
    t = x.shape[0]
    widths = (OFF_Q - OFF_SC, OFF_KV - OFF_Q, OFF_DQ - OFF_KV, OFF_DZ - OFF_DQ, OFF_AB - OFF_DZ, IN_COLS - OFF_AB)
    rotary = tiles_per_batch is not None
    pos_map = (lambda i: (i % tiles_per_batch, 0)) if rotary else (lambda i: (0, 0))
    return pl.pallas_call(
        functools.partial(_in_kernel, rotary=rotary),
        out_shape=[jax.ShapeDtypeStruct((t, w), BF16 if k == 1 else F32) for k, w in enumerate(widths)],
        grid=(t // TM_IN,),
        in_specs=[pl.BlockSpec((TM_IN, D_MODEL), lambda i: (i, 0)),
                  pl.BlockSpec((None, None, 6, D_MODEL), _mod_row_map(layer, tiles_per_batch)),
                  pl.BlockSpec((None, 1, D_MODEL), lambda i: (layer, 0, 0)),
                  pl.BlockSpec((None, D_MODEL, OFF_AB), lambda i: (layer, 0, 0),
                               pipeline_mode=pl.Buffered(1)),
                  pl.BlockSpec((None, D_MODEL, IN_COLS - OFF_AB), lambda i: (layer, 0, 0),
                               pipeline_mode=pl.Buffered(1)),
                  pl.BlockSpec((None, 1, LANES), lambda i: (layer, 0, 0)),
                  pl.BlockSpec((TM_IN, LANES), pos_map),
                  pl.BlockSpec((TM_IN, LANES), pos_map)],
        out_specs=[pl.BlockSpec((TM_IN, w), lambda i: (i, 0)) for w in widths],
        compiler_params=pltpu.CompilerParams(vmem_limit_bytes=VMEM_LIMIT),
        name="in_proj",
    )(x, mod, norm1_g, w_main_b, w_ab_b, q_norm_g, cos_t, sin_t)


def _head_logits(qa, half, key_sets):
    lo = _lo_mask(qa.shape)
    qh = jnp.where(lo if half == 0 else jnp.logical_not(lo), qa, jnp.zeros_like(qa))
    logits = []
    for k, _, mask in key_sets:
        s = _dot_nt(qh, k)
        logits.append(s if mask is None else jnp.where(mask, s, NEG))
    return logits


def _head_softmax_pv(logits, key_sets, sink):
    m = jnp.maximum(jnp.max(jnp.concatenate(logits, axis=1), axis=-1, keepdims=True), sink)
    acc = None
    for s, (_, v1, _) in zip(logits, key_sets):
        o = _dot(jnp.exp2(s - m).astype(BF16), v1)
        acc = o if acc is None else acc + o
    return acc[:, :LANES] / (acc[:, LANES:] + jnp.exp2(sink - m))


def _attend_heads(qs, key_sets_of, sink_ref, layer, o_ref):
    lo = _lo_mask(qs[0].shape)
    heads = [(pair, half) for pair in range(len(qs)) for half in range(2)]
    keys = {kvh: key_sets_of(kvh) for kvh in range(N_KV_HEADS)}
    kvh_of = lambda pair: (2 * pair) // N_GROUP
    logits = _head_logits(qs[0], 0, keys[0])
    done = []
    for idx, (pair, half) in enumerate(heads):
        nxt = None
        if idx + 1 < len(heads):
            npair, nhalf = heads[idx + 1]
            nxt = _head_logits(qs[npair], nhalf, keys[kvh_of(npair)])
        done.append(_head_softmax_pv(logits, keys[kvh_of(pair)], sink_ref[layer, 2 * pair + half] * LOG2E))
        logits = nxt
        if half == 1:
            o_ref[:, pair * LANES:(pair + 1) * LANES] = jnp.where(lo, done[-2], done[-1])


def _values_with_ones(v2, kvh):
    return jnp.concatenate([_dup_half(v2, kvh), jnp.ones(v2.shape, F32)], axis=1).astype(BF16)


def _attn_ctx_kernel(sink_ref, q_ref, kv_ref, kg_ref, o_ref, kn_ref, *, layer):
    kv = kv_ref[...]
    kn = _head_rms(kv[:, :LANES], kg_ref[...])
    kn_ref[...] = kn
    v2 = kv[:, LANES:]
    qs = [q_ref[:, g * LANES:(g + 1) * LANES] for g in range(ATT_DIM // LANES)]
    key_sets_of = lambda kvh: [(_dup_half(kn, kvh).astype(BF16), _values_with_ones(v2, kvh), None)]
    _attend_heads(qs, key_sets_of, sink_ref, layer, o_ref)


def _attention_ctx(q, kv, k_norm_g, attn_sink, layer):
    t = q.shape[0]
    return pl.pallas_call(
        functools.partial(_attn_ctx_kernel, layer=layer),
        out_shape=[jax.ShapeDtypeStruct((t, ATT_DIM), F32), jax.ShapeDtypeStruct((t, KV_DIM), F32)],
        grid=(t // SEQ,),
        in_specs=[pl.BlockSpec(memory_space=pltpu.SMEM),
                  pl.BlockSpec((SEQ, ATT_DIM), lambda b: (b, 0)),
                  pl.BlockSpec((SEQ, 2 * KV_DIM), lambda b: (b, 0)),
                  pl.BlockSpec((None, 1, LANES), lambda b: (layer, 0, 0))],
        out_specs=[pl.BlockSpec((SEQ, ATT_DIM), lambda b: (b, 0)),
                   pl.BlockSpec((SEQ, KV_DIM), lambda b: (b, 0))],
        compiler_params=pltpu.CompilerParams(vmem_limit_bytes=VMEM_LIMIT),
        name="attn_ctx",
    )(attn_sink, q, kv, k_norm_g)


QBLK = 128
WIN_KEYS = QBLK + 2 * WINDOW
PREP_ROWS = 512


def _attn_lat_kernel(sink_ref, q_ref, kv_ref, kc_ref, vc_ref, kg_ref, cos_ref, sin_ref, o_ref,
                     kd_s, vd_s, kcd_s, vcd_s, *, layer):
    i = pl.program_id(1)
    n = DEC_SEQ

    @pl.when(i == 0)
    def _():
        for kvh in range(N_KV_HEADS):
            for s in (kd_s, vd_s):
                zpad = jnp.zeros((WINDOW, s.shape[-1]), BF16)
                s[kvh, 0:WINDOW, :] = zpad
                s[kvh, WINDOW + n:2 * WINDOW + n, :] = zpad
            kcd_s[kvh] = _dup_half(kc_ref[...], kvh).astype(BF16)
            vcd_s[kvh] = _values_with_ones(vc_ref[...], kvh)

        def prep(r, carry):
            r0 = pl.multiple_of(r * PREP_ROWS, PREP_ROWS)
            kv = kv_ref[pl.ds(r0, PREP_ROWS), :]
            kn = _head_rms(kv[:, :LANES], kg_ref[...])
            kr = _rope(kn, cos_ref[pl.ds(r0, PREP_ROWS), :], sin_ref[pl.ds(r0, PREP_ROWS), :])
            v2 = kv[:, LANES:]
            for kvh in range(N_KV_HEADS):
                kd_s[kvh, pl.ds(WINDOW + r0, PREP_ROWS), :] = _dup_half(kr, kvh).astype(BF16)
                vd_s[kvh, pl.ds(WINDOW + r0, PREP_ROWS), :] = _values_with_ones(v2, kvh)
            return carry

        lax.fori_loop(0, n // PREP_ROWS, prep, 0)

    start = pl.multiple_of(i * QBLK, QBLK)
    qs = [q_ref[:, g * LANES:(g + 1) * LANES] for g in range(ATT_DIM // LANES)]
    rr = lax.broadcasted_iota(jnp.int32, (QBLK, WIN_KEYS), 0)
    jj = lax.broadcasted_iota(jnp.int32, (QBLK, WIN_KEYS), 1)
    kpos = jj + (start - WINDOW)
    valid = (jj >= rr) & (jj <= rr + 2 * WINDOW) & (kpos >= 0) & (kpos < n)
    key_sets_of = lambda kvh: [
        (kcd_s[kvh], vcd_s[kvh], None),
        (kd_s[kvh, pl.ds(start, WIN_KEYS), :], vd_s[kvh, pl.ds(start, WIN_KEYS), :], valid)]
    _attend_heads(qs, key_sets_of, sink_ref, layer, o_ref)


def _attention_lat(q, kv, cache_k, cache_v, k_norm_g, attn_sink, cos_t, sin_t, layer):
    t = q.shape[0]
    n = DEC_SEQ
    nblk = n // QBLK
    return pl.pallas_call(
        functools.partial(_attn_lat_kernel, layer=layer),
        out_shape=jax.ShapeDtypeStruct((t, ATT_DIM), F32),
        grid=(DEC_BATCH, nblk),
        in_specs=[pl.BlockSpec(memory_space=pltpu.SMEM),
                  pl.BlockSpec((QBLK, ATT_DIM), lambda b, i: (b * nblk + i, 0)),
                  pl.BlockSpec((n, 2 * KV_DIM), lambda b, i: (b, 0)),
                  pl.BlockSpec((None, None, PAST_LEN, KV_DIM), lambda b, i: (b, layer, 0, 0)),
                  pl.BlockSpec((None, None, PAST_LEN, KV_DIM), lambda b, i: (b, layer, 0, 0)),
                  pl.BlockSpec((None, 1, LANES), lambda b, i: (layer, 0, 0)),
                  pl.BlockSpec((n, LANES), lambda b, i: (0, 0)),
                  pl.BlockSpec((n, LANES), lambda b, i: (0, 0))],
        out_specs=pl.BlockSpec((QBLK, ATT_DIM), lambda b, i: (b * nblk + i, 0)),
        scratch_shapes=[pltpu.VMEM((N_KV_HEADS, n + 2 * WINDOW, LANES), BF16),
                        pltpu.VMEM((N_KV_HEADS, n + 2 * WINDOW, 2 * LANES), BF16),
                        pltpu.VMEM((N_KV_HEADS, PAST_LEN, LANES), BF16),
                        pltpu.VMEM((N_KV_HEADS, PAST_LEN, 2 * LANES), BF16)],
        compiler_params=pltpu.CompilerParams(vmem_limit_bytes=VMEM_LIMIT),
        name="attn_lat",
    )(attn_sink, q, kv, cache_k, cache_v, k_norm_g, cos_t, sin_t)


DN_ROWS = 128
DN_OUT_ROWS = 256
DN_CTX_SEQS = 4


def _dn_conv(x_ref, w_ref, r0, n):
    x = x_ref[pl.ds(r0, DN_ROWS), :]
    total = x_ref.shape[0]
    above = jnp.where(r0 % n > 0, x_ref[pl.ds(jnp.maximum(r0 - 1, 0), 1), :], 0.0)
    below = jnp.where(r0 % n + DN_ROWS < n, x_ref[pl.ds(jnp.minimum(r0 + DN_ROWS, total - 1), 1), :], 0.0)
    row = lax.broadcasted_iota(jnp.int32, x.shape, 0)
    prev = jnp.where(row == 0, above, pltpu.roll(x, 1, axis=0))
    nxt = jnp.where(row == DN_ROWS - 1, below, pltpu.roll(x, DN_ROWS - 1, axis=0))
    return _silu(prev * w_ref[0:1, :] + x * w_ref[1:2, :] + nxt * w_ref[2:3, :])


N_UNITS = 4
UW = N_UNITS * CHUNK
N_LEVELS = 6
M_INCL, M_STRICT, M_EYE = N_LEVELS, N_LEVELS + 1, N_LEVELS + 2
PREP_BLOCK = 8


def _dn_kernel(q_ref, k_ref, v_ref, z_ref, ab_ref, wq_ref, wk_ref, wv_ref, pa_ref, pdt_ref, ng_ref, s0_ref,
               o_ref, sout_ref, qs, ks, vs, gbs, pfs, r_s, au_s, m_s, qa_s, sh_s, el_s, st, mc, ex_s, *, n):
    nc = n // CHUNK
    nseq = q_ref.shape[0] // n
    pblk = min(PREP_BLOCK, nseq * nc)

    def prep_rows(b, carry):
        r0 = pl.multiple_of(b * DN_ROWS, DN_ROWS)
        rows = pl.ds(r0, DN_ROWS)
        qs[rows, :] = _head_l2n(_dn_conv(q_ref, wq_ref, r0, n)) * (DN_HEAD_DIM ** -0.5)
        ks[rows, :] = _head_l2n(_dn_conv(k_ref, wk_ref, r0, n))
        vs[rows, :] = _dn_conv(v_ref, wv_ref, r0, n)
        ab = ab_ref[rows, :]
        lane = lax.broadcasted_iota(jnp.int32, ab.shape, 1)
        gb = jnp.where(lane < N_UNITS, -jnp.exp(pa_ref[...]) * _softplus(ab + pdt_ref[...]), _sigmoid(ab))
        gbs[rows, :] = gb
        row_in_chunk = lax.broadcasted_iota(jnp.int32, ab.shape, 0) % CHUNK
        pref = gb
        for sh in (1, 2, 4, 8, 16, 32):
            pref = pref + jnp.where(row_in_chunk >= sh, pltpu.roll(pref, sh, axis=0), 0.0)
        pfs[rows, :] = pref
        return carry

    lax.fori_loop(0, nseq * n // DN_ROWS, prep_rows, 0)

    @pl.when((pl.program_id(0) == 0) & (pl.program_id(1) == 0))
    def _():
        as_f32 = lambda m: jnp.where(m, 1.0, 0.0).astype(F32)
        ii = lax.broadcasted_iota(jnp.int32, (CHUNK, UW), 0)
        ll = lax.broadcasted_iota(jnp.int32, (CHUNK, UW), 1)
        unit_of_lane, jj = ll // CHUNK, ll % CHUNK
        fwd, bwd = unit_of_lane < 2, unit_of_lane >= 2
        for k in range(N_LEVELS):
            rk, ck = ii >> k, jj >> k
            lower = (rk - ck == 1) & ((rk & 1) == 1)
            upper = (ck - rk == 1) & ((ck & 1) == 1)
            mc[k] = as_f32((fwd & lower) | (bwd & upper))
        mc[M_INCL] = as_f32((fwd & (ii >= jj)) | (bwd & (ii <= jj)))
        mc[M_STRICT] = as_f32((fwd & (ii > jj)) | (bwd & (ii < jj)))
        mc[M_EYE] = as_f32(ii == jj)
        ec = lax.broadcasted_iota(jnp.int32, (LANES, 2 * UW), 0)
        el_ = lax.broadcasted_iota(jnp.int32, (LANES, 2 * UW), 1)
        ex_s[...] = as_f32(ec == (el_ // UW) * N_UNITS + (el_ % UW) // CHUNK).astype(BF16)

    for q in range(nseq):
        st[q] = jnp.concatenate([s0_ref[q, u // 2, u % 2] for u in range(N_UNITS)], axis=1)
    lo = _lo_mask((CHUNK, LANES))
    lane_c = lax.broadcasted_iota(jnp.int32, (CHUNK, LANES), 1)

    def spread_columns(m):
        hi = m.astype(BF16)
        r1 = m - hi.astype(F32)
        mid = r1.astype(BF16)
        low = (r1 - mid.astype(F32)).astype(BF16)
        e = ex_s[...]
        return _dot(hi, e) + _dot(mid, e) + _dot(low, e)

    def pair_block_diag(x):
        zero = jnp.zeros((CHUNK, LANES), BF16)
        return jnp.concatenate([jnp.where(lo, x, zero), jnp.where(lo, zero, x)], axis=0)

    def unit_dot(a, xb):
        return jnp.concatenate([_dot(a[:, h * LANES:(h + 1) * LANES], pair_block_diag(xb[:, h * LANES:(h + 1) * LANES]))
                                for h in range(2)], axis=1)

    def unit_rows(m):
        t = m.T
        return jnp.concatenate([t[u:u + 1, :] for u in range(N_UNITS)], axis=1)

    dup = lambda x: jnp.concatenate([x, x], axis=1)

    def chunk_prep(blk, carry):
        cs = [blk * pblk + s for s in range(pblk)]
        rows = [pl.ds(pl.multiple_of(c * CHUNK, CHUNK), CHUNK) for c in cs]
        each = lambda f, *lists: [f(*xs) for xs in zip(*lists)]

        gx = [gbs[r, :] for r in rows]
        pc = [pfs[r, :] for r in rows]
        tot = [p[CHUNK - 1:CHUNK, :] for p in pc]
        gcm = each(lambda g, p, t: jnp.where(lane_c < 2, p, t - p + g), gx, pc, tot)
        rsm = each(lambda g, p, t: jnp.where(lane_c < 2, t - p, p - g), gx, pc, tot)
        spread = each(lambda m, g: spread_columns(jnp.where(lane_c < N_UNITS, m, g)), gcm, gx)
        gc4 = [x[:, :UW] for x in spread]
        b4 = [x[:, UW:] for x in spread]
        e_gc = each(jnp.exp, gc4)
        dec = each(lambda c, m: jnp.exp(jnp.where(mc[M_INCL] > 0.0, c - unit_rows(m), NEG)), gc4, gcm)
        kt = [ks[r, :].T for r in rows]
        kt_pair = each(lambda t: jnp.concatenate([t[0:CHUNK], t[CHUNK:]], axis=1), kt)
        kt_bd = each(lambda t: pair_block_diag(t.astype(BF16)), kt_pair)
        gram = each(lambda r, b: dup(_dot(ks[r, :].astype(BF16), b)), rows, kt_bd)
        qk = each(lambda r, b: dup(_dot(qs[r, :].astype(BF16), b)), rows, kt_bd)
        lmat = each(lambda g, b, d: g * b * d * mc[M_STRICT], gram, b4, dec)
        amat = each(lambda m, d: (m * d).astype(BF16), qk, dec)
        t = each(lambda l: mc[M_EYE] - l * mc[0], lmat)
        for k in range(1, N_LEVELS):
            lb = each(lambda l: (l * mc[k]).astype(BF16), lmat)
            tb = each(lambda x: x.astype(BF16), t)
            x = each(lambda a, b: unit_dot(a, b).astype(BF16), tb, lb)
            t = each(lambda t0, a, b: t0 - unit_dot(a, b), t, x, tb)
        tm = each(lambda x: (x - mc[M_EYE]).astype(BF16), t)
        k4 = [dup(ks[r, :]) for r in rows]
        ru = each(lambda r, b: dup(vs[r, :]) * b, rows, b4)
        rw = each(lambda k, b, e: k * b * e, k4, b4, e_gc)
        ub = each(lambda x, tmi: (x + unit_dot(tmi, x.astype(BF16))).astype(BF16), ru, tm)
        wb = each(lambda x, tmi: (x + unit_dot(tmi, x.astype(BF16))).astype(BF16), rw, tm)
        kdt = each(lambda t, m: (dup(t) * jnp.exp(unit_rows(m))).astype(BF16), kt_pair, rsm)
        mmat = each(unit_dot, kdt, wb)
        rmat = each(unit_dot, kdt, ub)
        aw = each(unit_dot, amat, wb)
        au = each(unit_dot, amat, ub)
        for i, (c, r) in enumerate(zip(cs, rows)):
            m_s[r, :] = mmat[i].astype(BF16)
            r_s[r, :] = rmat[i]
            qa_s[r, :] = (dup(qs[r, :]) * e_gc[i] - aw[i]).astype(BF16)
            au_s[r, :] = au[i]
            el = jnp.concatenate([e_gc[i][CHUNK - 1:CHUNK, :LANES], e_gc[i][0:1, LANES:]], axis=1)
            el_s[c] = jnp.broadcast_to(el, (8, UW))
        return carry

    lax.fori_loop(0, nseq * nc // pblk, chunk_prep, 0)

    def scan_step(i, carry):
        cf = [q * nc + i for q in range(nseq)]
        cb = [q * nc + nc - 1 - i for q in range(nseq)]
        rf = [pl.ds(pl.multiple_of(c * CHUNK, CHUNK), CHUNK) for c in cf]
        rb = [pl.ds(pl.multiple_of(c * CHUNK, CHUNK), CHUNK) for c in cb]
        mixed = lambda ref, q: jnp.concatenate([ref[rf[q], :LANES], ref[rb[q], LANES:]], axis=1)
        s = [st[q] for q in range(nseq)]
        sb = [x.astype(BF16) for x in s]
        ms = [unit_dot(mixed(m_s, q), sb[q]) for q in range(nseq)]
        for q in range(nseq):
            el = jnp.concatenate([el_s[cf[q]][0:1, :LANES], el_s[cb[q]][0:1, LANES:]], axis=1)
            st[q] = s[q] * el - ms[q] + mixed(r_s, q)
            sh_s[rf[q], :LANES] = sb[q][:, :LANES]
            sh_s[rb[q], LANES:] = sb[q][:, LANES:]
        return carry

    lax.fori_loop(0, nc, scan_step, 0)

    for q in range(nseq):
        s = st[q]
        for u in range(N_UNITS):
            sout_ref[q, u // 2, u % 2] = s[:, u * CHUNK:(u + 1) * CHUNK]

    def finish_rows(b, carry):
        r0 = b * DN_OUT_ROWS
        parts = []
        for c in range(DN_OUT_ROWS // CHUNK):
            rows = pl.ds(pl.multiple_of(r0 + c * CHUNK, CHUNK), CHUNK)
            parts.append(unit_dot(qa_s[rows, :], sh_s[rows, :]) + au_s[rows, :])
        o = jnp.concatenate([p[:, :LANES] + p[:, LANES:] for p in parts], axis=0)
        rows = pl.ds(pl.multiple_of(r0, DN_OUT_ROWS), DN_OUT_ROWS)
        o_ref[rows, :] = _head_rms(o, ng_ref[...]) * _silu(z_ref[rows, :])
        return carry

    lax.fori_loop(0, nseq * n // DN_OUT_ROWS, finish_rows, 0)


def _delta_net(dq, dz, dab, dn_conv_w, pa, pdt, dn_norm_g, s0, layer, s0_layer, n, nseq):
    t = dq.shape[0]
    bsz = t // n
    rows = nseq * n
    nc = n // CHUNK
    conv_spec = lambda off: pl.BlockSpec((None, 3, LANES), lambda b, p: (layer, 0, off + p))
    col_spec = lambda off, bufs=2: pl.BlockSpec((rows, LANES), lambda b, p: (b, off + p),
                                                pipeline_mode=pl.Buffered(bufs))
    state_block = (nseq, 2, 2, DN_HEAD_DIM, DN_HEAD_DIM)
    return pl.pallas_call(
        functools.partial(_dn_kernel, n=n),
        out_shape=[jax.ShapeDtypeStruct((t, DN_DIM), F32),
                   jax.ShapeDtypeStruct((bsz, 2, N_DN_HEADS, DN_HEAD_DIM, DN_HEAD_DIM), F32)],
        grid=(bsz // nseq, N_PAIRS),
        in_specs=[col_spec(0), col_spec(N_PAIRS), col_spec(2 * N_PAIRS), col_spec(0, 1), col_spec(0),
                  conv_spec(0), conv_spec(N_PAIRS), conv_spec(2 * N_PAIRS),
                  pl.BlockSpec((None, None, 1, LANES), lambda b, p: (layer, p, 0, 0)),
                  pl.BlockSpec((None, None, 1, LANES), lambda b, p: (layer, p, 0, 0)),
                  pl.BlockSpec((None, 1, LANES), lambda b, p: (layer, 0, 0)),
                  pl.BlockSpec((nseq, None) + state_block[1:], lambda b, p: (b, s0_layer, 0, p, 0, 0))],
        out_specs=[pl.BlockSpec((rows, LANES), lambda b, p: (b, p)),
                   pl.BlockSpec(state_block, lambda b, p: (b, 0, p, 0, 0))],
        scratch_shapes=[pltpu.VMEM((rows, LANES), F32) for _ in range(5)]
                       + [pltpu.VMEM((rows, UW), F32) for _ in range(2)]
                       + [pltpu.VMEM((rows, UW), BF16) for _ in range(3)]
                       + [pltpu.VMEM((nseq * nc, 8, UW), F32),
                          pltpu.VMEM((nseq, CHUNK, UW), F32),
                          pltpu.VMEM((N_LEVELS + 3, CHUNK, UW), F32),
                          pltpu.VMEM((LANES, 2 * UW), BF16)],
        compiler_params=pltpu.CompilerParams(vmem_limit_bytes=(V7X_VMEM_BYTES * 7) // 8),
        name="delta_net",
    )(dq, dq, dq, dz, dab, dn_conv_w, dn_conv_w, dn_conv_w, pa, pdt, dn_norm_g, s0)


def _out_kernel(x_ref, sc_ref, scp_ref, scn_ref, att_ref, dn_ref, mod_ref, cw_ref, wo_ref, n2_ref,
                wg_ref, wu_ref, wd_ref, o_ref, *, tiles_per_seq):
    t = pl.program_id(0) % tiles_per_seq
    sc = sc_ref[...]
    prod = sc[:, SC_DIM:2 * SC_DIM] * sc[:, 2 * SC_DIM:]
    pv = scp_ref[7:8, :]
    nx = scn_ref[0:1, :]
    prev_row = jnp.where(t > 0, pv[:, SC_DIM:2 * SC_DIM] * pv[:, 2 * SC_DIM:], 0.0)
    next_row = jnp.where(t < tiles_per_seq - 1, nx[:, SC_DIM:2 * SC_DIM] * nx[:, 2 * SC_DIM:], 0.0)
    row = lax.broadcasted_iota(jnp.int32, prod.shape, 0)
    p_prev = jnp.where(row == 0, prev_row, pltpu.roll(prod, 1, axis=0))
    p_next = jnp.where(row == TM - 1, next_row, pltpu.roll(prod, TM - 1, axis=0))
    y_sc = sc[:, :SC_DIM] * (p_prev * cw_ref[0:1, :] + prod * cw_ref[1:2, :] + p_next * cw_ref[2:3, :])
    y = (_dot(y_sc.astype(BF16), wo_ref[0:SC_DIM, :])
         + _dot(att_ref[...].astype(BF16), wo_ref[SC_DIM:SC_DIM + ATT_DIM, :])
         + _dot(dn_ref[...].astype(BF16), wo_ref[SC_DIM + ATT_DIM:, :]))
    x1 = x_ref[...] + mod_ref[2:3, :] * y
    ms = jnp.mean(x1 * x1, axis=-1, keepdims=True)
    h2 = (x1 * lax.rsqrt(ms + EPS) * n2_ref[...]) * (1.0 + mod_ref[4:5, :]) + mod_ref[3:4, :]
    hb = h2.astype(BF16)
    act = _silu(_dot(hb, wg_ref[...])) * _dot(hb, wu_ref[...])
    o_ref[...] = x1 + mod_ref[5:6, :] * _dot(act.astype(BF16), wd_ref[...])


def _output_stage(x, sc, att, dn, mod, sc_conv_w, w_out_b, norm2_g, w_gate_b, w_up_b, w_down_b, layer,
                  tiles_per_batch, tiles_per_seq):
    t = x.shape[0]
    halo = TM // 8
    last = t // 8 - 1
    resident = lambda shape: pl.BlockSpec((None,) + shape, lambda i: (layer, 0, 0), pipeline_mode=pl.Buffered(1))
    return pl.pallas_call(
        functools.partial(_out_kernel, tiles_per_seq=tiles_per_seq),
        out_shape=jax.ShapeDtypeStruct((t, D_MODEL), F32),
        grid=(t // TM,),
        in_specs=[pl.BlockSpec((TM, D_MODEL), lambda i: (i, 0)),
                  pl.BlockSpec((TM, 3 * SC_DIM), lambda i: (i, 0)),
                  pl.BlockSpec((8, 3 * SC_DIM), lambda i: (jnp.maximum(i * halo - 1, 0), 0)),
                  pl.BlockSpec((8, 3 * SC_DIM), lambda i: (jnp.minimum((i + 1) * halo, last), 0)),
                  pl.BlockSpec((TM, ATT_DIM), lambda i: (i, 0)),
                  pl.BlockSpec((TM, DN_DIM), lambda i: (i, 0)),
                  pl.BlockSpec((None, None, 6, D_MODEL), _mod_row_map(layer, tiles_per_batch)),
                  pl.BlockSpec((None, 3, SC_DIM), lambda i: (layer, 0, 0)),
                  resident((MIX_DIM, D_MODEL)),
                  pl.BlockSpec((None, 1, D_MODEL), lambda i: (layer, 0, 0)),
                  resident((D_MODEL, D_FF)), resident((D_MODEL, D_FF)), resident((D_FF, D_MODEL))],
        out_specs=pl.BlockSpec((TM, D_MODEL), lambda i: (i, 0)),
        compiler_params=pltpu.CompilerParams(vmem_limit_bytes=VMEM_LIMIT),
        name="out_stage",
    )(x, sc, sc, sc, att, dn, mod, sc_conv_w, w_out_b, norm2_g, w_gate_b, w_up_b, w_down_b)


def _rope_tables():
    pos = jnp.arange(DEC_SEQ)
    half = HEAD_DIM // 4
    inv = 1.0 / (ROPE_BASE ** (jnp.arange(half, dtype=F32) / half))
    ang_r = (pos // GRID_W).astype(F32)[:, None] * inv
    ang_c = (pos % GRID_W).astype(F32)[:, None] * inv
    cos = jnp.concatenate([jnp.cos(ang_r)] * 2 + [jnp.cos(ang_c)] * 2, axis=-1)
    sin = jnp.concatenate([-jnp.sin(ang_r), jnp.sin(ang_r), -jnp.sin(ang_c), jnp.sin(ang_c)], axis=-1)
    return jnp.tile(cos, (1, 2)), jnp.tile(sin, (1, 2))


def _pack_w_in(w_in):
    a0 = OFF_AB
    b0 = OFF_AB + 2 * N_DN_HEADS
    blocks = []
    for p in range(N_PAIRS):
        idx = [a0 + d * N_DN_HEADS + 2 * p + j for d in range(2) for j in range(2)]
        idx += [b0 + d * N_DN_HEADS + 2 * p + j for d in range(2) for j in range(2)]
        blocks += [w_in[:, :, c:c + 1] for c in idx]
        blocks.append(jnp.zeros((DEPTH, D_MODEL, LANES - len(idx)), w_in.dtype))
    return w_in[:, :, :OFF_AB].astype(BF16), jnp.concatenate(blocks, axis=2).astype(BF16)


def _pair_lanes(p):
    x = p.reshape(DEPTH, 2, N_PAIRS, 2).transpose(0, 2, 1, 3).reshape(DEPTH, N_PAIRS, 1, 4)
    return jnp.pad(x, ((0, 0), (0, 0), (0, 0), (0, LANES - 4)))


def kernel(x_prompt, x_sample, cache_k, cache_v, state_delta, c, c_ctx, w_in, w_out, ada_w, ada_b,
           norm1_g, norm2_g, sc_conv_w, dn_conv_w, q_norm_g, k_norm_g, attn_sink, dn_A_log,
           dn_dt_bias, dn_norm_g, w_gate, w_up, w_down):
    w_main_b, w_ab_b = _pack_w_in(w_in)
    w_out_b, w_gate_b, w_up_b, w_down_b = (w.astype(BF16) for w in (w_out, w_gate, w_up, w_down))
    n1 = norm1_g.reshape(DEPTH, 1, D_MODEL)
    n2 = norm2_g.reshape(DEPTH, 1, D_MODEL)
    qg = jnp.tile(q_norm_g, (1, 2)).reshape(DEPTH, 1, LANES)
    kg = jnp.tile(k_norm_g, (1, 2)).reshape(DEPTH, 1, LANES)
    ng = jnp.tile(dn_norm_g, (1, 2)).reshape(DEPTH, 1, LANES)
    pa, pdt = _pair_lanes(dn_A_log), _pair_lanes(dn_dt_bias)
    cos_t, sin_t = _rope_tables()
    ck = cache_k.reshape(DEC_BATCH, DEPTH, PAST_LEN, KV_DIM)
    cv = cache_v.reshape(DEC_BATCH, DEPTH, PAST_LEN, KV_DIM)
    zero_state = jnp.zeros((BATCH, 1, 2, N_DN_HEADS, DN_HEAD_DIM, DN_HEAD_DIM), F32)

    cvecs = jnp.concatenate([c_ctx[None, :], c, jnp.zeros((MOD_ROWS - 1 - DEC_BATCH, D_MODEL), F32)], axis=0)
    mod = _modulation(cvecs, ada_w, ada_b).reshape(DEPTH, MOD_ROWS, 6, D_MODEL)

    xp = x_prompt.reshape(BATCH * SEQ, D_MODEL)
    xs = x_sample.reshape(DEC_BATCH * DEC_SEQ, D_MODEL)
    new_k, new_v, new_s = [], [], []
    for l in range(DEPTH):
        out_args = (mod, sc_conv_w, w_out_b, n2, w_gate_b, w_up_b, w_down_b, l)
        sc, q, kv, dq, dz, dab = _in_projection(xp, mod, n1, w_main_b, w_ab_b, qg, cos_t, sin_t, l, None)
        att, kn = _attention_ctx(q, kv, kg, attn_sink, l)
        dn, s_out = _delta_net(dq, dz, dab, dn_conv_w, pa, pdt, ng, zero_state, l, 0, SEQ, DN_CTX_SEQS)
        xp = _output_stage(xp, sc, att, dn, *out_args, None, SEQ // TM)
        new_k.append(kn.reshape(BATCH, SEQ, N_KV_HEADS, HEAD_DIM))
        new_v.append(kv[:, KV_DIM:].reshape(BATCH, SEQ, N_KV_HEADS, HEAD_DIM))
        new_s.append(s_out)
        sc, q, kv, dq, dz, dab = _in_projection(xs, mod, n1, w_main_b, w_ab_b, qg, cos_t, sin_t, l, DEC_SEQ // TM_IN)
        att = _attention_lat(q, kv, ck, cv, kg, attn_sink, cos_t, sin_t, l)
        dn, _ = _delta_net(dq, dz, dab, dn_conv_w, pa, pdt, ng, state_delta, l, l, DEC_SEQ, 1)
        xs = _output_stage(xs, sc, att, dn, *out_args, DEC_SEQ // TM, DEC_SEQ // TM)
    return (xp.reshape(BATCH, SEQ, D_MODEL), xs.reshape(DEC_BATCH, DEC_SEQ, D_MODEL),
            jnp.stack(new_k, axis=1), jnp.stack(new_v, axis=1), jnp.stack(new_s, axis=1))
```

```python
import functools

import numpy as np
import jax
import jax.numpy as jnp
from jax import lax
from jax.experimental import pallas as pl
from jax.experimental.pallas import tpu as pltpu

F32 = jnp.float32
BF16 = jnp.bfloat16

D_MODEL = 1024
BATCH = 16
SEQ = 256
DEPTH = 2
DEC_BATCH = 4
DEC_SEQ = 4096
PAST_LEN = 512
GRID_W = 64
SC_DIM = 256
N_Q_HEADS = 8
N_KV_HEADS = 2
N_GROUP = N_Q_HEADS // N_KV_HEADS
HEAD_DIM = 64
ATT_DIM = N_Q_HEADS * HEAD_DIM
KV_DIM = N_KV_HEADS * HEAD_DIM
WINDOW = 128
N_DN_HEADS = 4
DN_HEAD_DIM = 64
DN_DIM = N_DN_HEADS * DN_HEAD_DIM
CHUNK = 64
MIX_DIM = SC_DIM + ATT_DIM + DN_DIM
D_FF = -(-8 * D_MODEL // (3 * 256)) * 256
ROPE_BASE = 10000.0
EPS = 1e-6
NEG = -1e30
LOG2E = 1.4426950408889634

LANES = 128
V7X_VMEM_BYTES = 64 * 1024 * 1024
VMEM_LIMIT = (V7X_VMEM_BYTES * 3) // 4

OFF_SC, OFF_Q, OFF_KV, OFF_DQ, OFF_DZ, OFF_AB = 0, 768, 1280, 1536, 2304, 2560
IN_COLS = OFF_AB + 2 * LANES
N_PAIRS = N_DN_HEADS // 2
MOD_ROWS = 8
TM = 256
TM_LAT = 512
TM_IN = 512


def _sigmoid(x):
    return 0.5 * jnp.tanh(0.5 * x) + 0.5


def _silu(x):
    return x * _sigmoid(x)


def _softplus(x):
    return jnp.maximum(x, 0.0) + jnp.log1p(jnp.exp(-jnp.abs(x)))


def _dot(a, b):
    return jnp.dot(a, b, preferred_element_type=F32)


def _dot_nt(a, b):
    return lax.dot_general(a, b, (((1,), (1,)), ((), ())), preferred_element_type=F32)


def _lo_mask(shape):
    return lax.broadcasted_iota(jnp.int32, shape, 1) % LANES < HEAD_DIM


def _half_sums(xx, lo):
    s_lo = jnp.sum(jnp.where(lo, xx, 0.0), axis=-1, keepdims=True)
    s_hi = jnp.sum(jnp.where(lo, 0.0, xx), axis=-1, keepdims=True)
    return jnp.where(lo, s_lo, s_hi)


def _head_rms(x, g_row):
    lo = _lo_mask(x.shape)
    ms = _half_sums(x * x, lo) * (1.0 / HEAD_DIM)
    return x * lax.rsqrt(ms + EPS) * g_row


def _head_l2n(x):
    lo = _lo_mask(x.shape)
    return x * lax.rsqrt(_half_sums(x * x, lo) + EPS)


def _rope(x, cos, sin):
    n = x.shape[1]
    lane = lax.broadcasted_iota(jnp.int32, x.shape, 1)
    swapped = jnp.where((lane & 16) == 0, pltpu.roll(x, n - 16, axis=1), pltpu.roll(x, 16, axis=1))
    return x * cos + swapped * sin


def _dup_half(x, half):
    lo = _lo_mask(x.shape)
    r = pltpu.roll(x, HEAD_DIM, axis=1)
    return jnp.where(lo, x, r) if half == 0 else jnp.where(lo, r, x)


def _mod_kernel(c_ref, w_ref, b_ref, o_ref):
    s = _silu(c_ref[...])
    o_ref[...] = _dot(s.astype(BF16), w_ref[...].astype(BF16)) + b_ref[...]


def _modulation(cvecs, ada_w, ada_b):
    nblk = 6
    return pl.pallas_call(
        _mod_kernel,
        out_shape=jax.ShapeDtypeStruct((DEPTH, MOD_ROWS, 6 * D_MODEL), F32),
        grid=(DEPTH, nblk),
        in_specs=[pl.BlockSpec((MOD_ROWS, D_MODEL), lambda l, j: (0, 0)),
                  pl.BlockSpec((None, D_MODEL, D_MODEL), lambda l, j: (l, 0, j)),
                  pl.BlockSpec((None, 1, D_MODEL), lambda l, j: (l, 0, j))],
        out_specs=pl.BlockSpec((None, MOD_ROWS, D_MODEL), lambda l, j: (l, 0, j)),
        compiler_params=pltpu.CompilerParams(vmem_limit_bytes=VMEM_LIMIT),
        name="adaln_mod",
    )(cvecs, ada_w, ada_b.reshape(DEPTH, 1, 6 * D_MODEL))


def _mod_row_map(layer, tiles_per_batch):
    if tiles_per_batch is None:
        return lambda i: (layer, 0, 0, 0)
    return lambda i: (layer, 1 + i // tiles_per_batch, 0, 0)


def _prep_q(q, qg, cos, sin):
    out = []
    for g in range(ATT_DIM // LANES):
        x = _head_rms(q[:, g * LANES:(g + 1) * LANES], qg)
        if cos is not None:
            x = _rope(x, cos, sin)
        out.append((x * (LOG2E * HEAD_DIM ** -0.5)).astype(BF16))
    return jnp.concatenate(out, axis=1)


def _in_kernel(x_ref, mod_ref, g_ref, w_ref, wab_ref, qg_ref, cos_ref, sin_ref,
               sc_ref, q_ref, kv_ref, dq_ref, dz_ref, ab_ref, *, rotary):
    x = x_ref[...]
    ms = jnp.mean(x * x, axis=-1, keepdims=True)
    h = (x * lax.rsqrt(ms + EPS) * g_ref[...]) * (1.0 + mod_ref[1:2, :]) + mod_ref[0:1, :]
    hb = h.astype(BF16)
    q = _dot(hb, w_ref[:, OFF_Q:OFF_KV])
    q_ref[...] = _prep_q(q, qg_ref[...], cos_ref[...] if rotary else None, sin_ref[...] if rotary else None)
    for ref, a, b in ((sc_ref, OFF_SC, OFF_Q), (kv_ref, OFF_KV, OFF_DQ), (dq_ref, OFF_DQ, OFF_DZ),
                      (dz_ref, OFF_DZ, OFF_AB)):
        ref[...] = _dot(hb, w_ref[:, a:b])
    ab_ref[...] = _dot(hb, wab_ref[...])


def _in_projection(x, mod, norm1_g, w_main_b, w_ab_b, q_norm_g, cos_t, sin_t, layer, tiles_per_batch):
    t = x.shape[0]
    widths = (OFF_Q - OFF_SC, OFF_KV - OFF_Q, OFF_DQ - OFF_KV, OFF_DZ - OFF_DQ, OFF_AB - OFF_DZ, IN_COLS - OFF_AB)
    rotary = tiles_per_batch is not None
    pos_map = (lambda i: (i % tiles_per_batch, 0)) if rotary else (lambda i: (0, 0))
    return pl.pallas_call(
        functools.partial(_in_kernel, rotary=rotary),
        out_shape=[jax.ShapeDtypeStruct((t, w), BF16 if k == 1 else F32) for k, w in enumerate(widths)],
        grid=(t // TM_IN,),
        in_specs=[pl.BlockSpec((TM_IN, D_MODEL), lambda i: (i, 0)),
                  pl.BlockSpec((None, None, 6, D_MODEL), _mod_row_map(layer, tiles_per_batch)),
                  pl.BlockSpec((None, 1, D_MODEL), lambda i: (layer, 0, 0)),
                  pl.BlockSpec((None, D_MODEL, OFF_AB), lambda i: (layer, 0, 0),
                               pipeline_mode=pl.Buffered(1)),
                  pl.BlockSpec((None, D_MODEL, IN_COLS - OFF_AB), lambda i: (layer, 0, 0),
                               pipeline_mode=pl.Buffered(1)),
                  pl.BlockSpec((None, 1, LANES), lambda i: (layer, 0, 0)),
                  pl.BlockSpec((TM_IN, LANES), pos_map),
                  pl.BlockSpec((TM_IN, LANES), pos_map)],
        out_specs=[pl.BlockSpec((TM_IN, w), lambda i: (i, 0)) for w in widths],
        compiler_params=pltpu.CompilerParams(vmem_limit_bytes=VMEM_LIMIT),
        name="in_proj",
    )(x, mod, norm1_g, w_main_b, w_ab_b, q_norm_g, cos_t, sin_t)


def _head_logits(qa, half, key_sets):
    lo = _lo_mask(qa.shape)
    qh = jnp.where(lo if half == 0 else jnp.logical_not(lo), qa, jnp.zeros_like(qa))
    logits = []
    for k, _, mask in key_sets:
        s = _dot_nt(qh, k())
        logits.append(s if mask is None else jnp.where(mask, s, NEG))
    return logits


def _head_softmax_pv(logits, key_sets, sink):
    m = jnp.maximum(jnp.max(jnp.concatenate(logits, axis=1), axis=-1, keepdims=True), sink)
    acc = None
    for s, (_, v1, _) in zip(logits, key_sets):
        o = _dot(jnp.exp2(s - m).astype(BF16), v1())
        acc = o if acc is None else acc + o
    return acc[:, :LANES] / (acc[:, LANES:] + jnp.exp2(sink - m))


def _attend_heads(q_ref, blocks, sink_ref, layer, o_ref):
    n_pairs = ATT_DIM // LANES
    items = [(blk, pair, half) for blk in range(len(blocks)) for pair in range(n_pairs) for half in range(2)]
    kvh_of = lambda pair: (2 * pair) // N_GROUP
    keys = {}

    def key_sets(blk, pair):
        if (blk, kvh_of(pair)) not in keys:
            keys[blk, kvh_of(pair)] = blocks[blk][1](kvh_of(pair))
        return keys[blk, kvh_of(pair)]

    def logits_of(blk, pair, half):
        return _head_logits(q_ref[blocks[blk][0], pair * LANES:(pair + 1) * LANES], half, key_sets(blk, pair))

    logits = logits_of(*items[0])
    done = []
    for idx, (blk, pair, half) in enumerate(items):
        nxt = logits_of(*items[idx + 1]) if idx + 1 < len(items) else None
        done.append(_head_softmax_pv(logits, key_sets(blk, pair), sink_ref[layer, 2 * pair + half] * LOG2E))
        logits = nxt
        if half == 1:
            lo = _lo_mask(done[-1].shape)
            o_ref[blocks[blk][0], pair * LANES:(pair + 1) * LANES] = jnp.where(lo, done[-2], done[-1])


def _values_with_ones(v2, kvh):
    return jnp.concatenate([_dup_half(v2, kvh), jnp.ones(v2.shape, F32)], axis=1).astype(BF16)


def _attn_ctx_kernel(sink_ref, q_ref, kv_ref, kg_ref, o_ref, kn_ref, *, layer):
    kv = kv_ref[...]
    kn = _head_rms(kv[:, :LANES], kg_ref[...])
    kn_ref[...] = kn
    v2 = kv[:, LANES:]
    kd = [_dup_half(kn, kvh).astype(BF16) for kvh in range(N_KV_HEADS)]
    vd = [_values_with_ones(v2, kvh) for kvh in range(N_KV_HEADS)]
    key_sets_of = lambda kvh: [(lambda: kd[kvh], lambda: vd[kvh], None)]
    _attend_heads(q_ref, [(slice(0, SEQ), key_sets_of)], sink_ref, layer, o_ref)


def _attention_ctx(q, kv, k_norm_g, attn_sink, layer):
    t = q.shape[0]
    return pl.pallas_call(
        functools.partial(_attn_ctx_kernel, layer=layer),
        out_shape=[jax.ShapeDtypeStruct((t, ATT_DIM), F32), jax.ShapeDtypeStruct((t, KV_DIM), F32)],
        grid=(t // SEQ,),
        in_specs=[pl.BlockSpec(memory_space=pltpu.SMEM),
                  pl.BlockSpec((SEQ, ATT_DIM), lambda b: (b, 0)),
                  pl.BlockSpec((SEQ, 2 * KV_DIM), lambda b: (b, 0)),
                  pl.BlockSpec((None, 1, LANES), lambda b: (layer, 0, 0))],
        out_specs=[pl.BlockSpec((SEQ, ATT_DIM), lambda b: (b, 0)),
                   pl.BlockSpec((SEQ, KV_DIM), lambda b: (b, 0))],
        compiler_params=pltpu.CompilerParams(vmem_limit_bytes=VMEM_LIMIT),
        name="attn_ctx",
    )(attn_sink, q, kv, k_norm_g)


QBLK = 128
QSTEP = 512
WIN_KEYS = QBLK + 2 * WINDOW
PREP_ROWS = 512


def _attn_lat_kernel(sink_ref, q_ref, kv_ref, kc_ref, vc_ref, kg_ref, cos_ref, sin_ref, o_ref,
                     kd_s, vd_s, kcd_s, vcd_s, *, layer):
    i = pl.program_id(1)
    n = DEC_SEQ

    @pl.when(i == 0)
    def _():
        for kvh in range(N_KV_HEADS):
            for s in (kd_s, vd_s):
                zpad = jnp.zeros((WINDOW, s.shape[-1]), BF16)
                s[kvh, 0:WINDOW, :] = zpad
                s[kvh, WINDOW + n:2 * WINDOW + n, :] = zpad
            kcd_s[kvh] = _dup_half(kc_ref[...], kvh).astype(BF16)
            vcd_s[kvh] = _values_with_ones(vc_ref[...], kvh)

        def prep(r, carry):
            r0 = pl.multiple_of(r * PREP_ROWS, PREP_ROWS)
            kv = kv_ref[pl.ds(r0, PREP_ROWS), :]
            kn = _head_rms(kv[:, :LANES], kg_ref[...])
            kr = _rope(kn, cos_ref[pl.ds(r0, PREP_ROWS), :], sin_ref[pl.ds(r0, PREP_ROWS), :])
            v2 = kv[:, LANES:]
            for kvh in range(N_KV_HEADS):
                kd_s[kvh, pl.ds(WINDOW + r0, PREP_ROWS), :] = _dup_half(kr, kvh).astype(BF16)
                vd_s[kvh, pl.ds(WINDOW + r0, PREP_ROWS), :] = _values_with_ones(v2, kvh)
            return carry

        lax.fori_loop(0, n // PREP_ROWS, prep, 0)

    rr = lax.broadcasted_iota(jnp.int32, (QBLK, WIN_KEYS), 0)
    jj = lax.broadcasted_iota(jnp.int32, (QBLK, WIN_KEYS), 1)
    band = (jj >= rr) & (jj <= rr + 2 * WINDOW)
    blocks = []
    for j in range(QSTEP // QBLK):
        start = pl.multiple_of(i * QSTEP + j * QBLK, QBLK)
        kpos = jj + (start - WINDOW)
        valid = band & (kpos >= 0) & (kpos < n)
        key_sets_of = lambda kvh, start=start, valid=valid: [
            (lambda: kcd_s[kvh], lambda: vcd_s[kvh], None),
            (lambda: kd_s[kvh, pl.ds(start, WIN_KEYS), :], lambda: vd_s[kvh, pl.ds(start, WIN_KEYS), :], valid)]
        blocks.append((slice(j * QBLK, (j + 1) * QBLK), key_sets_of))
    _attend_heads(q_ref, blocks, sink_ref, layer, o_ref)


def _attention_lat(q, kv, cache_k, cache_v, k_norm_g, attn_sink, cos_t, sin_t, layer):
    t = q.shape[0]
    n = DEC_SEQ
    nblk = n // QSTEP
    return pl.pallas_call(
        functools.partial(_attn_lat_kernel, layer=layer),
        out_shape=jax.ShapeDtypeStruct((t, ATT_DIM), F32),
        grid=(DEC_BATCH, nblk),
        in_specs=[pl.BlockSpec(memory_space=pltpu.SMEM),
                  pl.BlockSpec((QSTEP, ATT_DIM), lambda b, i: (b * nblk + i, 0)),
                  pl.BlockSpec((n, 2 * KV_DIM), lambda b, i: (b, 0)),
                  pl.BlockSpec((None, None, PAST_LEN, KV_DIM), lambda b, i: (b, layer, 0, 0)),
                  pl.BlockSpec((None, None, PAST_LEN, KV_DIM), lambda b, i: (b, layer, 0, 0)),
                  pl.BlockSpec((None, 1, LANES), lambda b, i: (layer, 0, 0)),
                  pl.BlockSpec((n, LANES), lambda b, i: (0, 0)),
                  pl.BlockSpec((n, LANES), lambda b, i: (0, 0))],
        out_specs=pl.BlockSpec((QSTEP, ATT_DIM), lambda b, i: (b * nblk + i, 0)),
        scratch_shapes=[pltpu.VMEM((N_KV_HEADS, n + 2 * WINDOW, LANES), BF16),
                        pltpu.VMEM((N_KV_HEADS, n + 2 * WINDOW, 2 * LANES), BF16),
                        pltpu.VMEM((N_KV_HEADS, PAST_LEN, LANES), BF16),
                        pltpu.VMEM((N_KV_HEADS, PAST_LEN, 2 * LANES), BF16)],
        compiler_params=pltpu.CompilerParams(vmem_limit_bytes=VMEM_LIMIT),
        name="attn_lat",
    )(attn_sink, q, kv, cache_k, cache_v, k_norm_g, cos_t, sin_t)


DN_ROWS = 128
DN_OUT_ROWS = 256
DN_CTX_SEQS = 4


def _dn_conv(x_ref, w_ref, r0, n):
    x = x_ref[pl.ds(r0, DN_ROWS), :]
    total = x_ref.shape[0]
    above = jnp.where(r0 % n > 0, x_ref[pl.ds(jnp.maximum(r0 - 1, 0), 1), :], 0.0)
    below = jnp.where(r0 % n + DN_ROWS < n, x_ref[pl.ds(jnp.minimum(r0 + DN_ROWS, total - 1), 1), :], 0.0)
    row = lax.broadcasted_iota(jnp.int32, x.shape, 0)
    prev = jnp.where(row == 0, above, pltpu.roll(x, 1, axis=0))
    nxt = jnp.where(row == DN_ROWS - 1, below, pltpu.roll(x, DN_ROWS - 1, axis=0))
    return _silu(prev * w_ref[0:1, :] + x * w_ref[1:2, :] + nxt * w_ref[2:3, :])


N_UNITS = 4
UW = N_UNITS * CHUNK
N_LEVELS = 6
M_INCL, M_STRICT, M_EYE = N_LEVELS, N_LEVELS + 1, N_LEVELS + 2
PREP_BLOCK = 8


def _dn_kernel(q_ref, k_ref, v_ref, z_ref, ab_ref, wq_ref, wk_ref, wv_ref, pa_ref, pdt_ref, ng_ref, s0_ref,
               o_ref, sout_ref, qs, ks, vs, gbs, pfs, r_s, au_s, m_s, qa_s, sh_s, el_s, st, mc, ex_s, *, n):
    nc = n // CHUNK
    nseq = q_ref.shape[0] // n
    pblk = min(PREP_BLOCK, nseq * nc)

    def prep_rows(b, carry):
        r0 = pl.multiple_of(b * DN_ROWS, DN_ROWS)
        rows = pl.ds(r0, DN_ROWS)
        qs[rows, :] = _head_l2n(_dn_conv(q_ref, wq_ref, r0, n)) * (DN_HEAD_DIM ** -0.5)
        ks[rows, :] = _head_l2n(_dn_conv(k_ref, wk_ref, r0, n))
        vs[rows, :] = _dn_conv(v_ref, wv_ref, r0, n)
        ab = ab_ref[rows, :]
        lane = lax.broadcasted_iota(jnp.int32, ab.shape, 1)
        gb = jnp.where(lane < N_UNITS, -jnp.exp(pa_ref[...]) * _softplus(ab + pdt_ref[...]), _sigmoid(ab))
        gbs[rows, :] = gb
        row_in_chunk = lax.broadcasted_iota(jnp.int32, ab.shape, 0) % CHUNK
        pref = gb
        for sh in (1, 2, 4, 8, 16, 32):
            pref = pref + jnp.where(row_in_chunk >= sh, pltpu.roll(pref, sh, axis=0), 0.0)
        pfs[rows, :] = pref
        return carry

    lax.fori_loop(0, nseq * n // DN_ROWS, prep_rows, 0)

    @pl.when((pl.program_id(0) == 0) & (pl.program_id(1) == 0))
    def _():
        as_f32 = lambda m: jnp.where(m, 1.0, 0.0).astype(F32)
        ii = lax.broadcasted_iota(jnp.int32, (CHUNK, UW), 0)
        ll = lax.broadcasted_iota(jnp.int32, (CHUNK, UW), 1)
        unit_of_lane, jj = ll // CHUNK, ll % CHUNK
        fwd, bwd = unit_of_lane < 2, unit_of_lane >= 2
        for k in range(N_LEVELS):
            rk, ck = ii >> k, jj >> k
            lower = (rk - ck == 1) & ((rk & 1) == 1)
            upper = (ck - rk == 1) & ((ck & 1) == 1)
            mc[k] = as_f32((fwd & lower) | (bwd & upper))
        mc[M_INCL] = as_f32((fwd & (ii >= jj)) | (bwd & (ii <= jj)))
        mc[M_STRICT] = as_f32((fwd & (ii > jj)) | (bwd & (ii < jj)))
        mc[M_EYE] = as_f32(ii == jj)
        ec = lax.broadcasted_iota(jnp.int32, (LANES, 2 * UW), 0)
        el_ = lax.broadcasted_iota(jnp.int32, (LANES, 2 * UW), 1)
        ex_s[...] = as_f32(ec == (el_ // UW) * N_UNITS + (el_ % UW) // CHUNK).astype(BF16)

    for q in range(nseq):
        st[q] = jnp.concatenate([s0_ref[q, u // 2, u % 2] for u in range(N_UNITS)], axis=1)
    lo = _lo_mask((CHUNK, LANES))
    lane_c = lax.broadcasted_iota(jnp.int32, (CHUNK, LANES), 1)

    def spread_columns(m):
        hi = m.astype(BF16)
        r1 = m - hi.astype(F32)
        mid = r1.astype(BF16)
        low = (r1 - mid.astype(F32)).astype(BF16)
        e = ex_s[...]
        return _dot(hi, e) + _dot(mid, e) + _dot(low, e)

    def pair_block_diag(x):
        zero = jnp.zeros((CHUNK, LANES), BF16)
        return jnp.concatenate([jnp.where(lo, x, zero), jnp.where(lo, zero, x)], axis=0)

    def unit_dot(a, xb):
        return jnp.concatenate([_dot(a[:, h * LANES:(h + 1) * LANES], pair_block_diag(xb[:, h * LANES:(h + 1) * LANES]))
                                for h in range(2)], axis=1)

    def unit_rows(m):
        t = m.T
        return jnp.concatenate([t[u:u + 1, :] for u in range(N_UNITS)], axis=1)

    dup = lambda x: jnp.concatenate([x, x], axis=1)

    def chunk_prep(blk, carry):
        cs = [blk * pblk + s for s in range(pblk)]
        rows = [pl.ds(pl.multiple_of(c * CHUNK, CHUNK), CHUNK) for c in cs]
        each = lambda f, *lists: [f(*xs) for xs in zip(*lists)]

        gx = [gbs[r, :] for r in rows]
        pc = [pfs[r, :] for r in rows]
        tot = [p[CHUNK - 1:CHUNK, :] for p in pc]
        gcm = each(lambda g, p, t: jnp.where(lane_c < 2, p, t - p + g), gx, pc, tot)
        rsm = each(lambda g, p, t: jnp.where(lane_c < 2, t - p, p - g), gx, pc, tot)
        spread = each(lambda m, g: spread_columns(jnp.where(lane_c < N_UNITS, m, g)), gcm, gx)
        gc4 = [x[:, :UW] for x in spread]
        b4 = [x[:, UW:] for x in spread]
        e_gc = each(jnp.exp, gc4)
        dec = each(lambda c, m: jnp.exp(jnp.where(mc[M_INCL] > 0.0, c - unit_rows(m), NEG)), gc4, gcm)
        kt = [ks[r, :].T for r in rows]
        kt_pair = each(lambda t: jnp.concatenate([t[0:CHUNK], t[CHUNK:]], axis=1), kt)
        kt_bd = each(lambda t: pair_block_diag(t.astype(BF16)), kt_pair)
        gram = each(lambda r, b: dup(_dot(ks[r, :].astype(BF16), b)), rows, kt_bd)
        qk = each(lambda r, b: dup(_dot(qs[r, :].astype(BF16), b)), rows, kt_bd)
        lmat = each(lambda g, b, d: g * b * d * mc[M_STRICT], gram, b4, dec)
        amat = each(lambda m, d: (m * d).astype(BF16), qk, dec)
        t = each(lambda l: mc[M_EYE] - l * mc[0], lmat)
        for k in range(1, N_LEVELS):
            lb = each(lambda l: (l * mc[k]).astype(BF16), lmat)
            tb = each(lambda x: x.astype(BF16), t)
            x = each(lambda a, b: unit_dot(a, b).astype(BF16), tb, lb)
            t = each(lambda t0, a, b: t0 - unit_dot(a, b), t, x, tb)
        tm = each(lambda x: (x - mc[M_EYE]).astype(BF16), t)
        k4 = [dup(ks[r, :]) for r in rows]
        ru = each(lambda r, b: dup(vs[r, :]) * b, rows, b4)
        rw = each(lambda k, b, e: k * b * e, k4, b4, e_gc)
        ub = each(lambda x, tmi: (x + unit_dot(tmi, x.astype(BF16))).astype(BF16), ru, tm)
        wb = each(lambda x, tmi: (x + unit_dot(tmi, x.astype(BF16))).astype(BF16), rw, tm)
        kdt = each(lambda t, m: (dup(t) * jnp.exp(unit_rows(m))).astype(BF16), kt_pair, rsm)
        mmat = each(unit_dot, kdt, wb)
        rmat = each(unit_dot, kdt, ub)
        aw = each(unit_dot, amat, wb)
        au = each(unit_dot, amat, ub)
        for i, (c, r) in enumerate(zip(cs, rows)):
            m_s[r, :] = mmat[i].astype(BF16)
            r_s[r, :] = rmat[i]
            qa_s[r, :] = (dup(qs[r, :]) * e_gc[i] - aw[i]).astype(BF16)
            au_s[r, :] = au[i]
            el = jnp.concatenate([e_gc[i][CHUNK - 1:CHUNK, :LANES], e_gc[i][0:1, LANES:]], axis=1)
            el_s[c] = jnp.broadcast_to(el, (8, UW))
        return carry

    lax.fori_loop(0, nseq * nc // pblk, chunk_prep, 0)

    def scan_step(i, carry):
        cf = [q * nc + i for q in range(nseq)]
        cb = [q * nc + nc - 1 - i for q in range(nseq)]
        rf = [pl.ds(pl.multiple_of(c * CHUNK, CHUNK), CHUNK) for c in cf]
        rb = [pl.ds(pl.multiple_of(c * CHUNK, CHUNK), CHUNK) for c in cb]
        mixed = lambda ref, q: jnp.concatenate([ref[rf[q], :LANES], ref[rb[q], LANES:]], axis=1)
        s = [st[q] for q in range(nseq)]
        sb = [x.astype(BF16) for x in s]
        ms = [unit_dot(mixed(m_s, q), sb[q]) for q in range(nseq)]
        for q in range(nseq):
            el = jnp.concatenate([el_s[cf[q]][0:1, :LANES], el_s[cb[q]][0:1, LANES:]], axis=1)
            st[q] = s[q] * el - ms[q] + mixed(r_s, q)
            sh_s[rf[q], :LANES] = sb[q][:, :LANES]
            sh_s[rb[q], LANES:] = sb[q][:, LANES:]
        return carry

    lax.fori_loop(0, nc, scan_step, 0)

    for q in range(nseq):
        s = st[q]
        for u in range(N_UNITS):
            sout_ref[q, u // 2, u % 2] = s[:, u * CHUNK:(u + 1) * CHUNK]

    def finish_rows(b, carry):
        r0 = b * DN_OUT_ROWS
        parts = []
        for c in range(DN_OUT_ROWS // CHUNK):
            rows = pl.ds(pl.multiple_of(r0 + c * CHUNK, CHUNK), CHUNK)
            parts.append(unit_dot(qa_s[rows, :], sh_s[rows, :]) + au_s[rows, :])
        o = jnp.concatenate([p[:, :LANES] + p[:, LANES:] for p in parts], axis=0)
        rows = pl.ds(pl.multiple_of(r0, DN_OUT_ROWS), DN_OUT_ROWS)
        o_ref[rows, :] = _head_rms(o, ng_ref[...]) * _silu(z_ref[rows, :])
        return carry

    lax.fori_loop(0, nseq * n // DN_OUT_ROWS, finish_rows, 0)


def _delta_net(dq, dz, dab, dn_conv_w, pa, pdt, dn_norm_g, s0, layer, s0_layer, n, nseq):
    t = dq.shape[0]
    bsz = t // n
    rows = nseq * n
    nc = n // CHUNK
    conv_spec = lambda off: pl.BlockSpec((None, 3, LANES), lambda b, p: (layer, 0, off + p))
    col_spec = lambda off, bufs=2: pl.BlockSpec((rows, LANES), lambda b, p: (b, off + p),
                                                pipeline_mode=pl.Buffered(bufs))
    state_block = (nseq, 2, 2, DN_HEAD_DIM, DN_HEAD_DIM)
    return pl.pallas_call(
        functools.partial(_dn_kernel, n=n),
        out_shape=[jax.ShapeDtypeStruct((t, DN_DIM), F32),
                   jax.ShapeDtypeStruct((bsz, 2, N_DN_HEADS, DN_HEAD_DIM, DN_HEAD_DIM), F32)],
        grid=(bsz // nseq, N_PAIRS),
        in_specs=[col_spec(0), col_spec(N_PAIRS), col_spec(2 * N_PAIRS), col_spec(0, 1), col_spec(0),
                  conv_spec(0), conv_spec(N_PAIRS), conv_spec(2 * N_PAIRS),
                  pl.BlockSpec((None, None, 1, LANES), lambda b, p: (layer, p, 0, 0)),
                  pl.BlockSpec((None, None, 1, LANES), lambda b, p: (layer, p, 0, 0)),
                  pl.BlockSpec((None, 1, LANES), lambda b, p: (layer, 0, 0)),
                  pl.BlockSpec((nseq, None) + state_block[1:], lambda b, p: (b, s0_layer, 0, p, 0, 0))],
        out_specs=[pl.BlockSpec((rows, LANES), lambda b, p: (b, p)),
                   pl.BlockSpec(state_block, lambda b, p: (b, 0, p, 0, 0))],
        scratch_shapes=[pltpu.VMEM((rows, LANES), F32) for _ in range(5)]
                       + [pltpu.VMEM((rows, UW), F32) for _ in range(2)]
                       + [pltpu.VMEM((rows, UW), BF16) for _ in range(3)]
                       + [pltpu.VMEM((nseq * nc, 8, UW), F32),
                          pltpu.VMEM((nseq, CHUNK, UW), F32),
                          pltpu.VMEM((N_LEVELS + 3, CHUNK, UW), F32),
                          pltpu.VMEM((LANES, 2 * UW), BF16)],
        compiler_params=pltpu.CompilerParams(vmem_limit_bytes=(V7X_VMEM_BYTES * 7) // 8),
        name="delta_net",
    )(dq, dq, dq, dz, dab, dn_conv_w, dn_conv_w, dn_conv_w, pa, pdt, dn_norm_g, s0)


def _out_kernel(x_ref, sc_ref, scp_ref, scn_ref, att_ref, dn_ref, mod_ref, cw_ref, wo_ref, n2_ref,
                wg_ref, wu_ref, wd_ref, o_ref, *, tm, tiles_per_seq):
    t = pl.program_id(0) % tiles_per_seq
    sc = sc_ref[...]
    prod = sc[:, SC_DIM:2 * SC_DIM] * sc[:, 2 * SC_DIM:]
    pv = scp_ref[7:8, :]
    nx = scn_ref[0:1, :]
    prev_row = jnp.where(t > 0, pv[:, SC_DIM:2 * SC_DIM] * pv[:, 2 * SC_DIM:], 0.0)
    next_row = jnp.where(t < tiles_per_seq - 1, nx[:, SC_DIM:2 * SC_DIM] * nx[:, 2 * SC_DIM:], 0.0)
    row = lax.broadcasted_iota(jnp.int32, prod.shape, 0)
    p_prev = jnp.where(row == 0, prev_row, pltpu.roll(prod, 1, axis=0))
    p_next = jnp.where(row == tm - 1, next_row, pltpu.roll(prod, tm - 1, axis=0))
    y_sc = sc[:, :SC_DIM] * (p_prev * cw_ref[0:1, :] + prod * cw_ref[1:2, :] + p_next * cw_ref[2:3, :])
    y = (_dot(y_sc.astype(BF16), wo_ref[0:SC_DIM, :])
         + _dot(att_ref[...].astype(BF16), wo_ref[SC_DIM:SC_DIM + ATT_DIM, :])
         + _dot(dn_ref[...].astype(BF16), wo_ref[SC_DIM + ATT_DIM:, :]))
    x1 = x_ref[...] + mod_ref[2:3, :] * y
    ms = jnp.mean(x1 * x1, axis=-1, keepdims=True)
    h2 = (x1 * lax.rsqrt(ms + EPS) * n2_ref[...]) * (1.0 + mod_ref[4:5, :]) + mod_ref[3:4, :]
    hb = h2.astype(BF16)
    act = _silu(_dot(hb, wg_ref[...])) * _dot(hb, wu_ref[...])
    o_ref[...] = x1 + mod_ref[5:6, :] * _dot(act.astype(BF16), wd_ref[...])


def _output_stage(x, sc, att, dn, mod, sc_conv_w, w_out_b, norm2_g, w_gate_b, w_up_b, w_down_b, layer,
                  tm, seq_len, per_batch_mod):
    t = x.shape[0]
    tiles_per_seq = seq_len // tm
    tiles_per_batch = tiles_per_seq if per_batch_mod else None
    halo = tm // 8
    last = t // 8 - 1
    resident = lambda shape: pl.BlockSpec((None,) + shape, lambda i: (layer, 0, 0), pipeline_mode=pl.Buffered(1))
    return pl.pallas_call(
        functools.partial(_out_kernel, tm=tm, tiles_per_seq=tiles_per_seq),
        out_shape=jax.ShapeDtypeStruct((t, D_MODEL), F32),
        grid=(t // tm,),
        in_specs=[pl.BlockSpec((tm, D_MODEL), lambda i: (i, 0)),
                  pl.BlockSpec((tm, 3 * SC_DIM), lambda i: (i, 0)),
                  pl.BlockSpec((8, 3 * SC_DIM), lambda i: (jnp.maximum(i * halo - 1, 0), 0)),
                  pl.BlockSpec((8, 3 * SC_DIM), lambda i: (jnp.minimum((i + 1) * halo, last), 0)),
                  pl.BlockSpec((tm, ATT_DIM), lambda i: (i, 0)),
                  pl.BlockSpec((tm, DN_DIM), lambda i: (i, 0)),
                  pl.BlockSpec((None, None, 6, D_MODEL), _mod_row_map(layer, tiles_per_batch)),
                  pl.BlockSpec((None, 3, SC_DIM), lambda i: (layer, 0, 0)),
                  resident((MIX_DIM, D_MODEL)),
                  pl.BlockSpec((None, 1, D_MODEL), lambda i: (layer, 0, 0)),
                  resident((D_MODEL, D_FF)), resident((D_MODEL, D_FF)), resident((D_FF, D_MODEL))],
        out_specs=pl.BlockSpec((tm, D_MODEL), lambda i: (i, 0)),
        compiler_params=pltpu.CompilerParams(vmem_limit_bytes=VMEM_LIMIT),
        name="out_stage",
    )(x, sc, sc, sc, att, dn, mod, sc_conv_w, w_out_b, norm2_g, w_gate_b, w_up_b, w_down_b)


def _rope_tables():
    pos = jnp.arange(DEC_SEQ)
    half = HEAD_DIM // 4
    inv = 1.0 / (ROPE_BASE ** (jnp.arange(half, dtype=F32) / half))
    ang_r = (pos // GRID_W).astype(F32)[:, None] * inv
    ang_c = (pos % GRID_W).astype(F32)[:, None] * inv
    cos = jnp.concatenate([jnp.cos(ang_r)] * 2 + [jnp.cos(ang_c)] * 2, axis=-1)
    sin = jnp.concatenate([-jnp.sin(ang_r), jnp.sin(ang_r), -jnp.sin(ang_c), jnp.sin(ang_c)], axis=-1)
    return jnp.tile(cos, (1, 2)), jnp.tile(sin, (1, 2))


def _pack_w_in(w_in):
    a0 = OFF_AB
    b0 = OFF_AB + 2 * N_DN_HEADS
    blocks = []
    for p in range(N_PAIRS):
        idx = [a0 + d * N_DN_HEADS + 2 * p + j for d in range(2) for j in range(2)]
        idx += [b0 + d * N_DN_HEADS + 2 * p + j for d in range(2) for j in range(2)]
        blocks += [w_in[:, :, c:c + 1] for c in idx]
        blocks.append(jnp.zeros((DEPTH, D_MODEL, LANES - len(idx)), w_in.dtype))
    return w_in[:, :, :OFF_AB].astype(BF16), jnp.concatenate(blocks, axis=2).astype(BF16)


def _pair_lanes(p):
    x = p.reshape(DEPTH, 2, N_PAIRS, 2).transpose(0, 2, 1, 3).reshape(DEPTH, N_PAIRS, 1, 4)
    return jnp.pad(x, ((0, 0), (0, 0), (0, 0), (0, LANES - 4)))


def kernel(x_prompt, x_sample, cache_k, cache_v, state_delta, c, c_ctx, w_in, w_out, ada_w, ada_b,
           norm1_g, norm2_g, sc_conv_w, dn_conv_w, q_norm_g, k_norm_g, attn_sink, dn_A_log,
           dn_dt_bias, dn_norm_g, w_gate, w_up, w_down):
    w_main_b, w_ab_b = _pack_w_in(w_in)
    w_out_b, w_gate_b, w_up_b, w_down_b = (w.astype(BF16) for w in (w_out, w_gate, w_up, w_down))
    n1 = norm1_g.reshape(DEPTH, 1, D_MODEL)
    n2 = norm2_g.reshape(DEPTH, 1, D_MODEL)
    qg = jnp.tile(q_norm_g, (1, 2)).reshape(DEPTH, 1, LANES)
    kg = jnp.tile(k_norm_g, (1, 2)).reshape(DEPTH, 1, LANES)
    ng = jnp.tile(dn_norm_g, (1, 2)).reshape(DEPTH, 1, LANES)
    pa, pdt = _pair_lanes(dn_A_log), _pair_lanes(dn_dt_bias)
    cos_t, sin_t = _rope_tables()
    ck = cache_k.reshape(DEC_BATCH, DEPTH, PAST_LEN, KV_DIM)
    cv = cache_v.reshape(DEC_BATCH, DEPTH, PAST_LEN, KV_DIM)
    zero_state = jnp.zeros((BATCH, 1, 2, N_DN_HEADS, DN_HEAD_DIM, DN_HEAD_DIM), F32)

    cvecs = jnp.concatenate([c_ctx[None, :], c, jnp.zeros((MOD_ROWS - 1 - DEC_BATCH, D_MODEL), F32)], axis=0)
    mod = _modulation(cvecs, ada_w, ada_b).reshape(DEPTH, MOD_ROWS, 6, D_MODEL)

    xp = x_prompt.reshape(BATCH * SEQ, D_MODEL)
    xs = x_sample.reshape(DEC_BATCH * DEC_SEQ, D_MODEL)
    new_k, new_v, new_s = [], [], []
    for l in range(DEPTH):
        out_args = (mod, sc_conv_w, w_out_b, n2, w_gate_b, w_up_b, w_down_b, l)
        sc, q, kv, dq, dz, dab = _in_projection(xp, mod, n1, w_main_b, w_ab_b, qg, cos_t, sin_t, l, None)
        att, kn = _attention_ctx(q, kv, kg, attn_sink, l)
        dn, s_out = _delta_net(dq, dz, dab, dn_conv_w, pa, pdt, ng, zero_state, l, 0, SEQ, DN_CTX_SEQS)
        xp = _output_stage(xp, sc, att, dn, *out_args, TM, SEQ, False)
        new_k.append(kn.reshape(BATCH, SEQ, N_KV_HEADS, HEAD_DIM))
        new_v.append(kv[:, KV_DIM:].reshape(BATCH, SEQ, N_KV_HEADS, HEAD_DIM))
        new_s.append(s_out)
        sc, q, kv, dq, dz, dab = _in_projection(xs, mod, n1, w_main_b, w_ab_b, qg, cos_t, sin_t, l, DEC_SEQ // TM_IN)
        att = _attention_lat(q, kv, ck, cv, kg, attn_sink, cos_t, sin_t, l)
        dn, _ = _delta_net(dq, dz, dab, dn_conv_w, pa, pdt, ng, state_delta, l, l, DEC_SEQ, 1)
        xs = _output_stage(xs, sc, att, dn, *out_args, TM_LAT, DEC_SEQ, True)
    return (xp.reshape(BATCH, SEQ, D_MODEL), xs.reshape(DEC_BATCH, DEC_SEQ, D_MODEL),
            jnp.stack(new_k, axis=1), jnp.stack(new_v, axis=1), jnp.stack(new_s, axis=1))
```

```python
import functools

import numpy as np
import jax
import jax.numpy as jnp
from jax import lax
from jax.experimental import pallas as pl
from jax.experimental.pallas import tpu as pltpu

F32 = jnp.float32
BF16 = jnp.bfloat16

D_MODEL = 1024
BATCH = 16
SEQ = 256
DEPTH = 2
DEC_BATCH = 4
DEC_SEQ = 4096
PAST_LEN = 512
GRID_W = 64
SC_DIM = 256
N_Q_HEADS = 8
N_KV_HEADS = 2
N_GROUP = N_Q_HEADS // N_KV_HEADS
HEAD_DIM = 64
ATT_DIM = N_Q_HEADS * HEAD_DIM
KV_DIM = N_KV_HEADS * HEAD_DIM
WINDOW = 128
N_DN_HEADS = 4
DN_HEAD_DIM = 64
DN_DIM = N_DN_HEADS * DN_HEAD_DIM
CHUNK = 64
MIX_DIM = SC_DIM + ATT_DIM + DN_DIM
D_FF = -(-8 * D_MODEL // (3 * 256)) * 256
ROPE_BASE = 10000.0
EPS = 1e-6
NEG = -1e30
LOG2E = 1.4426950408889634

LANES = 128
V7X_VMEM_BYTES = 64 * 1024 * 1024
VMEM_LIMIT = (V7X_VMEM_BYTES * 3) // 4

OFF_SC, OFF_Q, OFF_KV, OFF_DQ, OFF_DZ, OFF_AB = 0, 768, 1280, 1536, 2304, 2560
IN_COLS = OFF_AB + 2 * LANES
N_PAIRS = N_DN_HEADS // 2
MOD_ROWS = 8
TM = 256
TM_LAT = 512
TM_IN = 512


def _sigmoid(x):
    return 0.5 * jnp.tanh(0.5 * x) + 0.5


def _silu(x):
    return x * _sigmoid(x)


def _softplus(x):
    return jnp.maximum(x, 0.0) + jnp.log1p(jnp.exp(-jnp.abs(x)))


def _dot(a, b):
    return jnp.dot(a, b, preferred_element_type=F32)


def _dot_nt(a, b):
    return lax.dot_general(a, b, (((1,), (1,)), ((), ())), preferred_element_type=F32)


def _lo_mask(shape):
    return lax.broadcasted_iota(jnp.int32, shape, 1) % LANES < HEAD_DIM


def _half_sums(xx, lo):
    s_lo = jnp.sum(jnp.where(lo, xx, 0.0), axis=-1, keepdims=True)
    s_hi = jnp.sum(jnp.where(lo, 0.0, xx), axis=-1, keepdims=True)
    return jnp.where(lo, s_lo, s_hi)


def _head_rms(x, g_row):
    lo = _lo_mask(x.shape)
    ms = _half_sums(x * x, lo) * (1.0 / HEAD_DIM)
    return x * lax.rsqrt(ms + EPS) * g_row


def _head_l2n(x):
    lo = _lo_mask(x.shape)
    return x * lax.rsqrt(_half_sums(x * x, lo) + EPS)


def _rope(x, cos, sin):
    n = x.shape[1]
    lane = lax.broadcasted_iota(jnp.int32, x.shape, 1)
    swapped = jnp.where((lane & 16) == 0, pltpu.roll(x, n - 16, axis=1), pltpu.roll(x, 16, axis=1))
    return x * cos + swapped * sin


def _dup_half(x, half):
    lo = _lo_mask(x.shape)
    r = pltpu.roll(x, HEAD_DIM, axis=1)
    return jnp.where(lo, x, r) if half == 0 else jnp.where(lo, r, x)


def _mod_kernel(c_ref, w_ref, b_ref, o_ref):
    s = _silu(c_ref[...])
    o_ref[...] = _dot(s.astype(BF16), w_ref[...].astype(BF16)) + b_ref[...]


def _modulation(cvecs, ada_w, ada_b):
    nblk = 6
    return pl.pallas_call(
        _mod_kernel,
        out_shape=jax.ShapeDtypeStruct((DEPTH, MOD_ROWS, 6 * D_MODEL), F32),
        grid=(DEPTH, nblk),
        in_specs=[pl.BlockSpec((MOD_ROWS, D_MODEL), lambda l, j: (0, 0)),
                  pl.BlockSpec((None, D_MODEL, D_MODEL), lambda l, j: (l, 0, j)),
                  pl.BlockSpec((None, 1, D_MODEL), lambda l, j: (l, 0, j))],
        out_specs=pl.BlockSpec((None, MOD_ROWS, D_MODEL), lambda l, j: (l, 0, j)),
        compiler_params=pltpu.CompilerParams(vmem_limit_bytes=VMEM_LIMIT),
        name="adaln_mod",
    )(cvecs, ada_w, ada_b.reshape(DEPTH, 1, 6 * D_MODEL))


def _mod_row_map(layer, tiles_per_batch):
    if tiles_per_batch is None:
        return lambda i: (layer, 0, 0, 0)
    return lambda i: (layer, 1 + i // tiles_per_batch, 0, 0)


def _prep_q(q, qg, cos, sin):
    out = []
    for g in range(ATT_DIM // LANES):
        x = _head_rms(q[:, g * LANES:(g + 1) * LANES], qg)
        if cos is not None:
            x = _rope(x, cos, sin)
        out.append((x * (LOG2E * HEAD_DIM ** -0.5)).astype(BF16))
    return jnp.concatenate(out, axis=1)


def _in_kernel(x_ref, mod_ref, g_ref, w_ref, wab_ref, qg_ref, cos_ref, sin_ref,
               sc_ref, q_ref, kv_ref, dq_ref, dz_ref, ab_ref, *, rotary):
    x = x_ref[...]
    ms = jnp.mean(x * x, axis=-1, keepdims=True)
    h = (x * lax.rsqrt(ms + EPS) * g_ref[...]) * (1.0 + mod_ref[1:2, :]) + mod_ref[0:1, :]
    hb = h.astype(BF16)
    q = _dot(hb, w_ref[:, OFF_Q:OFF_KV])
    q_ref[...] = _prep_q(q, qg_ref[...], cos_ref[...] if rotary else None, sin_ref[...] if rotary else None)
    for ref, a, b in ((sc_ref, OFF_SC, OFF_Q), (kv_ref, OFF_KV, OFF_DQ), (dq_ref, OFF_DQ, OFF_DZ),
                      (dz_ref, OFF_DZ, OFF_AB)):
        ref[...] = _dot(hb, w_ref[:, a:b])
    ab_ref[...] = _dot(hb, wab_ref[...])


def _in_projection(x, mod, norm1_g, w_main_b, w_ab_b, q_norm_g, cos_t, sin_t, layer, tiles_per_batch):
    t = x.shape[0]
    widths = (OFF_Q - OFF_SC, OFF_KV - OFF_Q, OFF_DQ - OFF_KV, OFF_DZ - OFF_DQ, OFF_AB - OFF_DZ, IN_COLS - OFF_AB)
    rotary = tiles_per_batch is not None
    pos_map = (lambda i: (i % tiles_per_batch, 0)) if rotary else (lambda i: (0, 0))
    return pl.pallas_call(
        functools.partial(_in_kernel, rotary=rotary),
        out_shape=[jax.ShapeDtypeStruct((t, w), BF16 if k == 1 else F32) for k, w in enumerate(widths)],
        grid=(t // TM_IN,),
        in_specs=[pl.BlockSpec((TM_IN, D_MODEL), lambda i: (i, 0)),
                  pl.BlockSpec((None, None, 6, D_MODEL), _mod_row_map(layer, tiles_per_batch)),
                  pl.BlockSpec((None, 1, D_MODEL), lambda i: (layer, 0, 0)),
                  pl.BlockSpec((None, D_MODEL, OFF_AB), lambda i: (layer, 0, 0),
                               pipeline_mode=pl.Buffered(1)),
                  pl.BlockSpec((None, D_MODEL, IN_COLS - OFF_AB), lambda i: (layer, 0, 0),
                               pipeline_mode=pl.Buffered(1)),
                  pl.BlockSpec((None, 1, LANES), lambda i: (layer, 0, 0)),
                  pl.BlockSpec((TM_IN, LANES), pos_map),
                  pl.BlockSpec((TM_IN, LANES), pos_map)],
        out_specs=[pl.BlockSpec((TM_IN, w), lambda i: (i, 0)) for w in widths],
        compiler_params=pltpu.CompilerParams(vmem_limit_bytes=VMEM_LIMIT),
        name="in_proj",
    )(x, mod, norm1_g, w_main_b, w_ab_b, q_norm_g, cos_t, sin_t)


def _head_logits(qa, half, key_sets):
    lo = _lo_mask(qa.shape)
    qh = jnp.where(lo if half == 0 else jnp.logical_not(lo), qa, jnp.zeros_like(qa))
    logits = []
    for k, _, mask in key_sets:
        s = _dot_nt(qh, k())
        logits.append(s if mask is None else jnp.where(mask, s, NEG))
    return logits


def _head_softmax_pv(logits, key_sets, sink):
    m = jnp.maximum(jnp.max(jnp.concatenate(logits, axis=1), axis=-1, keepdims=True), sink)
    acc = None
    for s, (_, v1, _) in zip(logits, key_sets):
        o = _dot(jnp.exp2(s - m).astype(BF16), v1())
        acc = o if acc is None else acc + o
    return acc[:, :LANES] / (acc[:, LANES:] + jnp.exp2(sink - m))


def _attend_heads(q_ref, blocks, sink_ref, layer, o_ref):
    n_pairs = ATT_DIM // LANES
    items = [(blk, pair, half) for blk in range(len(blocks)) for pair in range(n_pairs) for half in range(2)]
    kvh_of = lambda pair: (2 * pair) // N_GROUP
    keys = {}

    def key_sets(blk, pair):
        if (blk, kvh_of(pair)) not in keys:
            keys[blk, kvh_of(pair)] = blocks[blk][1](kvh_of(pair))
        return keys[blk, kvh_of(pair)]

    def logits_of(blk, pair, half):
        return _head_logits(q_ref[blocks[blk][0], pair * LANES:(pair + 1) * LANES], half, key_sets(blk, pair))

    logits = logits_of(*items[0])
    done = []
    for idx, (blk, pair, half) in enumerate(items):
        nxt = logits_of(*items[idx + 1]) if idx + 1 < len(items) else None
        done.append(_head_softmax_pv(logits, key_sets(blk, pair), sink_ref[layer, 2 * pair + half] * LOG2E))
        logits = nxt
        if half == 1:
            lo = _lo_mask(done[-1].shape)
            o_ref[blocks[blk][0], pair * LANES:(pair + 1) * LANES] = jnp.where(lo, done[-2], done[-1])


def _values_with_ones(v2, kvh):
    return jnp.concatenate([_dup_half(v2, kvh), jnp.ones(v2.shape, F32)], axis=1).astype(BF16)


def _attn_ctx_kernel(sink_ref, q_ref, kv_ref, kg_ref, o_ref, kn_ref, *, layer):
    kv = kv_ref[...]
    kn = _head_rms(kv[:, :LANES], kg_ref[...])
    kn_ref[...] = kn
    v2 = kv[:, LANES:]
    kd = [_dup_half(kn, kvh).astype(BF16) for kvh in range(N_KV_HEADS)]
    vd = [_values_with_ones(v2, kvh) for kvh in range(N_KV_HEADS)]
    key_sets_of = lambda kvh: [(lambda: kd[kvh], lambda: vd[kvh], None)]
    _attend_heads(q_ref, [(slice(0, SEQ), key_sets_of)], sink_ref, layer, o_ref)


def _attention_ctx(q, kv, k_norm_g, attn_sink, layer):
    t = q.shape[0]
    return pl.pallas_call(
        functools.partial(_attn_ctx_kernel, layer=layer),
        out_shape=[jax.ShapeDtypeStruct((t, ATT_DIM), F32), jax.ShapeDtypeStruct((t, KV_DIM), F32)],
        grid=(t // SEQ,),
        in_specs=[pl.BlockSpec(memory_space=pltpu.SMEM),
                  pl.BlockSpec((SEQ, ATT_DIM), lambda b: (b, 0)),
                  pl.BlockSpec((SEQ, 2 * KV_DIM), lambda b: (b, 0)),
                  pl.BlockSpec((None, 1, LANES), lambda b: (layer, 0, 0))],
        out_specs=[pl.BlockSpec((SEQ, ATT_DIM), lambda b: (b, 0)),
                   pl.BlockSpec((SEQ, KV_DIM), lambda b: (b, 0))],
        compiler_params=pltpu.CompilerParams(vmem_limit_bytes=VMEM_LIMIT),
        name="attn_ctx",
    )(attn_sink, q, kv, k_norm_g)


QBLK = 128
QSTEP = 512
WIN_KEYS = QBLK + 2 * WINDOW
PREP_ROWS = 512


def _attn_lat_kernel(sink_ref, q_ref, kv_ref, kc_ref, vc_ref, kg_ref, cos_ref, sin_ref, o_ref,
                     kd_s, vd_s, kcd_s, vcd_s, *, layer):
    i = pl.program_id(1)
    n = DEC_SEQ

    @pl.when(i == 0)
    def _():
        for kvh in range(N_KV_HEADS):
            for s in (kd_s, vd_s):
                zpad = jnp.zeros((WINDOW, s.shape[-1]), BF16)
                s[kvh, 0:WINDOW, :] = zpad
                s[kvh, WINDOW + n:2 * WINDOW + n, :] = zpad
            kcd_s[kvh] = _dup_half(kc_ref[...], kvh).astype(BF16)
            vcd_s[kvh] = _values_with_ones(vc_ref[...], kvh)

        def prep(r, carry):
            r0 = pl.multiple_of(r * PREP_ROWS, PREP_ROWS)
            kv = kv_ref[pl.ds(r0, PREP_ROWS), :]
            kn = _head_rms(kv[:, :LANES], kg_ref[...])
            kr = _rope(kn, cos_ref[pl.ds(r0, PREP_ROWS), :], sin_ref[pl.ds(r0, PREP_ROWS), :])
            v2 = kv[:, LANES:]
            for kvh in range(N_KV_HEADS):
                kd_s[kvh, pl.ds(WINDOW + r0, PREP_ROWS), :] = _dup_half(kr, kvh).astype(BF16)
                vd_s[kvh, pl.ds(WINDOW + r0, PREP_ROWS), :] = _values_with_ones(v2, kvh)
            return carry

        lax.fori_loop(0, n // PREP_ROWS, prep, 0)

    rr = lax.broadcasted_iota(jnp.int32, (QBLK, WIN_KEYS), 0)
    jj = lax.broadcasted_iota(jnp.int32, (QBLK, WIN_KEYS), 1)
    band = (jj >= rr) & (jj <= rr + 2 * WINDOW)
    blocks = []
    for j in range(QSTEP // QBLK):
        start = pl.multiple_of(i * QSTEP + j * QBLK, QBLK)
        kpos = jj + (start - WINDOW)
        valid = band & (kpos >= 0) & (kpos < n)
        key_sets_of = lambda kvh, start=start, valid=valid: [
            (lambda: kcd_s[kvh], lambda: vcd_s[kvh], None),
            (lambda: kd_s[kvh, pl.ds(start, WIN_KEYS), :], lambda: vd_s[kvh, pl.ds(start, WIN_KEYS), :], valid)]
        blocks.append((slice(j * QBLK, (j + 1) * QBLK), key_sets_of))
    _attend_heads(q_ref, blocks, sink_ref, layer, o_ref)


def _attention_lat(q, kv, cache_k, cache_v, k_norm_g, attn_sink, cos_t, sin_t, layer):
    t = q.shape[0]
    n = DEC_SEQ
    nblk = n // QSTEP
    return pl.pallas_call(
        functools.partial(_attn_lat_kernel, layer=layer),
        out_shape=jax.ShapeDtypeStruct((t, ATT_DIM), F32),
        grid=(DEC_BATCH, nblk),
        in_specs=[pl.BlockSpec(memory_space=pltpu.SMEM),
                  pl.BlockSpec((QSTEP, ATT_DIM), lambda b, i: (b * nblk + i, 0)),
                  pl.BlockSpec((n, 2 * KV_DIM), lambda b, i: (b, 0)),
                  pl.BlockSpec((None, None, PAST_LEN, KV_DIM), lambda b, i: (b, layer, 0, 0)),
                  pl.BlockSpec((None, None, PAST_LEN, KV_DIM), lambda b, i: (b, layer, 0, 0)),
                  pl.BlockSpec((None, 1, LANES), lambda b, i: (layer, 0, 0)),
                  pl.BlockSpec((n, LANES), lambda b, i: (0, 0)),
                  pl.BlockSpec((n, LANES), lambda b, i: (0, 0))],
        out_specs=pl.BlockSpec((QSTEP, ATT_DIM), lambda b, i: (b * nblk + i, 0)),
        scratch_shapes=[pltpu.VMEM((N_KV_HEADS, n + 2 * WINDOW, LANES), BF16),
                        pltpu.VMEM((N_KV_HEADS, n + 2 * WINDOW, 2 * LANES), BF16),
                        pltpu.VMEM((N_KV_HEADS, PAST_LEN, LANES), BF16),
                        pltpu.VMEM((N_KV_HEADS, PAST_LEN, 2 * LANES), BF16)],
        compiler_params=pltpu.CompilerParams(vmem_limit_bytes=VMEM_LIMIT),
        name="attn_lat",
    )(attn_sink, q, kv, cache_k, cache_v, k_norm_g, cos_t, sin_t)


DN_ROWS = 128
DN_OUT_ROWS = 256
DN_CTX_SEQS = 4


def _dn_conv(x_ref, w_ref, r0, n):
    x = x_ref[pl.ds(r0, DN_ROWS), :]
    total = x_ref.shape[0]
    above = jnp.where(r0 % n > 0, x_ref[pl.ds(jnp.maximum(r0 - 1, 0), 1), :], 0.0)
    below = jnp.where(r0 % n + DN_ROWS < n, x_ref[pl.ds(jnp.minimum(r0 + DN_ROWS, total - 1), 1), :], 0.0)
    row = lax.broadcasted_iota(jnp.int32, x.shape, 0)
    prev = jnp.where(row == 0, above, pltpu.roll(x, 1, axis=0))
    nxt = jnp.where(row == DN_ROWS - 1, below, pltpu.roll(x, DN_ROWS - 1, axis=0))
    return _silu(prev * w_ref[0:1, :] + x * w_ref[1:2, :] + nxt * w_ref[2:3, :])


N_UNITS = 4
UW = N_UNITS * CHUNK
N_LEVELS = 6
M_INCL, M_STRICT, M_EYE = N_LEVELS, N_LEVELS + 1, N_LEVELS + 2
PREP_BLOCK = 8


def _dn_kernel(q_ref, k_ref, v_ref, z_ref, ab_ref, wq_ref, wk_ref, wv_ref, pa_ref, pdt_ref, ng_ref, s0_ref,
               o_ref, sout_ref, qs, ks, vs, gbs, pfs, r_s, au_s, m_s, qa_s, sh_s, el_s, st, mc, ex_s, tri_s, *, n):
    nc = n // CHUNK
    nseq = q_ref.shape[0] // n
    pblk = min(PREP_BLOCK, nseq * nc)

    @pl.when((pl.program_id(0) == 0) & (pl.program_id(1) == 0))
    def _():
        as_f32 = lambda m: jnp.where(m, 1.0, 0.0).astype(F32)
        ii = lax.broadcasted_iota(jnp.int32, (CHUNK, UW), 0)
        ll = lax.broadcasted_iota(jnp.int32, (CHUNK, UW), 1)
        unit_of_lane, jj = ll // CHUNK, ll % CHUNK
        fwd, bwd = unit_of_lane < 2, unit_of_lane >= 2
        for k in range(N_LEVELS):
            rk, ck = ii >> k, jj >> k
            lower = (rk - ck == 1) & ((rk & 1) == 1)
            upper = (ck - rk == 1) & ((ck & 1) == 1)
            mc[k] = as_f32((fwd & lower) | (bwd & upper))
        mc[M_INCL] = as_f32((fwd & (ii >= jj)) | (bwd & (ii <= jj)))
        mc[M_STRICT] = as_f32((fwd & (ii > jj)) | (bwd & (ii < jj)))
        mc[M_EYE] = as_f32(ii == jj)
        ec = lax.broadcasted_iota(jnp.int32, (LANES, 2 * UW), 0)
        el_ = lax.broadcasted_iota(jnp.int32, (LANES, 2 * UW), 1)
        ex_s[...] = as_f32(ec == (el_ // UW) * N_UNITS + (el_ % UW) // CHUNK).astype(BF16)
        ri = lax.broadcasted_iota(jnp.int32, (DN_ROWS, DN_ROWS), 0)
        ti = lax.broadcasted_iota(jnp.int32, (DN_ROWS, DN_ROWS), 1)
        tri_s[...] = as_f32((ti <= ri) & (ri // CHUNK == ti // CHUNK)).astype(BF16)

    def prep_rows(b, carry):
        r0 = pl.multiple_of(b * DN_ROWS, DN_ROWS)
        rows = pl.ds(r0, DN_ROWS)
        qs[rows, :] = _head_l2n(_dn_conv(q_ref, wq_ref, r0, n)) * (DN_HEAD_DIM ** -0.5)
        ks[rows, :] = _head_l2n(_dn_conv(k_ref, wk_ref, r0, n))
        vs[rows, :] = _dn_conv(v_ref, wv_ref, r0, n)
        ab = ab_ref[rows, :]
        lane = lax.broadcasted_iota(jnp.int32, ab.shape, 1)
        gb = jnp.where(lane < N_UNITS, -jnp.exp(pa_ref[...]) * _softplus(ab + pdt_ref[...]), _sigmoid(ab))
        gbs[rows, :] = gb
        hi = gb.astype(BF16)
        r1 = gb - hi.astype(F32)
        mid = r1.astype(BF16)
        low = (r1 - mid.astype(F32)).astype(BF16)
        tri = tri_s[...]
        pfs[rows, :] = _dot(tri, hi) + _dot(tri, mid) + _dot(tri, low)
        return carry

    lax.fori_loop(0, nseq * n // DN_ROWS, prep_rows, 0)

    for q in range(nseq):
        st[q] = jnp.concatenate([s0_ref[q, u // 2, u % 2] for u in range(N_UNITS)], axis=1)
    lo = _lo_mask((CHUNK, LANES))
    lane_c = lax.broadcasted_iota(jnp.int32, (CHUNK, LANES), 1)

    def spread_columns(m):
        hi = m.astype(BF16)
        r1 = m - hi.astype(F32)
        mid = r1.astype(BF16)
        low = (r1 - mid.astype(F32)).astype(BF16)
        e = ex_s[...]
        return _dot(hi, e) + _dot(mid, e) + _dot(low, e)

    def pair_block_diag(x):
        zero = jnp.zeros((CHUNK, LANES), BF16)
        return jnp.concatenate([jnp.where(lo, x, zero), jnp.where(lo, zero, x)], axis=0)

    def unit_dot(a, xb):
        return jnp.concatenate([_dot(a[:, h * LANES:(h + 1) * LANES], pair_block_diag(xb[:, h * LANES:(h + 1) * LANES]))
                                for h in range(2)], axis=1)

    def unit_dot2(a, xb, yb):
        outs = []
        for h in range(2):
            cols = slice(h * LANES, (h + 1) * LANES)
            rhs = jnp.concatenate([pair_block_diag(xb[:, cols]), pair_block_diag(yb[:, cols])], axis=1)
            outs.append(_dot(a[:, cols], rhs))
        return (jnp.concatenate([o[:, :LANES] for o in outs], axis=1),
                jnp.concatenate([o[:, LANES:] for o in outs], axis=1))

    def unit_rows(m):
        t = m.T
        return jnp.concatenate([t[u:u + 1, :] for u in range(N_UNITS)], axis=1)

    dup = lambda x: jnp.concatenate([x, x], axis=1)

    def chunk_prep(blk, carry):
        cs = [blk * pblk + s for s in range(pblk)]
        rows = [pl.ds(pl.multiple_of(c * CHUNK, CHUNK), CHUNK) for c in cs]
        each = lambda f, *lists: [f(*xs) for xs in zip(*lists)]

        gx = [gbs[r, :] for r in rows]
        pc = [pfs[r, :] for r in rows]
        tot = [p[CHUNK - 1:CHUNK, :] for p in pc]
        gcm = each(lambda g, p, t: jnp.where(lane_c < 2, p, t - p + g), gx, pc, tot)
        rsm = each(lambda g, p, t: jnp.where(lane_c < 2, t - p, p - g), gx, pc, tot)
        spread = each(lambda m, g: spread_columns(jnp.where(lane_c < N_UNITS, m, g)), gcm, gx)
        gc4 = [x[:, :UW] for x in spread]
        b4 = [x[:, UW:] for x in spread]
        e_gc = each(jnp.exp, gc4)
        dec = each(lambda c, m: jnp.exp(jnp.where(mc[M_INCL] > 0.0, c - unit_rows(m), NEG)), gc4, gcm)
        kt = [ks[r, :].T for r in rows]
        kt_pair = each(lambda t: jnp.concatenate([t[0:CHUNK], t[CHUNK:]], axis=1), kt)
        kt_bd = each(lambda t: pair_block_diag(t.astype(BF16)), kt_pair)
        gram = each(lambda r, b: dup(_dot(ks[r, :].astype(BF16), b)), rows, kt_bd)
        qk = each(lambda r, b: dup(_dot(qs[r, :].astype(BF16), b)), rows, kt_bd)
        lmat = each(lambda g, b, d: g * b * d * mc[M_STRICT], gram, b4, dec)
        amat = each(lambda m, d: (m * d).astype(BF16), qk, dec)
        t = each(lambda l: mc[M_EYE] - l * mc[0], lmat)
        for k in range(1, N_LEVELS):
            lb = each(lambda l: (l * mc[k]).astype(BF16), lmat)
            tb = each(lambda x: x.astype(BF16), t)
            x = each(lambda a, b: unit_dot(a, b).astype(BF16), tb, lb)
            t = each(lambda t0, a, b: t0 - unit_dot(a, b), t, x, tb)
        tm = each(lambda x: (x - mc[M_EYE]).astype(BF16), t)
        k4 = [dup(ks[r, :]) for r in rows]
        ru = each(lambda r, b: dup(vs[r, :]) * b, rows, b4)
        rw = each(lambda k, b, e: k * b * e, k4, b4, e_gc)
        tuw = each(lambda tmi, u, w: unit_dot2(tmi, u.astype(BF16), w.astype(BF16)), tm, ru, rw)
        ub = each(lambda x, d: (x + d[0]).astype(BF16), ru, tuw)
        wb = each(lambda x, d: (x + d[1]).astype(BF16), rw, tuw)
        kdt = each(lambda t, m: (dup(t) * jnp.exp(unit_rows(m))).astype(BF16), kt_pair, rsm)
        both = each(lambda k, a, w, u: unit_dot2(jnp.concatenate([k, a], axis=0), w, u), kdt, amat, wb, ub)
        mmat, rmat = [x[0][:CHUNK] for x in both], [x[1][:CHUNK] for x in both]
        aw, au = [x[0][CHUNK:] for x in both], [x[1][CHUNK:] for x in both]
        for i, (c, r) in enumerate(zip(cs, rows)):
            m_s[r, :] = mmat[i].astype(BF16)
            r_s[r, :] = rmat[i]
            qa_s[r, :] = (dup(qs[r, :]) * e_gc[i] - aw[i]).astype(BF16)
            au_s[r, :] = au[i]
            el = jnp.concatenate([e_gc[i][CHUNK - 1:CHUNK, :LANES], e_gc[i][0:1, LANES:]], axis=1)
            el_s[c] = jnp.broadcast_to(el, (8, UW))
        return carry

    lax.fori_loop(0, nseq * nc // pblk, chunk_prep, 0)

    def scan_step(i, carry):
        cf = [q * nc + i for q in range(nseq)]
        cb = [q * nc + nc - 1 - i for q in range(nseq)]
        rf = [pl.ds(pl.multiple_of(c * CHUNK, CHUNK), CHUNK) for c in cf]
        rb = [pl.ds(pl.multiple_of(c * CHUNK, CHUNK), CHUNK) for c in cb]
        mixed = lambda ref, q: jnp.concatenate([ref[rf[q], :LANES], ref[rb[q], LANES:]], axis=1)
        s = [st[q] for q in range(nseq)]
        sb = [x.astype(BF16) for x in s]
        ms = [unit_dot(mixed(m_s, q), sb[q]) for q in range(nseq)]
        for q in range(nseq):
            el = jnp.concatenate([el_s[cf[q]][0:1, :LANES], el_s[cb[q]][0:1, LANES:]], axis=1)
            st[q] = s[q] * el - ms[q] + mixed(r_s, q)
            sh_s[rf[q], :LANES] = sb[q][:, :LANES]
            sh_s[rb[q], LANES:] = sb[q][:, LANES:]
        return carry

    lax.fori_loop(0, nc, scan_step, 0)

    for q in range(nseq):
        s = st[q]
        for u in range(N_UNITS):
            sout_ref[q, u // 2, u % 2] = s[:, u * CHUNK:(u + 1) * CHUNK]

    def finish_rows(b, carry):
        r0 = b * DN_OUT_ROWS
        parts = []
        for c in range(DN_OUT_ROWS // CHUNK):
            rows = pl.ds(pl.multiple_of(r0 + c * CHUNK, CHUNK), CHUNK)
            parts.append(unit_dot(qa_s[rows, :], sh_s[rows, :]) + au_s[rows, :])
        o = jnp.concatenate([p[:, :LANES] + p[:, LANES:] for p in parts], axis=0)
        rows = pl.ds(pl.multiple_of(r0, DN_OUT_ROWS), DN_OUT_ROWS)
        o_ref[rows, :] = _head_rms(o, ng_ref[...]) * _silu(z_ref[rows, :])
        return carry

    lax.fori_loop(0, nseq * n // DN_OUT_ROWS, finish_rows, 0)


def _delta_net(dq, dz, dab, dn_conv_w, pa, pdt, dn_norm_g, s0, layer, s0_layer, n, nseq):
    t = dq.shape[0]
    bsz = t // n
    rows = nseq * n
    nc = n // CHUNK
    conv_spec = lambda off: pl.BlockSpec((None, 3, LANES), lambda b, p: (layer, 0, off + p))
    col_spec = lambda off, bufs=2: pl.BlockSpec((rows, LANES), lambda b, p: (b, off + p),
                                                pipeline_mode=pl.Buffered(bufs))
    state_block = (nseq, 2, 2, DN_HEAD_DIM, DN_HEAD_DIM)
    return pl.pallas_call(
        functools.partial(_dn_kernel, n=n),
        out_shape=[jax.ShapeDtypeStruct((t, DN_DIM), F32),
                   jax.ShapeDtypeStruct((bsz, 2, N_DN_HEADS, DN_HEAD_DIM, DN_HEAD_DIM), F32)],
        grid=(bsz // nseq, N_PAIRS),
        in_specs=[col_spec(0), col_spec(N_PAIRS), col_spec(2 * N_PAIRS), col_spec(0, 1), col_spec(0),
                  conv_spec(0), conv_spec(N_PAIRS), conv_spec(2 * N_PAIRS),
                  pl.BlockSpec((None, None, 1, LANES), lambda b, p: (layer, p, 0, 0)),
                  pl.BlockSpec((None, None, 1, LANES), lambda b, p: (layer, p, 0, 0)),
                  pl.BlockSpec((None, 1, LANES), lambda b, p: (layer, 0, 0)),
                  pl.BlockSpec((nseq, None) + state_block[1:], lambda b, p: (b, s0_layer, 0, p, 0, 0))],
        out_specs=[pl.BlockSpec((rows, LANES), lambda b, p: (b, p)),
                   pl.BlockSpec(state_block, lambda b, p: (b, 0, p, 0, 0))],
        scratch_shapes=[pltpu.VMEM((rows, LANES), F32) for _ in range(5)]
                       + [pltpu.VMEM((rows, UW), F32) for _ in range(2)]
                       + [pltpu.VMEM((rows, UW), BF16) for _ in range(3)]
                       + [pltpu.VMEM((nseq * nc, 8, UW), F32),
                          pltpu.VMEM((nseq, CHUNK, UW), F32),
                          pltpu.VMEM((N_LEVELS + 3, CHUNK, UW), F32),
                          pltpu.VMEM((LANES, 2 * UW), BF16),
                          pltpu.VMEM((DN_ROWS, DN_ROWS), BF16)],
        compiler_params=pltpu.CompilerParams(vmem_limit_bytes=(V7X_VMEM_BYTES * 7) // 8),
        name="delta_net",
    )(dq, dq, dq, dz, dab, dn_conv_w, dn_conv_w, dn_conv_w, pa, pdt, dn_norm_g, s0)


def _out_kernel(x_ref, sc_ref, scp_ref, scn_ref, att_ref, dn_ref, mod_ref, cw_ref, wo_ref, n2_ref,
                wg_ref, wu_ref, wd_ref, o_ref, *, tm, tiles_per_seq):
    t = pl.program_id(0) % tiles_per_seq
    sc = sc_ref[...]
    prod = sc[:, SC_DIM:2 * SC_DIM] * sc[:, 2 * SC_DIM:]
    pv = scp_ref[7:8, :]
    nx = scn_ref[0:1, :]
    prev_row = jnp.where(t > 0, pv[:, SC_DIM:2 * SC_DIM] * pv[:, 2 * SC_DIM:], 0.0)
    next_row = jnp.where(t < tiles_per_seq - 1, nx[:, SC_DIM:2 * SC_DIM] * nx[:, 2 * SC_DIM:], 0.0)
    row = lax.broadcasted_iota(jnp.int32, prod.shape, 0)
    p_prev = jnp.where(row == 0, prev_row, pltpu.roll(prod, 1, axis=0))
    p_next = jnp.where(row == tm - 1, next_row, pltpu.roll(prod, tm - 1, axis=0))
    y_sc = sc[:, :SC_DIM] * (p_prev * cw_ref[0:1, :] + prod * cw_ref[1:2, :] + p_next * cw_ref[2:3, :])
    y = (_dot(y_sc.astype(BF16), wo_ref[0:SC_DIM, :])
         + _dot(att_ref[...].astype(BF16), wo_ref[SC_DIM:SC_DIM + ATT_DIM, :])
         + _dot(dn_ref[...].astype(BF16), wo_ref[SC_DIM + ATT_DIM:, :]))
    x1 = x_ref[...] + mod_ref[2:3, :] * y
    ms = jnp.mean(x1 * x1, axis=-1, keepdims=True)
    h2 = (x1 * lax.rsqrt(ms + EPS) * n2_ref[...]) * (1.0 + mod_ref[4:5, :]) + mod_ref[3:4, :]
    hb = h2.astype(BF16)
    act = _silu(_dot(hb, wg_ref[...])) * _dot(hb, wu_ref[...])
    o_ref[...] = x1 + mod_ref[5:6, :] * _dot(act.astype(BF16), wd_ref[...])


def _output_stage(x, sc, att, dn, mod, sc_conv_w, w_out_b, norm2_g, w_gate_b, w_up_b, w_down_b, layer,
                  tm, seq_len, per_batch_mod):
    t = x.shape[0]
    tiles_per_seq = seq_len // tm
    tiles_per_batch = tiles_per_seq if per_batch_mod else None
    halo = tm // 8
    last = t // 8 - 1
    resident = lambda shape: pl.BlockSpec((None,) + shape, lambda i: (layer, 0, 0), pipeline_mode=pl.Buffered(1))
    return pl.pallas_call(
        functools.partial(_out_kernel, tm=tm, tiles_per_seq=tiles_per_seq),
        out_shape=jax.ShapeDtypeStruct((t, D_MODEL), F32),
        grid=(t // tm,),
        in_specs=[pl.BlockSpec((tm, D_MODEL), lambda i: (i, 0)),
                  pl.BlockSpec((tm, 3 * SC_DIM), lambda i: (i, 0)),
                  pl.BlockSpec((8, 3 * SC_DIM), lambda i: (jnp.maximum(i * halo - 1, 0), 0)),
                  pl.BlockSpec((8, 3 * SC_DIM), lambda i: (jnp.minimum((i + 1) * halo, last), 0)),
                  pl.BlockSpec((tm, ATT_DIM), lambda i: (i, 0)),
                  pl.BlockSpec((tm, DN_DIM), lambda i: (i, 0)),
                  pl.BlockSpec((None, None, 6, D_MODEL), _mod_row_map(layer, tiles_per_batch)),
                  pl.BlockSpec((None, 3, SC_DIM), lambda i: (layer, 0, 0)),
                  resident((MIX_DIM, D_MODEL)),
                  pl.BlockSpec((None, 1, D_MODEL), lambda i: (layer, 0, 0)),
                  resident((D_MODEL, D_FF)), resident((D_MODEL, D_FF)), resident((D_FF, D_MODEL))],
        out_specs=pl.BlockSpec((tm, D_MODEL), lambda i: (i, 0)),
        compiler_params=pltpu.CompilerParams(vmem_limit_bytes=VMEM_LIMIT),
        name="out_stage",
    )(x, sc, sc, sc, att, dn, mod, sc_conv_w, w_out_b, norm2_g, w_gate_b, w_up_b, w_down_b)


def _rope_tables():
    pos = jnp.arange(DEC_SEQ)
    half = HEAD_DIM // 4
    inv = 1.0 / (ROPE_BASE ** (jnp.arange(half, dtype=F32) / half))
    ang_r = (pos // GRID_W).astype(F32)[:, None] * inv
    ang_c = (pos % GRID_W).astype(F32)[:, None] * inv
    cos = jnp.concatenate([jnp.cos(ang_r)] * 2 + [jnp.cos(ang_c)] * 2, axis=-1)
    sin = jnp.concatenate([-jnp.sin(ang_r), jnp.sin(ang_r), -jnp.sin(ang_c), jnp.sin(ang_c)], axis=-1)
    return jnp.tile(cos, (1, 2)), jnp.tile(sin, (1, 2))


def _pack_w_in(w_in):
    a0 = OFF_AB
    b0 = OFF_AB + 2 * N_DN_HEADS
    blocks = []
    for p in range(N_PAIRS):
        idx = [a0 + d * N_DN_HEADS + 2 * p + j for d in range(2) for j in range(2)]
        idx += [b0 + d * N_DN_HEADS + 2 * p + j for d in range(2) for j in range(2)]
        blocks += [w_in[:, :, c:c + 1] for c in idx]
        blocks.append(jnp.zeros((DEPTH, D_MODEL, LANES - len(idx)), w_in.dtype))
    return w_in[:, :, :OFF_AB].astype(BF16), jnp.concatenate(blocks, axis=2).astype(BF16)


def _pair_lanes(p):
    x = p.reshape(DEPTH, 2, N_PAIRS, 2).transpose(0, 2, 1, 3).reshape(DEPTH, N_PAIRS, 1, 4)
    return jnp.pad(x, ((0, 0), (0, 0), (0, 0), (0, LANES - 4)))


def kernel(x_prompt, x_sample, cache_k, cache_v, state_delta, c, c_ctx, w_in, w_out, ada_w, ada_b,
           norm1_g, norm2_g, sc_conv_w, dn_conv_w, q_norm_g, k_norm_g, attn_sink, dn_A_log,
           dn_dt_bias, dn_norm_g, w_gate, w_up, w_down):
    w_main_b, w_ab_b = _pack_w_in(w_in)
    w_out_b, w_gate_b, w_up_b, w_down_b = (w.astype(BF16) for w in (w_out, w_gate, w_up, w_down))
    n1 = norm1_g.reshape(DEPTH, 1, D_MODEL)
    n2 = norm2_g.reshape(DEPTH, 1, D_MODEL)
    qg = jnp.tile(q_norm_g, (1, 2)).reshape(DEPTH, 1, LANES)
    kg = jnp.tile(k_norm_g, (1, 2)).reshape(DEPTH, 1, LANES)
    ng = jnp.tile(dn_norm_g, (1, 2)).reshape(DEPTH, 1, LANES)
    pa, pdt = _pair_lanes(dn_A_log), _pair_lanes(dn_dt_bias)
    cos_t, sin_t = _rope_tables()
    ck = cache_k.reshape(DEC_BATCH, DEPTH, PAST_LEN, KV_DIM)
    cv = cache_v.reshape(DEC_BATCH, DEPTH, PAST_LEN, KV_DIM)
    zero_state = jnp.zeros((BATCH, 1, 2, N_DN_HEADS, DN_HEAD_DIM, DN_HEAD_DIM), F32)

    cvecs = jnp.concatenate([c_ctx[None, :], c, jnp.zeros((MOD_ROWS - 1 - DEC_BATCH, D_MODEL), F32)], axis=0)
    mod = _modulation(cvecs, ada_w, ada_b).reshape(DEPTH, MOD_ROWS, 6, D_MODEL)

    xp = x_prompt.reshape(BATCH * SEQ, D_MODEL)
    xs = x_sample.reshape(DEC_BATCH * DEC_SEQ, D_MODEL)
    new_k, new_v, new_s = [], [], []
    for l in range(DEPTH):
        out_args = (mod, sc_conv_w, w_out_b, n2, w_gate_b, w_up_b, w_down_b, l)
        sc, q, kv, dq, dz, dab = _in_projection(xp, mod, n1, w_main_b, w_ab_b, qg, cos_t, sin_t, l, None)
        att, kn = _attention_ctx(q, kv, kg, attn_sink, l)
        dn, s_out = _delta_net(dq, dz, dab, dn_conv_w, pa, pdt, ng, zero_state, l, 0, SEQ, DN_CTX_SEQS)
        xp = _output_stage(xp, sc, att, dn, *out_args, TM, SEQ, False)
        new_k.append(kn.reshape(BATCH, SEQ, N_KV_HEADS, HEAD_DIM))
        new_v.append(kv[:, KV_DIM:].reshape(BATCH, SEQ, N_KV_HEADS, HEAD_DIM))
        new_s.append(s_out)
        sc, q, kv, dq, dz, dab = _in_projection(xs, mod, n1, w_main_b, w_ab_b, qg, cos_t, sin_t, l, DEC_SEQ // TM_IN)
        att = _attention_lat(q, kv, ck, cv, kg, attn_sink, cos_t, sin_t, l)
        dn, _ = _delta_net(dq, dz, dab, dn_conv_w, pa, pdt, ng, state_delta, l, l, DEC_SEQ, 1)
        xs = _output_stage(xs, sc, att, dn, *out_args, TM_LAT, DEC_SEQ, True)
    return (xp.reshape(BATCH, SEQ, D_MODEL), xs.reshape(DEC_BATCH, DEC_SEQ, D_MODEL),
            jnp.stack(new_k, axis=1), jnp.stack(new_v, axis=1), jnp.stack(new_s, axis=1))
```

```python
import functools

import numpy as np
import jax
import jax.numpy as jnp
from jax import lax
from jax.experimental import pallas as pl
from jax.experimental.pallas import tpu as pltpu

F32 = jnp.float32
BF16 = jnp.bfloat16

D_MODEL = 1024
BATCH = 16
SEQ = 256
DEPTH = 2
DEC_BATCH = 4
DEC_SEQ = 4096
PAST_LEN = 512
GRID_W = 64
SC_DIM = 256
N_Q_HEADS = 8
N_KV_HEADS = 2
N_GROUP = N_Q_HEADS // N_KV_HEADS
HEAD_DIM = 64
ATT_DIM = N_Q_HEADS * HEAD_DIM
KV_DIM = N_KV_HEADS * HEAD_DIM
WINDOW = 128
N_DN_HEADS = 4
DN_HEAD_DIM = 64
DN_DIM = N_DN_HEADS * DN_HEAD_DIM
CHUNK = 64
MIX_DIM = SC_DIM + ATT_DIM + DN_DIM
D_FF = -(-8 * D_MODEL // (3 * 256)) * 256
ROPE_BASE = 10000.0
EPS = 1e-6
NEG = -1e30
LOG2E = 1.4426950408889634

LANES = 128
V7X_VMEM_BYTES = 64 * 1024 * 1024
VMEM_LIMIT = (V7X_VMEM_BYTES * 3) // 4

OFF_SC, OFF_Q, OFF_KV, OFF_DQ, OFF_DZ, OFF_AB = 0, 768, 1280, 1536, 2304, 2560
IN_COLS = OFF_AB + 2 * LANES
N_PAIRS = N_DN_HEADS // 2
MOD_ROWS = 8
TM = 256
TM_LAT = 512
TM_IN = 512


def _sigmoid(x):
    return 0.5 * jnp.tanh(0.5 * x) + 0.5


def _silu(x):
    return x * _sigmoid(x)


def _softplus(x):
    return jnp.maximum(x, 0.0) + jnp.log1p(jnp.exp(-jnp.abs(x)))


def _dot(a, b):
    return jnp.dot(a, b, preferred_element_type=F32)


def _dot_nt(a, b):
    return lax.dot_general(a, b, (((1,), (1,)), ((), ())), preferred_element_type=F32)


def _lo_mask(shape):
    return lax.broadcasted_iota(jnp.int32, shape, 1) % LANES < HEAD_DIM


def _half_sums(xx, lo):
    s_lo = jnp.sum(jnp.where(lo, xx, 0.0), axis=-1, keepdims=True)
    s_hi = jnp.sum(jnp.where(lo, 0.0, xx), axis=-1, keepdims=True)
    return jnp.where(lo, s_lo, s_hi)


def _head_rms(x, g_row):
    lo = _lo_mask(x.shape)
    ms = _half_sums(x * x, lo) * (1.0 / HEAD_DIM)
    return x * lax.rsqrt(ms + EPS) * g_row


def _head_l2n(x):
    lo = _lo_mask(x.shape)
    return x * lax.rsqrt(_half_sums(x * x, lo) + EPS)


def _rope(x, cos, sin):
    n = x.shape[1]
    lane = lax.broadcasted_iota(jnp.int32, x.shape, 1)
    swapped = jnp.where((lane & 16) == 0, pltpu.roll(x, n - 16, axis=1), pltpu.roll(x, 16, axis=1))
    return x * cos + swapped * sin


def _dup_half(x, half):
    lo = _lo_mask(x.shape)
    r = pltpu.roll(x, HEAD_DIM, axis=1)
    return jnp.where(lo, x, r) if half == 0 else jnp.where(lo, r, x)


def _mod_kernel(c_ref, w_ref, b_ref, o_ref):
    s = _silu(c_ref[...])
    o_ref[...] = _dot(s.astype(BF16), w_ref[...].astype(BF16)) + b_ref[...]


def _modulation(cvecs, ada_w, ada_b):
    nblk = 6
    return pl.pallas_call(
        _mod_kernel,
        out_shape=jax.ShapeDtypeStruct((DEPTH, MOD_ROWS, 6 * D_MODEL), F32),
        grid=(DEPTH, nblk),
        in_specs=[pl.BlockSpec((MOD_ROWS, D_MODEL), lambda l, j: (0, 0)),
                  pl.BlockSpec((None, D_MODEL, D_MODEL), lambda l, j: (l, 0, j)),
                  pl.BlockSpec((None, 1, D_MODEL), lambda l, j: (l, 0, j))],
        out_specs=pl.BlockSpec((None, MOD_ROWS, D_MODEL), lambda l, j: (l, 0, j)),
        compiler_params=pltpu.CompilerParams(vmem_limit_bytes=VMEM_LIMIT),
        name="adaln_mod",
    )(cvecs, ada_w, ada_b.reshape(DEPTH, 1, 6 * D_MODEL))


def _mod_row_map(layer, tiles_per_batch):
    if tiles_per_batch is None:
        return lambda i: (layer, 0, 0, 0)
    return lambda i: (layer, 1 + i // tiles_per_batch, 0, 0)


def _prep_q(q, qg, cos, sin):
    out = []
    for g in range(ATT_DIM // LANES):
        x = _head_rms(q[:, g * LANES:(g + 1) * LANES], qg)
        if cos is not None:
            x = _rope(x, cos, sin)
        out.append((x * (LOG2E * HEAD_DIM ** -0.5)).astype(BF16))
    return jnp.concatenate(out, axis=1)


def _in_kernel(x_ref, mod_ref, g_ref, w_ref, wab_ref, qg_ref, cos_ref, sin_ref,
               sc_ref, q_ref, kv_ref, dq_ref, dz_ref, ab_ref, *, rotary):
    x = x_ref[...]
    ms = jnp.mean(x * x, axis=-1, keepdims=True)
    h = (x * lax.rsqrt(ms + EPS) * g_ref[...]) * (1.0 + mod_ref[1:2, :]) + mod_ref[0:1, :]
    hb = h.astype(BF16)
    q = _dot(hb, w_ref[:, OFF_Q:OFF_KV])
    q_ref[...] = _prep_q(q, qg_ref[...], cos_ref[...] if rotary else None, sin_ref[...] if rotary else None)
    for ref, a, b in ((sc_ref, OFF_SC, OFF_Q), (kv_ref, OFF_KV, OFF_DQ), (dq_ref, OFF_DQ, OFF_DZ),
                      (dz_ref, OFF_DZ, OFF_AB)):
        ref[...] = _dot(hb, w_ref[:, a:b])
    ab_ref[...] = _dot(hb, wab_ref[...])


def _in_projection(x, mod, norm1_g, w_main_b, w_ab_b, q_norm_g, cos_t, sin_t, layer, tiles_per_batch):
    t = x.shape[0]
    widths = (OFF_Q - OFF_SC, OFF_KV - OFF_Q, OFF_DQ - OFF_KV, OFF_DZ - OFF_DQ, OFF_AB - OFF_DZ, IN_COLS - OFF_AB)
    rotary = tiles_per_batch is not None
    pos_map = (lambda i: (i % tiles_per_batch, 0)) if rotary else (lambda i: (0, 0))
    return pl.pallas_call(
        functools.partial(_in_kernel, rotary=rotary),
        out_shape=[jax.ShapeDtypeStruct((t, w), BF16 if k == 1 else F32) for k, w in enumerate(widths)],
        grid=(t // TM_IN,),
        in_specs=[pl.BlockSpec((TM_IN, D_MODEL), lambda i: (i, 0)),
                  pl.BlockSpec((None, None, 6, D_MODEL), _mod_row_map(layer, tiles_per_batch)),
                  pl.BlockSpec((None, 1, D_MODEL), lambda i: (layer, 0, 0)),
                  pl.BlockSpec((None, D_MODEL, OFF_AB), lambda i: (layer, 0, 0),
                               pipeline_mode=pl.Buffered(1)),
                  pl.BlockSpec((None, D_MODEL, IN_COLS - OFF_AB), lambda i: (layer, 0, 0),
                               pipeline_mode=pl.Buffered(1)),
                  pl.BlockSpec((None, 1, LANES), lambda i: (layer, 0, 0)),
                  pl.BlockSpec((TM_IN, LANES), pos_map),
                  pl.BlockSpec((TM_IN, LANES), pos_map)],
        out_specs=[pl.BlockSpec((TM_IN, w), lambda i: (i, 0)) for w in widths],
        compiler_params=pltpu.CompilerParams(vmem_limit_bytes=VMEM_LIMIT),
        name="in_proj",
    )(x, mod, norm1_g, w_main_b, w_ab_b, q_norm_g, cos_t, sin_t)


def _head_logits(qa, half, key_sets):
    lo = _lo_mask(qa.shape)
    qh = jnp.where(lo if half == 0 else jnp.logical_not(lo), qa, jnp.zeros_like(qa))
    logits = []
    for k, _, mask in key_sets:
        s = _dot_nt(qh, k())
        logits.append(s if mask is None else jnp.where(mask, s, NEG))
    return logits


def _head_softmax_pv(logits, key_sets, sink):
    m = jnp.maximum(jnp.max(jnp.concatenate(logits, axis=1), axis=-1, keepdims=True), sink)
    acc = None
    for s, (_, v1, _) in zip(logits, key_sets):
        o = _dot(jnp.exp2(s - m).astype(BF16), v1())
        acc = o if acc is None else acc + o
    return acc[:, :LANES] / (acc[:, LANES:] + jnp.exp2(sink - m))


def _attend_heads(q_ref, blocks, sink_ref, layer, o_ref):
    n_pairs = ATT_DIM // LANES
    items = [(blk, pair, half) for blk in range(len(blocks)) for pair in range(n_pairs) for half in range(2)]
    kvh_of = lambda pair: (2 * pair) // N_GROUP
    keys = {}

    def key_sets(blk, pair):
        if (blk, kvh_of(pair)) not in keys:
            keys[blk, kvh_of(pair)] = blocks[blk][1](kvh_of(pair))
        return keys[blk, kvh_of(pair)]

    def logits_of(blk, pair, half):
        return _head_logits(q_ref[blocks[blk][0], pair * LANES:(pair + 1) * LANES], half, key_sets(blk, pair))

    logits = logits_of(*items[0])
    done = []
    for idx, (blk, pair, half) in enumerate(items):
        nxt = logits_of(*items[idx + 1]) if idx + 1 < len(items) else None
        done.append(_head_softmax_pv(logits, key_sets(blk, pair), sink_ref[layer, 2 * pair + half] * LOG2E))
        logits = nxt
        if half == 1:
            lo = _lo_mask(done[-1].shape)
            o_ref[blocks[blk][0], pair * LANES:(pair + 1) * LANES] = jnp.where(lo, done[-2], done[-1])


def _values_with_ones(v2, kvh):
    return jnp.concatenate([_dup_half(v2, kvh), jnp.ones(v2.shape, F32)], axis=1).astype(BF16)


def _attn_ctx_kernel(sink_ref, q_ref, kv_ref, kg_ref, o_ref, kn_ref, *, layer):
    kv = kv_ref[...]
    kn = _head_rms(kv[:, :LANES], kg_ref[...])
    kn_ref[...] = kn
    v2 = kv[:, LANES:]
    kd = [_dup_half(kn, kvh).astype(BF16) for kvh in range(N_KV_HEADS)]
    vd = [_values_with_ones(v2, kvh) for kvh in range(N_KV_HEADS)]
    key_sets_of = lambda kvh: [(lambda: kd[kvh], lambda: vd[kvh], None)]
    _attend_heads(q_ref, [(slice(0, SEQ), key_sets_of)], sink_ref, layer, o_ref)


def _attention_ctx(q, kv, k_norm_g, attn_sink, layer):
    t = q.shape[0]
    return pl.pallas_call(
        functools.partial(_attn_ctx_kernel, layer=layer),
        out_shape=[jax.ShapeDtypeStruct((t, ATT_DIM), F32), jax.ShapeDtypeStruct((t, KV_DIM), F32)],
        grid=(t // SEQ,),
        in_specs=[pl.BlockSpec(memory_space=pltpu.SMEM),
                  pl.BlockSpec((SEQ, ATT_DIM), lambda b: (b, 0)),
                  pl.BlockSpec((SEQ, 2 * KV_DIM), lambda b: (b, 0)),
                  pl.BlockSpec((None, 1, LANES), lambda b: (layer, 0, 0))],
        out_specs=[pl.BlockSpec((SEQ, ATT_DIM), lambda b: (b, 0)),
                   pl.BlockSpec((SEQ, KV_DIM), lambda b: (b, 0))],
        compiler_params=pltpu.CompilerParams(vmem_limit_bytes=VMEM_LIMIT),
        name="attn_ctx",
    )(attn_sink, q, kv, k_norm_g)


QBLK = 128
QSTEP = 512
WIN_KEYS = QBLK + 2 * WINDOW
PREP_ROWS = 512


def _attn_lat_kernel(sink_ref, q_ref, kv_ref, kc_ref, vc_ref, kg_ref, cos_ref, sin_ref, o_ref,
                     kd_s, vd_s, kcd_s, vcd_s, *, layer):
    i = pl.program_id(1)
    n = DEC_SEQ

    @pl.when(i == 0)
    def _():
        for kvh in range(N_KV_HEADS):
            for s in (kd_s, vd_s):
                zpad = jnp.zeros((WINDOW, s.shape[-1]), BF16)
                s[kvh, 0:WINDOW, :] = zpad
                s[kvh, WINDOW + n:2 * WINDOW + n, :] = zpad
            kcd_s[kvh] = _dup_half(kc_ref[...], kvh).astype(BF16)
            vcd_s[kvh] = _values_with_ones(vc_ref[...], kvh)

        def prep(r, carry):
            r0 = pl.multiple_of(r * PREP_ROWS, PREP_ROWS)
            kv = kv_ref[pl.ds(r0, PREP_ROWS), :]
            kn = _head_rms(kv[:, :LANES], kg_ref[...])
            kr = _rope(kn, cos_ref[pl.ds(r0, PREP_ROWS), :], sin_ref[pl.ds(r0, PREP_ROWS), :])
            v2 = kv[:, LANES:]
            for kvh in range(N_KV_HEADS):
                kd_s[kvh, pl.ds(WINDOW + r0, PREP_ROWS), :] = _dup_half(kr, kvh).astype(BF16)
                vd_s[kvh, pl.ds(WINDOW + r0, PREP_ROWS), :] = _values_with_ones(v2, kvh)
            return carry

        lax.fori_loop(0, n // PREP_ROWS, prep, 0)

    rr = lax.broadcasted_iota(jnp.int32, (QBLK, WIN_KEYS), 0)
    jj = lax.broadcasted_iota(jnp.int32, (QBLK, WIN_KEYS), 1)
    band = (jj >= rr) & (jj <= rr + 2 * WINDOW)
    blocks = []
    for j in range(QSTEP // QBLK):
        start = pl.multiple_of(i * QSTEP + j * QBLK, QBLK)
        kpos = jj + (start - WINDOW)
        valid = band & (kpos >= 0) & (kpos < n)
        key_sets_of = lambda kvh, start=start, valid=valid: [
            (lambda: kcd_s[kvh], lambda: vcd_s[kvh], None),
            (lambda: kd_s[kvh, pl.ds(start, WIN_KEYS), :], lambda: vd_s[kvh, pl.ds(start, WIN_KEYS), :], valid)]
        blocks.append((slice(j * QBLK, (j + 1) * QBLK), key_sets_of))
    _attend_heads(q_ref, blocks, sink_ref, layer, o_ref)


def _attention_lat(q, kv, cache_k, cache_v, k_norm_g, attn_sink, cos_t, sin_t, layer):
    t = q.shape[0]
    n = DEC_SEQ
    nblk = n // QSTEP
    return pl.pallas_call(
        functools.partial(_attn_lat_kernel, layer=layer),
        out_shape=jax.ShapeDtypeStruct((t, ATT_DIM), F32),
        grid=(DEC_BATCH, nblk),
        in_specs=[pl.BlockSpec(memory_space=pltpu.SMEM),
                  pl.BlockSpec((QSTEP, ATT_DIM), lambda b, i: (b * nblk + i, 0)),
                  pl.BlockSpec((n, 2 * KV_DIM), lambda b, i: (b, 0)),
                  pl.BlockSpec((None, None, PAST_LEN, KV_DIM), lambda b, i: (b, layer, 0, 0)),
                  pl.BlockSpec((None, None, PAST_LEN, KV_DIM), lambda b, i: (b, layer, 0, 0)),
                  pl.BlockSpec((None, 1, LANES), lambda b, i: (layer, 0, 0)),
                  pl.BlockSpec((n, LANES), lambda b, i: (0, 0)),
                  pl.BlockSpec((n, LANES), lambda b, i: (0, 0))],
        out_specs=pl.BlockSpec((QSTEP, ATT_DIM), lambda b, i: (b * nblk + i, 0)),
        scratch_shapes=[pltpu.VMEM((N_KV_HEADS, n + 2 * WINDOW, LANES), BF16),
                        pltpu.VMEM((N_KV_HEADS, n + 2 * WINDOW, 2 * LANES), BF16),
                        pltpu.VMEM((N_KV_HEADS, PAST_LEN, LANES), BF16),
                        pltpu.VMEM((N_KV_HEADS, PAST_LEN, 2 * LANES), BF16)],
        compiler_params=pltpu.CompilerParams(vmem_limit_bytes=VMEM_LIMIT),
        name="attn_lat",
    )(attn_sink, q, kv, cache_k, cache_v, k_norm_g, cos_t, sin_t)


DN_ROWS = 128
DN_OUT_ROWS = 256
DN_CTX_SEQS = 4


def _dn_conv(x_ref, w_ref, r0, n):
    x = x_ref[pl.ds(r0, DN_ROWS), :]
    total = x_ref.shape[0]
    above = jnp.where(r0 % n > 0, x_ref[pl.ds(jnp.maximum(r0 - 1, 0), 1), :], 0.0)
    below = jnp.where(r0 % n + DN_ROWS < n, x_ref[pl.ds(jnp.minimum(r0 + DN_ROWS, total - 1), 1), :], 0.0)
    row = lax.broadcasted_iota(jnp.int32, x.shape, 0)
    prev = jnp.where(row == 0, above, pltpu.roll(x, 1, axis=0))
    nxt = jnp.where(row == DN_ROWS - 1, below, pltpu.roll(x, DN_ROWS - 1, axis=0))
    return _silu(prev * w_ref[0:1, :] + x * w_ref[1:2, :] + nxt * w_ref[2:3, :])


N_UNITS = 4
UW = N_UNITS * CHUNK
N_LEVELS = 6
M_INCL, M_STRICT, M_EYE = N_LEVELS, N_LEVELS + 1, N_LEVELS + 2
PREP_BLOCK = 8


def _dn_kernel(q_ref, k_ref, v_ref, z_ref, ab_ref, wq_ref, wk_ref, wv_ref, pa_ref, pdt_ref, ng_ref, s0_ref,
               o_ref, sout_ref, qs, ks, vs, gbs, pfs, r_s, au_s, m_s, qa_s, sh_s, el_s, st, mc, ex_s, tri_s, *, n):
    nc = n // CHUNK
    nseq = q_ref.shape[0] // n
    pblk = min(PREP_BLOCK, nseq * nc)

    @pl.when((pl.program_id(0) == 0) & (pl.program_id(1) == 0))
    def _():
        as_f32 = lambda m: jnp.where(m, 1.0, 0.0).astype(F32)
        ii = lax.broadcasted_iota(jnp.int32, (CHUNK, UW), 0)
        ll = lax.broadcasted_iota(jnp.int32, (CHUNK, UW), 1)
        unit_of_lane, jj = ll // CHUNK, ll % CHUNK
        fwd, bwd = unit_of_lane < 2, unit_of_lane >= 2
        for k in range(N_LEVELS):
            rk, ck = ii >> k, jj >> k
            lower = (rk - ck == 1) & ((rk & 1) == 1)
            upper = (ck - rk == 1) & ((ck & 1) == 1)
            mc[k] = as_f32((fwd & lower) | (bwd & upper))
        mc[M_INCL] = as_f32((fwd & (ii >= jj)) | (bwd & (ii <= jj)))
        mc[M_STRICT] = as_f32((fwd & (ii > jj)) | (bwd & (ii < jj)))
        mc[M_EYE] = as_f32(ii == jj)
        ec = lax.broadcasted_iota(jnp.int32, (LANES, 2 * UW), 0)
        el_ = lax.broadcasted_iota(jnp.int32, (LANES, 2 * UW), 1)
        ex_s[...] = as_f32(ec == (el_ // UW) * N_UNITS + (el_ % UW) // CHUNK).astype(BF16)
        ri = lax.broadcasted_iota(jnp.int32, (DN_ROWS, DN_ROWS), 0)
        ti = lax.broadcasted_iota(jnp.int32, (DN_ROWS, DN_ROWS), 1)
        tri_s[...] = as_f32((ti <= ri) & (ri // CHUNK == ti // CHUNK)).astype(BF16)

    def prep_rows(b, carry):
        r0 = pl.multiple_of(b * DN_ROWS, DN_ROWS)
        rows = pl.ds(r0, DN_ROWS)
        qs[rows, :] = _head_l2n(_dn_conv(q_ref, wq_ref, r0, n)) * (DN_HEAD_DIM ** -0.5)
        ks[rows, :] = _head_l2n(_dn_conv(k_ref, wk_ref, r0, n))
        vs[rows, :] = _dn_conv(v_ref, wv_ref, r0, n)
        ab = ab_ref[rows, :]
        lane = lax.broadcasted_iota(jnp.int32, ab.shape, 1)
        gb = jnp.where(lane < N_UNITS, -jnp.exp(pa_ref[...]) * _softplus(ab + pdt_ref[...]), _sigmoid(ab))
        gbs[rows, :] = gb
        hi = gb.astype(BF16)
        r1 = gb - hi.astype(F32)
        mid = r1.astype(BF16)
        low = (r1 - mid.astype(F32)).astype(BF16)
        tri = tri_s[...]
        pfs[rows, :] = _dot(tri, hi) + _dot(tri, mid) + _dot(tri, low)
        return carry

    lax.fori_loop(0, nseq * n // DN_ROWS, prep_rows, 0)

    for q in range(nseq):
        st[q] = jnp.concatenate([s0_ref[q, u // 2, u % 2] for u in range(N_UNITS)], axis=1)
    lo = _lo_mask((CHUNK, LANES))
    lane_c = lax.broadcasted_iota(jnp.int32, (CHUNK, LANES), 1)

    def spread_columns(m):
        hi = m.astype(BF16)
        r1 = m - hi.astype(F32)
        mid = r1.astype(BF16)
        low = (r1 - mid.astype(F32)).astype(BF16)
        e = ex_s[...]
        return _dot(hi, e) + _dot(mid, e) + _dot(low, e)

    def pair_block_diag(x):
        zero = jnp.zeros((CHUNK, LANES), BF16)
        return jnp.concatenate([jnp.where(lo, x, zero), jnp.where(lo, zero, x)], axis=0)

    def unit_dot(a, xb):
        return jnp.concatenate([_dot(a[:, h * LANES:(h + 1) * LANES], pair_block_diag(xb[:, h * LANES:(h + 1) * LANES]))
                                for h in range(2)], axis=1)

    def unit_dot2(a, xb, yb):
        outs = []
        for h in range(2):
            cols = slice(h * LANES, (h + 1) * LANES)
            rhs = jnp.concatenate([pair_block_diag(xb[:, cols]), pair_block_diag(yb[:, cols])], axis=1)
            outs.append(_dot(a[:, cols], rhs))
        return (jnp.concatenate([o[:, :LANES] for o in outs], axis=1),
                jnp.concatenate([o[:, LANES:] for o in outs], axis=1))

    def unit_rows(m):
        t = m.T
        return jnp.concatenate([t[u:u + 1, :] for u in range(N_UNITS)], axis=1)

    dup = lambda x: jnp.concatenate([x, x], axis=1)

    def chunk_prep(cs, between=()):
        rows = [pl.ds(pl.multiple_of(c * CHUNK, CHUNK), CHUNK) for c in cs]
        each = lambda f, *lists: [f(*xs) for xs in zip(*lists)]
        pending = list(between)

        gx = [gbs[r, :] for r in rows]
        pc = [pfs[r, :] for r in rows]
        tot = [p[CHUNK - 1:CHUNK, :] for p in pc]
        gcm = each(lambda g, p, t: jnp.where(lane_c < 2, p, t - p + g), gx, pc, tot)
        rsm = each(lambda g, p, t: jnp.where(lane_c < 2, t - p, p - g), gx, pc, tot)
        spread = each(lambda m, g: spread_columns(jnp.where(lane_c < N_UNITS, m, g)), gcm, gx)
        gc4 = [x[:, :UW] for x in spread]
        b4 = [x[:, UW:] for x in spread]
        e_gc = each(jnp.exp, gc4)
        dec = each(lambda c, m: jnp.exp(jnp.where(mc[M_INCL] > 0.0, c - unit_rows(m), NEG)), gc4, gcm)
        kt = [ks[r, :].T for r in rows]
        kt_pair = each(lambda t: jnp.concatenate([t[0:CHUNK], t[CHUNK:]], axis=1), kt)
        kt_bd = each(lambda t: pair_block_diag(t.astype(BF16)), kt_pair)
        gram = each(lambda r, b: dup(_dot(ks[r, :].astype(BF16), b)), rows, kt_bd)
        qk = each(lambda r, b: dup(_dot(qs[r, :].astype(BF16), b)), rows, kt_bd)
        lmat = each(lambda g, b, d: g * b * d * mc[M_STRICT], gram, b4, dec)
        amat = each(lambda m, d: (m * d).astype(BF16), qk, dec)
        t = each(lambda l: mc[M_EYE] - l * mc[0], lmat)
        for k in range(1, N_LEVELS):
            lb = each(lambda l: (l * mc[k]).astype(BF16), lmat)
            tb = each(lambda x: x.astype(BF16), t)
            x = each(lambda a, b: unit_dot(a, b).astype(BF16), tb, lb)
            t = each(lambda t0, a, b: t0 - unit_dot(a, b), t, x, tb)
            if pending:
                pending.pop(0)()
        tm = each(lambda x: (x - mc[M_EYE]).astype(BF16), t)
        k4 = [dup(ks[r, :]) for r in rows]
        ru = each(lambda r, b: dup(vs[r, :]) * b, rows, b4)
        rw = each(lambda k, b, e: k * b * e, k4, b4, e_gc)
        tuw = each(lambda tmi, u, w: unit_dot2(tmi, u.astype(BF16), w.astype(BF16)), tm, ru, rw)
        ub = each(lambda x, d: (x + d[0]).astype(BF16), ru, tuw)
        wb = each(lambda x, d: (x + d[1]).astype(BF16), rw, tuw)
        kdt = each(lambda t, m: (dup(t) * jnp.exp(unit_rows(m))).astype(BF16), kt_pair, rsm)
        both = each(lambda k, a, w, u: unit_dot2(jnp.concatenate([k, a], axis=0), w, u), kdt, amat, wb, ub)
        mmat, rmat = [x[0][:CHUNK] for x in both], [x[1][:CHUNK] for x in both]
        aw, au = [x[0][CHUNK:] for x in both], [x[1][CHUNK:] for x in both]
        for i, (c, r) in enumerate(zip(cs, rows)):
            m_s[r, :] = mmat[i].astype(BF16)
            r_s[r, :] = rmat[i]
            qa_s[r, :] = (dup(qs[r, :]) * e_gc[i] - aw[i]).astype(BF16)
            au_s[r, :] = au[i]
            el = jnp.concatenate([e_gc[i][CHUNK - 1:CHUNK, :LANES], e_gc[i][0:1, LANES:]], axis=1)
            el_s[c] = jnp.broadcast_to(el, (8, UW))

    def scan_step(i):
        cf = [q * nc + i for q in range(nseq)]
        cb = [q * nc + nc - 1 - i for q in range(nseq)]
        rf = [pl.ds(pl.multiple_of(c * CHUNK, CHUNK), CHUNK) for c in cf]
        rb = [pl.ds(pl.multiple_of(c * CHUNK, CHUNK), CHUNK) for c in cb]
        mixed = lambda ref, q: jnp.concatenate([ref[rf[q], :LANES], ref[rb[q], LANES:]], axis=1)
        s = [st[q] for q in range(nseq)]
        sb = [x.astype(BF16) for x in s]
        ms = [unit_dot(mixed(m_s, q), sb[q]) for q in range(nseq)]
        for q in range(nseq):
            el = jnp.concatenate([el_s[cf[q]][0:1, :LANES], el_s[cb[q]][0:1, LANES:]], axis=1)
            st[q] = s[q] * el - ms[q] + mixed(r_s, q)
            sh_s[rf[q], :LANES] = sb[q][:, :LANES]
            sh_s[rb[q], LANES:] = sb[q][:, LANES:]

    def scan_only(i, carry):
        scan_step(i)
        return carry

    half = pblk // 2
    if nseq == 1 and nc % pblk == 0 and half <= N_LEVELS - 1:
        ends = lambda t: [t * half + s for s in range(half)] + [nc - 1 - (t * half + s) for s in range(half)]
        steps = lambda t: [functools.partial(scan_step, t * half + s) for s in range(half)]
        trips = nc // pblk
        chunk_prep(ends(0))

        def prep_and_scan(t, carry):
            chunk_prep(ends(t), steps(t - 1))
            return carry

        lax.fori_loop(1, trips, prep_and_scan, 0)
        lax.fori_loop((trips - 1) * half, nc, scan_only, 0)
    else:
        def prep_block(blk, carry):
            chunk_prep([blk * pblk + s for s in range(pblk)])
            return carry

        lax.fori_loop(0, nseq * nc // pblk, prep_block, 0)
        lax.fori_loop(0, nc, scan_only, 0)

    for q in range(nseq):
        s = st[q]
        for u in range(N_UNITS):
            sout_ref[q, u // 2, u % 2] = s[:, u * CHUNK:(u + 1) * CHUNK]

    def finish_rows(b, carry):
        r0 = b * DN_OUT_ROWS
        parts = []
        for c in range(DN_OUT_ROWS // CHUNK):
            rows = pl.ds(pl.multiple_of(r0 + c * CHUNK, CHUNK), CHUNK)
            parts.append(unit_dot(qa_s[rows, :], sh_s[rows, :]) + au_s[rows, :])
        o = jnp.concatenate([p[:, :LANES] + p[:, LANES:] for p in parts], axis=0)
        rows = pl.ds(pl.multiple_of(r0, DN_OUT_ROWS), DN_OUT_ROWS)
        o_ref[rows, :] = _head_rms(o, ng_ref[...]) * _silu(z_ref[rows, :])
        return carry

    lax.fori_loop(0, nseq * n // DN_OUT_ROWS, finish_rows, 0)


def _delta_net(dq, dz, dab, dn_conv_w, pa, pdt, dn_norm_g, s0, layer, s0_layer, n, nseq):
    t = dq.shape[0]
    bsz = t // n
    rows = nseq * n
    nc = n // CHUNK
    conv_spec = lambda off: pl.BlockSpec((None, 3, LANES), lambda b, p: (layer, 0, off + p))
    col_spec = lambda off, bufs=2: pl.BlockSpec((rows, LANES), lambda b, p: (b, off + p),
                                                pipeline_mode=pl.Buffered(bufs))
    state_block = (nseq, 2, 2, DN_HEAD_DIM, DN_HEAD_DIM)
    return pl.pallas_call(
        functools.partial(_dn_kernel, n=n),
        out_shape=[jax.ShapeDtypeStruct((t, DN_DIM), F32),
                   jax.ShapeDtypeStruct((bsz, 2, N_DN_HEADS, DN_HEAD_DIM, DN_HEAD_DIM), F32)],
        grid=(bsz // nseq, N_PAIRS),
        in_specs=[col_spec(0), col_spec(N_PAIRS), col_spec(2 * N_PAIRS), col_spec(0, 1), col_spec(0),
                  conv_spec(0), conv_spec(N_PAIRS), conv_spec(2 * N_PAIRS),
                  pl.BlockSpec((None, None, 1, LANES), lambda b, p: (layer, p, 0, 0)),
                  pl.BlockSpec((None, None, 1, LANES), lambda b, p: (layer, p, 0, 0)),
                  pl.BlockSpec((None, 1, LANES), lambda b, p: (layer, 0, 0)),
                  pl.BlockSpec((nseq, None) + state_block[1:], lambda b, p: (b, s0_layer, 0, p, 0, 0))],
        out_specs=[pl.BlockSpec((rows, LANES), lambda b, p: (b, p)),
                   pl.BlockSpec(state_block, lambda b, p: (b, 0, p, 0, 0))],
        scratch_shapes=[pltpu.VMEM((rows, LANES), F32) for _ in range(5)]
                       + [pltpu.VMEM((rows, UW), F32) for _ in range(2)]
                       + [pltpu.VMEM((rows, UW), BF16) for _ in range(3)]
                       + [pltpu.VMEM((nseq * nc, 8, UW), F32),
                          pltpu.VMEM((nseq, CHUNK, UW), F32),
                          pltpu.VMEM((N_LEVELS + 3, CHUNK, UW), F32),
                          pltpu.VMEM((LANES, 2 * UW), BF16),
                          pltpu.VMEM((DN_ROWS, DN_ROWS), BF16)],
        compiler_params=pltpu.CompilerParams(vmem_limit_bytes=(V7X_VMEM_BYTES * 7) // 8),
        name="delta_net",
    )(dq, dq, dq, dz, dab, dn_conv_w, dn_conv_w, dn_conv_w, pa, pdt, dn_norm_g, s0)


def _out_kernel(x_ref, sc_ref, scp_ref, scn_ref, att_ref, dn_ref, mod_ref, cw_ref, wo_ref, n2_ref,
                wg_ref, wu_ref, wd_ref, o_ref, *, tm, tiles_per_seq):
    t = pl.program_id(0) % tiles_per_seq
    sc = sc_ref[...]
    prod = sc[:, SC_DIM:2 * SC_DIM] * sc[:, 2 * SC_DIM:]
    pv = scp_ref[7:8, :]
    nx = scn_ref[0:1, :]
    prev_row = jnp.where(t > 0, pv[:, SC_DIM:2 * SC_DIM] * pv[:, 2 * SC_DIM:], 0.0)
    next_row = jnp.where(t < tiles_per_seq - 1, nx[:, SC_DIM:2 * SC_DIM] * nx[:, 2 * SC_DIM:], 0.0)
    row = lax.broadcasted_iota(jnp.int32, prod.shape, 0)
    p_prev = jnp.where(row == 0, prev_row, pltpu.roll(prod, 1, axis=0))
    p_next = jnp.where(row == tm - 1, next_row, pltpu.roll(prod, tm - 1, axis=0))
    y_sc = sc[:, :SC_DIM] * (p_prev * cw_ref[0:1, :] + prod * cw_ref[1:2, :] + p_next * cw_ref[2:3, :])
    y = (_dot(y_sc.astype(BF16), wo_ref[0:SC_DIM, :])
         + _dot(att_ref[...].astype(BF16), wo_ref[SC_DIM:SC_DIM + ATT_DIM, :])
         + _dot(dn_ref[...].astype(BF16), wo_ref[SC_DIM + ATT_DIM:, :]))
    x1 = x_ref[...] + mod_ref[2:3, :] * y
    ms = jnp.mean(x1 * x1, axis=-1, keepdims=True)
    h2 = (x1 * lax.rsqrt(ms + EPS) * n2_ref[...]) * (1.0 + mod_ref[4:5, :]) + mod_ref[3:4, :]
    hb = h2.astype(BF16)
    act = _silu(_dot(hb, wg_ref[...])) * _dot(hb, wu_ref[...])
    o_ref[...] = x1 + mod_ref[5:6, :] * _dot(act.astype(BF16), wd_ref[...])


def _output_stage(x, sc, att, dn, mod, sc_conv_w, w_out_b, norm2_g, w_gate_b, w_up_b, w_down_b, layer,
                  tm, seq_len, per_batch_mod):
    t = x.shape[0]
    tiles_per_seq = seq_len // tm
    tiles_per_batch = tiles_per_seq if per_batch_mod else None
    halo = tm // 8
    last = t // 8 - 1
    resident = lambda shape: pl.BlockSpec((None,) + shape, lambda i: (layer, 0, 0), pipeline_mode=pl.Buffered(1))
    return pl.pallas_call(
        functools.partial(_out_kernel, tm=tm, tiles_per_seq=tiles_per_seq),
        out_shape=jax.ShapeDtypeStruct((t, D_MODEL), F32),
        grid=(t // tm,),
        in_specs=[pl.BlockSpec((tm, D_MODEL), lambda i: (i, 0)),
                  pl.BlockSpec((tm, 3 * SC_DIM), lambda i: (i, 0)),
                  pl.BlockSpec((8, 3 * SC_DIM), lambda i: (jnp.maximum(i * halo - 1, 0), 0)),
                  pl.BlockSpec((8, 3 * SC_DIM), lambda i: (jnp.minimum((i + 1) * halo, last), 0)),
                  pl.BlockSpec((tm, ATT_DIM), lambda i: (i, 0)),
                  pl.BlockSpec((tm, DN_DIM), lambda i: (i, 0)),
                  pl.BlockSpec((None, None, 6, D_MODEL), _mod_row_map(layer, tiles_per_batch)),
                  pl.BlockSpec((None, 3, SC_DIM), lambda i: (layer, 0, 0)),
                  resident((MIX_DIM, D_MODEL)),
                  pl.BlockSpec((None, 1, D_MODEL), lambda i: (layer, 0, 0)),
                  resident((D_MODEL, D_FF)), resident((D_MODEL, D_FF)), resident((D_FF, D_MODEL))],
        out_specs=pl.BlockSpec((tm, D_MODEL), lambda i: (i, 0)),
        compiler_params=pltpu.CompilerParams(vmem_limit_bytes=VMEM_LIMIT),
        name="out_stage",
    )(x, sc, sc, sc, att, dn, mod, sc_conv_w, w_out_b, norm2_g, w_gate_b, w_up_b, w_down_b)


def _rope_tables():
    pos = jnp.arange(DEC_SEQ)
    half = HEAD_DIM // 4
    inv = 1.0 / (ROPE_BASE ** (jnp.arange(half, dtype=F32) / half))
    ang_r = (pos // GRID_W).astype(F32)[:, None] * inv
    ang_c = (pos % GRID_W).astype(F32)[:, None] * inv
    cos = jnp.concatenate([jnp.cos(ang_r)] * 2 + [jnp.cos(ang_c)] * 2, axis=-1)
    sin = jnp.concatenate([-jnp.sin(ang_r), jnp.sin(ang_r), -jnp.sin(ang_c), jnp.sin(ang_c)], axis=-1)
    return jnp.tile(cos, (1, 2)), jnp.tile(sin, (1, 2))


def _pack_w_in(w_in):
    a0 = OFF_AB
    b0 = OFF_AB + 2 * N_DN_HEADS
    blocks = []
    for p in range(N_PAIRS):
        idx = [a0 + d * N_DN_HEADS + 2 * p + j for d in range(2) for j in range(2)]
        idx += [b0 + d * N_DN_HEADS + 2 * p + j for d in range(2) for j in range(2)]
        blocks += [w_in[:, :, c:c + 1] for c in idx]
        blocks.append(jnp.zeros((DEPTH, D_MODEL, LANES - len(idx)), w_in.dtype))
    return w_in[:, :, :OFF_AB].astype(BF16), jnp.concatenate(blocks, axis=2).astype(BF16)


def _pair_lanes(p):
    x = p.reshape(DEPTH, 2, N_PAIRS, 2).transpose(0, 2, 1, 3).reshape(DEPTH, N_PAIRS, 1, 4)
    return jnp.pad(x, ((0, 0), (0, 0), (0, 0), (0, LANES - 4)))


def kernel(x_prompt, x_sample, cache_k, cache_v, state_delta, c, c_ctx, w_in, w_out, ada_w, ada_b,
           norm1_g, norm2_g, sc_conv_w, dn_conv_w, q_norm_g, k_norm_g, attn_sink, dn_A_log,
           dn_dt_bias, dn_norm_g, w_gate, w_up, w_down):
    w_main_b, w_ab_b = _pack_w_in(w_in)
    w_out_b, w_gate_b, w_up_b, w_down_b = (w.astype(BF16) for w in (w_out, w_gate, w_up, w_down))
    n1 = norm1_g.reshape(DEPTH, 1, D_MODEL)
    n2 = norm2_g.reshape(DEPTH, 1, D_MODEL)
    qg = jnp.tile(q_norm_g, (1, 2)).reshape(DEPTH, 1, LANES)
    kg = jnp.tile(k_norm_g, (1, 2)).reshape(DEPTH, 1, LANES)
    ng = jnp.tile(dn_norm_g, (1, 2)).reshape(DEPTH, 1, LANES)
    pa, pdt = _pair_lanes(dn_A_log), _pair_lanes(dn_dt_bias)
    cos_t, sin_t = _rope_tables()
    ck = cache_k.reshape(DEC_BATCH, DEPTH, PAST_LEN, KV_DIM)
    cv = cache_v.reshape(DEC_BATCH, DEPTH, PAST_LEN, KV_DIM)
    zero_state = jnp.zeros((BATCH, 1, 2, N_DN_HEADS, DN_HEAD_DIM, DN_HEAD_DIM), F32)

    cvecs = jnp.concatenate([c_ctx[None, :], c, jnp.zeros((MOD_ROWS - 1 - DEC_BATCH, D_MODEL), F32)], axis=0)
    mod = _modulation(cvecs, ada_w, ada_b).reshape(DEPTH, MOD_ROWS, 6, D_MODEL)

    xp = x_prompt.reshape(BATCH * SEQ, D_MODEL)
    xs = x_sample.reshape(DEC_BATCH * DEC_SEQ, D_MODEL)
    new_k, new_v, new_s = [], [], []
    for l in range(DEPTH):
        out_args = (mod, sc_conv_w, w_out_b, n2, w_gate_b, w_up_b, w_down_b, l)
        sc, q, kv, dq, dz, dab = _in_projection(xp, mod, n1, w_main_b, w_ab_b, qg, cos_t, sin_t, l, None)
        att, kn = _attention_ctx(q, kv, kg, attn_sink, l)
        dn, s_out = _delta_net(dq, dz, dab, dn_conv_w, pa, pdt, ng, zero_state, l, 0, SEQ, DN_CTX_SEQS)
        xp = _output_stage(xp, sc, att, dn, *out_args, TM, SEQ, False)
        new_k.append(kn.reshape(BATCH, SEQ, N_KV_HEADS, HEAD_DIM))
        new_v.append(kv[:, KV_DIM:].reshape(BATCH, SEQ, N_KV_HEADS, HEAD_DIM))
        new_s.append(s_out)
        sc, q, kv, dq, dz, dab = _in_projection(xs, mod, n1, w_main_b, w_ab_b, qg, cos_t, sin_t, l, DEC_SEQ // TM_IN)
        att = _attention_lat(q, kv, ck, cv, kg, attn_sink, cos_t, sin_t, l)
        dn, _ = _delta_net(dq, dz, dab, dn_conv_w, pa, pdt, ng, state_delta, l, l, DEC_SEQ, 1)
        xs = _output_stage(xs, sc, att, dn, *out_args, TM_LAT, DEC_SEQ, True)
    return (xp.reshape(BATCH, SEQ, D_MODEL), xs.reshape(DEC_BATCH, DEC_SEQ, D_MODEL),
            jnp.stack(new_k, axis=1), jnp.stack(new_v, axis=1), jnp.stack(new_s, axis=1))
```

```python
import functools

import numpy as np
import jax
import jax.numpy as jnp
from jax import lax
from jax.experimental import pallas as pl
from jax.experimental.pallas import tpu as pltpu

F32 = jnp.float32
BF16 = jnp.bfloat16

D_MODEL = 1024
BATCH = 16
SEQ = 256
DEPTH = 2
DEC_BATCH = 4
DEC_SEQ = 4096
PAST_LEN = 512
GRID_W = 64
SC_DIM = 256
N_Q_HEADS = 8
N_KV_HEADS = 2
N_GROUP = N_Q_HEADS // N_KV_HEADS
HEAD_DIM = 64
ATT_DIM = N_Q_HEADS * HEAD_DIM
KV_DIM = N_KV_HEADS * HEAD_DIM
WINDOW = 128
N_DN_HEADS = 4
DN_HEAD_DIM = 64
DN_DIM = N_DN_HEADS * DN_HEAD_DIM
CHUNK = 64
MIX_DIM = SC_DIM + ATT_DIM + DN_DIM
D_FF = -(-8 * D_MODEL // (3 * 256)) * 256
ROPE_BASE = 10000.0
EPS = 1e-6
NEG = -1e30
LOG2E = 1.4426950408889634

LANES = 128
V7X_VMEM_BYTES = 64 * 1024 * 1024
VMEM_LIMIT = (V7X_VMEM_BYTES * 3) // 4

OFF_SC, OFF_Q, OFF_KV, OFF_DQ, OFF_DZ, OFF_AB = 0, 768, 1280, 1536, 2304, 2560
IN_COLS = OFF_AB + 2 * LANES
N_PAIRS = N_DN_HEADS // 2
MOD_ROWS = 8
TM = 256
TM_LAT = 512
TM_IN = 512


def _sigmoid(x):
    return 0.5 * jnp.tanh(0.5 * x) + 0.5


def _silu(x):
    return x * _sigmoid(x)


def _softplus(x):
    return jnp.maximum(x, 0.0) + jnp.log1p(jnp.exp(-jnp.abs(x)))


def _dot(a, b):
    return jnp.dot(a, b, preferred_element_type=F32)


def _dot_nt(a, b):
    return lax.dot_general(a, b, (((1,), (1,)), ((), ())), preferred_element_type=F32)


def _lo_mask(shape):
    return lax.broadcasted_iota(jnp.int32, shape, 1) % LANES < HEAD_DIM


def _half_sums(xx, lo):
    s_lo = jnp.sum(jnp.where(lo, xx, 0.0), axis=-1, keepdims=True)
    s_hi = jnp.sum(jnp.where(lo, 0.0, xx), axis=-1, keepdims=True)
    return jnp.where(lo, s_lo, s_hi)


def _head_rms(x, g_row):
    lo = _lo_mask(x.shape)
    ms = _half_sums(x * x, lo) * (1.0 / HEAD_DIM)
    return x * lax.rsqrt(ms + EPS) * g_row


def _head_l2n(x):
    lo = _lo_mask(x.shape)
    return x * lax.rsqrt(_half_sums(x * x, lo) + EPS)


def _rope(x, cos, sin):
    n = x.shape[1]
    lane = lax.broadcasted_iota(jnp.int32, x.shape, 1)
    swapped = jnp.where((lane & 16) == 0, pltpu.roll(x, n - 16, axis=1), pltpu.roll(x, 16, axis=1))
    return x * cos + swapped * sin


def _dup_half(x, half):
    lo = _lo_mask(x.shape)
    r = pltpu.roll(x, HEAD_DIM, axis=1)
    return jnp.where(lo, x, r) if half == 0 else jnp.where(lo, r, x)


def _mod_kernel(c_ref, w_ref, b_ref, o_ref):
    s = _silu(c_ref[...])
    o_ref[...] = _dot(s.astype(BF16), w_ref[...].astype(BF16)) + b_ref[...]


def _modulation(cvecs, ada_w, ada_b):
    nblk = 6
    return pl.pallas_call(
        _mod_kernel,
        out_shape=jax.ShapeDtypeStruct((DEPTH, MOD_ROWS, 6 * D_MODEL), F32),
        grid=(DEPTH, nblk),
        in_specs=[pl.BlockSpec((MOD_ROWS, D_MODEL), lambda l, j: (0, 0)),
                  pl.BlockSpec((None, D_MODEL, D_MODEL), lambda l, j: (l, 0, j)),
                  pl.BlockSpec((None, 1, D_MODEL), lambda l, j: (l, 0, j))],
        out_specs=pl.BlockSpec((None, MOD_ROWS, D_MODEL), lambda l, j: (l, 0, j)),
        compiler_params=pltpu.CompilerParams(vmem_limit_bytes=VMEM_LIMIT),
        name="adaln_mod",
    )(cvecs, ada_w, ada_b.reshape(DEPTH, 1, 6 * D_MODEL))


def _mod_row_map(layer, tiles_per_batch):
    if tiles_per_batch is None:
        return lambda i: (layer, 0, 0, 0)
    return lambda i: (layer, 1 + i // tiles_per_batch, 0, 0)


def _prep_q(q, qg, cos, sin):
    out = []
    for g in range(ATT_DIM // LANES):
        x = _head_rms(q[:, g * LANES:(g + 1) * LANES], qg)
        if cos is not None:
            x = _rope(x, cos, sin)
        out.append((x * (LOG2E * HEAD_DIM ** -0.5)).astype(BF16))
    return jnp.concatenate(out, axis=1)


def _in_kernel(x_ref, mod_ref, g_ref, w_ref, wab_ref, qg_ref, cos_ref, sin_ref,
               sc_ref, q_ref, kv_ref, dq_ref, dz_ref, ab_ref, *, rotary):
    x = x_ref[...]
    ms = jnp.mean(x * x, axis=-1, keepdims=True)
    h = (x * lax.rsqrt(ms + EPS) * g_ref[...]) * (1.0 + mod_ref[1:2, :]) + mod_ref[0:1, :]
    hb = h.astype(BF16)
    q = _dot(hb, w_ref[:, OFF_Q:OFF_KV])
    q_ref[...] = _prep_q(q, qg_ref[...], cos_ref[...] if rotary else None, sin_ref[...] if rotary else None)
    for ref, a, b in ((sc_ref, OFF_SC, OFF_Q), (kv_ref, OFF_KV, OFF_DQ), (dq_ref, OFF_DQ, OFF_DZ),
                      (dz_ref, OFF_DZ, OFF_AB)):
        ref[...] = _dot(hb, w_ref[:, a:b])
    ab_ref[...] = _dot(hb, wab_ref[...])


def _in_projection(x, mod, norm1_g, w_main_b, w_ab_b, q_norm_g, cos_t, sin_t, layer, tiles_per_batch):
    t = x.shape[0]
    widths = (OFF_Q - OFF_SC, OFF_KV - OFF_Q, OFF_DQ - OFF_KV, OFF_DZ - OFF_DQ, OFF_AB - OFF_DZ, IN_COLS - OFF_AB)
    rotary = tiles_per_batch is not None
    pos_map = (lambda i: (i % tiles_per_batch, 0)) if rotary else (lambda i: (0, 0))
    return pl.pallas_call(
        functools.partial(_in_kernel, rotary=rotary),
        out_shape=[jax.ShapeDtypeStruct((t, w), BF16 if k == 1 else F32) for k, w in enumerate(widths)],
        grid=(t // TM_IN,),
        in_specs=[pl.BlockSpec((TM_IN, D_MODEL), lambda i: (i, 0)),
                  pl.BlockSpec((None, None, 6, D_MODEL), _mod_row_map(layer, tiles_per_batch)),
                  pl.BlockSpec((None, 1, D_MODEL), lambda i: (layer, 0, 0)),
                  pl.BlockSpec((None, D_MODEL, OFF_AB), lambda i: (layer, 0, 0),
                               pipeline_mode=pl.Buffered(1)),
                  pl.BlockSpec((None, D_MODEL, IN_COLS - OFF_AB), lambda i: (layer, 0, 0),
                               pipeline_mode=pl.Buffered(1)),
                  pl.BlockSpec((None, 1, LANES), lambda i: (layer, 0, 0)),
                  pl.BlockSpec((TM_IN, LANES), pos_map),
                  pl.BlockSpec((TM_IN, LANES), pos_map)],
        out_specs=[pl.BlockSpec((TM_IN, w), lambda i: (i, 0)) for w in widths],
        compiler_params=pltpu.CompilerParams(vmem_limit_bytes=VMEM_LIMIT),
        name="in_proj",
    )(x, mod, norm1_g, w_main_b, w_ab_b, q_norm_g, cos_t, sin_t)


def _head_logits(qa, half, key_sets):
    lo = _lo_mask(qa.shape)
    qh = jnp.where(lo if half == 0 else jnp.logical_not(lo), qa, jnp.zeros_like(qa))
    logits = []
    for k, _, mask in key_sets:
        s = _dot_nt(qh, k())
        logits.append(s if mask is None else jnp.where(mask, s, NEG))
    return logits


def _head_softmax_pv(logits, key_sets, sink):
    m = jnp.maximum(jnp.max(jnp.concatenate(logits, axis=1), axis=-1, keepdims=True), sink)
    acc = None
    for s, (_, v1, _) in zip(logits, key_sets):
        o = _dot(jnp.exp2(s - m).astype(BF16), v1())
        acc = o if acc is None else acc + o
    return acc[:, :LANES] / (acc[:, LANES:] + jnp.exp2(sink - m))


def _attend_heads(q_ref, blocks, sink_ref, layer, o_ref):
    n_pairs = ATT_DIM // LANES
    items = [(blk, pair, half) for blk in range(len(blocks)) for pair in range(n_pairs) for half in range(2)]
    kvh_of = lambda pair: (2 * pair) // N_GROUP
    keys = {}

    def key_sets(blk, pair):
        if (blk, kvh_of(pair)) not in keys:
            keys[blk, kvh_of(pair)] = blocks[blk][1](kvh_of(pair))
        return keys[blk, kvh_of(pair)]

    def logits_of(blk, pair, half):
        return _head_logits(q_ref[blocks[blk][0], pair * LANES:(pair + 1) * LANES], half, key_sets(blk, pair))

    logits = logits_of(*items[0])
    done = []
    for idx, (blk, pair, half) in enumerate(items):
        nxt = logits_of(*items[idx + 1]) if idx + 1 < len(items) else None
        done.append(_head_softmax_pv(logits, key_sets(blk, pair), sink_ref[layer, 2 * pair + half] * LOG2E))
        logits = nxt
        if half == 1:
            lo = _lo_mask(done[-1].shape)
            o_ref[blocks[blk][0], pair * LANES:(pair + 1) * LANES] = jnp.where(lo, done[-2], done[-1])


def _values_with_ones(v2, kvh):
    return jnp.concatenate([_dup_half(v2, kvh), jnp.ones(v2.shape, F32)], axis=1).astype(BF16)


def _attn_ctx_kernel(sink_ref, q_ref, kv_ref, kg_ref, o_ref, kn_ref, *, layer):
    kv = kv_ref[...]
    kn = _head_rms(kv[:, :LANES], kg_ref[...])
    kn_ref[...] = kn
    v2 = kv[:, LANES:]
    kd = [_dup_half(kn, kvh).astype(BF16) for kvh in range(N_KV_HEADS)]
    vd = [_values_with_ones(v2, kvh) for kvh in range(N_KV_HEADS)]
    key_sets_of = lambda kvh: [(lambda: kd[kvh], lambda: vd[kvh], None)]
    _attend_heads(q_ref, [(slice(0, SEQ), key_sets_of)], sink_ref, layer, o_ref)


def _attention_ctx(q, kv, k_norm_g, attn_sink, layer):
    t = q.shape[0]
    return pl.pallas_call(
        functools.partial(_attn_ctx_kernel, layer=layer),
        out_shape=[jax.ShapeDtypeStruct((t, ATT_DIM), F32), jax.ShapeDtypeStruct((t, KV_DIM), F32)],
        grid=(t // SEQ,),
        in_specs=[pl.BlockSpec(memory_space=pltpu.SMEM),
                  pl.BlockSpec((SEQ, ATT_DIM), lambda b: (b, 0)),
                  pl.BlockSpec((SEQ, 2 * KV_DIM), lambda b: (b, 0)),
                  pl.BlockSpec((None, 1, LANES), lambda b: (layer, 0, 0))],
        out_specs=[pl.BlockSpec((SEQ, ATT_DIM), lambda b: (b, 0)),
                   pl.BlockSpec((SEQ, KV_DIM), lambda b: (b, 0))],
        compiler_params=pltpu.CompilerParams(vmem_limit_bytes=VMEM_LIMIT),
        name="attn_ctx",
    )(attn_sink, q, kv, k_norm_g)


QBLK = 128
QSTEP = 512
WIN_KEYS = QBLK + 2 * WINDOW
PREP_ROWS = 512


def _attn_lat_kernel(sink_ref, q_ref, kv_ref, kc_ref, vc_ref, kg_ref, cos_ref, sin_ref, o_ref,
                     kd_s, vd_s, kcd_s, vcd_s, *, layer):
    i = pl.program_id(1)
    n = DEC_SEQ

    @pl.when(i == 0)
    def _():
        for kvh in range(N_KV_HEADS):
            for s in (kd_s, vd_s):
                zpad = jnp.zeros((WINDOW, s.shape[-1]), BF16)
                s[kvh, 0:WINDOW, :] = zpad
                s[kvh, WINDOW + n:2 * WINDOW + n, :] = zpad
            kcd_s[kvh] = _dup_half(kc_ref[...], kvh).astype(BF16)
            vcd_s[kvh] = _values_with_ones(vc_ref[...], kvh)

        def prep(r, carry):
            r0 = pl.multiple_of(r * PREP_ROWS, PREP_ROWS)
            kv = kv_ref[pl.ds(r0, PREP_ROWS), :]
            kn = _head_rms(kv[:, :LANES], kg_ref[...])
            kr = _rope(kn, cos_ref[pl.ds(r0, PREP_ROWS), :], sin_ref[pl.ds(r0, PREP_ROWS), :])
            v2 = kv[:, LANES:]
            for kvh in range(N_KV_HEADS):
                kd_s[kvh, pl.ds(WINDOW + r0, PREP_ROWS), :] = _dup_half(kr, kvh).astype(BF16)
                vd_s[kvh, pl.ds(WINDOW + r0, PREP_ROWS), :] = _values_with_ones(v2, kvh)
            return carry

        lax.fori_loop(0, n // PREP_ROWS, prep, 0)

    rr = lax.broadcasted_iota(jnp.int32, (QBLK, WIN_KEYS), 0)
    jj = lax.broadcasted_iota(jnp.int32, (QBLK, WIN_KEYS), 1)
    band = (jj >= rr) & (jj <= rr + 2 * WINDOW)
    blocks = []
    for j in range(QSTEP // QBLK):
        start = pl.multiple_of(i * QSTEP + j * QBLK, QBLK)
        kpos = jj + (start - WINDOW)
        valid = band & (kpos >= 0) & (kpos < n)
        key_sets_of = lambda kvh, start=start, valid=valid: [
            (lambda: kcd_s[kvh], lambda: vcd_s[kvh], None),
            (lambda: kd_s[kvh, pl.ds(start, WIN_KEYS), :], lambda: vd_s[kvh, pl.ds(start, WIN_KEYS), :], valid)]
        blocks.append((slice(j * QBLK, (j + 1) * QBLK), key_sets_of))
    _attend_heads(q_ref, blocks, sink_ref, layer, o_ref)


def _attention_lat(q, kv, cache_k, cache_v, k_norm_g, attn_sink, cos_t, sin_t, layer):
    t = q.shape[0]
    n = DEC_SEQ
    nblk = n // QSTEP
    return pl.pallas_call(
        functools.partial(_attn_lat_kernel, layer=layer),
        out_shape=jax.ShapeDtypeStruct((t, ATT_DIM), F32),
        grid=(DEC_BATCH, nblk),
        in_specs=[pl.BlockSpec(memory_space=pltpu.SMEM),
                  pl.BlockSpec((QSTEP, ATT_DIM), lambda b, i: (b * nblk + i, 0)),
                  pl.BlockSpec((n, 2 * KV_DIM), lambda b, i: (b, 0)),
                  pl.BlockSpec((None, None, PAST_LEN, KV_DIM), lambda b, i: (b, layer, 0, 0)),
                  pl.BlockSpec((None, None, PAST_LEN, KV_DIM), lambda b, i: (b, layer, 0, 0)),
                  pl.BlockSpec((None, 1, LANES), lambda b, i: (layer, 0, 0)),
                  pl.BlockSpec((n, LANES), lambda b, i: (0, 0)),
                  pl.BlockSpec((n, LANES), lambda b, i: (0, 0))],
        out_specs=pl.BlockSpec((QSTEP, ATT_DIM), lambda b, i: (b * nblk + i, 0)),
        scratch_shapes=[pltpu.VMEM((N_KV_HEADS, n + 2 * WINDOW, LANES), BF16),
                        pltpu.VMEM((N_KV_HEADS, n + 2 * WINDOW, 2 * LANES), BF16),
                        pltpu.VMEM((N_KV_HEADS, PAST_LEN, LANES), BF16),
                        pltpu.VMEM((N_KV_HEADS, PAST_LEN, 2 * LANES), BF16)],
        compiler_params=pltpu.CompilerParams(vmem_limit_bytes=VMEM_LIMIT),
        name="attn_lat",
    )(attn_sink, q, kv, cache_k, cache_v, k_norm_g, cos_t, sin_t)


DN_ROWS = 128
DN_OUT_ROWS = 256
DN_CTX_SEQS = 4


def _dn_conv(x_ref, w_ref, r0, n):
    x = x_ref[pl.ds(r0, DN_ROWS), :]
    total = x_ref.shape[0]
    above = jnp.where(r0 % n > 0, x_ref[pl.ds(jnp.maximum(r0 - 1, 0), 1), :], 0.0)
    below = jnp.where(r0 % n + DN_ROWS < n, x_ref[pl.ds(jnp.minimum(r0 + DN_ROWS, total - 1), 1), :], 0.0)
    row = lax.broadcasted_iota(jnp.int32, x.shape, 0)
    prev = jnp.where(row == 0, above, pltpu.roll(x, 1, axis=0))
    nxt = jnp.where(row == DN_ROWS - 1, below, pltpu.roll(x, DN_ROWS - 1, axis=0))
    return _silu(prev * w_ref[0:1, :] + x * w_ref[1:2, :] + nxt * w_ref[2:3, :])


N_UNITS = 4
UW = N_UNITS * CHUNK
N_LEVELS = 6
M_INCL, M_STRICT, M_EYE = N_LEVELS, N_LEVELS + 1, N_LEVELS + 2
PREP_BLOCK = 8


def _dn_kernel(q_ref, k_ref, v_ref, z_ref, ab_ref, wq_ref, wk_ref, wv_ref, pa_ref, pdt_ref, ng_ref, s0_ref,
               o_ref, sout_ref, qs, ks, vs, gbs, pfs, r_s, au_s, m_s, qa_s, sh_s, el_s, st, mc, ex_s, tri_s, *, n):
    nc = n // CHUNK
    nseq = q_ref.shape[0] // n
    pblk = min(PREP_BLOCK, nseq * nc)

    @pl.when((pl.program_id(0) == 0) & (pl.program_id(1) == 0))
    def _():
        as_f32 = lambda m: jnp.where(m, 1.0, 0.0).astype(F32)
        ii = lax.broadcasted_iota(jnp.int32, (CHUNK, UW), 0)
        ll = lax.broadcasted_iota(jnp.int32, (CHUNK, UW), 1)
        unit_of_lane, jj = ll // CHUNK, ll % CHUNK
        fwd, bwd = unit_of_lane < 2, unit_of_lane >= 2
        for k in range(N_LEVELS):
            rk, ck = ii >> k, jj >> k
            lower = (rk - ck == 1) & ((rk & 1) == 1)
            upper = (ck - rk == 1) & ((ck & 1) == 1)
            mc[k] = as_f32((fwd & lower) | (bwd & upper))
        mc[M_INCL] = as_f32((fwd & (ii >= jj)) | (bwd & (ii <= jj)))
        mc[M_STRICT] = as_f32((fwd & (ii > jj)) | (bwd & (ii < jj)))
        mc[M_EYE] = as_f32(ii == jj)
        ec = lax.broadcasted_iota(jnp.int32, (LANES, 2 * UW), 0)
        el_ = lax.broadcasted_iota(jnp.int32, (LANES, 2 * UW), 1)
        ex_s[...] = as_f32(ec == (el_ // UW) * N_UNITS + (el_ % UW) // CHUNK).astype(BF16)
        ri = lax.broadcasted_iota(jnp.int32, (DN_ROWS, DN_ROWS), 0)
        ti = lax.broadcasted_iota(jnp.int32, (DN_ROWS, DN_ROWS), 1)
        tri_s[...] = as_f32((ti <= ri) & (ri // CHUNK == ti // CHUNK)).astype(BF16)

    def prep_rows(b, carry):
        r0 = pl.multiple_of(b * DN_ROWS, DN_ROWS)
        rows = pl.ds(r0, DN_ROWS)
        qs[rows, :] = _head_l2n(_dn_conv(q_ref, wq_ref, r0, n)) * (DN_HEAD_DIM ** -0.5)
        ks[rows, :] = _head_l2n(_dn_conv(k_ref, wk_ref, r0, n))
        vs[rows, :] = _dn_conv(v_ref, wv_ref, r0, n)
        ab = ab_ref[rows, :]
        lane = lax.broadcasted_iota(jnp.int32, ab.shape, 1)
        gb = jnp.where(lane < N_UNITS, -jnp.exp(pa_ref[...]) * _softplus(ab + pdt_ref[...]), _sigmoid(ab))
        gbs[rows, :] = gb
        hi = gb.astype(BF16)
        r1 = gb - hi.astype(F32)
        mid = r1.astype(BF16)
        low = (r1 - mid.astype(F32)).astype(BF16)
        tri = tri_s[...]
        pfs[rows, :] = _dot(tri, hi) + _dot(tri, mid) + _dot(tri, low)
        return carry

    lax.fori_loop(0, nseq * n // DN_ROWS, prep_rows, 0)

    for q in range(nseq):
        st[q] = jnp.concatenate([s0_ref[q, u // 2, u % 2] for u in range(N_UNITS)], axis=1)
    lo = _lo_mask((CHUNK, LANES))
    lane_c = lax.broadcasted_iota(jnp.int32, (CHUNK, LANES), 1)

    def spread_columns(m):
        hi = m.astype(BF16)
        r1 = m - hi.astype(F32)
        mid = r1.astype(BF16)
        low = (r1 - mid.astype(F32)).astype(BF16)
        e = ex_s[...]
        return _dot(hi, e) + _dot(mid, e) + _dot(low, e)

    def pair_block_diag(x):
        zero = jnp.zeros((CHUNK, LANES), BF16)
        return jnp.concatenate([jnp.where(lo, x, zero), jnp.where(lo, zero, x)], axis=0)

    def unit_dot(a, xb):
        return jnp.concatenate([_dot(a[:, h * LANES:(h + 1) * LANES], pair_block_diag(xb[:, h * LANES:(h + 1) * LANES]))
                                for h in range(2)], axis=1)

    def unit_dot2(a, xb, yb):
        outs = []
        for h in range(2):
            cols = slice(h * LANES, (h + 1) * LANES)
            rhs = jnp.concatenate([pair_block_diag(xb[:, cols]), pair_block_diag(yb[:, cols])], axis=1)
            outs.append(_dot(a[:, cols], rhs))
        return (jnp.concatenate([o[:, :LANES] for o in outs], axis=1),
                jnp.concatenate([o[:, LANES:] for o in outs], axis=1))

    def unit_rows(m):
        t = m.T
        return jnp.concatenate([t[u:u + 1, :] for u in range(N_UNITS)], axis=1)

    dup = lambda x: jnp.concatenate([x, x], axis=1)

    def chunk_prep(cs, between=()):
        rows = [pl.ds(pl.multiple_of(c * CHUNK, CHUNK), CHUNK) for c in cs]
        each = lambda f, *lists: [f(*xs) for xs in zip(*lists)]
        pending = list(between)

        gx = [gbs[r, :] for r in rows]
        pc = [pfs[r, :] for r in rows]
        tot = [p[CHUNK - 1:CHUNK, :] for p in pc]
        gcm = each(lambda g, p, t: jnp.where(lane_c < 2, p, t - p + g), gx, pc, tot)
        rsm = each(lambda g, p, t: jnp.where(lane_c < 2, t - p, p - g), gx, pc, tot)
        spread = each(lambda m, g: spread_columns(jnp.where(lane_c < N_UNITS, m, g)), gcm, gx)
        gc4 = [x[:, :UW] for x in spread]
        b4 = [x[:, UW:] for x in spread]
        e_gc = each(jnp.exp, gc4)
        dec = each(lambda c, m: jnp.exp(jnp.where(mc[M_INCL] > 0.0, c - unit_rows(m), NEG)), gc4, gcm)
        kt = [ks[r, :].T for r in rows]
        kt_pair = each(lambda t: jnp.concatenate([t[0:CHUNK], t[CHUNK:]], axis=1), kt)
        kt_bd = each(lambda t: pair_block_diag(t.astype(BF16)), kt_pair)
        gram = each(lambda r, b: dup(_dot(ks[r, :].astype(BF16), b)), rows, kt_bd)
        qk = each(lambda r, b: dup(_dot(qs[r, :].astype(BF16), b)), rows, kt_bd)
        lmat = each(lambda g, b, d: g * b * d * mc[M_STRICT], gram, b4, dec)
        amat = each(lambda m, d: (m * d).astype(BF16), qk, dec)
        t = each(lambda l: mc[M_EYE] - l * mc[0], lmat)
        for k in range(1, N_LEVELS):
            lb = each(lambda l: (l * mc[k]).astype(BF16), lmat)
            tb = each(lambda x: x.astype(BF16), t)
            x = each(lambda a, b: unit_dot(a, b).astype(BF16), tb, lb)
            t = each(lambda t0, a, b: t0 - unit_dot(a, b), t, x, tb)
            if pending:
                pending.pop(0)()
        tm = each(lambda x: (x - mc[M_EYE]).astype(BF16), t)
        k4 = [dup(ks[r, :]) for r in rows]
        ru = each(lambda r, b: dup(vs[r, :]) * b, rows, b4)
        rw = each(lambda k, b, e: k * b * e, k4, b4, e_gc)
        tuw = each(lambda tmi, u, w: unit_dot2(tmi, u.astype(BF16), w.astype(BF16)), tm, ru, rw)
        ub = each(lambda x, d: (x + d[0]).astype(BF16), ru, tuw)
        wb = each(lambda x, d: (x + d[1]).astype(BF16), rw, tuw)
        kdt = each(lambda t, m: (dup(t) * jnp.exp(unit_rows(m))).astype(BF16), kt_pair, rsm)
        both = each(lambda k, a, w, u: unit_dot2(jnp.concatenate([k, a], axis=0), w, u), kdt, amat, wb, ub)
        mmat, rmat = [x[0][:CHUNK] for x in both], [x[1][:CHUNK] for x in both]
        aw, au = [x[0][CHUNK:] for x in both], [x[1][CHUNK:] for x in both]
        for i, (c, r) in enumerate(zip(cs, rows)):
            m_s[r, :] = mmat[i].astype(BF16)
            r_s[r, :] = rmat[i]
            qa_s[r, :] = (dup(qs[r, :]) * e_gc[i] - aw[i]).astype(BF16)
            au_s[r, :] = au[i]
            el = jnp.concatenate([e_gc[i][CHUNK - 1:CHUNK, :LANES], e_gc[i][0:1, LANES:]], axis=1)
            el_s[c] = jnp.broadcast_to(el, (8, UW))

    def scan_step(i):
        cf = [q * nc + i for q in range(nseq)]
        cb = [q * nc + nc - 1 - i for q in range(nseq)]
        rf = [pl.ds(pl.multiple_of(c * CHUNK, CHUNK), CHUNK) for c in cf]
        rb = [pl.ds(pl.multiple_of(c * CHUNK, CHUNK), CHUNK) for c in cb]
        mixed = lambda ref, q: jnp.concatenate([ref[rf[q], :LANES], ref[rb[q], LANES:]], axis=1)
        s = [st[q] for q in range(nseq)]
        sb = [x.astype(BF16) for x in s]
        for q in range(nseq):
            sh_s[rf[q], :LANES] = sb[q][:, :LANES]
            sh_s[rb[q], LANES:] = sb[q][:, LANES:]
        ms = [unit_dot(mixed(m_s, q), sb[q]) for q in range(nseq)]
        for q in range(nseq):
            el = jnp.concatenate([el_s[cf[q]][0:1, :LANES], el_s[cb[q]][0:1, LANES:]], axis=1)
            st[q] = s[q] * el - ms[q] + mixed(r_s, q)

    def finish_chunks(r0, count):
        parts = []
        for c in range(count):
            rows = pl.ds(pl.multiple_of(r0 + c * CHUNK, CHUNK), CHUNK)
            parts.append(unit_dot(qa_s[rows, :], sh_s[rows, :]) + au_s[rows, :])
        o = jnp.concatenate([p[:, :LANES] + p[:, LANES:] for p in parts], axis=0)
        rows = pl.ds(pl.multiple_of(r0, count * CHUNK), count * CHUNK)
        o_ref[rows, :] = _head_rms(o, ng_ref[...]) * _silu(z_ref[rows, :])

    def finish_rows(b, carry):
        finish_chunks(b * DN_OUT_ROWS, DN_OUT_ROWS // CHUNK)
        return carry

    def scan_only(i, carry):
        scan_step(i)
        return carry

    half = pblk // 2
    if nseq == 1 and nc % pblk == 0 and half <= N_LEVELS - 1:
        ends = lambda t: [t * half + s for s in range(half)] + [nc - 1 - (t * half + s) for s in range(half)]
        steps = lambda t: [functools.partial(scan_step, t * half + s) for s in range(half)]
        trips = nc // pblk
        chunk_prep(ends(0))

        def prep_and_scan(t, carry):
            chunk_prep(ends(t), steps(t - 1))
            return carry

        lax.fori_loop(1, trips, prep_and_scan, 0)
        mid = nc // 2
        lax.fori_loop((trips - 1) * half, mid + 2, scan_only, 0)

        def finish_pair(j):
            finish_chunks((mid - 1 - j) * CHUNK, 1)
            finish_chunks((mid + j) * CHUNK, 1)

        def finish_and_scan(i, carry):
            finish_pair(2 * i)
            finish_pair(2 * i + 1)
            scan_step(mid + 2 + 2 * i)
            scan_step(mid + 3 + 2 * i)
            return carry

        lax.fori_loop(0, (mid - 2) // 2, finish_and_scan, 0)
        finish_pair(mid - 2)
        finish_pair(mid - 1)
    else:
        def prep_block(blk, carry):
            chunk_prep([blk * pblk + s for s in range(pblk)])
            return carry

        lax.fori_loop(0, nseq * nc // pblk, prep_block, 0)
        lax.fori_loop(0, nc, scan_only, 0)
        lax.fori_loop(0, nseq * n // DN_OUT_ROWS, finish_rows, 0)

    for q in range(nseq):
        s = st[q]
        for u in range(N_UNITS):
            sout_ref[q, u // 2, u % 2] = s[:, u * CHUNK:(u + 1) * CHUNK]


def _delta_net(dq, dz, dab, dn_conv_w, pa, pdt, dn_norm_g, s0, layer, s0_layer, n, nseq):
    t = dq.shape[0]
    bsz = t // n
    rows = nseq * n
    nc = n // CHUNK
    conv_spec = lambda off: pl.BlockSpec((None, 3, LANES), lambda b, p: (layer, 0, off + p))
    col_spec = lambda off, bufs=2: pl.BlockSpec((rows, LANES), lambda b, p: (b, off + p),
                                                pipeline_mode=pl.Buffered(bufs))
    state_block = (nseq, 2, 2, DN_HEAD_DIM, DN_HEAD_DIM)
    return pl.pallas_call(
        functools.partial(_dn_kernel, n=n),
        out_shape=[jax.ShapeDtypeStruct((t, DN_DIM), F32),
                   jax.ShapeDtypeStruct((bsz, 2, N_DN_HEADS, DN_HEAD_DIM, DN_HEAD_DIM), F32)],
        grid=(bsz // nseq, N_PAIRS),
        in_specs=[col_spec(0), col_spec(N_PAIRS), col_spec(2 * N_PAIRS), col_spec(0, 1), col_spec(0),
                  conv_spec(0), conv_spec(N_PAIRS), conv_spec(2 * N_PAIRS),
                  pl.BlockSpec((None, None, 1, LANES), lambda b, p: (layer, p, 0, 0)),
                  pl.BlockSpec((None, None, 1, LANES), lambda b, p: (layer, p, 0, 0)),
                  pl.BlockSpec((None, 1, LANES), lambda b, p: (layer, 0, 0)),
                  pl.BlockSpec((nseq, None) + state_block[1:], lambda b, p: (b, s0_layer, 0, p, 0, 0))],
        out_specs=[pl.BlockSpec((rows, LANES), lambda b, p: (b, p)),
                   pl.BlockSpec(state_block, lambda b, p: (b, 0, p, 0, 0))],
        scratch_shapes=[pltpu.VMEM((rows, LANES), F32) for _ in range(5)]
                       + [pltpu.VMEM((rows, UW), F32) for _ in range(2)]
                       + [pltpu.VMEM((rows, UW), BF16) for _ in range(3)]
                       + [pltpu.VMEM((nseq * nc, 8, UW), F32),
                          pltpu.VMEM((nseq, CHUNK, UW), F32),
                          pltpu.VMEM((N_LEVELS + 3, CHUNK, UW), F32),
                          pltpu.VMEM((LANES, 2 * UW), BF16),
                          pltpu.VMEM((DN_ROWS, DN_ROWS), BF16)],
        compiler_params=pltpu.CompilerParams(vmem_limit_bytes=(V7X_VMEM_BYTES * 7) // 8),
        name="delta_net",
    )(dq, dq, dq, dz, dab, dn_conv_w, dn_conv_w, dn_conv_w, pa, pdt, dn_norm_g, s0)


def _out_kernel(x_ref, sc_ref, scp_ref, scn_ref, att_ref, dn_ref, mod_ref, cw_ref, wo_ref, n2_ref,
                wg_ref, wu_ref, wd_ref, o_ref, *, tm, tiles_per_seq):
    t = pl.program_id(0) % tiles_per_seq
    sc = sc_ref[...]
    prod = sc[:, SC_DIM:2 * SC_DIM] * sc[:, 2 * SC_DIM:]
    pv = scp_ref[7:8, :]
    nx = scn_ref[0:1, :]
    prev_row = jnp.where(t > 0, pv[:, SC_DIM:2 * SC_DIM] * pv[:, 2 * SC_DIM:], 0.0)
    next_row = jnp.where(t < tiles_per_seq - 1, nx[:, SC_DIM:2 * SC_DIM] * nx[:, 2 * SC_DIM:], 0.0)
    row = lax.broadcasted_iota(jnp.int32, prod.shape, 0)
    p_prev = jnp.where(row == 0, prev_row, pltpu.roll(prod, 1, axis=0))
    p_next = jnp.where(row == tm - 1, next_row, pltpu.roll(prod, tm - 1, axis=0))
    y_sc = sc[:, :SC_DIM] * (p_prev * cw_ref[0:1, :] + prod * cw_ref[1:2, :] + p_next * cw_ref[2:3, :])
    y = (_dot(y_sc.astype(BF16), wo_ref[0:SC_DIM, :])
         + _dot(att_ref[...].astype(BF16), wo_ref[SC_DIM:SC_DIM + ATT_DIM, :])
         + _dot(dn_ref[...].astype(BF16), wo_ref[SC_DIM + ATT_DIM:, :]))
    x1 = x_ref[...] + mod_ref[2:3, :] * y
    ms = jnp.mean(x1 * x1, axis=-1, keepdims=True)
    h2 = (x1 * lax.rsqrt(ms + EPS) * n2_ref[...]) * (1.0 + mod_ref[4:5, :]) + mod_ref[3:4, :]
    hb = h2.astype(BF16)
    act = _silu(_dot(hb, wg_ref[...])) * _dot(hb, wu_ref[...])
    o_ref[...] = x1 + mod_ref[5:6, :] * _dot(act.astype(BF16), wd_ref[...])


def _output_stage(x, sc, att, dn, mod, sc_conv_w, w_out_b, norm2_g, w_gate_b, w_up_b, w_down_b, layer,
                  tm, seq_len, per_batch_mod):
    t = x.shape[0]
    tiles_per_seq = seq_len // tm
    tiles_per_batch = tiles_per_seq if per_batch_mod else None
    halo = tm // 8
    last = t // 8 - 1
    resident = lambda shape: pl.BlockSpec((None,) + shape, lambda i: (layer, 0, 0), pipeline_mode=pl.Buffered(1))
    return pl.pallas_call(
        functools.partial(_out_kernel, tm=tm, tiles_per_seq=tiles_per_seq),
        out_shape=jax.ShapeDtypeStruct((t, D_MODEL), F32),
        grid=(t // tm,),
        in_specs=[pl.BlockSpec((tm, D_MODEL), lambda i: (i, 0)),
                  pl.BlockSpec((tm, 3 * SC_DIM), lambda i: (i, 0)),
                  pl.BlockSpec((8, 3 * SC_DIM), lambda i: (jnp.maximum(i * halo - 1, 0), 0)),
                  pl.BlockSpec((8, 3 * SC_DIM), lambda i: (jnp.minimum((i + 1) * halo, last), 0)),
                  pl.BlockSpec((tm, ATT_DIM), lambda i: (i, 0)),
                  pl.BlockSpec((tm, DN_DIM), lambda i: (i, 0)),
                  pl.BlockSpec((None, None, 6, D_MODEL), _mod_row_map(layer, tiles_per_batch)),
                  pl.BlockSpec((None, 3, SC_DIM), lambda i: (layer, 0, 0)),
                  resident((MIX_DIM, D_MODEL)),
                  pl.BlockSpec((None, 1, D_MODEL), lambda i: (layer, 0, 0)),
                  resident((D_MODEL, D_FF)), resident((D_MODEL, D_FF)), resident((D_FF, D_MODEL))],
        out_specs=pl.BlockSpec((tm, D_MODEL), lambda i: (i, 0)),
        compiler_params=pltpu.CompilerParams(vmem_limit_bytes=VMEM_LIMIT),
        name="out_stage",
    )(x, sc, sc, sc, att, dn, mod, sc_conv_w, w_out_b, norm2_g, w_gate_b, w_up_b, w_down_b)


def _rope_tables():
    pos = jnp.arange(DEC_SEQ)
    half = HEAD_DIM // 4
    inv = 1.0 / (ROPE_BASE ** (jnp.arange(half, dtype=F32) / half))
    ang_r = (pos // GRID_W).astype(F32)[:, None] * inv
    ang_c = (pos % GRID_W).astype(F32)[:, None] * inv
    cos = jnp.concatenate([jnp.cos(ang_r)] * 2 + [jnp.cos(ang_c)] * 2, axis=-1)
    sin = jnp.concatenate([-jnp.sin(ang_r), jnp.sin(ang_r), -jnp.sin(ang_c), jnp.sin(ang_c)], axis=-1)
    return jnp.tile(cos, (1, 2)), jnp.tile(sin, (1, 2))


def _pack_w_in(w_in):
    a0 = OFF_AB
    b0 = OFF_AB + 2 * N_DN_HEADS
    blocks = []
    for p in range(N_PAIRS):
        idx = [a0 + d * N_DN_HEADS + 2 * p + j for d in range(2) for j in range(2)]
        idx += [b0 + d * N_DN_HEADS + 2 * p + j for d in range(2) for j in range(2)]
        blocks += [w_in[:, :, c:c + 1] for c in idx]
        blocks.append(jnp.zeros((DEPTH, D_MODEL, LANES - len(idx)), w_in.dtype))
    return w_in[:, :, :OFF_AB].astype(BF16), jnp.concatenate(blocks, axis=2).astype(BF16)


def _pair_lanes(p):
    x = p.reshape(DEPTH, 2, N_PAIRS, 2).transpose(0, 2, 1, 3).reshape(DEPTH, N_PAIRS, 1, 4)
    return jnp.pad(x, ((0, 0), (0, 0), (0, 0), (0, LANES - 4)))


def kernel(x_prompt, x_sample, cache_k, cache_v, state_delta, c, c_ctx, w_in, w_out, ada_w, ada_b,
           norm1_g, norm2_g, sc_conv_w, dn_conv_w, q_norm_g, k_norm_g, attn_sink, dn_A_log,
           dn_dt_bias, dn_norm_g, w_gate, w_up, w_down):
    w_main_b, w_ab_b = _pack_w_in(w_in)
    w_out_b, w_gate_b, w_up_b, w_down_b = (w.astype(BF16) for w in (w_out, w_gate, w_up, w_down))
    n1 = norm1_g.reshape(DEPTH, 1, D_MODEL)
    n2 = norm2_g.reshape(DEPTH, 1, D_MODEL)
    qg = jnp.tile(q_norm_g, (1, 2)).reshape(DEPTH, 1, LANES)
    kg = jnp.tile(k_norm_g, (1, 2)).reshape(DEPTH, 1, LANES)
    ng = jnp.tile(dn_norm_g, (1, 2)).reshape(DEPTH, 1, LANES)
    pa, pdt = _pair_lanes(dn_A_log), _pair_lanes(dn_dt_bias)
    cos_t, sin_t = _rope_tables()
    ck = cache_k.reshape(DEC_BATCH, DEPTH, PAST_LEN, KV_DIM)
    cv = cache_v.reshape(DEC_BATCH, DEPTH, PAST_LEN, KV_DIM)
    zero_state = jnp.zeros((BATCH, 1, 2, N_DN_HEADS, DN_HEAD_DIM, DN_HEAD_DIM), F32)

    cvecs = jnp.concatenate([c_ctx[None, :], c, jnp.zeros((MOD_ROWS - 1 - DEC_BATCH, D_MODEL), F32)], axis=0)
    mod = _modulation(cvecs, ada_w, ada_b).reshape(DEPTH, MOD_ROWS, 6, D_MODEL)

    xp = x_prompt.reshape(BATCH * SEQ, D_MODEL)
    xs = x_sample.reshape(DEC_BATCH * DEC_SEQ, D_MODEL)
    new_k, new_v, new_s = [], [], []
    for l in range(DEPTH):
        out_args = (mod, sc_conv_w, w_out_b, n2, w_gate_b, w_up_b, w_down_b, l)
        sc, q, kv, dq, dz, dab = _in_projection(xp, mod, n1, w_main_b, w_ab_b, qg, cos_t, sin_t, l, None)
        att, kn = _attention_ctx(q, kv, kg, attn_sink, l)
        dn, s_out = _delta_net(dq, dz, dab, dn_conv_w, pa, pdt, ng, zero_state, l, 0, SEQ, DN_CTX_SEQS)
        xp = _output_stage(xp, sc, att, dn, *out_args, TM, SEQ, False)
        new_k.append(kn.reshape(BATCH, SEQ, N_KV_HEADS, HEAD_DIM))
        new_v.append(kv[:, KV_DIM:].reshape(BATCH, SEQ, N_KV_HEADS, HEAD_DIM))
        new_s.append(s_out)
        sc, q, kv, dq, dz, dab = _in_projection(xs, mod, n1, w_main_b, w_ab_b, qg, cos_t, sin_t, l, DEC_SEQ // TM_IN)
        att = _attention_lat(q, kv, ck, cv, kg, attn_sink, cos_t, sin_t, l)
        dn, _ = _delta_net(dq, dz, dab, dn_conv_w, pa, pdt, ng, state_delta, l, l, DEC_SEQ, 1)
        xs = _output_stage(xs, sc, att, dn, *out_args, TM_LAT, DEC_SEQ, True)
    return (xp.reshape(BATCH, SEQ, D_MODEL), xs.reshape(DEC_BATCH, DEC_SEQ, D_MODEL),
            jnp.stack(new_k, axis=1), jnp.stack(new_v, axis=1), jnp.stack(new_s, axis=1))
```

```python
import functools

import numpy as np
import jax
import jax.numpy as jnp
from jax import lax
from jax.experimental import pallas as pl
from jax.experimental.pallas import tpu as pltpu

F32 = jnp.float32
BF16 = jnp.bfloat16

D_MODEL = 1024
BATCH = 16
SEQ = 256
DEPTH = 2
DEC_BATCH = 4
DEC_SEQ = 4096
PAST_LEN = 512
GRID_W = 64
SC_DIM = 256
N_Q_HEADS = 8
N_KV_HEADS = 2
N_GROUP = N_Q_HEADS // N_KV_HEADS
HEAD_DIM = 64
ATT_DIM = N_Q_HEADS * HEAD_DIM
KV_DIM = N_KV_HEADS * HEAD_DIM
WINDOW = 128
N_DN_HEADS = 4
DN_HEAD_DIM = 64
DN_DIM = N_DN_HEADS * DN_HEAD_DIM
CHUNK = 64
MIX_DIM = SC_DIM + ATT_DIM + DN_DIM
D_FF = -(-8 * D_MODEL // (3 * 256)) * 256
ROPE_BASE = 10000.0
EPS = 1e-6
NEG = -1e30
LOG2E = 1.4426950408889634

LANES = 128
V7X_VMEM_BYTES = 64 * 1024 * 1024
VMEM_LIMIT = (V7X_VMEM_BYTES * 3) // 4

OFF_SC, OFF_Q, OFF_KV, OFF_DQ, OFF_DZ, OFF_AB = 0, 768, 1280, 1536, 2304, 2560
IN_COLS = OFF_AB + 2 * LANES
N_PAIRS = N_DN_HEADS // 2
MOD_ROWS = 8
TM = 256
TM_LAT = 512
TM_IN = 512


def _sigmoid(x):
    return 0.5 * jnp.tanh(0.5 * x) + 0.5


def _silu(x):
    return x * _sigmoid(x)


def _softplus(x):
    return jnp.maximum(x, 0.0) + jnp.log1p(jnp.exp(-jnp.abs(x)))


def _dot(a, b):
    return jnp.dot(a, b, preferred_element_type=F32)


def _dot_nt(a, b):
    return lax.dot_general(a, b, (((1,), (1,)), ((), ())), preferred_element_type=F32)


def _lo_mask(shape):
    return lax.broadcasted_iota(jnp.int32, shape, 1) % LANES < HEAD_DIM


def _half_sums(xx, lo):
    s_lo = jnp.sum(jnp.where(lo, xx, 0.0), axis=-1, keepdims=True)
    s_hi = jnp.sum(jnp.where(lo, 0.0, xx), axis=-1, keepdims=True)
    return jnp.where(lo, s_lo, s_hi)


def _head_rms(x, g_row):
    lo = _lo_mask(x.shape)
    ms = _half_sums(x * x, lo) * (1.0 / HEAD_DIM)
    return x * lax.rsqrt(ms + EPS) * g_row


def _head_l2n(x):
    lo = _lo_mask(x.shape)
    return x * lax.rsqrt(_half_sums(x * x, lo) + EPS)


def _rope(x, cos, sin):
    n = x.shape[1]
    lane = lax.broadcasted_iota(jnp.int32, x.shape, 1)
    swapped = jnp.where((lane & 16) == 0, pltpu.roll(x, n - 16, axis=1), pltpu.roll(x, 16, axis=1))
    return x * cos + swapped * sin


def _dup_half(x, half):
    lo = _lo_mask(x.shape)
    r = pltpu.roll(x, HEAD_DIM, axis=1)
    return jnp.where(lo, x, r) if half == 0 else jnp.where(lo, r, x)


def _mod_kernel(c_ref, w_ref, b_ref, o_ref):
    s = _silu(c_ref[...])
    o_ref[...] = _dot(s.astype(BF16), w_ref[...].astype(BF16)) + b_ref[...]


def _modulation(cvecs, ada_w, ada_b):
    nblk = 6
    return pl.pallas_call(
        _mod_kernel,
        out_shape=jax.ShapeDtypeStruct((DEPTH, MOD_ROWS, 6 * D_MODEL), F32),
        grid=(DEPTH, nblk),
        in_specs=[pl.BlockSpec((MOD_ROWS, D_MODEL), lambda l, j: (0, 0)),
                  pl.BlockSpec((None, D_MODEL, D_MODEL), lambda l, j: (l, 0, j)),
                  pl.BlockSpec((None, 1, D_MODEL), lambda l, j: (l, 0, j))],
        out_specs=pl.BlockSpec((None, MOD_ROWS, D_MODEL), lambda l, j: (l, 0, j)),
        compiler_params=pltpu.CompilerParams(vmem_limit_bytes=VMEM_LIMIT),
        name="adaln_mod",
    )(cvecs, ada_w, ada_b.reshape(DEPTH, 1, 6 * D_MODEL))


def _mod_row_map(layer, tiles_per_batch):
    if tiles_per_batch is None:
        return lambda i: (layer, 0, 0, 0)
    return lambda i: (layer, 1 + i // tiles_per_batch, 0, 0)


def _prep_q(q, qg, cos, sin):
    out = []
    for g in range(ATT_DIM // LANES):
        x = _head_rms(q[:, g * LANES:(g + 1) * LANES], qg)
        if cos is not None:
            x = _rope(x, cos, sin)
        out.append((x * (LOG2E * HEAD_DIM ** -0.5)).astype(BF16))
    return jnp.concatenate(out, axis=1)


def _in_kernel(x_ref, mod_ref, g_ref, w_ref, wab_ref, qg_ref, cos_ref, sin_ref,
               sc_ref, q_ref, kv_ref, dq_ref, dz_ref, ab_ref, *, rotary):
    x = x_ref[...]
    ms = jnp.mean(x * x, axis=-1, keepdims=True)
    h = (x * lax.rsqrt(ms + EPS) * g_ref[...]) * (1.0 + mod_ref[1:2, :]) + mod_ref[0:1, :]
    hb = h.astype(BF16)
    q = _dot(hb, w_ref[:, OFF_Q:OFF_KV])
    q_ref[...] = _prep_q(q, qg_ref[...], cos_ref[...] if rotary else None, sin_ref[...] if rotary else None)
    for ref, a, b in ((sc_ref, OFF_SC, OFF_Q), (kv_ref, OFF_KV, OFF_DQ), (dq_ref, OFF_DQ, OFF_DZ),
                      (dz_ref, OFF_DZ, OFF_AB)):
        ref[...] = _dot(hb, w_ref[:, a:b])
    ab_ref[...] = _dot(hb, wab_ref[...])


def _in_projection(x, mod, norm1_g, w_main_b, w_ab_b, q_norm_g, cos_t, sin_t, layer, tiles_per_batch):
    t = x.shape[0]
    widths = (OFF_Q - OFF_SC, OFF_KV - OFF_Q, OFF_DQ - OFF_KV, OFF_DZ - OFF_DQ, OFF_AB - OFF_DZ, IN_COLS - OFF_AB)
    rotary = tiles_per_batch is not None
    pos_map = (lambda i: (i % tiles_per_batch, 0)) if rotary else (lambda i: (0, 0))
    return pl.pallas_call(
        functools.partial(_in_kernel, rotary=rotary),
        out_shape=[jax.ShapeDtypeStruct((t, w), BF16 if k == 1 else F32) for k, w in enumerate(widths)],
        grid=(t // TM_IN,),
        in_specs=[pl.BlockSpec((TM_IN, D_MODEL), lambda i: (i, 0)),
                  pl.BlockSpec((None, None, 6, D_MODEL), _mod_row_map(layer, tiles_per_batch)),
                  pl.BlockSpec((None, 1, D_MODEL), lambda i: (layer, 0, 0)),
                  pl.BlockSpec((None, D_MODEL, OFF_AB), lambda i: (layer, 0, 0),
                               pipeline_mode=pl.Buffered(1)),
                  pl.BlockSpec((None, D_MODEL, IN_COLS - OFF_AB), lambda i: (layer, 0, 0),
                               pipeline_mode=pl.Buffered(1)),
                  pl.BlockSpec((None, 1, LANES), lambda i: (layer, 0, 0)),
                  pl.BlockSpec((TM_IN, LANES), pos_map),
                  pl.BlockSpec((TM_IN, LANES), pos_map)],
        out_specs=[pl.BlockSpec((TM_IN, w), lambda i: (i, 0)) for w in widths],
        compiler_params=pltpu.CompilerParams(vmem_limit_bytes=VMEM_LIMIT),
        name="in_proj",
    )(x, mod, norm1_g, w_main_b, w_ab_b, q_norm_g, cos_t, sin_t)


def _head_logits(qa, half, key_sets):
    lo = _lo_mask(qa.shape)
    qh = jnp.where(lo if half == 0 else jnp.logical_not(lo), qa, jnp.zeros_like(qa))
    logits = []
    for k, _, mask in key_sets:
        s = _dot_nt(qh, k())
        logits.append(s if mask is None else jnp.where(mask, s, NEG))
    return logits


def _head_softmax_pv(logits, key_sets, sink):
    m = jnp.maximum(jnp.max(jnp.concatenate(logits, axis=1), axis=-1, keepdims=True), sink)
    acc = None
    for s, (_, v1, _) in zip(logits, key_sets):
        o = _dot(jnp.exp2(s - m).astype(BF16), v1())
        acc = o if acc is None else acc + o
    return acc[:, :LANES] / (acc[:, LANES:] + jnp.exp2(sink - m))


def _attend_heads(q_ref, blocks, sink_ref, layer, o_ref):
    n_pairs = ATT_DIM // LANES
    items = [(blk, pair, half) for blk in range(len(blocks)) for pair in range(n_pairs) for half in range(2)]
    kvh_of = lambda pair: (2 * pair) // N_GROUP
    keys = {}

    def key_sets(blk, pair):
        if (blk, kvh_of(pair)) not in keys:
            keys[blk, kvh_of(pair)] = blocks[blk][1](kvh_of(pair))
        return keys[blk, kvh_of(pair)]

    def logits_of(blk, pair, half):
        return _head_logits(q_ref[blocks[blk][0], pair * LANES:(pair + 1) * LANES], half, key_sets(blk, pair))

    logits = logits_of(*items[0])
    done = []
    for idx, (blk, pair, half) in enumerate(items):
        nxt = logits_of(*items[idx + 1]) if idx + 1 < len(items) else None
        done.append(_head_softmax_pv(logits, key_sets(blk, pair), sink_ref[layer, 2 * pair + half] * LOG2E))
        logits = nxt
        if half == 1:
            lo = _lo_mask(done[-1].shape)
            o_ref[blocks[blk][0], pair * LANES:(pair + 1) * LANES] = jnp.where(lo, done[-2], done[-1])


def _values_with_ones(v2, kvh):
    return jnp.concatenate([_dup_half(v2, kvh), jnp.ones(v2.shape, F32)], axis=1).astype(BF16)


def _attn_ctx_kernel(sink_ref, q_ref, kv_ref, kg_ref, o_ref, kn_ref, *, layer):
    kv = kv_ref[...]
    kn = _head_rms(kv[:, :LANES], kg_ref[...])
    kn_ref[...] = kn
    v2 = kv[:, LANES:]
    kd = [_dup_half(kn, kvh).astype(BF16) for kvh in range(N_KV_HEADS)]
    vd = [_values_with_ones(v2, kvh) for kvh in range(N_KV_HEADS)]
    key_sets_of = lambda kvh: [(lambda: kd[kvh], lambda: vd[kvh], None)]
    _attend_heads(q_ref, [(slice(0, SEQ), key_sets_of)], sink_ref, layer, o_ref)


def _attention_ctx(q, kv, k_norm_g, attn_sink, layer):
    t = q.shape[0]
    return pl.pallas_call(
        functools.partial(_attn_ctx_kernel, layer=layer),
        out_shape=[jax.ShapeDtypeStruct((t, ATT_DIM), F32), jax.ShapeDtypeStruct((t, KV_DIM), F32)],
        grid=(t // SEQ,),
        in_specs=[pl.BlockSpec(memory_space=pltpu.SMEM),
                  pl.BlockSpec((SEQ, ATT_DIM), lambda b: (b, 0)),
                  pl.BlockSpec((SEQ, 2 * KV_DIM), lambda b: (b, 0)),
                  pl.BlockSpec((None, 1, LANES), lambda b: (layer, 0, 0))],
        out_specs=[pl.BlockSpec((SEQ, ATT_DIM), lambda b: (b, 0)),
                   pl.BlockSpec((SEQ, KV_DIM), lambda b: (b, 0))],
        compiler_params=pltpu.CompilerParams(vmem_limit_bytes=VMEM_LIMIT),
        name="attn_ctx",
    )(attn_sink, q, kv, k_norm_g)


QBLK = 128
QSTEP = 512
WIN_KEYS = QBLK + 2 * WINDOW
PREP_ROWS = 512


def _attn_lat_kernel(sink_ref, q_ref, kv_ref, kc_ref, vc_ref, kg_ref, cos_ref, sin_ref, o_ref,
                     kd_s, vd_s, kcd_s, vcd_s, *, layer):
    i = pl.program_id(1)
    n = DEC_SEQ

    @pl.when(i == 0)
    def _():
        for kvh in range(N_KV_HEADS):
            for s in (kd_s, vd_s):
                zpad = jnp.zeros((WINDOW, s.shape[-1]), BF16)
                s[kvh, 0:WINDOW, :] = zpad
                s[kvh, WINDOW + n:2 * WINDOW + n, :] = zpad
            kcd_s[kvh] = _dup_half(kc_ref[...], kvh).astype(BF16)
            vcd_s[kvh] = _values_with_ones(vc_ref[...], kvh)

        def prep(r, carry):
            r0 = pl.multiple_of(r * PREP_ROWS, PREP_ROWS)
            kv = kv_ref[pl.ds(r0, PREP_ROWS), :]
            kn = _head_rms(kv[:, :LANES], kg_ref[...])
            kr = _rope(kn, cos_ref[pl.ds(r0, PREP_ROWS), :], sin_ref[pl.ds(r0, PREP_ROWS), :])
            v2 = kv[:, LANES:]
            for kvh in range(N_KV_HEADS):
                kd_s[kvh, pl.ds(WINDOW + r0, PREP_ROWS), :] = _dup_half(kr, kvh).astype(BF16)
                vd_s[kvh, pl.ds(WINDOW + r0, PREP_ROWS), :] = _values_with_ones(v2, kvh)
            return carry

        lax.fori_loop(0, n // PREP_ROWS, prep, 0)

    rr = lax.broadcasted_iota(jnp.int32, (QBLK, WIN_KEYS), 0)
    jj = lax.broadcasted_iota(jnp.int32, (QBLK, WIN_KEYS), 1)
    band = (jj >= rr) & (jj <= rr + 2 * WINDOW)
    blocks = []
    for j in range(QSTEP // QBLK):
        start = pl.multiple_of(i * QSTEP + j * QBLK, QBLK)
        kpos = jj + (start - WINDOW)
        valid = band & (kpos >= 0) & (kpos < n)
        key_sets_of = lambda kvh, start=start, valid=valid: [
            (lambda: kcd_s[kvh], lambda: vcd_s[kvh], None),
            (lambda: kd_s[kvh, pl.ds(start, WIN_KEYS), :], lambda: vd_s[kvh, pl.ds(start, WIN_KEYS), :], valid)]
        blocks.append((slice(j * QBLK, (j + 1) * QBLK), key_sets_of))
    _attend_heads(q_ref, blocks, sink_ref, layer, o_ref)


def _attention_lat(q, kv, cache_k, cache_v, k_norm_g, attn_sink, cos_t, sin_t, layer):
    t = q.shape[0]
    n = DEC_SEQ
    nblk = n // QSTEP
    return pl.pallas_call(
        functools.partial(_attn_lat_kernel, layer=layer),
        out_shape=jax.ShapeDtypeStruct((t, ATT_DIM), F32),
        grid=(DEC_BATCH, nblk),
        in_specs=[pl.BlockSpec(memory_space=pltpu.SMEM),
                  pl.BlockSpec((QSTEP, ATT_DIM), lambda b, i: (b * nblk + i, 0)),
                  pl.BlockSpec((n, 2 * KV_DIM), lambda b, i: (b, 0)),
                  pl.BlockSpec((None, None, PAST_LEN, KV_DIM), lambda b, i: (b, layer, 0, 0)),
                  pl.BlockSpec((None, None, PAST_LEN, KV_DIM), lambda b, i: (b, layer, 0, 0)),
                  pl.BlockSpec((None, 1, LANES), lambda b, i: (layer, 0, 0)),
                  pl.BlockSpec((n, LANES), lambda b, i: (0, 0)),
                  pl.BlockSpec((n, LANES), lambda b, i: (0, 0))],
        out_specs=pl.BlockSpec((QSTEP, ATT_DIM), lambda b, i: (b * nblk + i, 0)),
        scratch_shapes=[pltpu.VMEM((N_KV_HEADS, n + 2 * WINDOW, LANES), BF16),
                        pltpu.VMEM((N_KV_HEADS, n + 2 * WINDOW, 2 * LANES), BF16),
                        pltpu.VMEM((N_KV_HEADS, PAST_LEN, LANES), BF16),
                        pltpu.VMEM((N_KV_HEADS, PAST_LEN, 2 * LANES), BF16)],
        compiler_params=pltpu.CompilerParams(vmem_limit_bytes=VMEM_LIMIT),
        name="attn_lat",
    )(attn_sink, q, kv, cache_k, cache_v, k_norm_g, cos_t, sin_t)


DN_ROWS = 128
DN_OUT_ROWS = 256
DN_CTX_SEQS = 4


def _dn_conv(x_ref, w_ref, r0, n):
    x = x_ref[pl.ds(r0, DN_ROWS), :]
    total = x_ref.shape[0]
    above = jnp.where(r0 % n > 0, x_ref[pl.ds(jnp.maximum(r0 - 1, 0), 1), :], 0.0)
    below = jnp.where(r0 % n + DN_ROWS < n, x_ref[pl.ds(jnp.minimum(r0 + DN_ROWS, total - 1), 1), :], 0.0)
    row = lax.broadcasted_iota(jnp.int32, x.shape, 0)
    prev = jnp.where(row == 0, above, pltpu.roll(x, 1, axis=0))
    nxt = jnp.where(row == DN_ROWS - 1, below, pltpu.roll(x, DN_ROWS - 1, axis=0))
    return _silu(prev * w_ref[0:1, :] + x * w_ref[1:2, :] + nxt * w_ref[2:3, :])


N_UNITS = 4
UW = N_UNITS * CHUNK
N_LEVELS = 6
M_INCL, M_STRICT, M_EYE = N_LEVELS, N_LEVELS + 1, N_LEVELS + 2
PREP_BLOCK = 8


def _dn_kernel(q_ref, k_ref, v_ref, z_ref, ab_ref, wq_ref, wk_ref, wv_ref, pa_ref, pdt_ref, ng_ref, s0_ref,
               o_ref, sout_ref, qs, ks, vs, gbs, pfs, r_s, au_s, m_s, qa_s, sh_s, el_s, st, mc, ex_s, tri_s, *, n):
    nc = n // CHUNK
    nseq = q_ref.shape[0] // n
    pblk = min(PREP_BLOCK, nseq * nc)

    @pl.when((pl.program_id(0) == 0) & (pl.program_id(1) == 0))
    def _():
        as_f32 = lambda m: jnp.where(m, 1.0, 0.0).astype(F32)
        ii = lax.broadcasted_iota(jnp.int32, (CHUNK, UW), 0)
        ll = lax.broadcasted_iota(jnp.int32, (CHUNK, UW), 1)
        unit_of_lane, jj = ll // CHUNK, ll % CHUNK
        fwd, bwd = unit_of_lane < 2, unit_of_lane >= 2
        for k in range(N_LEVELS):
            rk, ck = ii >> k, jj >> k
            lower = (rk - ck == 1) & ((rk & 1) == 1)
            upper = (ck - rk == 1) & ((ck & 1) == 1)
            mc[k] = as_f32((fwd & lower) | (bwd & upper))
        mc[M_INCL] = as_f32((fwd & (ii >= jj)) | (bwd & (ii <= jj)))
        mc[M_STRICT] = as_f32((fwd & (ii > jj)) | (bwd & (ii < jj)))
        mc[M_EYE] = as_f32(ii == jj)
        ec = lax.broadcasted_iota(jnp.int32, (LANES, 2 * UW), 0)
        el_ = lax.broadcasted_iota(jnp.int32, (LANES, 2 * UW), 1)
        ex_s[...] = as_f32(ec == (el_ // UW) * N_UNITS + (el_ % UW) // CHUNK).astype(BF16)
        ri = lax.broadcasted_iota(jnp.int32, (DN_ROWS, DN_ROWS), 0)
        ti = lax.broadcasted_iota(jnp.int32, (DN_ROWS, DN_ROWS), 1)
        tri_s[...] = as_f32((ti <= ri) & (ri // CHUNK == ti // CHUNK)).astype(BF16)

    def prep_rows(b):
        r0 = pl.multiple_of(b * DN_ROWS, DN_ROWS)
        rows = pl.ds(r0, DN_ROWS)
        qs[rows, :] = _head_l2n(_dn_conv(q_ref, wq_ref, r0, n)) * (DN_HEAD_DIM ** -0.5)
        ks[rows, :] = _head_l2n(_dn_conv(k_ref, wk_ref, r0, n))
        vs[rows, :] = _dn_conv(v_ref, wv_ref, r0, n)
        ab = ab_ref[rows, :]
        lane = lax.broadcasted_iota(jnp.int32, ab.shape, 1)
        gb = jnp.where(lane < N_UNITS, -jnp.exp(pa_ref[...]) * _softplus(ab + pdt_ref[...]), _sigmoid(ab))
        gbs[rows, :] = gb
        hi = gb.astype(BF16)
        r1 = gb - hi.astype(F32)
        mid = r1.astype(BF16)
        low = (r1 - mid.astype(F32)).astype(BF16)
        tri = tri_s[...]
        pfs[rows, :] = _dot(tri, hi) + _dot(tri, mid) + _dot(tri, low)

    for q in range(nseq):
        st[q] = jnp.concatenate([s0_ref[q, u // 2, u % 2] for u in range(N_UNITS)], axis=1)
    lo = _lo_mask((CHUNK, LANES))
    lane_c = lax.broadcasted_iota(jnp.int32, (CHUNK, LANES), 1)

    def spread_columns(m):
        hi = m.astype(BF16)
        r1 = m - hi.astype(F32)
        mid = r1.astype(BF16)
        low = (r1 - mid.astype(F32)).astype(BF16)
        e = ex_s[...]
        return _dot(hi, e) + _dot(mid, e) + _dot(low, e)

    def pair_block_diag(x):
        zero = jnp.zeros((CHUNK, LANES), BF16)
        return jnp.concatenate([jnp.where(lo, x, zero), jnp.where(lo, zero, x)], axis=0)

    def unit_dot(a, xb):
        return jnp.concatenate([_dot(a[:, h * LANES:(h + 1) * LANES], pair_block_diag(xb[:, h * LANES:(h + 1) * LANES]))
                                for h in range(2)], axis=1)

    def unit_dot2(a, xb, yb):
        outs = []
        for h in range(2):
            cols = slice(h * LANES, (h + 1) * LANES)
            rhs = jnp.concatenate([pair_block_diag(xb[:, cols]), pair_block_diag(yb[:, cols])], axis=1)
            outs.append(_dot(a[:, cols], rhs))
        return (jnp.concatenate([o[:, :LANES] for o in outs], axis=1),
                jnp.concatenate([o[:, LANES:] for o in outs], axis=1))

    def unit_rows(m):
        t = m.T
        return jnp.concatenate([t[u:u + 1, :] for u in range(N_UNITS)], axis=1)

    dup = lambda x: jnp.concatenate([x, x], axis=1)

    def chunk_prep(cs, between=(), first=()):
        for thunk in first:
            thunk()
        rows = [pl.ds(pl.multiple_of(c * CHUNK, CHUNK), CHUNK) for c in cs]
        each = lambda f, *lists: [f(*xs) for xs in zip(*lists)]
        pending = list(between)

        gx = [gbs[r, :] for r in rows]
        pc = [pfs[r, :] for r in rows]
        tot = [p[CHUNK - 1:CHUNK, :] for p in pc]
        gcm = each(lambda g, p, t: jnp.where(lane_c < 2, p, t - p + g), gx, pc, tot)
        rsm = each(lambda g, p, t: jnp.where(lane_c < 2, t - p, p - g), gx, pc, tot)
        spread = each(lambda m, g: spread_columns(jnp.where(lane_c < N_UNITS, m, g)), gcm, gx)
        gc4 = [x[:, :UW] for x in spread]
        b4 = [x[:, UW:] for x in spread]
        e_gc = each(jnp.exp, gc4)
        dec = each(lambda c, m: jnp.exp(jnp.where(mc[M_INCL] > 0.0, c - unit_rows(m), NEG)), gc4, gcm)
        kt = [ks[r, :].T for r in rows]
        kt_pair = each(lambda t: jnp.concatenate([t[0:CHUNK], t[CHUNK:]], axis=1), kt)
        kt_bd = each(lambda t: pair_block_diag(t.astype(BF16)), kt_pair)
        gram = each(lambda r, b: dup(_dot(ks[r, :].astype(BF16), b)), rows, kt_bd)
        qk = each(lambda r, b: dup(_dot(qs[r, :].astype(BF16), b)), rows, kt_bd)
        lmat = each(lambda g, b, d: g * b * d * mc[M_STRICT], gram, b4, dec)
        amat = each(lambda m, d: (m * d).astype(BF16), qk, dec)
        t = each(lambda l: mc[M_EYE] - l * mc[0], lmat)
        for k in range(1, N_LEVELS):
            lb = each(lambda l: (l * mc[k]).astype(BF16), lmat)
            tb = each(lambda x: x.astype(BF16), t)
            x = each(lambda a, b: unit_dot(a, b).astype(BF16), tb, lb)
            t = each(lambda t0, a, b: t0 - unit_dot(a, b), t, x, tb)
            if pending:
                pending.pop(0)()
        tm = each(lambda x: (x - mc[M_EYE]).astype(BF16), t)
        k4 = [dup(ks[r, :]) for r in rows]
        ru = each(lambda r, b: dup(vs[r, :]) * b, rows, b4)
        rw = each(lambda k, b, e: k * b * e, k4, b4, e_gc)
        tuw = each(lambda tmi, u, w: unit_dot2(tmi, u.astype(BF16), w.astype(BF16)), tm, ru, rw)
        ub = each(lambda x, d: (x + d[0]).astype(BF16), ru, tuw)
        wb = each(lambda x, d: (x + d[1]).astype(BF16), rw, tuw)
        kdt = each(lambda t, m: (dup(t) * jnp.exp(unit_rows(m))).astype(BF16), kt_pair, rsm)
        both = each(lambda k, a, w, u: unit_dot2(jnp.concatenate([k, a], axis=0), w, u), kdt, amat, wb, ub)
        mmat, rmat = [x[0][:CHUNK] for x in both], [x[1][:CHUNK] for x in both]
        aw, au = [x[0][CHUNK:] for x in both], [x[1][CHUNK:] for x in both]
        for i, (c, r) in enumerate(zip(cs, rows)):
            m_s[r, :] = mmat[i].astype(BF16)
            r_s[r, :] = rmat[i]
            qa_s[r, :] = (dup(qs[r, :]) * e_gc[i] - aw[i]).astype(BF16)
            au_s[r, :] = au[i]
            el = jnp.concatenate([e_gc[i][CHUNK - 1:CHUNK, :LANES], e_gc[i][0:1, LANES:]], axis=1)
            el_s[c] = jnp.broadcast_to(el, (8, UW))

    def scan_step(i):
        cf = [q * nc + i for q in range(nseq)]
        cb = [q * nc + nc - 1 - i for q in range(nseq)]
        rf = [pl.ds(pl.multiple_of(c * CHUNK, CHUNK), CHUNK) for c in cf]
        rb = [pl.ds(pl.multiple_of(c * CHUNK, CHUNK), CHUNK) for c in cb]
        mixed = lambda ref, q: jnp.concatenate([ref[rf[q], :LANES], ref[rb[q], LANES:]], axis=1)
        s = [st[q] for q in range(nseq)]
        sb = [x.astype(BF16) for x in s]
        for q in range(nseq):
            sh_s[rf[q], :LANES] = sb[q][:, :LANES]
            sh_s[rb[q], LANES:] = sb[q][:, LANES:]
        ms = [unit_dot(mixed(m_s, q), sb[q]) for q in range(nseq)]
        for q in range(nseq):
            el = jnp.concatenate([el_s[cf[q]][0:1, :LANES], el_s[cb[q]][0:1, LANES:]], axis=1)
            st[q] = s[q] * el - ms[q] + mixed(r_s, q)

    def finish_chunks(r0, count):
        parts = []
        for c in range(count):
            rows = pl.ds(pl.multiple_of(r0 + c * CHUNK, CHUNK), CHUNK)
            parts.append(unit_dot(qa_s[rows, :], sh_s[rows, :]) + au_s[rows, :])
        o = jnp.concatenate([p[:, :LANES] + p[:, LANES:] for p in parts], axis=0)
        rows = pl.ds(pl.multiple_of(r0, count * CHUNK), count * CHUNK)
        o_ref[rows, :] = _head_rms(o, ng_ref[...]) * _silu(z_ref[rows, :])

    def finish_rows(b, carry):
        finish_chunks(b * DN_OUT_ROWS, DN_OUT_ROWS // CHUNK)
        return carry

    def scan_only(i, carry):
        scan_step(i)
        return carry

    half = pblk // 2
    if nseq == 1 and nc % pblk == 0 and nc // pblk >= 3 and half <= N_LEVELS - 1 and (half * CHUNK) % DN_ROWS == 0:
        ends = lambda t: [t * half + s for s in range(half)] + [nc - 1 - (t * half + s) for s in range(half)]
        steps = lambda t: [functools.partial(scan_step, t * half + s) for s in range(half)]
        side = half * CHUNK // DN_ROWS
        last_block = n // DN_ROWS - 1
        row_blocks = lambda t: [t * side + s for s in range(side)] + [last_block - (t * side + s) for s in range(side)]
        input_pass = lambda t: [functools.partial(prep_rows, b) for b in row_blocks(t)]
        trips = nc // pblk
        for thunk in input_pass(0):
            thunk()
        chunk_prep(ends(0), first=input_pass(1))

        def prep_and_scan(t, carry):
            chunk_prep(ends(t), steps(t - 1), input_pass(t + 1))
            return carry

        lax.fori_loop(1, trips - 1, prep_and_scan, 0)
        chunk_prep(ends(trips - 1), steps(trips - 2))
        mid = nc // 2
        lax.fori_loop((trips - 1) * half, mid + 2, scan_only, 0)

        def finish_pair(j):
            finish_chunks((mid - 1 - j) * CHUNK, 1)
            finish_chunks((mid + j) * CHUNK, 1)

        def finish_and_scan(i, carry):
            finish_pair(2 * i)
            finish_pair(2 * i + 1)
            scan_step(mid + 2 + 2 * i)
            scan_step(mid + 3 + 2 * i)
            return carry

        lax.fori_loop(0, (mid - 2) // 2, finish_and_scan, 0)
        finish_pair(mid - 2)
        finish_pair(mid - 1)
    else:
        def input_block(b, carry):
            prep_rows(b)
            return carry

        def prep_block(blk, carry):
            chunk_prep([blk * pblk + s for s in range(pblk)])
            return carry

        lax.fori_loop(0, nseq * n // DN_ROWS, input_block, 0)
        lax.fori_loop(0, nseq * nc // pblk, prep_block, 0)
        lax.fori_loop(0, nc, scan_only, 0)
        lax.fori_loop(0, nseq * n // DN_OUT_ROWS, finish_rows, 0)

    for q in range(nseq):
        s = st[q]
        for u in range(N_UNITS):
            sout_ref[q, u // 2, u % 2] = s[:, u * CHUNK:(u + 1) * CHUNK]


def _delta_net(dq, dz, dab, dn_conv_w, pa, pdt, dn_norm_g, s0, layer, s0_layer, n, nseq):
    t = dq.shape[0]
    bsz = t // n
    rows = nseq * n
    nc = n // CHUNK
    conv_spec = lambda off: pl.BlockSpec((None, 3, LANES), lambda b, p: (layer, 0, off + p))
    col_spec = lambda off, bufs=2: pl.BlockSpec((rows, LANES), lambda b, p: (b, off + p),
                                                pipeline_mode=pl.Buffered(bufs))
    state_block = (nseq, 2, 2, DN_HEAD_DIM, DN_HEAD_DIM)
    return pl.pallas_call(
        functools.partial(_dn_kernel, n=n),
        out_shape=[jax.ShapeDtypeStruct((t, DN_DIM), F32),
                   jax.ShapeDtypeStruct((bsz, 2, N_DN_HEADS, DN_HEAD_DIM, DN_HEAD_DIM), F32)],
        grid=(bsz // nseq, N_PAIRS),
        in_specs=[col_spec(0), col_spec(N_PAIRS), col_spec(2 * N_PAIRS), col_spec(0, 1), col_spec(0),
                  conv_spec(0), conv_spec(N_PAIRS), conv_spec(2 * N_PAIRS),
                  pl.BlockSpec((None, None, 1, LANES), lambda b, p: (layer, p, 0, 0)),
                  pl.BlockSpec((None, None, 1, LANES), lambda b, p: (layer, p, 0, 0)),
                  pl.BlockSpec((None, 1, LANES), lambda b, p: (layer, 0, 0)),
                  pl.BlockSpec((nseq, None) + state_block[1:], lambda b, p: (b, s0_layer, 0, p, 0, 0))],
        out_specs=[pl.BlockSpec((rows, LANES), lambda b, p: (b, p)),
                   pl.BlockSpec(state_block, lambda b, p: (b, 0, p, 0, 0))],
        scratch_shapes=[pltpu.VMEM((rows, LANES), F32) for _ in range(5)]
                       + [pltpu.VMEM((rows, UW), F32) for _ in range(2)]
                       + [pltpu.VMEM((rows, UW), BF16) for _ in range(3)]
                       + [pltpu.VMEM((nseq * nc, 8, UW), F32),
                          pltpu.VMEM((nseq, CHUNK, UW), F32),
                          pltpu.VMEM((N_LEVELS + 3, CHUNK, UW), F32),
                          pltpu.VMEM((LANES, 2 * UW), BF16),
                          pltpu.VMEM((DN_ROWS, DN_ROWS), BF16)],
        compiler_params=pltpu.CompilerParams(vmem_limit_bytes=(V7X_VMEM_BYTES * 7) // 8),
        name="delta_net",
    )(dq, dq, dq, dz, dab, dn_conv_w, dn_conv_w, dn_conv_w, pa, pdt, dn_norm_g, s0)


def _out_kernel(x_ref, sc_ref, scp_ref, scn_ref, att_ref, dn_ref, mod_ref, cw_ref, wo_ref, n2_ref,
                wg_ref, wu_ref, wd_ref, o_ref, *, tm, tiles_per_seq):
    t = pl.program_id(0) % tiles_per_seq
    sc = sc_ref[...]
    prod = sc[:, SC_DIM:2 * SC_DIM] * sc[:, 2 * SC_DIM:]
    pv = scp_ref[7:8, :]
    nx = scn_ref[0:1, :]
    prev_row = jnp.where(t > 0, pv[:, SC_DIM:2 * SC_DIM] * pv[:, 2 * SC_DIM:], 0.0)
    next_row = jnp.where(t < tiles_per_seq - 1, nx[:, SC_DIM:2 * SC_DIM] * nx[:, 2 * SC_DIM:], 0.0)
    row = lax.broadcasted_iota(jnp.int32, prod.shape, 0)
    p_prev = jnp.where(row == 0, prev_row, pltpu.roll(prod, 1, axis=0))
    p_next = jnp.where(row == tm - 1, next_row, pltpu.roll(prod, tm - 1, axis=0))
    y_sc = sc[:, :SC_DIM] * (p_prev * cw_ref[0:1, :] + prod * cw_ref[1:2, :] + p_next * cw_ref[2:3, :])
    y = (_dot(y_sc.astype(BF16), wo_ref[0:SC_DIM, :])
         + _dot(att_ref[...].astype(BF16), wo_ref[SC_DIM:SC_DIM + ATT_DIM, :])
         + _dot(dn_ref[...].astype(BF16), wo_ref[SC_DIM + ATT_DIM:, :]))
    x1 = x_ref[...] + mod_ref[2:3, :] * y
    ms = jnp.mean(x1 * x1, axis=-1, keepdims=True)
    h2 = (x1 * lax.rsqrt(ms + EPS) * n2_ref[...]) * (1.0 + mod_ref[4:5, :]) + mod_ref[3:4, :]
    hb = h2.astype(BF16)
    act = _silu(_dot(hb, wg_ref[...])) * _dot(hb, wu_ref[...])
    o_ref[...] = x1 + mod_ref[5:6, :] * _dot(act.astype(BF16), wd_ref[...])


def _output_stage(x, sc, att, dn, mod, sc_conv_w, w_out_b, norm2_g, w_gate_b, w_up_b, w_down_b, layer,
                  tm, seq_len, per_batch_mod):
    t = x.shape[0]
    tiles_per_seq = seq_len // tm
    tiles_per_batch = tiles_per_seq if per_batch_mod else None
    halo = tm // 8
    last = t // 8 - 1
    resident = lambda shape: pl.BlockSpec((None,) + shape, lambda i: (layer, 0, 0), pipeline_mode=pl.Buffered(1))
    return pl.pallas_call(
        functools.partial(_out_kernel, tm=tm, tiles_per_seq=tiles_per_seq),
        out_shape=jax.ShapeDtypeStruct((t, D_MODEL), F32),
        grid=(t // tm,),
        in_specs=[pl.BlockSpec((tm, D_MODEL), lambda i: (i, 0)),
                  pl.BlockSpec((tm, 3 * SC_DIM), lambda i: (i, 0)),
                  pl.BlockSpec((8, 3 * SC_DIM), lambda i: (jnp.maximum(i * halo - 1, 0), 0)),
                  pl.BlockSpec((8, 3 * SC_DIM), lambda i: (jnp.minimum((i + 1) * halo, last), 0)),
                  pl.BlockSpec((tm, ATT_DIM), lambda i: (i, 0)),
                  pl.BlockSpec((tm, DN_DIM), lambda i: (i, 0)),
                  pl.BlockSpec((None, None, 6, D_MODEL), _mod_row_map(layer, tiles_per_batch)),
                  pl.BlockSpec((None, 3, SC_DIM), lambda i: (layer, 0, 0)),
                  resident((MIX_DIM, D_MODEL)),
                  pl.BlockSpec((None, 1, D_MODEL), lambda i: (layer, 0, 0)),
                  resident((D_MODEL, D_FF)), resident((D_MODEL, D_FF)), resident((D_FF, D_MODEL))],
        out_specs=pl.BlockSpec((tm, D_MODEL), lambda i: (i, 0)),
        compiler_params=pltpu.CompilerParams(vmem_limit_bytes=VMEM_LIMIT),
        name="out_stage",
    )(x, sc, sc, sc, att, dn, mod, sc_conv_w, w_out_b, norm2_g, w_gate_b, w_up_b, w_down_b)


def _rope_tables():
    pos = jnp.arange(DEC_SEQ)
    half = HEAD_DIM // 4
    inv = 1.0 / (ROPE_BASE ** (jnp.arange(half, dtype=F32) / half))
    ang_r = (pos // GRID_W).astype(F32)[:, None] * inv
    ang_c = (pos % GRID_W).astype(F32)[:, None] * inv
    cos = jnp.concatenate([jnp.cos(ang_r)] * 2 + [jnp.cos(ang_c)] * 2, axis=-1)
    sin = jnp.concatenate([-jnp.sin(ang_r), jnp.sin(ang_r), -jnp.sin(ang_c), jnp.sin(ang_c)], axis=-1)
    return jnp.tile(cos, (1, 2)), jnp.tile(sin, (1, 2))


def _pack_w_in(w_in):
    a0 = OFF_AB
    b0 = OFF_AB + 2 * N_DN_HEADS
    blocks = []
    for p in range(N_PAIRS):
        idx = [a0 + d * N_DN_HEADS + 2 * p + j for d in range(2) for j in range(2)]
        idx += [b0 + d * N_DN_HEADS + 2 * p + j for d in range(2) for j in range(2)]
        blocks += [w_in[:, :, c:c + 1] for c in idx]
        blocks.append(jnp.zeros((DEPTH, D_MODEL, LANES - len(idx)), w_in.dtype))
    return w_in[:, :, :OFF_AB].astype(BF16), jnp.concatenate(blocks, axis=2).astype(BF16)


def _pair_lanes(p):
    x = p.reshape(DEPTH, 2, N_PAIRS, 2).transpose(0, 2, 1, 3).reshape(DEPTH, N_PAIRS, 1, 4)
    return jnp.pad(x, ((0, 0), (0, 0), (0, 0), (0, LANES - 4)))


def kernel(x_prompt, x_sample, cache_k, cache_v, state_delta, c, c_ctx, w_in, w_out, ada_w, ada_b,
           norm1_g, norm2_g, sc_conv_w, dn_conv_w, q_norm_g, k_norm_g, attn_sink, dn_A_log,
           dn_dt_bias, dn_norm_g, w_gate, w_up, w_down):
    w_main_b, w_ab_b = _pack_w_in(w_in)
    w_out_b, w_gate_b, w_up_b, w_down_b = (w.astype(BF16) for w in (w_out, w_gate, w_up, w_down))
    n1 = norm1_g.reshape(DEPTH, 1, D_MODEL)
    n2 = norm2_g.reshape(DEPTH, 1, D_MODEL)
    qg = jnp.tile(q_norm_g, (1, 2)).reshape(DEPTH, 1, LANES)
    kg = jnp.tile(k_norm_g, (1, 2)).reshape(DEPTH, 1, LANES)
    ng = jnp.tile(dn_norm_g, (1, 2)).reshape(DEPTH, 1, LANES)
    pa, pdt = _pair_lanes(dn_A_log), _pair_lanes(dn_dt_bias)
    cos_t, sin_t = _rope_tables()
    ck = cache_k.reshape(DEC_BATCH, DEPTH, PAST_LEN, KV_DIM)
    cv = cache_v.reshape(DEC_BATCH, DEPTH, PAST_LEN, KV_DIM)
    zero_state = jnp.zeros((BATCH, 1, 2, N_DN_HEADS, DN_HEAD_DIM, DN_HEAD_DIM), F32)

    cvecs = jnp.concatenate([c_ctx[None, :], c, jnp.zeros((MOD_ROWS - 1 - DEC_BATCH, D_MODEL), F32)], axis=0)
    mod = _modulation(cvecs, ada_w, ada_b).reshape(DEPTH, MOD_ROWS, 6, D_MODEL)

    xp = x_prompt.reshape(BATCH * SEQ, D_MODEL)
    xs = x_sample.reshape(DEC_BATCH * DEC_SEQ, D_MODEL)
    new_k, new_v, new_s = [], [], []
    for l in range(DEPTH):
        out_args = (mod, sc_conv_w, w_out_b, n2, w_gate_b, w_up_b, w_down_b, l)
        sc, q, kv, dq, dz, dab = _in_projection(xp, mod, n1, w_main_b, w_ab_b, qg, cos_t, sin_t, l, None)
        att, kn = _attention_ctx(q, kv, kg, attn_sink, l)
        dn, s_out = _delta_net(dq, dz, dab, dn_conv_w, pa, pdt, ng, zero_state, l, 0, SEQ, DN_CTX_SEQS)
        xp = _output_stage(xp, sc, att, dn, *out_args, TM, SEQ, False)
        new_k.append(kn.reshape(BATCH, SEQ, N_KV_HEADS, HEAD_DIM))
        new_v.append(kv[:, KV_DIM:].reshape(BATCH, SEQ, N_KV_HEADS, HEAD_DIM))
        new_s.append(s_out)
        sc, q, kv, dq, dz, dab = _in_projection(xs, mod, n1, w_main_b, w_ab_b, qg, cos_t, sin_t, l, DEC_SEQ // TM_IN)
        att = _attention_lat(q, kv, ck, cv, kg, attn_sink, cos_t, sin_t, l)
        dn, _ = _delta_net(dq, dz, dab, dn_conv_w, pa, pdt, ng, state_delta, l, l, DEC_SEQ, 1)
        xs = _output_stage(xs, sc, att, dn, *out_args, TM_LAT, DEC_SEQ, True)
    return (xp.reshape(BATCH, SEQ, D_MODEL), xs.reshape(DEC_BATCH, DEC_SEQ, D_MODEL),
            jnp.stack(new_k, axis=1), jnp.stack(new_v, axis=1), jnp.stack(new_s, axis=1))
```

```python
import functools

import numpy as np
import jax
import jax.numpy as jnp
from jax import lax
from jax.experimental import pallas as pl
from jax.experimental.pallas import tpu as pltpu

F32 = jnp.float32
BF16 = jnp.bfloat16

D_MODEL = 1024
BATCH = 16
SEQ = 256
DEPTH = 2
DEC_BATCH = 4
DEC_SEQ = 4096
PAST_LEN = 512
GRID_W = 64
SC_DIM = 256
N_Q_HEADS = 8
N_KV_HEADS = 2
N_GROUP = N_Q_HEADS // N_KV_HEADS
HEAD_DIM = 64
ATT_DIM = N_Q_HEADS * HEAD_DIM
KV_DIM = N_KV_HEADS * HEAD_DIM
WINDOW = 128
N_DN_HEADS = 4
DN_HEAD_DIM = 64
DN_DIM = N_DN_HEADS * DN_HEAD_DIM
CHUNK = 64
MIX_DIM = SC_DIM + ATT_DIM + DN_DIM
D_FF = -(-8 * D_MODEL // (3 * 256)) * 256
ROPE_BASE = 10000.0
EPS = 1e-6
NEG = -1e30
LOG2E = 1.4426950408889634

LANES = 128
V7X_VMEM_BYTES = 64 * 1024 * 1024
VMEM_LIMIT = (V7X_VMEM_BYTES * 3) // 4

OFF_SC, OFF_Q, OFF_KV, OFF_DQ, OFF_DZ, OFF_AB = 0, 768, 1280, 1536, 2304, 2560
IN_COLS = OFF_AB + 2 * LANES
N_PAIRS = N_DN_HEADS // 2
MOD_ROWS = 8
TM = 256
TM_LAT = 512
TM_IN = 512


def _sigmoid(x):
    return 0.5 * jnp.tanh(0.5 * x) + 0.5


def _silu(x):
    return x * _sigmoid(x)


def _softplus(x):
    return jnp.maximum(x, 0.0) + jnp.log1p(jnp.exp(-jnp.abs(x)))


def _dot(a, b):
    return jnp.dot(a, b, preferred_element_type=F32)


def _dot_nt(a, b):
    return lax.dot_general(a, b, (((1,), (1,)), ((), ())), preferred_element_type=F32)


def _lo_mask(shape):
    return lax.broadcasted_iota(jnp.int32, shape, 1) % LANES < HEAD_DIM


def _half_sums(xx, lo):
    s_lo = jnp.sum(jnp.where(lo, xx, 0.0), axis=-1, keepdims=True)
    s_hi = jnp.sum(jnp.where(lo, 0.0, xx), axis=-1, keepdims=True)
    return jnp.where(lo, s_lo, s_hi)


def _head_rms(x, g_row):
    lo = _lo_mask(x.shape)
    ms = _half_sums(x * x, lo) * (1.0 / HEAD_DIM)
    return x * lax.rsqrt(ms + EPS) * g_row


def _head_l2n(x):
    lo = _lo_mask(x.shape)
    return x * lax.rsqrt(_half_sums(x * x, lo) + EPS)


def _rope(x, cos, sin):
    n = x.shape[1]
    lane = lax.broadcasted_iota(jnp.int32, x.shape, 1)
    swapped = jnp.where((lane & 16) == 0, pltpu.roll(x, n - 16, axis=1), pltpu.roll(x, 16, axis=1))
    return x * cos + swapped * sin


def _dup_half(x, half):
    lo = _lo_mask(x.shape)
    r = pltpu.roll(x, HEAD_DIM, axis=1)
    return jnp.where(lo, x, r) if half == 0 else jnp.where(lo, r, x)


def _mod_kernel(c_ref, w_ref, b_ref, o_ref):
    s = _silu(c_ref[...])
    o_ref[...] = _dot(s.astype(BF16), w_ref[...].astype(BF16)) + b_ref[...]


def _modulation(cvecs, ada_w, ada_b):
    nblk = 6
    return pl.pallas_call(
        _mod_kernel,
        out_shape=jax.ShapeDtypeStruct((DEPTH, MOD_ROWS, 6 * D_MODEL), F32),
        grid=(DEPTH, nblk),
        in_specs=[pl.BlockSpec((MOD_ROWS, D_MODEL), lambda l, j: (0, 0)),
                  pl.BlockSpec((None, D_MODEL, D_MODEL), lambda l, j: (l, 0, j)),
                  pl.BlockSpec((None, 1, D_MODEL), lambda l, j: (l, 0, j))],
        out_specs=pl.BlockSpec((None, MOD_ROWS, D_MODEL), lambda l, j: (l, 0, j)),
        compiler_params=pltpu.CompilerParams(vmem_limit_bytes=VMEM_LIMIT),
        name="adaln_mod",
    )(cvecs, ada_w, ada_b.reshape(DEPTH, 1, 6 * D_MODEL))


def _mod_row_map(layer, tiles_per_batch):
    if tiles_per_batch is None:
        return lambda i: (layer, 0, 0, 0)
    return lambda i: (layer, 1 + i // tiles_per_batch, 0, 0)


def _prep_q(q, qg, cos, sin):
    out = []
    for g in range(ATT_DIM // LANES):
        x = _head_rms(q[:, g * LANES:(g + 1) * LANES], qg)
        if cos is not None:
            x = _rope(x, cos, sin)
        out.append((x * (LOG2E * HEAD_DIM ** -0.5)).astype(BF16))
    return jnp.concatenate(out, axis=1)


def _in_kernel(x_ref, mod_ref, g_ref, w_ref, wab_ref, qg_ref, cos_ref, sin_ref,
               sc_ref, q_ref, kv_ref, dq_ref, dz_ref, ab_ref, *, rotary):
    x = x_ref[...]
    ms = jnp.mean(x * x, axis=-1, keepdims=True)
    h = (x * lax.rsqrt(ms + EPS) * g_ref[...]) * (1.0 + mod_ref[1:2, :]) + mod_ref[0:1, :]
    hb = h.astype(BF16)
    q = _dot(hb, w_ref[:, OFF_Q:OFF_KV])
    q_ref[...] = _prep_q(q, qg_ref[...], cos_ref[...] if rotary else None, sin_ref[...] if rotary else None)
    for ref, a, b in ((sc_ref, OFF_SC, OFF_Q), (kv_ref, OFF_KV, OFF_DQ), (dq_ref, OFF_DQ, OFF_DZ),
                      (dz_ref, OFF_DZ, OFF_AB)):
        ref[...] = _dot(hb, w_ref[:, a:b])
    ab_ref[...] = _dot(hb, wab_ref[...])


def _in_projection(x, mod, norm1_g, w_main_b, w_ab_b, q_norm_g, cos_t, sin_t, layer, tiles_per_batch):
    t = x.shape[0]
    widths = (OFF_Q - OFF_SC, OFF_KV - OFF_Q, OFF_DQ - OFF_KV, OFF_DZ - OFF_DQ, OFF_AB - OFF_DZ, IN_COLS - OFF_AB)
    rotary = tiles_per_batch is not None
    pos_map = (lambda i: (i % tiles_per_batch, 0)) if rotary else (lambda i: (0, 0))
    return pl.pallas_call(
        functools.partial(_in_kernel, rotary=rotary),
        out_shape=[jax.ShapeDtypeStruct((t, w), BF16 if k == 1 else F32) for k, w in enumerate(widths)],
        grid=(t // TM_IN,),
        in_specs=[pl.BlockSpec((TM_IN, D_MODEL), lambda i: (i, 0)),
                  pl.BlockSpec((None, None, 6, D_MODEL), _mod_row_map(layer, tiles_per_batch)),
                  pl.BlockSpec((None, 1, D_MODEL), lambda i: (layer, 0, 0)),
                  pl.BlockSpec((None, D_MODEL, OFF_AB), lambda i: (layer, 0, 0),
                               pipeline_mode=pl.Buffered(1)),
                  pl.BlockSpec((None, D_MODEL, IN_COLS - OFF_AB), lambda i: (layer, 0, 0),
                               pipeline_mode=pl.Buffered(1)),
                  pl.BlockSpec((None, 1, LANES), lambda i: (layer, 0, 0)),
                  pl.BlockSpec((TM_IN, LANES), pos_map),
                  pl.BlockSpec((TM_IN, LANES), pos_map)],
        out_specs=[pl.BlockSpec((TM_IN, w), lambda i: (i, 0)) for w in widths],
        compiler_params=pltpu.CompilerParams(vmem_limit_bytes=VMEM_LIMIT),
        name="in_proj",
    )(x, mod, norm1_g, w_main_b, w_ab_b, q_norm_g, cos_t, sin_t)


def _head_logits(qa, half, key_sets):
    lo = _lo_mask(qa.shape)
    qh = jnp.where(lo if half == 0 else jnp.logical_not(lo), qa, jnp.zeros_like(qa))
    logits = []
    for k, _, mask in key_sets:
        s = _dot_nt(qh, k())
        logits.append(s if mask is None else jnp.where(mask, s, NEG))
    return logits


def _head_softmax_pv(logits, key_sets, sink):
    m = jnp.maximum(jnp.max(jnp.concatenate(logits, axis=1), axis=-1, keepdims=True), sink)
    acc = None
    for s, (_, v1, _) in zip(logits, key_sets):
        o = _dot(jnp.exp2(s - m).astype(BF16), v1())
        acc = o if acc is None else acc + o
    return acc[:, :LANES] / (acc[:, LANES:] + jnp.exp2(sink - m))


def _attend_heads(q_ref, blocks, sink_ref, layer, o_ref):
    n_pairs = ATT_DIM // LANES
    items = [(blk, pair, half) for blk in range(len(blocks)) for pair in range(n_pairs) for half in range(2)]
    kvh_of = lambda pair: (2 * pair) // N_GROUP
    keys = {}

    def key_sets(blk, pair):
        if (blk, kvh_of(pair)) not in keys:
            keys[blk, kvh_of(pair)] = blocks[blk][1](kvh_of(pair))
        return keys[blk, kvh_of(pair)]

    def logits_of(blk, pair, half):
        return _head_logits(q_ref[blocks[blk][0], pair * LANES:(pair + 1) * LANES], half, key_sets(blk, pair))

    logits = logits_of(*items[0])
    done = []
    for idx, (blk, pair, half) in enumerate(items):
        nxt = logits_of(*items[idx + 1]) if idx + 1 < len(items) else None
        done.append(_head_softmax_pv(logits, key_sets(blk, pair), sink_ref[layer, 2 * pair + half] * LOG2E))
        logits = nxt
        if half == 1:
            lo = _lo_mask(done[-1].shape)
            o_ref[blocks[blk][0], pair * LANES:(pair + 1) * LANES] = jnp.where(lo, done[-2], done[-1])


def _values_with_ones(v2, kvh):
    return jnp.concatenate([_dup_half(v2, kvh), jnp.ones(v2.shape, F32)], axis=1).astype(BF16)


def _attn_ctx_kernel(sink_ref, q_ref, kv_ref, kg_ref, o_ref, kn_ref, *, layer):
    kv = kv_ref[...]
    kn = _head_rms(kv[:, :LANES], kg_ref[...])
    kn_ref[...] = kn
    v2 = kv[:, LANES:]
    kd = [_dup_half(kn, kvh).astype(BF16) for kvh in range(N_KV_HEADS)]
    vd = [_values_with_ones(v2, kvh) for kvh in range(N_KV_HEADS)]
    key_sets_of = lambda kvh: [(lambda: kd[kvh], lambda: vd[kvh], None)]
    _attend_heads(q_ref, [(slice(0, SEQ), key_sets_of)], sink_ref, layer, o_ref)


def _attention_ctx(q, kv, k_norm_g, attn_sink, layer):
    t = q.shape[0]
    return pl.pallas_call(
        functools.partial(_attn_ctx_kernel, layer=layer),
        out_shape=[jax.ShapeDtypeStruct((t, ATT_DIM), F32), jax.ShapeDtypeStruct((t, KV_DIM), F32)],
        grid=(t // SEQ,),
        in_specs=[pl.BlockSpec(memory_space=pltpu.SMEM),
                  pl.BlockSpec((SEQ, ATT_DIM), lambda b: (b, 0)),
                  pl.BlockSpec((SEQ, 2 * KV_DIM), lambda b: (b, 0)),
                  pl.BlockSpec((None, 1, LANES), lambda b: (layer, 0, 0))],
        out_specs=[pl.BlockSpec((SEQ, ATT_DIM), lambda b: (b, 0)),
                   pl.BlockSpec((SEQ, KV_DIM), lambda b: (b, 0))],
        compiler_params=pltpu.CompilerParams(vmem_limit_bytes=VMEM_LIMIT),
        name="attn_ctx",
    )(attn_sink, q, kv, k_norm_g)


QBLK = 128
QSTEP = 512
WIN_KEYS = QBLK + 2 * WINDOW
PREP_ROWS = 512


def _attn_lat_kernel(sink_ref, q_ref, kv_ref, kc_ref, vc_ref, kg_ref, cos_ref, sin_ref, o_ref,
                     kd_s, vd_s, kcd_s, vcd_s, *, layer):
    i = pl.program_id(1)
    n = DEC_SEQ

    @pl.when(i == 0)
    def _():
        for kvh in range(N_KV_HEADS):
            for s in (kd_s, vd_s):
                zpad = jnp.zeros((WINDOW, s.shape[-1]), BF16)
                s[kvh, 0:WINDOW, :] = zpad
                s[kvh, WINDOW + n:2 * WINDOW + n, :] = zpad
            kcd_s[kvh] = _dup_half(kc_ref[...], kvh).astype(BF16)
            vcd_s[kvh] = _values_with_ones(vc_ref[...], kvh)

        def prep(r, carry):
            r0 = pl.multiple_of(r * PREP_ROWS, PREP_ROWS)
            kv = kv_ref[pl.ds(r0, PREP_ROWS), :]
            kn = _head_rms(kv[:, :LANES], kg_ref[...])
            kr = _rope(kn, cos_ref[pl.ds(r0, PREP_ROWS), :], sin_ref[pl.ds(r0, PREP_ROWS), :])
            v2 = kv[:, LANES:]
            for kvh in range(N_KV_HEADS):
                kd_s[kvh, pl.ds(WINDOW + r0, PREP_ROWS), :] = _dup_half(kr, kvh).astype(BF16)
                vd_s[kvh, pl.ds(WINDOW + r0, PREP_ROWS), :] = _values_with_ones(v2, kvh)
            return carry

        lax.fori_loop(0, n // PREP_ROWS, prep, 0)

    rr = lax.broadcasted_iota(jnp.int32, (QBLK, WIN_KEYS), 0)
    jj = lax.broadcasted_iota(jnp.int32, (QBLK, WIN_KEYS), 1)
    band = (jj >= rr) & (jj <= rr + 2 * WINDOW)
    blocks = []
    for j in range(QSTEP // QBLK):
        start = pl.multiple_of(i * QSTEP + j * QBLK, QBLK)
        kpos = jj + (start - WINDOW)
        valid = band & (kpos >= 0) & (kpos < n)
        key_sets_of = lambda kvh, start=start, valid=valid: [
            (lambda: kcd_s[kvh], lambda: vcd_s[kvh], None),
            (lambda: kd_s[kvh, pl.ds(start, WIN_KEYS), :], lambda: vd_s[kvh, pl.ds(start, WIN_KEYS), :], valid)]
        blocks.append((slice(j * QBLK, (j + 1) * QBLK), key_sets_of))
    _attend_heads(q_ref, blocks, sink_ref, layer, o_ref)


def _attention_lat(q, kv, cache_k, cache_v, k_norm_g, attn_sink, cos_t, sin_t, layer):
    t = q.shape[0]
    n = DEC_SEQ
    nblk = n // QSTEP
    return pl.pallas_call(
        functools.partial(_attn_lat_kernel, layer=layer),
        out_shape=jax.ShapeDtypeStruct((t, ATT_DIM), F32),
        grid=(DEC_BATCH, nblk),
        in_specs=[pl.BlockSpec(memory_space=pltpu.SMEM),
                  pl.BlockSpec((QSTEP, ATT_DIM), lambda b, i: (b * nblk + i, 0)),
                  pl.BlockSpec((n, 2 * KV_DIM), lambda b, i: (b, 0)),
                  pl.BlockSpec((None, None, PAST_LEN, KV_DIM), lambda b, i: (b, layer, 0, 0)),
                  pl.BlockSpec((None, None, PAST_LEN, KV_DIM), lambda b, i: (b, layer, 0, 0)),
                  pl.BlockSpec((None, 1, LANES), lambda b, i: (layer, 0, 0)),
                  pl.BlockSpec((n, LANES), lambda b, i: (0, 0)),
                  pl.BlockSpec((n, LANES), lambda b, i: (0, 0))],
        out_specs=pl.BlockSpec((QSTEP, ATT_DIM), lambda b, i: (b * nblk + i, 0)),
        scratch_shapes=[pltpu.VMEM((N_KV_HEADS, n + 2 * WINDOW, LANES), BF16),
                        pltpu.VMEM((N_KV_HEADS, n + 2 * WINDOW, 2 * LANES), BF16),
                        pltpu.VMEM((N_KV_HEADS, PAST_LEN, LANES), BF16),
                        pltpu.VMEM((N_KV_HEADS, PAST_LEN, 2 * LANES), BF16)],
        compiler_params=pltpu.CompilerParams(vmem_limit_bytes=VMEM_LIMIT),
        name="attn_lat",
    )(attn_sink, q, kv, cache_k, cache_v, k_norm_g, cos_t, sin_t)


DN_ROWS = 128
DN_OUT_ROWS = 256
DN_CTX_SEQS = 4


def _dn_conv(x_ref, w_ref, r0, n):
    x = x_ref[pl.ds(r0, DN_ROWS), :]
    total = x_ref.shape[0]
    above = jnp.where(r0 % n > 0, x_ref[pl.ds(jnp.maximum(r0 - 1, 0), 1), :], 0.0)
    below = jnp.where(r0 % n + DN_ROWS < n, x_ref[pl.ds(jnp.minimum(r0 + DN_ROWS, total - 1), 1), :], 0.0)
    row = lax.broadcasted_iota(jnp.int32, x.shape, 0)
    prev = jnp.where(row == 0, above, pltpu.roll(x, 1, axis=0))
    nxt = jnp.where(row == DN_ROWS - 1, below, pltpu.roll(x, DN_ROWS - 1, axis=0))
    return _silu(prev * w_ref[0:1, :] + x * w_ref[1:2, :] + nxt * w_ref[2:3, :])


N_UNITS = 4
UW = N_UNITS * CHUNK
N_LEVELS = 6
M_INCL, M_STRICT, M_EYE = N_LEVELS, N_LEVELS + 1, N_LEVELS + 2
PREP_BLOCK = 8


def _dn_kernel(q_ref, k_ref, v_ref, z_ref, ab_ref, wq_ref, wk_ref, wv_ref, pa_ref, pdt_ref, ng_ref, s0_ref,
               o_ref, sout_ref, qs, ks, vs, gbs, pfs, r_s, au_s, m_s, qa_s, sh_s, el_s, st, mc, ex_s, tri_s, *, n):
    nc = n // CHUNK
    nseq = q_ref.shape[0] // n
    pblk = min(PREP_BLOCK, nseq * nc)

    @pl.when((pl.program_id(0) == 0) & (pl.program_id(1) == 0))
    def _():
        as_f32 = lambda m: jnp.where(m, 1.0, 0.0).astype(F32)
        ii = lax.broadcasted_iota(jnp.int32, (CHUNK, UW), 0)
        ll = lax.broadcasted_iota(jnp.int32, (CHUNK, UW), 1)
        unit_of_lane, jj = ll // CHUNK, ll % CHUNK
        fwd, bwd = unit_of_lane < 2, unit_of_lane >= 2
        for k in range(N_LEVELS):
            rk, ck = ii >> k, jj >> k
            lower = (rk - ck == 1) & ((rk & 1) == 1)
            upper = (ck - rk == 1) & ((ck & 1) == 1)
            mc[k] = as_f32((fwd & lower) | (bwd & upper))
        mc[M_INCL] = as_f32((fwd & (ii >= jj)) | (bwd & (ii <= jj)))
        mc[M_STRICT] = as_f32((fwd & (ii > jj)) | (bwd & (ii < jj)))
        mc[M_EYE] = as_f32(ii == jj)
        ec = lax.broadcasted_iota(jnp.int32, (LANES, 2 * UW), 0)
        el_ = lax.broadcasted_iota(jnp.int32, (LANES, 2 * UW), 1)
        ex_s[...] = as_f32(ec == (el_ // UW) * N_UNITS + (el_ % UW) // CHUNK).astype(BF16)
        ri = lax.broadcasted_iota(jnp.int32, (DN_ROWS, DN_ROWS), 0)
        ti = lax.broadcasted_iota(jnp.int32, (DN_ROWS, DN_ROWS), 1)
        tri_s[...] = as_f32((ti <= ri) & (ri // CHUNK == ti // CHUNK)).astype(BF16)

    def prep_rows(b):
        r0 = pl.multiple_of(b * DN_ROWS, DN_ROWS)
        rows = pl.ds(r0, DN_ROWS)
        qs[rows, :] = _head_l2n(_dn_conv(q_ref, wq_ref, r0, n)) * (DN_HEAD_DIM ** -0.5)
        ks[rows, :] = _head_l2n(_dn_conv(k_ref, wk_ref, r0, n))
        vs[rows, :] = _dn_conv(v_ref, wv_ref, r0, n)
        ab = ab_ref[rows, :]
        lane = lax.broadcasted_iota(jnp.int32, ab.shape, 1)
        gb = jnp.where(lane < N_UNITS, -jnp.exp(pa_ref[...]) * _softplus(ab + pdt_ref[...]), _sigmoid(ab))
        gbs[rows, :] = gb
        hi = gb.astype(BF16)
        r1 = gb - hi.astype(F32)
        mid = r1.astype(BF16)
        low = (r1 - mid.astype(F32)).astype(BF16)
        tri = tri_s[...]
        pfs[rows, :] = _dot(tri, hi) + _dot(tri, mid) + _dot(tri, low)

    for q in range(nseq):
        st[q] = jnp.concatenate([s0_ref[q, u // 2, u % 2] for u in range(N_UNITS)], axis=1)
    lo = _lo_mask((CHUNK, LANES))
    lane_c = lax.broadcasted_iota(jnp.int32, (CHUNK, LANES), 1)

    def spread_columns(m):
        hi = m.astype(BF16)
        r1 = m - hi.astype(F32)
        mid = r1.astype(BF16)
        low = (r1 - mid.astype(F32)).astype(BF16)
        e = ex_s[...]
        return _dot(hi, e) + _dot(mid, e) + _dot(low, e)

    def pair_block_diag(x):
        zero = jnp.zeros((CHUNK, LANES), BF16)
        return jnp.concatenate([jnp.where(lo, x, zero), jnp.where(lo, zero, x)], axis=0)

    def unit_dot(a, xb):
        return jnp.concatenate([_dot(a[:, h * LANES:(h + 1) * LANES], pair_block_diag(xb[:, h * LANES:(h + 1) * LANES]))
                                for h in range(2)], axis=1)

    def unit_dot2(a, xb, yb):
        outs = []
        for h in range(2):
            cols = slice(h * LANES, (h + 1) * LANES)
            rhs = jnp.concatenate([pair_block_diag(xb[:, cols]), pair_block_diag(yb[:, cols])], axis=1)
            outs.append(_dot(a[:, cols], rhs))
        return (jnp.concatenate([o[:, :LANES] for o in outs], axis=1),
                jnp.concatenate([o[:, LANES:] for o in outs], axis=1))

    def unit_rows(m):
        t = m.T
        return jnp.concatenate([t[u:u + 1, :] for u in range(N_UNITS)], axis=1)

    dup = lambda x: jnp.concatenate([x, x], axis=1)

    def chunk_prep(cs, between=(), first=()):
        rows = [pl.ds(pl.multiple_of(c * CHUNK, CHUNK), CHUNK) for c in cs]
        each = lambda f, *lists: [f(*xs) for xs in zip(*lists)]
        pending = list(between)
        early = list(first)

        def run_early():
            if early:
                early.pop(0)()

        run_early()

        gx = [gbs[r, :] for r in rows]
        pc = [pfs[r, :] for r in rows]
        tot = [p[CHUNK - 1:CHUNK, :] for p in pc]
        gcm = each(lambda g, p, t: jnp.where(lane_c < 2, p, t - p + g), gx, pc, tot)
        rsm = each(lambda g, p, t: jnp.where(lane_c < 2, t - p, p - g), gx, pc, tot)
        spread = each(lambda m, g: spread_columns(jnp.where(lane_c < N_UNITS, m, g)), gcm, gx)
        gc4 = [x[:, :UW] for x in spread]
        b4 = [x[:, UW:] for x in spread]
        e_gc = each(jnp.exp, gc4)
        dec = each(lambda c, m: jnp.exp(jnp.where(mc[M_INCL] > 0.0, c - unit_rows(m), NEG)), gc4, gcm)
        run_early()
        kt = [ks[r, :].T for r in rows]
        kt_pair = each(lambda t: jnp.concatenate([t[0:CHUNK], t[CHUNK:]], axis=1), kt)
        kt_bd = each(lambda t: pair_block_diag(t.astype(BF16)), kt_pair)
        gram = each(lambda r, b: dup(_dot(ks[r, :].astype(BF16), b)), rows, kt_bd)
        qk = each(lambda r, b: dup(_dot(qs[r, :].astype(BF16), b)), rows, kt_bd)
        run_early()
        lmat = each(lambda g, b, d: g * b * d * mc[M_STRICT], gram, b4, dec)
        amat = each(lambda m, d: (m * d).astype(BF16), qk, dec)
        t = each(lambda l: mc[M_EYE] - l * mc[0], lmat)
        while early:
            run_early()
        for k in range(1, N_LEVELS):
            lb = each(lambda l: (l * mc[k]).astype(BF16), lmat)
            tb = each(lambda x: x.astype(BF16), t)
            x = each(lambda a, b: unit_dot(a, b).astype(BF16), tb, lb)
            t = each(lambda t0, a, b: t0 - unit_dot(a, b), t, x, tb)
            if pending:
                pending.pop(0)()
        tm = each(lambda x: (x - mc[M_EYE]).astype(BF16), t)
        k4 = [dup(ks[r, :]) for r in rows]
        ru = each(lambda r, b: dup(vs[r, :]) * b, rows, b4)
        rw = each(lambda k, b, e: k * b * e, k4, b4, e_gc)
        tuw = each(lambda tmi, u, w: unit_dot2(tmi, u.astype(BF16), w.astype(BF16)), tm, ru, rw)
        ub = each(lambda x, d: (x + d[0]).astype(BF16), ru, tuw)
        wb = each(lambda x, d: (x + d[1]).astype(BF16), rw, tuw)
        kdt = each(lambda t, m: (dup(t) * jnp.exp(unit_rows(m))).astype(BF16), kt_pair, rsm)
        both = each(lambda k, a, w, u: unit_dot2(jnp.concatenate([k, a], axis=0), w, u), kdt, amat, wb, ub)
        mmat, rmat = [x[0][:CHUNK] for x in both], [x[1][:CHUNK] for x in both]
        aw, au = [x[0][CHUNK:] for x in both], [x[1][CHUNK:] for x in both]
        for i, (c, r) in enumerate(zip(cs, rows)):
            m_s[r, :] = mmat[i].astype(BF16)
            r_s[r, :] = rmat[i]
            qa_s[r, :] = (dup(qs[r, :]) * e_gc[i] - aw[i]).astype(BF16)
            au_s[r, :] = au[i]
            el = jnp.concatenate([e_gc[i][CHUNK - 1:CHUNK, :LANES], e_gc[i][0:1, LANES:]], axis=1)
            el_s[c] = jnp.broadcast_to(el, (8, UW))

    def scan_step(i):
        cf = [q * nc + i for q in range(nseq)]
        cb = [q * nc + nc - 1 - i for q in range(nseq)]
        rf = [pl.ds(pl.multiple_of(c * CHUNK, CHUNK), CHUNK) for c in cf]
        rb = [pl.ds(pl.multiple_of(c * CHUNK, CHUNK), CHUNK) for c in cb]
        mixed = lambda ref, q: jnp.concatenate([ref[rf[q], :LANES], ref[rb[q], LANES:]], axis=1)
        s = [st[q] for q in range(nseq)]
        sb = [x.astype(BF16) for x in s]
        for q in range(nseq):
            sh_s[rf[q], :LANES] = sb[q][:, :LANES]
            sh_s[rb[q], LANES:] = sb[q][:, LANES:]
        ms = [unit_dot(mixed(m_s, q), sb[q]) for q in range(nseq)]
        for q in range(nseq):
            el = jnp.concatenate([el_s[cf[q]][0:1, :LANES], el_s[cb[q]][0:1, LANES:]], axis=1)
            st[q] = s[q] * el - ms[q] + mixed(r_s, q)

    def finish_chunks(r0, count):
        parts = []
        for c in range(count):
            rows = pl.ds(pl.multiple_of(r0 + c * CHUNK, CHUNK), CHUNK)
            parts.append(unit_dot(qa_s[rows, :], sh_s[rows, :]) + au_s[rows, :])
        o = jnp.concatenate([p[:, :LANES] + p[:, LANES:] for p in parts], axis=0)
        rows = pl.ds(pl.multiple_of(r0, count * CHUNK), count * CHUNK)
        o_ref[rows, :] = _head_rms(o, ng_ref[...]) * _silu(z_ref[rows, :])

    def finish_rows(b, carry):
        finish_chunks(b * DN_OUT_ROWS, DN_OUT_ROWS // CHUNK)
        return carry

    def scan_only(i, carry):
        scan_step(i)
        return carry

    half = pblk // 2
    if nseq == 1 and nc % pblk == 0 and nc // pblk >= 3 and half <= N_LEVELS - 1 and (half * CHUNK) % DN_ROWS == 0:
        ends = lambda t: [t * half + s for s in range(half)] + [nc - 1 - (t * half + s) for s in range(half)]
        steps = lambda t: [functools.partial(scan_step, t * half + s) for s in range(half)]
        side = half * CHUNK // DN_ROWS
        last_block = n // DN_ROWS - 1
        row_blocks = lambda t: [t * side + s for s in range(side)] + [last_block - (t * side + s) for s in range(side)]
        input_pass = lambda t: [functools.partial(prep_rows, b) for b in row_blocks(t)]
        trips = nc // pblk
        for thunk in input_pass(0):
            thunk()
        chunk_prep(ends(0), first=input_pass(1))

        def prep_and_scan(t, carry):
            chunk_prep(ends(t), steps(t - 1), input_pass(t + 1))
            return carry

        lax.fori_loop(1, trips - 1, prep_and_scan, 0)
        chunk_prep(ends(trips - 1), steps(trips - 2))
        mid = nc // 2
        lax.fori_loop((trips - 1) * half, mid + 2, scan_only, 0)

        def finish_pair(j):
            finish_chunks((mid - 1 - j) * CHUNK, 1)
            finish_chunks((mid + j) * CHUNK, 1)

        def finish_and_scan(i, carry):
            finish_pair(2 * i)
            finish_pair(2 * i + 1)
            scan_step(mid + 2 + 2 * i)
            scan_step(mid + 3 + 2 * i)
            return carry

        lax.fori_loop(0, (mid - 2) // 2, finish_and_scan, 0)
        finish_pair(mid - 2)
        finish_pair(mid - 1)
    else:
        def input_block(b, carry):
            prep_rows(b)
            return carry

        def prep_block(blk, carry):
            chunk_prep([blk * pblk + s for s in range(pblk)])
            return carry

        lax.fori_loop(0, nseq * n // DN_ROWS, input_block, 0)
        lax.fori_loop(0, nseq * nc // pblk, prep_block, 0)
        lax.fori_loop(0, nc, scan_only, 0)
        lax.fori_loop(0, nseq * n // DN_OUT_ROWS, finish_rows, 0)

    for q in range(nseq):
        s = st[q]
        for u in range(N_UNITS):
            sout_ref[q, u // 2, u % 2] = s[:, u * CHUNK:(u + 1) * CHUNK]


def _delta_net(dq, dz, dab, dn_conv_w, pa, pdt, dn_norm_g, s0, layer, s0_layer, n, nseq):
    t = dq.shape[0]
    bsz = t // n
    rows = nseq * n
    nc = n // CHUNK
    conv_spec = lambda off: pl.BlockSpec((None, 3, LANES), lambda b, p: (layer, 0, off + p))
    col_spec = lambda off, bufs=2: pl.BlockSpec((rows, LANES), lambda b, p: (b, off + p),
                                                pipeline_mode=pl.Buffered(bufs))
    state_block = (nseq, 2, 2, DN_HEAD_DIM, DN_HEAD_DIM)
    return pl.pallas_call(
        functools.partial(_dn_kernel, n=n),
        out_shape=[jax.ShapeDtypeStruct((t, DN_DIM), F32),
                   jax.ShapeDtypeStruct((bsz, 2, N_DN_HEADS, DN_HEAD_DIM, DN_HEAD_DIM), F32)],
        grid=(bsz // nseq, N_PAIRS),
        in_specs=[col_spec(0), col_spec(N_PAIRS), col_spec(2 * N_PAIRS), col_spec(0, 1), col_spec(0, 1),
                  conv_spec(0), conv_spec(N_PAIRS), conv_spec(2 * N_PAIRS),
                  pl.BlockSpec((None, None, 1, LANES), lambda b, p: (layer, p, 0, 0)),
                  pl.BlockSpec((None, None, 1, LANES), lambda b, p: (layer, p, 0, 0)),
                  pl.BlockSpec((None, 1, LANES), lambda b, p: (layer, 0, 0)),
                  pl.BlockSpec((nseq, None) + state_block[1:], lambda b, p: (b, s0_layer, 0, p, 0, 0))],
        out_specs=[pl.BlockSpec((rows, LANES), lambda b, p: (b, p)),
                   pl.BlockSpec(state_block, lambda b, p: (b, 0, p, 0, 0))],
        scratch_shapes=[pltpu.VMEM((rows, LANES), F32) for _ in range(5)]
                       + [pltpu.VMEM((rows, UW), F32) for _ in range(2)]
                       + [pltpu.VMEM((rows, UW), BF16) for _ in range(3)]
                       + [pltpu.VMEM((nseq * nc, 8, UW), F32),
                          pltpu.VMEM((nseq, CHUNK, UW), F32),
                          pltpu.VMEM((N_LEVELS + 3, CHUNK, UW), F32),
                          pltpu.VMEM((LANES, 2 * UW), BF16),
                          pltpu.VMEM((DN_ROWS, DN_ROWS), BF16)],
        compiler_params=pltpu.CompilerParams(vmem_limit_bytes=(V7X_VMEM_BYTES * 7) // 8),
        name="delta_net",
    )(dq, dq, dq, dz, dab, dn_conv_w, dn_conv_w, dn_conv_w, pa, pdt, dn_norm_g, s0)


def _out_kernel(x_ref, sc_ref, scp_ref, scn_ref, att_ref, dn_ref, mod_ref, cw_ref, wo_ref, n2_ref,
                wg_ref, wu_ref, wd_ref, o_ref, *, tm, tiles_per_seq):
    t = pl.program_id(0) % tiles_per_seq
    sc = sc_ref[...]
    prod = sc[:, SC_DIM:2 * SC_DIM] * sc[:, 2 * SC_DIM:]
    pv = scp_ref[7:8, :]
    nx = scn_ref[0:1, :]
    prev_row = jnp.where(t > 0, pv[:, SC_DIM:2 * SC_DIM] * pv[:, 2 * SC_DIM:], 0.0)
    next_row = jnp.where(t < tiles_per_seq - 1, nx[:, SC_DIM:2 * SC_DIM] * nx[:, 2 * SC_DIM:], 0.0)
    row = lax.broadcasted_iota(jnp.int32, prod.shape, 0)
    p_prev = jnp.where(row == 0, prev_row, pltpu.roll(prod, 1, axis=0))
    p_next = jnp.where(row == tm - 1, next_row, pltpu.roll(prod, tm - 1, axis=0))
    y_sc = sc[:, :SC_DIM] * (p_prev * cw_ref[0:1, :] + prod * cw_ref[1:2, :] + p_next * cw_ref[2:3, :])
    y = (_dot(y_sc.astype(BF16), wo_ref[0:SC_DIM, :])
         + _dot(att_ref[...].astype(BF16), wo_ref[SC_DIM:SC_DIM + ATT_DIM, :])
         + _dot(dn_ref[...].astype(BF16), wo_ref[SC_DIM + ATT_DIM:, :]))
    x1 = x_ref[...] + mod_ref[2:3, :] * y
    ms = jnp.mean(x1 * x1, axis=-1, keepdims=True)
    h2 = (x1 * lax.rsqrt(ms + EPS) * n2_ref[...]) * (1.0 + mod_ref[4:5, :]) + mod_ref[3:4, :]
    hb = h2.astype(BF16)
    act = _silu(_dot(hb, wg_ref[...])) * _dot(hb, wu_ref[...])
    o_ref[...] = x1 + mod_ref[5:6, :] * _dot(act.astype(BF16), wd_ref[...])


def _output_stage(x, sc, att, dn, mod, sc_conv_w, w_out_b, norm2_g, w_gate_b, w_up_b, w_down_b, layer,
                  tm, seq_len, per_batch_mod):
    t = x.shape[0]
    tiles_per_seq = seq_len // tm
    tiles_per_batch = tiles_per_seq if per_batch_mod else None
    halo = tm // 8
    last = t // 8 - 1
    resident = lambda shape: pl.BlockSpec((None,) + shape, lambda i: (layer, 0, 0), pipeline_mode=pl.Buffered(1))
    return pl.pallas_call(
        functools.partial(_out_kernel, tm=tm, tiles_per_seq=tiles_per_seq),
        out_shape=jax.ShapeDtypeStruct((t, D_MODEL), F32),
        grid=(t // tm,),
        in_specs=[pl.BlockSpec((tm, D_MODEL), lambda i: (i, 0)),
                  pl.BlockSpec((tm, 3 * SC_DIM), lambda i: (i, 0)),
                  pl.BlockSpec((8, 3 * SC_DIM), lambda i: (jnp.maximum(i * halo - 1, 0), 0)),
                  pl.BlockSpec((8, 3 * SC_DIM), lambda i: (jnp.minimum((i + 1) * halo, last), 0)),
                  pl.BlockSpec((tm, ATT_DIM), lambda i: (i, 0)),
                  pl.BlockSpec((tm, DN_DIM), lambda i: (i, 0)),
                  pl.BlockSpec((None, None, 6, D_MODEL), _mod_row_map(layer, tiles_per_batch)),
                  pl.BlockSpec((None, 3, SC_DIM), lambda i: (layer, 0, 0)),
                  resident((MIX_DIM, D_MODEL)),
                  pl.BlockSpec((None, 1, D_MODEL), lambda i: (layer, 0, 0)),
                  resident((D_MODEL, D_FF)), resident((D_MODEL, D_FF)), resident((D_FF, D_MODEL))],
        out_specs=pl.BlockSpec((tm, D_MODEL), lambda i: (i, 0)),
        compiler_params=pltpu.CompilerParams(vmem_limit_bytes=VMEM_LIMIT),
        name="out_stage",
    )(x, sc, sc, sc, att, dn, mod, sc_conv_w, w_out_b, norm2_g, w_gate_b, w_up_b, w_down_b)


def _rope_tables():
    pos = jnp.arange(DEC_SEQ)
    half = HEAD_DIM // 4
    inv = 1.0 / (ROPE_BASE ** (jnp.arange(half, dtype=F32) / half))
    ang_r = (pos // GRID_W).astype(F32)[:, None] * inv
    ang_c = (pos % GRID_W).astype(F32)[:, None] * inv
    cos = jnp.concatenate([jnp.cos(ang_r)] * 2 + [jnp.cos(ang_c)] * 2, axis=-1)
    sin = jnp.concatenate([-jnp.sin(ang_r), jnp.sin(ang_r), -jnp.sin(ang_c), jnp.sin(ang_c)], axis=-1)
    return jnp.tile(cos, (1, 2)), jnp.tile(sin, (1, 2))


def _pack_w_in(w_in):
    a0 = OFF_AB
    b0 = OFF_AB + 2 * N_DN_HEADS
    blocks = []
    for p in range(N_PAIRS):
        idx = [a0 + d * N_DN_HEADS + 2 * p + j for d in range(2) for j in range(2)]
        idx += [b0 + d * N_DN_HEADS + 2 * p + j for d in range(2) for j in range(2)]
        blocks += [w_in[:, :, c:c + 1] for c in idx]
        blocks.append(jnp.zeros((DEPTH, D_MODEL, LANES - len(idx)), w_in.dtype))
    return w_in[:, :, :OFF_AB].astype(BF16), jnp.concatenate(blocks, axis=2).astype(BF16)


def _pair_lanes(p):
    x = p.reshape(DEPTH, 2, N_PAIRS, 2).transpose(0, 2, 1, 3).reshape(DEPTH, N_PAIRS, 1, 4)
    return jnp.pad(x, ((0, 0), (0, 0), (0, 0), (0, LANES - 4)))


def kernel(x_prompt, x_sample, cache_k, cache_v, state_delta, c, c_ctx, w_in, w_out, ada_w, ada_b,
           norm1_g, norm2_g, sc_conv_w, dn_conv_w, q_norm_g, k_norm_g, attn_sink, dn_A_log,
           dn_dt_bias, dn_norm_g, w_gate, w_up, w_down):
    w_main_b, w_ab_b = _pack_w_in(w_in)
    w_out_b, w_gate_b, w_up_b, w_down_b = (w.astype(BF16) for w in (w_out, w_gate, w_up, w_down))
    n1 = norm1_g.reshape(DEPTH, 1, D_MODEL)
    n2 = norm2_g.reshape(DEPTH, 1, D_MODEL)
    qg = jnp.tile(q_norm_g, (1, 2)).reshape(DEPTH, 1, LANES)
    kg = jnp.tile(k_norm_g, (1, 2)).reshape(DEPTH, 1, LANES)
    ng = jnp.tile(dn_norm_g, (1, 2)).reshape(DEPTH, 1, LANES)
    pa, pdt = _pair_lanes(dn_A_log), _pair_lanes(dn_dt_bias)
    cos_t, sin_t = _rope_tables()
    ck = cache_k.reshape(DEC_BATCH, DEPTH, PAST_LEN, KV_DIM)
    cv = cache_v.reshape(DEC_BATCH, DEPTH, PAST_LEN, KV_DIM)
    zero_state = jnp.zeros((BATCH, 1, 2, N_DN_HEADS, DN_HEAD_DIM, DN_HEAD_DIM), F32)

    cvecs = jnp.concatenate([c_ctx[None, :], c, jnp.zeros((MOD_ROWS - 1 - DEC_BATCH, D_MODEL), F32)], axis=0)
    mod = _modulation(cvecs, ada_w, ada_b).reshape(DEPTH, MOD_ROWS, 6, D_MODEL)

    xp = x_prompt.reshape(BATCH * SEQ, D_MODEL)
    xs = x_sample.reshape(DEC_BATCH * DEC_SEQ, D_MODEL)
    new_k, new_v, new_s = [], [], []
    for l in range(DEPTH):
        out_args = (mod, sc_conv_w, w_out_b, n2, w_gate_b, w_up_b, w_down_b, l)
        sc, q, kv, dq, dz, dab = _in_projection(xp, mod, n1, w_main_b, w_ab_b, qg, cos_t, sin_t, l, None)
        att, kn = _attention_ctx(q, kv, kg, attn_sink, l)
        dn, s_out = _delta_net(dq, dz, dab, dn_conv_w, pa, pdt, ng, zero_state, l, 0, SEQ, DN_CTX_SEQS)
        xp = _output_stage(xp, sc, att, dn, *out_args, TM, SEQ, False)
        new_k.append(kn.reshape(BATCH, SEQ, N_KV_HEADS, HEAD_DIM))
        new_v.append(kv[:, KV_DIM:].reshape(BATCH, SEQ, N_KV_HEADS, HEAD_DIM))
        new_s.append(s_out)
        sc, q, kv, dq, dz, dab = _in_projection(xs, mod, n1, w_main_b, w_ab_b, qg, cos_t, sin_t, l, DEC_SEQ // TM_IN)
        att = _attention_lat(q, kv, ck, cv, kg, attn_sink, cos_t, sin_t, l)
        dn, _ = _delta_net(dq, dz, dab, dn_conv_w, pa, pdt, ng, state_delta, l, l, DEC_SEQ, 1)
        xs = _output_stage(xs, sc, att, dn, *out_args, TM_LAT, DEC_SEQ, True)
    return (xp.reshape(BATCH, SEQ, D_MODEL), xs.reshape(DEC_BATCH, DEC_SEQ, D_MODEL),
            jnp.stack(new_k, axis=1), jnp.stack(new_v, axis=1), jnp.stack(new_s, axis=1))
```

```python
import functools

import numpy as np
import jax
import jax.numpy as jnp
from jax import lax
from jax.experimental import pallas as pl
from jax.experimental.pallas import tpu as pltpu

F32 = jnp.float32
BF16 = jnp.bfloat16

D_MODEL = 1024
BATCH = 16
SEQ = 256
DEPTH = 2
DEC_BATCH = 4
DEC_SEQ = 4096
PAST_LEN = 512
GRID_W = 64
SC_DIM = 256
N_Q_HEADS = 8
N_KV_HEADS = 2
N_GROUP = N_Q_HEADS // N_KV_HEADS
HEAD_DIM = 64
ATT_DIM = N_Q_HEADS * HEAD_DIM
KV_DIM = N_KV_HEADS * HEAD_DIM
WINDOW = 128
N_DN_HEADS = 4
DN_HEAD_DIM = 64
DN_DIM = N_DN_HEADS * DN_HEAD_DIM
CHUNK = 64
MIX_DIM = SC_DIM + ATT_DIM + DN_DIM
D_FF = -(-8 * D_MODEL // (3 * 256)) * 256
ROPE_BASE = 10000.0
EPS = 1e-6
NEG = -1e30
LOG2E = 1.4426950408889634

LANES = 128
V7X_VMEM_BYTES = 64 * 1024 * 1024
VMEM_LIMIT = (V7X_VMEM_BYTES * 3) // 4

OFF_SC, OFF_Q, OFF_KV, OFF_DQ, OFF_DZ, OFF_AB = 0, 768, 1280, 1536, 2304, 2560
IN_COLS = OFF_AB + 2 * LANES
N_PAIRS = N_DN_HEADS // 2
MOD_ROWS = 8
TM = 256
TM_LAT = 512
TM_IN = 512


def _sigmoid(x):
    return 0.5 * jnp.tanh(0.5 * x) + 0.5


def _silu(x):
    return x * _sigmoid(x)


def _softplus(x):
    return jnp.maximum(x, 0.0) + jnp.log1p(jnp.exp(-jnp.abs(x)))


def _dot(a, b):
    return jnp.dot(a, b, preferred_element_type=F32)


def _dot_nt(a, b):
    return lax.dot_general(a, b, (((1,), (1,)), ((), ())), preferred_element_type=F32)


def _lo_mask(shape):
    return lax.broadcasted_iota(jnp.int32, shape, 1) % LANES < HEAD_DIM


def _half_sums(xx, lo):
    s_lo = jnp.sum(jnp.where(lo, xx, 0.0), axis=-1, keepdims=True)
    s_hi = jnp.sum(jnp.where(lo, 0.0, xx), axis=-1, keepdims=True)
    return jnp.where(lo, s_lo, s_hi)


def _head_rms(x, g_row):
    lo = _lo_mask(x.shape)
    ms = _half_sums(x * x, lo) * (1.0 / HEAD_DIM)
    return x * lax.rsqrt(ms + EPS) * g_row


def _head_l2n(x):
    lo = _lo_mask(x.shape)
    return x * lax.rsqrt(_half_sums(x * x, lo) + EPS)


def _rope(x, cos, sin):
    n = x.shape[1]
    lane = lax.broadcasted_iota(jnp.int32, x.shape, 1)
    swapped = jnp.where((lane & 16) == 0, pltpu.roll(x, n - 16, axis=1), pltpu.roll(x, 16, axis=1))
    return x * cos + swapped * sin


def _dup_half(x, half):
    lo = _lo_mask(x.shape)
    r = pltpu.roll(x, HEAD_DIM, axis=1)
    return jnp.where(lo, x, r) if half == 0 else jnp.where(lo, r, x)


def _mod_kernel(c_ref, w_ref, b_ref, o_ref):
    s = _silu(c_ref[...])
    o_ref[...] = _dot(s.astype(BF16), w_ref[...].astype(BF16)) + b_ref[...]


def _modulation(cvecs, ada_w, ada_b):
    nblk = 6
    return pl.pallas_call(
        _mod_kernel,
        out_shape=jax.ShapeDtypeStruct((DEPTH, MOD_ROWS, 6 * D_MODEL), F32),
        grid=(DEPTH, nblk),
        in_specs=[pl.BlockSpec((MOD_ROWS, D_MODEL), lambda l, j: (0, 0)),
                  pl.BlockSpec((None, D_MODEL, D_MODEL), lambda l, j: (l, 0, j)),
                  pl.BlockSpec((None, 1, D_MODEL), lambda l, j: (l, 0, j))],
        out_specs=pl.BlockSpec((None, MOD_ROWS, D_MODEL), lambda l, j: (l, 0, j)),
        compiler_params=pltpu.CompilerParams(vmem_limit_bytes=VMEM_LIMIT),
        name="adaln_mod",
    )(cvecs, ada_w, ada_b.reshape(DEPTH, 1, 6 * D_MODEL))


def _mod_row_map(layer, tiles_per_batch):
    if tiles_per_batch is None:
        return lambda i: (layer, 0, 0, 0)
    return lambda i: (layer, 1 + i // tiles_per_batch, 0, 0)


def _prep_q(q, qg, cos, sin):
    out = []
    for g in range(ATT_DIM // LANES):
        x = _head_rms(q[:, g * LANES:(g + 1) * LANES], qg)
        if cos is not None:
            x = _rope(x, cos, sin)
        out.append((x * (LOG2E * HEAD_DIM ** -0.5)).astype(BF16))
    return jnp.concatenate(out, axis=1)


def _in_kernel(x_ref, mod_ref, g_ref, w_ref, wab_ref, qg_ref, cos_ref, sin_ref,
               sc_ref, q_ref, kv_ref, dq_ref, dz_ref, ab_ref, *, rotary):
    x = x_ref[...]
    ms = jnp.mean(x * x, axis=-1, keepdims=True)
    h = (x * lax.rsqrt(ms + EPS) * g_ref[...]) * (1.0 + mod_ref[1:2, :]) + mod_ref[0:1, :]
    hb = h.astype(BF16)
    q = _dot(hb, w_ref[:, OFF_Q:OFF_KV])
    q_ref[...] = _prep_q(q, qg_ref[...], cos_ref[...] if rotary else None, sin_ref[...] if rotary else None)
    for ref, a, b in ((sc_ref, OFF_SC, OFF_Q), (kv_ref, OFF_KV, OFF_DQ), (dq_ref, OFF_DQ, OFF_DZ),
                      (dz_ref, OFF_DZ, OFF_AB)):
        ref[...] = _dot(hb, w_ref[:, a:b])
    ab_ref[...] = _dot(hb, wab_ref[...])


def _in_projection(x, mod, norm1_g, w_main_b, w_ab_b, q_norm_g, cos_t, sin_t, layer, tiles_per_batch):
    t = x.shape[0]
    widths = (OFF_Q - OFF_SC, OFF_KV - OFF_Q, OFF_DQ - OFF_KV, OFF_DZ - OFF_DQ, OFF_AB - OFF_DZ, IN_COLS - OFF_AB)
    rotary = tiles_per_batch is not None
    pos_map = (lambda i: (i % tiles_per_batch, 0)) if rotary else (lambda i: (0, 0))
    return pl.pallas_call(
        functools.partial(_in_kernel, rotary=rotary),
        out_shape=[jax.ShapeDtypeStruct((t, w), BF16 if k == 1 else F32) for k, w in enumerate(widths)],
        grid=(t // TM_IN,),
        in_specs=[pl.BlockSpec((TM_IN, D_MODEL), lambda i: (i, 0)),
                  pl.BlockSpec((None, None, 6, D_MODEL), _mod_row_map(layer, tiles_per_batch)),
                  pl.BlockSpec((None, 1, D_MODEL), lambda i: (layer, 0, 0)),
                  pl.BlockSpec((None, D_MODEL, OFF_AB), lambda i: (layer, 0, 0),
                               pipeline_mode=pl.Buffered(1)),
                  pl.BlockSpec((None, D_MODEL, IN_COLS - OFF_AB), lambda i: (layer, 0, 0),
                               pipeline_mode=pl.Buffered(1)),
                  pl.BlockSpec((None, 1, LANES), lambda i: (layer, 0, 0)),
                  pl.BlockSpec((TM_IN, LANES), pos_map),
                  pl.BlockSpec((TM_IN, LANES), pos_map)],
        out_specs=[pl.BlockSpec((TM_IN, w), lambda i: (i, 0)) for w in widths],
        compiler_params=pltpu.CompilerParams(vmem_limit_bytes=VMEM_LIMIT),
        name="in_proj",
    )(x, mod, norm1_g, w_main_b, w_ab_b, q_norm_g, cos_t, sin_t)


def _head_logits(qa, half, key_sets):
    lo = _lo_mask(qa.shape)
    qh = jnp.where(lo if half == 0 else jnp.logical_not(lo), qa, jnp.zeros_like(qa))
    logits = []
    for k, _, mask in key_sets:
        s = _dot_nt(qh, k())
        logits.append(s if mask is None else jnp.where(mask, s, NEG))
    return logits


def _head_softmax_pv(logits, key_sets, sink):
    m = jnp.maximum(jnp.max(jnp.concatenate(logits, axis=1), axis=-1, keepdims=True), sink)
    acc = None
    for s, (_, v1, _) in zip(logits, key_sets):
        o = _dot(jnp.exp2(s - m).astype(BF16), v1())
        acc = o if acc is None else acc + o
    return acc[:, :LANES] / (acc[:, LANES:] + jnp.exp2(sink - m))


def _attend_heads(q_ref, blocks, sink_ref, layer, o_ref):
    n_pairs = ATT_DIM // LANES
    items = [(blk, pair, half) for blk in range(len(blocks)) for pair in range(n_pairs) for half in range(2)]
    kvh_of = lambda pair: (2 * pair) // N_GROUP
    keys = {}

    def key_sets(blk, pair):
        if (blk, kvh_of(pair)) not in keys:
            keys[blk, kvh_of(pair)] = blocks[blk][1](kvh_of(pair))
        return keys[blk, kvh_of(pair)]

    def logits_of(blk, pair, half):
        return _head_logits(q_ref[blocks[blk][0], pair * LANES:(pair + 1) * LANES], half, key_sets(blk, pair))

    logits = logits_of(*items[0])
    done = []
    for idx, (blk, pair, half) in enumerate(items):
        nxt = logits_of(*items[idx + 1]) if idx + 1 < len(items) else None
        done.append(_head_softmax_pv(logits, key_sets(blk, pair), sink_ref[layer, 2 * pair + half] * LOG2E))
        logits = nxt
        if half == 1:
            lo = _lo_mask(done[-1].shape)
            o_ref[blocks[blk][0], pair * LANES:(pair + 1) * LANES] = jnp.where(lo, done[-2], done[-1])


def _values_with_ones(v2, kvh):
    return jnp.concatenate([_dup_half(v2, kvh), jnp.ones(v2.shape, F32)], axis=1).astype(BF16)


def _attn_ctx_kernel(sink_ref, q_ref, kv_ref, kg_ref, o_ref, kn_ref, *, layer):
    kv = kv_ref[...]
    kn = _head_rms(kv[:, :LANES], kg_ref[...])
    kn_ref[...] = kn
    v2 = kv[:, LANES:]
    kd = [_dup_half(kn, kvh).astype(BF16) for kvh in range(N_KV_HEADS)]
    vd = [_values_with_ones(v2, kvh) for kvh in range(N_KV_HEADS)]
    key_sets_of = lambda kvh: [(lambda: kd[kvh], lambda: vd[kvh], None)]
    _attend_heads(q_ref, [(slice(0, SEQ), key_sets_of)], sink_ref, layer, o_ref)


def _attention_ctx(q, kv, k_norm_g, attn_sink, layer):
    t = q.shape[0]
    return pl.pallas_call(
        functools.partial(_attn_ctx_kernel, layer=layer),
        out_shape=[jax.ShapeDtypeStruct((t, ATT_DIM), F32), jax.ShapeDtypeStruct((t, KV_DIM), F32)],
        grid=(t // SEQ,),
        in_specs=[pl.BlockSpec(memory_space=pltpu.SMEM),
                  pl.BlockSpec((SEQ, ATT_DIM), lambda b: (b, 0)),
                  pl.BlockSpec((SEQ, 2 * KV_DIM), lambda b: (b, 0)),
                  pl.BlockSpec((None, 1, LANES), lambda b: (layer, 0, 0))],
        out_specs=[pl.BlockSpec((SEQ, ATT_DIM), lambda b: (b, 0)),
                   pl.BlockSpec((SEQ, KV_DIM), lambda b: (b, 0))],
        compiler_params=pltpu.CompilerParams(vmem_limit_bytes=VMEM_LIMIT),
        name="attn_ctx",
    )(attn_sink, q, kv, k_norm_g)


QBLK = 128
QSTEP = 1024
WIN_KEYS = QBLK + 2 * WINDOW
PREP_ROWS = 512


def _attn_lat_kernel(sink_ref, q_ref, kv_ref, kc_ref, vc_ref, kg_ref, cos_ref, sin_ref, o_ref,
                     kd_s, vd_s, kcd_s, vcd_s, *, layer):
    i = pl.program_id(1)
    n = DEC_SEQ

    @pl.when(i == 0)
    def _():
        for kvh in range(N_KV_HEADS):
            for s in (kd_s, vd_s):
                zpad = jnp.zeros((WINDOW, s.shape[-1]), BF16)
                s[kvh, 0:WINDOW, :] = zpad
                s[kvh, WINDOW + n:2 * WINDOW + n, :] = zpad
            kcd_s[kvh] = _dup_half(kc_ref[...], kvh).astype(BF16)
            vcd_s[kvh] = _values_with_ones(vc_ref[...], kvh)

        def prep(r, carry):
            r0 = pl.multiple_of(r * PREP_ROWS, PREP_ROWS)
            kv = kv_ref[pl.ds(r0, PREP_ROWS), :]
            kn = _head_rms(kv[:, :LANES], kg_ref[...])
            kr = _rope(kn, cos_ref[pl.ds(r0, PREP_ROWS), :], sin_ref[pl.ds(r0, PREP_ROWS), :])
            v2 = kv[:, LANES:]
            for kvh in range(N_KV_HEADS):
                kd_s[kvh, pl.ds(WINDOW + r0, PREP_ROWS), :] = _dup_half(kr, kvh).astype(BF16)
                vd_s[kvh, pl.ds(WINDOW + r0, PREP_ROWS), :] = _values_with_ones(v2, kvh)
            return carry

        lax.fori_loop(0, n // PREP_ROWS, prep, 0)

    rr = lax.broadcasted_iota(jnp.int32, (QBLK, WIN_KEYS), 0)
    jj = lax.broadcasted_iota(jnp.int32, (QBLK, WIN_KEYS), 1)
    band = (jj >= rr) & (jj <= rr + 2 * WINDOW)
    blocks = []
    for j in range(QSTEP // QBLK):
        start = pl.multiple_of(i * QSTEP + j * QBLK, QBLK)
        kpos = jj + (start - WINDOW)
        valid = band & (kpos >= 0) & (kpos < n)
        key_sets_of = lambda kvh, start=start, valid=valid: [
            (lambda: kcd_s[kvh], lambda: vcd_s[kvh], None),
            (lambda: kd_s[kvh, pl.ds(start, WIN_KEYS), :], lambda: vd_s[kvh, pl.ds(start, WIN_KEYS), :], valid)]
        blocks.append((slice(j * QBLK, (j + 1) * QBLK), key_sets_of))
    _attend_heads(q_ref, blocks, sink_ref, layer, o_ref)


def _attention_lat(q, kv, cache_k, cache_v, k_norm_g, attn_sink, cos_t, sin_t, layer):
    t = q.shape[0]
    n = DEC_SEQ
    nblk = n // QSTEP
    return pl.pallas_call(
        functools.partial(_attn_lat_kernel, layer=layer),
        out_shape=jax.ShapeDtypeStruct((t, ATT_DIM), F32),
        grid=(DEC_BATCH, nblk),
        in_specs=[pl.BlockSpec(memory_space=pltpu.SMEM),
                  pl.BlockSpec((QSTEP, ATT_DIM), lambda b, i: (b * nblk + i, 0)),
                  pl.BlockSpec((n, 2 * KV_DIM), lambda b, i: (b, 0)),
                  pl.BlockSpec((None, None, PAST_LEN, KV_DIM), lambda b, i: (b, layer, 0, 0)),
                  pl.BlockSpec((None, None, PAST_LEN, KV_DIM), lambda b, i: (b, layer, 0, 0)),
                  pl.BlockSpec((None, 1, LANES), lambda b, i: (layer, 0, 0)),
                  pl.BlockSpec((n, LANES), lambda b, i: (0, 0)),
                  pl.BlockSpec((n, LANES), lambda b, i: (0, 0))],
        out_specs=pl.BlockSpec((QSTEP, ATT_DIM), lambda b, i: (b * nblk + i, 0)),
        scratch_shapes=[pltpu.VMEM((N_KV_HEADS, n + 2 * WINDOW, LANES), BF16),
                        pltpu.VMEM((N_KV_HEADS, n + 2 * WINDOW, 2 * LANES), BF16),
                        pltpu.VMEM((N_KV_HEADS, PAST_LEN, LANES), BF16),
                        pltpu.VMEM((N_KV_HEADS, PAST_LEN, 2 * LANES), BF16)],
        compiler_params=pltpu.CompilerParams(vmem_limit_bytes=VMEM_LIMIT),
        name="attn_lat",
    )(attn_sink, q, kv, cache_k, cache_v, k_norm_g, cos_t, sin_t)


DN_ROWS = 128
DN_OUT_ROWS = 256
DN_CTX_SEQS = 8


def _dn_conv(x_ref, w_ref, r0, n):
    x = x_ref[pl.ds(r0, DN_ROWS), :]
    total = x_ref.shape[0]
    above = jnp.where(r0 % n > 0, x_ref[pl.ds(jnp.maximum(r0 - 1, 0), 1), :], 0.0)
    below = jnp.where(r0 % n + DN_ROWS < n, x_ref[pl.ds(jnp.minimum(r0 + DN_ROWS, total - 1), 1), :], 0.0)
    row = lax.broadcasted_iota(jnp.int32, x.shape, 0)
    prev = jnp.where(row == 0, above, pltpu.roll(x, 1, axis=0))
    nxt = jnp.where(row == DN_ROWS - 1, below, pltpu.roll(x, DN_ROWS - 1, axis=0))
    return _silu(prev * w_ref[0:1, :] + x * w_ref[1:2, :] + nxt * w_ref[2:3, :])


N_UNITS = 4
UW = N_UNITS * CHUNK
N_LEVELS = 6
M_INCL, M_STRICT, M_EYE = N_LEVELS, N_LEVELS + 1, N_LEVELS + 2
PREP_BLOCK = 8


def _dn_kernel(q_ref, k_ref, v_ref, z_ref, ab_ref, wq_ref, wk_ref, wv_ref, pa_ref, pdt_ref, ng_ref, s0_ref,
               o_ref, sout_ref, qs, ks, vs, gbs, pfs, r_s, au_s, m_s, qa_s, sh_s, el_s, st, mc, ex_s, tri_s, *, n):
    nc = n // CHUNK
    nseq = q_ref.shape[0] // n
    pblk = min(PREP_BLOCK, nseq * nc)

    @pl.when((pl.program_id(0) == 0) & (pl.program_id(1) == 0))
    def _():
        as_f32 = lambda m: jnp.where(m, 1.0, 0.0).astype(F32)
        ii = lax.broadcasted_iota(jnp.int32, (CHUNK, UW), 0)
        ll = lax.broadcasted_iota(jnp.int32, (CHUNK, UW), 1)
        unit_of_lane, jj = ll // CHUNK, ll % CHUNK
        fwd, bwd = unit_of_lane < 2, unit_of_lane >= 2
        for k in range(N_LEVELS):
            rk, ck = ii >> k, jj >> k
            lower = (rk - ck == 1) & ((rk & 1) == 1)
            upper = (ck - rk == 1) & ((ck & 1) == 1)
            mc[k] = as_f32((fwd & lower) | (bwd & upper))
        mc[M_INCL] = as_f32((fwd & (ii >= jj)) | (bwd & (ii <= jj)))
        mc[M_STRICT] = as_f32((fwd & (ii > jj)) | (bwd & (ii < jj)))
        mc[M_EYE] = as_f32(ii == jj)
        ec = lax.broadcasted_iota(jnp.int32, (LANES, 2 * UW), 0)
        el_ = lax.broadcasted_iota(jnp.int32, (LANES, 2 * UW), 1)
        ex_s[...] = as_f32(ec == (el_ // UW) * N_UNITS + (el_ % UW) // CHUNK).astype(BF16)
        ri = lax.broadcasted_iota(jnp.int32, (DN_ROWS, DN_ROWS), 0)
        ti = lax.broadcasted_iota(jnp.int32, (DN_ROWS, DN_ROWS), 1)
        tri_s[...] = as_f32((ti <= ri) & (ri // CHUNK == ti // CHUNK)).astype(BF16)

    def prep_rows(b):
        r0 = pl.multiple_of(b * DN_ROWS, DN_ROWS)
        rows = pl.ds(r0, DN_ROWS)
        qs[rows, :] = _head_l2n(_dn_conv(q_ref, wq_ref, r0, n)) * (DN_HEAD_DIM ** -0.5)
        ks[rows, :] = _head_l2n(_dn_conv(k_ref, wk_ref, r0, n))
        vs[rows, :] = _dn_conv(v_ref, wv_ref, r0, n)
        ab = ab_ref[rows, :]
        lane = lax.broadcasted_iota(jnp.int32, ab.shape, 1)
        gb = jnp.where(lane < N_UNITS, -jnp.exp(pa_ref[...]) * _softplus(ab + pdt_ref[...]), _sigmoid(ab))
        gbs[rows, :] = gb
        hi = gb.astype(BF16)
        r1 = gb - hi.astype(F32)
        mid = r1.astype(BF16)
        low = (r1 - mid.astype(F32)).astype(BF16)
        tri = tri_s[...]
        pfs[rows, :] = _dot(tri, hi) + _dot(tri, mid) + _dot(tri, low)

    for q in range(nseq):
        st[q] = jnp.concatenate([s0_ref[q, u // 2, u % 2] for u in range(N_UNITS)], axis=1)
    lo = _lo_mask((CHUNK, LANES))
    lane_c = lax.broadcasted_iota(jnp.int32, (CHUNK, LANES), 1)

    def spread_columns(m):
        hi = m.astype(BF16)
        r1 = m - hi.astype(F32)
        mid = r1.astype(BF16)
        low = (r1 - mid.astype(F32)).astype(BF16)
        e = ex_s[...]
        return _dot(hi, e) + _dot(mid, e) + _dot(low, e)

    def pair_block_diag(x):
        zero = jnp.zeros((CHUNK, LANES), BF16)
        return jnp.concatenate([jnp.where(lo, x, zero), jnp.where(lo, zero, x)], axis=0)

    def unit_dot(a, xb):
        return jnp.concatenate([_dot(a[:, h * LANES:(h + 1) * LANES], pair_block_diag(xb[:, h * LANES:(h + 1) * LANES]))
                                for h in range(2)], axis=1)

    def unit_dot2(a, xb, yb):
        outs = []
        for h in range(2):
            cols = slice(h * LANES, (h + 1) * LANES)
            rhs = jnp.concatenate([pair_block_diag(xb[:, cols]), pair_block_diag(yb[:, cols])], axis=1)
            outs.append(_dot(a[:, cols], rhs))
        return (jnp.concatenate([o[:, :LANES] for o in outs], axis=1),
                jnp.concatenate([o[:, LANES:] for o in outs], axis=1))

    def unit_rows(m):
        t = m.T
        return jnp.concatenate([t[u:u + 1, :] for u in range(N_UNITS)], axis=1)

    dup = lambda x: jnp.concatenate([x, x], axis=1)

    def chunk_prep(cs, between=(), first=()):
        rows = [pl.ds(pl.multiple_of(c * CHUNK, CHUNK), CHUNK) for c in cs]
        each = lambda f, *lists: [f(*xs) for xs in zip(*lists)]
        pending = list(between)
        early = list(first)

        def run_early():
            if early:
                early.pop(0)()

        run_early()

        gx = [gbs[r, :] for r in rows]
        pc = [pfs[r, :] for r in rows]
        tot = [p[CHUNK - 1:CHUNK, :] for p in pc]
        gcm = each(lambda g, p, t: jnp.where(lane_c < 2, p, t - p + g), gx, pc, tot)
        rsm = each(lambda g, p, t: jnp.where(lane_c < 2, t - p, p - g), gx, pc, tot)
        spread = each(lambda m, g: spread_columns(jnp.where(lane_c < N_UNITS, m, g)), gcm, gx)
        gc4 = [x[:, :UW] for x in spread]
        b4 = [x[:, UW:] for x in spread]
        e_gc = each(jnp.exp, gc4)
        dec = each(lambda c, m: jnp.exp(jnp.where(mc[M_INCL] > 0.0, c - unit_rows(m), NEG)), gc4, gcm)
        run_early()
        kt = [ks[r, :].T for r in rows]
        kt_pair = each(lambda t: jnp.concatenate([t[0:CHUNK], t[CHUNK:]], axis=1), kt)
        kt_bd = each(lambda t: pair_block_diag(t.astype(BF16)), kt_pair)
        gram = each(lambda r, b: dup(_dot(ks[r, :].astype(BF16), b)), rows, kt_bd)
        qk = each(lambda r, b: dup(_dot(qs[r, :].astype(BF16), b)), rows, kt_bd)
        run_early()
        lmat = each(lambda g, b, d: g * b * d * mc[M_STRICT], gram, b4, dec)
        amat = each(lambda m, d: (m * d).astype(BF16), qk, dec)
        t = each(lambda l: mc[M_EYE] - l * mc[0], lmat)
        while early:
            run_early()
        for k in range(1, N_LEVELS):
            lb = each(lambda l: (l * mc[k]).astype(BF16), lmat)
            tb = each(lambda x: x.astype(BF16), t)
            x = each(lambda a, b: unit_dot(a, b).astype(BF16), tb, lb)
            t = each(lambda t0, a, b: t0 - unit_dot(a, b), t, x, tb)
            if pending:
                pending.pop(0)()
        tm = each(lambda x: (x - mc[M_EYE]).astype(BF16), t)
        k4 = [dup(ks[r, :]) for r in rows]
        ru = each(lambda r, b: dup(vs[r, :]) * b, rows, b4)
        rw = each(lambda k, b, e: k * b * e, k4, b4, e_gc)
        tuw = each(lambda tmi, u, w: unit_dot2(tmi, u.astype(BF16), w.astype(BF16)), tm, ru, rw)
        ub = each(lambda x, d: (x + d[0]).astype(BF16), ru, tuw)
        wb = each(lambda x, d: (x + d[1]).astype(BF16), rw, tuw)
        kdt = each(lambda t, m: (dup(t) * jnp.exp(unit_rows(m))).astype(BF16), kt_pair, rsm)
        both = each(lambda k, a, w, u: unit_dot2(jnp.concatenate([k, a], axis=0), w, u), kdt, amat, wb, ub)
        mmat, rmat = [x[0][:CHUNK] for x in both], [x[1][:CHUNK] for x in both]
        aw, au = [x[0][CHUNK:] for x in both], [x[1][CHUNK:] for x in both]
        for i, (c, r) in enumerate(zip(cs, rows)):
            m_s[r, :] = mmat[i].astype(BF16)
            r_s[r, :] = rmat[i]
            qa_s[r, :] = (dup(qs[r, :]) * e_gc[i] - aw[i]).astype(BF16)
            au_s[r, :] = au[i]
            el = jnp.concatenate([e_gc[i][CHUNK - 1:CHUNK, :LANES], e_gc[i][0:1, LANES:]], axis=1)
            el_s[c] = jnp.broadcast_to(el, (8, UW))

    def scan_step(i):
        cf = [q * nc + i for q in range(nseq)]
        cb = [q * nc + nc - 1 - i for q in range(nseq)]
        rf = [pl.ds(pl.multiple_of(c * CHUNK, CHUNK), CHUNK) for c in cf]
        rb = [pl.ds(pl.multiple_of(c * CHUNK, CHUNK), CHUNK) for c in cb]
        mixed = lambda ref, q: jnp.concatenate([ref[rf[q], :LANES], ref[rb[q], LANES:]], axis=1)
        s = [st[q] for q in range(nseq)]
        sb = [x.astype(BF16) for x in s]
        for q in range(nseq):
            sh_s[rf[q], :LANES] = sb[q][:, :LANES]
            sh_s[rb[q], LANES:] = sb[q][:, LANES:]
        ms = [unit_dot(mixed(m_s, q), sb[q]) for q in range(nseq)]
        for q in range(nseq):
            el = jnp.concatenate([el_s[cf[q]][0:1, :LANES], el_s[cb[q]][0:1, LANES:]], axis=1)
            st[q] = s[q] * el - ms[q] + mixed(r_s, q)

    def finish_chunks(r0, count):
        parts = []
        for c in range(count):
            rows = pl.ds(pl.multiple_of(r0 + c * CHUNK, CHUNK), CHUNK)
            parts.append(unit_dot(qa_s[rows, :], sh_s[rows, :]) + au_s[rows, :])
        o = jnp.concatenate([p[:, :LANES] + p[:, LANES:] for p in parts], axis=0)
        rows = pl.ds(pl.multiple_of(r0, count * CHUNK), count * CHUNK)
        o_ref[rows, :] = _head_rms(o, ng_ref[...]) * _silu(z_ref[rows, :])

    def finish_rows(b, carry):
        finish_chunks(b * DN_OUT_ROWS, DN_OUT_ROWS // CHUNK)
        return carry

    def scan_only(i, carry):
        scan_step(i)
        return carry

    half = pblk // 2
    if nseq == 1 and nc % pblk == 0 and nc // pblk >= 3 and half <= N_LEVELS - 1 and (half * CHUNK) % DN_ROWS == 0:
        ends = lambda t: [t * half + s for s in range(half)] + [nc - 1 - (t * half + s) for s in range(half)]
        steps = lambda t: [functools.partial(scan_step, t * half + s) for s in range(half)]
        side = half * CHUNK // DN_ROWS
        last_block = n // DN_ROWS - 1
        row_blocks = lambda t: [t * side + s for s in range(side)] + [last_block - (t * side + s) for s in range(side)]
        input_pass = lambda t: [functools.partial(prep_rows, b) for b in row_blocks(t)]
        trips = nc // pblk
        for thunk in input_pass(0):
            thunk()
        chunk_prep(ends(0), first=input_pass(1))

        def prep_and_scan(t, carry):
            chunk_prep(ends(t), steps(t - 1), input_pass(t + 1))
            return carry

        lax.fori_loop(1, trips - 1, prep_and_scan, 0)
        chunk_prep(ends(trips - 1), steps(trips - 2))
        mid = nc // 2
        lax.fori_loop((trips - 1) * half, mid + 2, scan_only, 0)

        def finish_pair(j):
            finish_chunks((mid - 1 - j) * CHUNK, 1)
            finish_chunks((mid + j) * CHUNK, 1)

        def finish_and_scan(i, carry):
            finish_pair(2 * i)
            finish_pair(2 * i + 1)
            scan_step(mid + 2 + 2 * i)
            scan_step(mid + 3 + 2 * i)
            return carry

        lax.fori_loop(0, (mid - 2) // 2, finish_and_scan, 0)
        finish_pair(mid - 2)
        finish_pair(mid - 1)
    else:
        def input_block(b, carry):
            prep_rows(b)
            return carry

        def prep_block(blk, carry):
            chunk_prep([blk * pblk + s for s in range(pblk)])
            return carry

        lax.fori_loop(0, nseq * n // DN_ROWS, input_block, 0)
        lax.fori_loop(0, nseq * nc // pblk, prep_block, 0)
        lax.fori_loop(0, nc, scan_only, 0)
        lax.fori_loop(0, nseq * n // DN_OUT_ROWS, finish_rows, 0)

    for q in range(nseq):
        s = st[q]
        for u in range(N_UNITS):
            sout_ref[q, u // 2, u % 2] = s[:, u * CHUNK:(u + 1) * CHUNK]


def _delta_net(dq, dz, dab, dn_conv_w, pa, pdt, dn_norm_g, s0, layer, s0_layer, n, nseq):
    t = dq.shape[0]
    bsz = t // n
    rows = nseq * n
    nc = n // CHUNK
    conv_spec = lambda off: pl.BlockSpec((None, 3, LANES), lambda b, p: (layer, 0, off + p))
    col_spec = lambda off, bufs=2: pl.BlockSpec((rows, LANES), lambda b, p: (b, off + p),
                                                pipeline_mode=pl.Buffered(bufs))
    state_block = (nseq, 2, 2, DN_HEAD_DIM, DN_HEAD_DIM)
    return pl.pallas_call(
        functools.partial(_dn_kernel, n=n),
        out_shape=[jax.ShapeDtypeStruct((t, DN_DIM), F32),
                   jax.ShapeDtypeStruct((bsz, 2, N_DN_HEADS, DN_HEAD_DIM, DN_HEAD_DIM), F32)],
        grid=(bsz // nseq, N_PAIRS),
        in_specs=[col_spec(0), col_spec(N_PAIRS), col_spec(2 * N_PAIRS), col_spec(0, 1), col_spec(0, 1),
                  conv_spec(0), conv_spec(N_PAIRS), conv_spec(2 * N_PAIRS),
                  pl.BlockSpec((None, None, 1, LANES), lambda b, p: (layer, p, 0, 0)),
                  pl.BlockSpec((None, None, 1, LANES), lambda b, p: (layer, p, 0, 0)),
                  pl.BlockSpec((None, 1, LANES), lambda b, p: (layer, 0, 0)),
                  pl.BlockSpec((nseq, None) + state_block[1:], lambda b, p: (b, s0_layer, 0, p, 0, 0))],
        out_specs=[pl.BlockSpec((rows, LANES), lambda b, p: (b, p)),
                   pl.BlockSpec(state_block, lambda b, p: (b, 0, p, 0, 0))],
        scratch_shapes=[pltpu.VMEM((rows, LANES), F32) for _ in range(5)]
                       + [pltpu.VMEM((rows, UW), F32) for _ in range(2)]
                       + [pltpu.VMEM((rows, UW), BF16) for _ in range(3)]
                       + [pltpu.VMEM((nseq * nc, 8, UW), F32),
                          pltpu.VMEM((nseq, CHUNK, UW), F32),
                          pltpu.VMEM((N_LEVELS + 3, CHUNK, UW), F32),
                          pltpu.VMEM((LANES, 2 * UW), BF16),
                          pltpu.VMEM((DN_ROWS, DN_ROWS), BF16)],
        compiler_params=pltpu.CompilerParams(vmem_limit_bytes=(V7X_VMEM_BYTES * 7) // 8),
        name="delta_net",
    )(dq, dq, dq, dz, dab, dn_conv_w, dn_conv_w, dn_conv_w, pa, pdt, dn_norm_g, s0)


def _out_kernel(x_ref, sc_ref, scp_ref, scn_ref, att_ref, dn_ref, mod_ref, cw_ref, wo_ref, n2_ref,
                wg_ref, wu_ref, wd_ref, o_ref, *, tm, tiles_per_seq):
    t = pl.program_id(0) % tiles_per_seq
    sc = sc_ref[...]
    prod = sc[:, SC_DIM:2 * SC_DIM] * sc[:, 2 * SC_DIM:]
    pv = scp_ref[7:8, :]
    nx = scn_ref[0:1, :]
    prev_row = jnp.where(t > 0, pv[:, SC_DIM:2 * SC_DIM] * pv[:, 2 * SC_DIM:], 0.0)
    next_row = jnp.where(t < tiles_per_seq - 1, nx[:, SC_DIM:2 * SC_DIM] * nx[:, 2 * SC_DIM:], 0.0)
    row = lax.broadcasted_iota(jnp.int32, prod.shape, 0)
    p_prev = jnp.where(row == 0, prev_row, pltpu.roll(prod, 1, axis=0))
    p_next = jnp.where(row == tm - 1, next_row, pltpu.roll(prod, tm - 1, axis=0))
    y_sc = sc[:, :SC_DIM] * (p_prev * cw_ref[0:1, :] + prod * cw_ref[1:2, :] + p_next * cw_ref[2:3, :])
    y = (_dot(y_sc.astype(BF16), wo_ref[0:SC_DIM, :])
         + _dot(att_ref[...].astype(BF16), wo_ref[SC_DIM:SC_DIM + ATT_DIM, :])
         + _dot(dn_ref[...].astype(BF16), wo_ref[SC_DIM + ATT_DIM:, :]))
    x1 = x_ref[...] + mod_ref[2:3, :] * y
    ms = jnp.mean(x1 * x1, axis=-1, keepdims=True)
    h2 = (x1 * lax.rsqrt(ms + EPS) * n2_ref[...]) * (1.0 + mod_ref[4:5, :]) + mod_ref[3:4, :]
    hb = h2.astype(BF16)
    act = _silu(_dot(hb, wg_ref[...])) * _dot(hb, wu_ref[...])
    o_ref[...] = x1 + mod_ref[5:6, :] * _dot(act.astype(BF16), wd_ref[...])


def _output_stage(x, sc, att, dn, mod, sc_conv_w, w_out_b, norm2_g, w_gate_b, w_up_b, w_down_b, layer,
                  tm, seq_len, per_batch_mod):
    t = x.shape[0]
    tiles_per_seq = seq_len // tm
    tiles_per_batch = tiles_per_seq if per_batch_mod else None
    halo = tm // 8
    last = t // 8 - 1
    resident = lambda shape: pl.BlockSpec((None,) + shape, lambda i: (layer, 0, 0), pipeline_mode=pl.Buffered(1))
    return pl.pallas_call(
        functools.partial(_out_kernel, tm=tm, tiles_per_seq=tiles_per_seq),
        out_shape=jax.ShapeDtypeStruct((t, D_MODEL), F32),
        grid=(t // tm,),
        in_specs=[pl.BlockSpec((tm, D_MODEL), lambda i: (i, 0)),
                  pl.BlockSpec((tm, 3 * SC_DIM), lambda i: (i, 0)),
                  pl.BlockSpec((8, 3 * SC_DIM), lambda i: (jnp.maximum(i * halo - 1, 0), 0)),
                  pl.BlockSpec((8, 3 * SC_DIM), lambda i: (jnp.minimum((i + 1) * halo, last), 0)),
                  pl.BlockSpec((tm, ATT_DIM), lambda i: (i, 0)),
                  pl.BlockSpec((tm, DN_DIM), lambda i: (i, 0)),
                  pl.BlockSpec((None, None, 6, D_MODEL), _mod_row_map(layer, tiles_per_batch)),
                  pl.BlockSpec((None, 3, SC_DIM), lambda i: (layer, 0, 0)),
                  resident((MIX_DIM, D_MODEL)),
                  pl.BlockSpec((None, 1, D_MODEL), lambda i: (layer, 0, 0)),
                  resident((D_MODEL, D_FF)), resident((D_MODEL, D_FF)), resident((D_FF, D_MODEL))],
        out_specs=pl.BlockSpec((tm, D_MODEL), lambda i: (i, 0)),
        compiler_params=pltpu.CompilerParams(vmem_limit_bytes=VMEM_LIMIT),
        name="out_stage",
    )(x, sc, sc, sc, att, dn, mod, sc_conv_w, w_out_b, norm2_g, w_gate_b, w_up_b, w_down_b)


def _rope_tables():
    pos = jnp.arange(DEC_SEQ)
    half = HEAD_DIM // 4
    inv = 1.0 / (ROPE_BASE ** (jnp.arange(half, dtype=F32) / half))
    ang_r = (pos // GRID_W).astype(F32)[:, None] * inv
    ang_c = (pos % GRID_W).astype(F32)[:, None] * inv
    cos = jnp.concatenate([jnp.cos(ang_r)] * 2 + [jnp.cos(ang_c)] * 2, axis=-1)
    sin = jnp.concatenate([-jnp.sin(ang_r), jnp.sin(ang_r), -jnp.sin(ang_c), jnp.sin(ang_c)], axis=-1)
    return jnp.tile(cos, (1, 2)), jnp.tile(sin, (1, 2))


def _pack_w_in(w_in):
    a0 = OFF_AB
    b0 = OFF_AB + 2 * N_DN_HEADS
    blocks = []
    for p in range(N_PAIRS):
        idx = [a0 + d * N_DN_HEADS + 2 * p + j for d in range(2) for j in range(2)]
        idx += [b0 + d * N_DN_HEADS + 2 * p + j for d in range(2) for j in range(2)]
        blocks += [w_in[:, :, c:c + 1] for c in idx]
        blocks.append(jnp.zeros((DEPTH, D_MODEL, LANES - len(idx)), w_in.dtype))
    return w_in.astype(BF16), jnp.concatenate(blocks, axis=2).astype(BF16)


def _pair_lanes(p):
    x = p.reshape(DEPTH, 2, N_PAIRS, 2).transpose(0, 2, 1, 3).reshape(DEPTH, N_PAIRS, 1, 4)
    return jnp.pad(x, ((0, 0), (0, 0), (0, 0), (0, LANES - 4)))


def kernel(x_prompt, x_sample, cache_k, cache_v, state_delta, c, c_ctx, w_in, w_out, ada_w, ada_b,
           norm1_g, norm2_g, sc_conv_w, dn_conv_w, q_norm_g, k_norm_g, attn_sink, dn_A_log,
           dn_dt_bias, dn_norm_g, w_gate, w_up, w_down):
    w_main_b, w_ab_b = _pack_w_in(w_in)
    w_out_b, w_gate_b, w_up_b, w_down_b = (w.astype(BF16) for w in (w_out, w_gate, w_up, w_down))
    n1 = norm1_g.reshape(DEPTH, 1, D_MODEL)
    n2 = norm2_g.reshape(DEPTH, 1, D_MODEL)
    qg = jnp.tile(q_norm_g, (1, 2)).reshape(DEPTH, 1, LANES)
    kg = jnp.tile(k_norm_g, (1, 2)).reshape(DEPTH, 1, LANES)
    ng = jnp.tile(dn_norm_g, (1, 2)).reshape(DEPTH, 1, LANES)
    pa, pdt = _pair_lanes(dn_A_log), _pair_lanes(dn_dt_bias)
    cos_t, sin_t = _rope_tables()
    ck = cache_k.reshape(DEC_BATCH, DEPTH, PAST_LEN, KV_DIM)
    cv = cache_v.reshape(DEC_BATCH, DEPTH, PAST_LEN, KV_DIM)
    zero_state = jnp.zeros((BATCH, 1, 2, N_DN_HEADS, DN_HEAD_DIM, DN_HEAD_DIM), F32)

    cvecs = jnp.concatenate([c_ctx[None, :], c, jnp.zeros((MOD_ROWS - 1 - DEC_BATCH, D_MODEL), F32)], axis=0)
    mod = _modulation(cvecs, ada_w, ada_b).reshape(DEPTH, MOD_ROWS, 6, D_MODEL)

    xp = x_prompt.reshape(BATCH * SEQ, D_MODEL)
    xs = x_sample.reshape(DEC_BATCH * DEC_SEQ, D_MODEL)
    new_k, new_v, new_s = [], [], []
    for l in range(DEPTH):
        out_args = (mod, sc_conv_w, w_out_b, n2, w_gate_b, w_up_b, w_down_b, l)
        sc, q, kv, dq, dz, dab = _in_projection(xp, mod, n1, w_main_b, w_ab_b, qg, cos_t, sin_t, l, None)
        att, kn = _attention_ctx(q, kv, kg, attn_sink, l)
        dn, s_out = _delta_net(dq, dz, dab, dn_conv_w, pa, pdt, ng, zero_state, l, 0, SEQ, DN_CTX_SEQS)
        xp = _output_stage(xp, sc, att, dn, *out_args, TM, SEQ, False)
        new_k.append(kn.reshape(BATCH, SEQ, N_KV_HEADS, HEAD_DIM))
        new_v.append(kv[:, KV_DIM:].reshape(BATCH, SEQ, N_KV_HEADS, HEAD_DIM))
        new_s.append(s_out)
        sc, q, kv, dq, dz, dab = _in_projection(xs, mod, n1, w_main_b, w_ab_b, qg, cos_t, sin_t, l, DEC_SEQ // TM_IN)
        att = _attention_lat(q, kv, ck, cv, kg, attn_sink, cos_t, sin_t, l)
        dn, _ = _delta_net(dq, dz, dab, dn_conv_w, pa, pdt, ng, state_delta, l, l, DEC_SEQ, 1)
        xs = _output_stage(xs, sc, att, dn, *out_args, TM_LAT, DEC_SEQ, True)
    return (xp.reshape(BATCH, SEQ, D_MODEL), xs.reshape(DEC_BATCH, DEC_SEQ, D_MODEL),
            jnp.stack(new_k, axis=1), jnp.stack(new_v, axis=1), jnp.stack(new_s, axis=1))
```

```python
import functools

import jax
import jax.numpy as jnp
from jax import lax
from jax.experimental import pallas as pl
from jax.experimental.pallas import tpu as pltpu

F32 = jnp.float32
BF16 = jnp.bfloat16

D_MODEL = 1024
BATCH = 16
SEQ = 256
DEPTH = 2
DEC_BATCH = 4
DEC_SEQ = 4096
PAST_LEN = 512
GRID_W = 64
SC_DIM = 256
N_Q_HEADS = 8
N_KV_HEADS = 2
N_GROUP = N_Q_HEADS // N_KV_HEADS
HEAD_DIM = 64
ATT_DIM = N_Q_HEADS * HEAD_DIM
KV_DIM = N_KV_HEADS * HEAD_DIM
WINDOW = 128
N_DN_HEADS = 4
DN_HEAD_DIM = 64
DN_DIM = N_DN_HEADS * DN_HEAD_DIM
CHUNK = 64
MIX_DIM = SC_DIM + ATT_DIM + DN_DIM
D_FF = -(-8 * D_MODEL // (3 * 256)) * 256
ROPE_BASE = 10000.0
EPS = 1e-6
NEG = -1e30
LOG2E = 1.4426950408889634

LANES = 128
V7X_VMEM_BYTES = 64 * 1024 * 1024
VMEM_LIMIT = (V7X_VMEM_BYTES * 3) // 4

OFF_SC, OFF_Q, OFF_KV, OFF_DQ, OFF_DZ, OFF_AB = 0, 768, 1280, 1536, 2304, 2560
IN_COLS = OFF_AB + 2 * LANES
N_PAIRS = N_DN_HEADS // 2
MOD_ROWS = 8
TM = 256
TM_LAT = 512
TM_IN = 512


def _sigmoid(x):
    return 0.5 * jnp.tanh(0.5 * x) + 0.5


def _silu(x):
    return x * _sigmoid(x)


def _softplus(x):
    return jnp.maximum(x, 0.0) + jnp.log1p(jnp.exp(-jnp.abs(x)))


def _dot(a, b):
    return jnp.dot(a, b, preferred_element_type=F32)


def _dot_nt(a, b):
    return lax.dot_general(a, b, (((1,), (1,)), ((), ())), preferred_element_type=F32)


def _lo_mask(shape):
    return lax.broadcasted_iota(jnp.int32, shape, 1) % LANES < HEAD_DIM


def _half_sums(xx, lo):
    s_lo = jnp.sum(jnp.where(lo, xx, 0.0), axis=-1, keepdims=True)
    s_hi = jnp.sum(jnp.where(lo, 0.0, xx), axis=-1, keepdims=True)
    return jnp.where(lo, s_lo, s_hi)


def _head_rms(x, g_row):
    lo = _lo_mask(x.shape)
    ms = _half_sums(x * x, lo) * (1.0 / HEAD_DIM)
    return x * lax.rsqrt(ms + EPS) * g_row


def _head_l2n(x):
    lo = _lo_mask(x.shape)
    return x * lax.rsqrt(_half_sums(x * x, lo) + EPS)


def _rope(x, cos, sin):
    n = x.shape[1]
    lane = lax.broadcasted_iota(jnp.int32, x.shape, 1)
    swapped = jnp.where((lane & 16) == 0, pltpu.roll(x, n - 16, axis=1), pltpu.roll(x, 16, axis=1))
    return x * cos + swapped * sin


def _dup_half(x, half):
    lo = _lo_mask(x.shape)
    r = pltpu.roll(x, HEAD_DIM, axis=1)
    return jnp.where(lo, x, r) if half == 0 else jnp.where(lo, r, x)


def _mod_kernel(c_ref, w_ref, b_ref, o_ref):
    s = _silu(c_ref[...])
    o_ref[...] = _dot(s.astype(BF16), w_ref[...].astype(BF16)) + b_ref[...]


def _modulation(cvecs, ada_w, ada_b):
    nblk = 6
    return pl.pallas_call(
        _mod_kernel,
        out_shape=jax.ShapeDtypeStruct((DEPTH, MOD_ROWS, 6 * D_MODEL), F32),
        grid=(DEPTH, nblk),
        in_specs=[pl.BlockSpec((MOD_ROWS, D_MODEL), lambda l, j: (0, 0)),
                  pl.BlockSpec((None, D_MODEL, D_MODEL), lambda l, j: (l, 0, j)),
                  pl.BlockSpec((None, 1, D_MODEL), lambda l, j: (l, 0, j))],
        out_specs=pl.BlockSpec((None, MOD_ROWS, D_MODEL), lambda l, j: (l, 0, j)),
        compiler_params=pltpu.CompilerParams(vmem_limit_bytes=VMEM_LIMIT),
        name="adaln_mod",
    )(cvecs, ada_w, ada_b.reshape(DEPTH, 1, 6 * D_MODEL))


def _mod_row_map(layer, tiles_per_batch):
    if tiles_per_batch is None:
        return lambda i: (layer, 0, 0, 0)
    return lambda i: (layer, 1 + i // tiles_per_batch, 0, 0)


def _prep_q(q, qg, cos, sin):
    out = []
    for g in range(ATT_DIM // LANES):
        x = _head_rms(q[:, g * LANES:(g + 1) * LANES], qg)
        if cos is not None:
            x = _rope(x, cos, sin)
        out.append((x * (LOG2E * HEAD_DIM ** -0.5)).astype(BF16))
    return jnp.concatenate(out, axis=1)


def _in_kernel(x_ref, mod_ref, g_ref, w_ref, wab_ref, qg_ref, cos_ref, sin_ref,
               sc_ref, q_ref, kv_ref, dq_ref, dz_ref, ab_ref, *, rotary):
    x = x_ref[...]
    ms = jnp.mean(x * x, axis=-1, keepdims=True)
    h = (x * lax.rsqrt(ms + EPS) * g_ref[...]) * (1.0 + mod_ref[1:2, :]) + mod_ref[0:1, :]
    hb = h.astype(BF16)
    q = _dot(hb, w_ref[:, OFF_Q:OFF_KV])
    q_ref[...] = _prep_q(q, qg_ref[...], cos_ref[...] if rotary else None, sin_ref[...] if rotary else None)
    for ref, a, b in ((sc_ref, OFF_SC, OFF_Q), (kv_ref, OFF_KV, OFF_DQ), (dq_ref, OFF_DQ, OFF_DZ),
                      (dz_ref, OFF_DZ, OFF_AB)):
        ref[...] = _dot(hb, w_ref[:, a:b])
    ab_ref[...] = _dot(hb, wab_ref[...])


def _in_projection(x, mod, norm1_g, w_main_b, w_ab_b, q_norm_g, cos_t, sin_t, layer, tiles_per_batch):
    t = x.shape[0]
    widths = (OFF_Q - OFF_SC, OFF_KV - OFF_Q, OFF_DQ - OFF_KV, OFF_DZ - OFF_DQ, OFF_AB - OFF_DZ, IN_COLS - OFF_AB)
    rotary = tiles_per_batch is not None
    pos_map = (lambda i: (i % tiles_per_batch, 0)) if rotary else (lambda i: (0, 0))
    return pl.pallas_call(
        functools.partial(_in_kernel, rotary=rotary),
        out_shape=[jax.ShapeDtypeStruct((t, w), BF16 if k == 1 else F32) for k, w in enumerate(widths)],
        grid=(t // TM_IN,),
        in_specs=[pl.BlockSpec((TM_IN, D_MODEL), lambda i: (i, 0)),
                  pl.BlockSpec((None, None, 6, D_MODEL), _mod_row_map(layer, tiles_per_batch)),
                  pl.BlockSpec((None, 1, D_MODEL), lambda i: (layer, 0, 0)),
                  pl.BlockSpec((None, D_MODEL, OFF_AB), lambda i: (layer, 0, 0),
                               pipeline_mode=pl.Buffered(1)),
                  pl.BlockSpec((None, D_MODEL, IN_COLS - OFF_AB), lambda i: (layer, 0, 0),
                               pipeline_mode=pl.Buffered(1)),
                  pl.BlockSpec((None, 1, LANES), lambda i: (layer, 0, 0)),
                  pl.BlockSpec((TM_IN, LANES), pos_map),
                  pl.BlockSpec((TM_IN, LANES), pos_map)],
        out_specs=[pl.BlockSpec((TM_IN, w), lambda i: (i, 0)) for w in widths],
        compiler_params=pltpu.CompilerParams(vmem_limit_bytes=VMEM_LIMIT),
        name="in_proj",
    )(x, mod, norm1_g, w_main_b, w_ab_b, q_norm_g, cos_t, sin_t)


def _head_logits(qa, half, key_sets):
    lo = _lo_mask(qa.shape)
    qh = jnp.where(lo if half == 0 else jnp.logical_not(lo), qa, jnp.zeros_like(qa))
    logits = []
    for k, _, mask in key_sets:
        s = _dot_nt(qh, k())
        logits.append(s if mask is None else jnp.where(mask, s, NEG))
    return logits


def _head_softmax_pv(logits, key_sets, sink):
    m = jnp.maximum(jnp.max(jnp.concatenate(logits, axis=1), axis=-1, keepdims=True), sink)
    acc = None
    for s, (_, v1, _) in zip(logits, key_sets):
        o = _dot(jnp.exp2(s - m).astype(BF16), v1())
        acc = o if acc is None else acc + o
    return acc[:, :LANES] / (acc[:, LANES:] + jnp.exp2(sink - m))


def _attend_heads(q_ref, blocks, sink_ref, layer, o_ref):
    n_pairs = ATT_DIM // LANES
    items = [(blk, pair, half) for blk in range(len(blocks)) for pair in range(n_pairs) for half in range(2)]
    kvh_of = lambda pair: (2 * pair) // N_GROUP
    keys = {}

    def key_sets(blk, pair):
        if (blk, kvh_of(pair)) not in keys:
            keys[blk, kvh_of(pair)] = blocks[blk][1](kvh_of(pair))
        return keys[blk, kvh_of(pair)]

    def logits_of(blk, pair, half):
        return _head_logits(q_ref[blocks[blk][0], pair * LANES:(pair + 1) * LANES], half, key_sets(blk, pair))

    logits = logits_of(*items[0])
    done = []
    for idx, (blk, pair, half) in enumerate(items):
        nxt = logits_of(*items[idx + 1]) if idx + 1 < len(items) else None
        done.append(_head_softmax_pv(logits, key_sets(blk, pair), sink_ref[layer, 2 * pair + half] * LOG2E))
        logits = nxt
        if half == 1:
            lo = _lo_mask(done[-1].shape)
            o_ref[blocks[blk][0], pair * LANES:(pair + 1) * LANES] = jnp.where(lo, done[-2], done[-1])


def _values_with_ones(v2, kvh):
    return jnp.concatenate([_dup_half(v2, kvh), jnp.ones(v2.shape, F32)], axis=1).astype(BF16)


def _attn_ctx_kernel(sink_ref, q_ref, kv_ref, kg_ref, o_ref, kn_ref, *, layer):
    kv = kv_ref[...]
    kn = _head_rms(kv[:, :LANES], kg_ref[...])
    kn_ref[...] = kn
    v2 = kv[:, LANES:]
    kd = [_dup_half(kn, kvh).astype(BF16) for kvh in range(N_KV_HEADS)]
    vd = [_values_with_ones(v2, kvh) for kvh in range(N_KV_HEADS)]
    key_sets_of = lambda kvh: [(lambda: kd[kvh], lambda: vd[kvh], None)]
    _attend_heads(q_ref, [(slice(0, SEQ), key_sets_of)], sink_ref, layer, o_ref)


def _attention_ctx(q, kv, k_norm_g, attn_sink, layer):
    t = q.shape[0]
    return pl.pallas_call(
        functools.partial(_attn_ctx_kernel, layer=layer),
        out_shape=[jax.ShapeDtypeStruct((t, ATT_DIM), F32), jax.ShapeDtypeStruct((t, KV_DIM), F32)],
        grid=(t // SEQ,),
        in_specs=[pl.BlockSpec(memory_space=pltpu.SMEM),
                  pl.BlockSpec((SEQ, ATT_DIM), lambda b: (b, 0)),
                  pl.BlockSpec((SEQ, 2 * KV_DIM), lambda b: (b, 0)),
                  pl.BlockSpec((None, 1, LANES), lambda b: (layer, 0, 0))],
        out_specs=[pl.BlockSpec((SEQ, ATT_DIM), lambda b: (b, 0)),
                   pl.BlockSpec((SEQ, KV_DIM), lambda b: (b, 0))],
        compiler_params=pltpu.CompilerParams(vmem_limit_bytes=VMEM_LIMIT),
        name="attn_ctx",
    )(attn_sink, q, kv, k_norm_g)


QBLK = 128
QSTEP = 1024
WIN_KEYS = QBLK + 2 * WINDOW
PREP_ROWS = 512


def _attn_lat_kernel(sink_ref, q_ref, kv_ref, kc_ref, vc_ref, kg_ref, cos_ref, sin_ref, o_ref,
                     kd_s, vd_s, kcd_s, vcd_s, *, layer):
    i = pl.program_id(1)
    n = DEC_SEQ

    @pl.when(i == 0)
    def _():
        for kvh in range(N_KV_HEADS):
            for s in (kd_s, vd_s):
                zpad = jnp.zeros((WINDOW, s.shape[-1]), BF16)
                s[kvh, 0:WINDOW, :] = zpad
                s[kvh, WINDOW + n:2 * WINDOW + n, :] = zpad
            kcd_s[kvh] = _dup_half(kc_ref[...], kvh).astype(BF16)
            vcd_s[kvh] = _values_with_ones(vc_ref[...], kvh)

        def prep(r, carry):
            r0 = pl.multiple_of(r * PREP_ROWS, PREP_ROWS)
            kv = kv_ref[pl.ds(r0, PREP_ROWS), :]
            kn = _head_rms(kv[:, :LANES], kg_ref[...])
            kr = _rope(kn, cos_ref[pl.ds(r0, PREP_ROWS), :], sin_ref[pl.ds(r0, PREP_ROWS), :])
            v2 = kv[:, LANES:]
            for kvh in range(N_KV_HEADS):
                kd_s[kvh, pl.ds(WINDOW + r0, PREP_ROWS), :] = _dup_half(kr, kvh).astype(BF16)
                vd_s[kvh, pl.ds(WINDOW + r0, PREP_ROWS), :] = _values_with_ones(v2, kvh)
            return carry

        lax.fori_loop(0, n // PREP_ROWS, prep, 0)

    rr = lax.broadcasted_iota(jnp.int32, (QBLK, WIN_KEYS), 0)
    jj = lax.broadcasted_iota(jnp.int32, (QBLK, WIN_KEYS), 1)
    band = (jj >= rr) & (jj <= rr + 2 * WINDOW)
    blocks = []
    for j in range(QSTEP // QBLK):
        start = pl.multiple_of(i * QSTEP + j * QBLK, QBLK)
        kpos = jj + (start - WINDOW)
        valid = band & (kpos >= 0) & (kpos < n)
        key_sets_of = lambda kvh, start=start, valid=valid: [
            (lambda: kcd_s[kvh], lambda: vcd_s[kvh], None),
            (lambda: kd_s[kvh, pl.ds(start, WIN_KEYS), :], lambda: vd_s[kvh, pl.ds(start, WIN_KEYS), :], valid)]
        blocks.append((slice(j * QBLK, (j + 1) * QBLK), key_sets_of))
    _attend_heads(q_ref, blocks, sink_ref, layer, o_ref)


def _attention_lat(q, kv, cache_k, cache_v, k_norm_g, attn_sink, cos_t, sin_t, layer):
    t = q.shape[0]
    n = DEC_SEQ
    nblk = n // QSTEP
    return pl.pallas_call(
        functools.partial(_attn_lat_kernel, layer=layer),
        out_shape=jax.ShapeDtypeStruct((t, ATT_DIM), F32),
        grid=(DEC_BATCH, nblk),
        in_specs=[pl.BlockSpec(memory_space=pltpu.SMEM),
                  pl.BlockSpec((QSTEP, ATT_DIM), lambda b, i: (b * nblk + i, 0)),
                  pl.BlockSpec((n, 2 * KV_DIM), lambda b, i: (b, 0)),
                  pl.BlockSpec((None, None, PAST_LEN, KV_DIM), lambda b, i: (b, layer, 0, 0)),
                  pl.BlockSpec((None, None, PAST_LEN, KV_DIM), lambda b, i: (b, layer, 0, 0)),
                  pl.BlockSpec((None, 1, LANES), lambda b, i: (layer, 0, 0)),
                  pl.BlockSpec((n, LANES), lambda b, i: (0, 0)),
                  pl.BlockSpec((n, LANES), lambda b, i: (0, 0))],
        out_specs=pl.BlockSpec((QSTEP, ATT_DIM), lambda b, i: (b * nblk + i, 0)),
        scratch_shapes=[pltpu.VMEM((N_KV_HEADS, n + 2 * WINDOW, LANES), BF16),
                        pltpu.VMEM((N_KV_HEADS, n + 2 * WINDOW, 2 * LANES), BF16),
                        pltpu.VMEM((N_KV_HEADS, PAST_LEN, LANES), BF16),
                        pltpu.VMEM((N_KV_HEADS, PAST_LEN, 2 * LANES), BF16)],
        compiler_params=pltpu.CompilerParams(vmem_limit_bytes=VMEM_LIMIT),
        name="attn_lat",
    )(attn_sink, q, kv, cache_k, cache_v, k_norm_g, cos_t, sin_t)


DN_ROWS = 128
DN_OUT_ROWS = 256
DN_CTX_SEQS = 8


def _dn_conv(x_ref, w_ref, r0, n):
    x = x_ref[pl.ds(r0, DN_ROWS), :]
    total = x_ref.shape[0]
    above = jnp.where(r0 % n > 0, x_ref[pl.ds(jnp.maximum(r0 - 1, 0), 1), :], 0.0)
    below = jnp.where(r0 % n + DN_ROWS < n, x_ref[pl.ds(jnp.minimum(r0 + DN_ROWS, total - 1), 1), :], 0.0)
    row = lax.broadcasted_iota(jnp.int32, x.shape, 0)
    prev = jnp.where(row == 0, above, pltpu.roll(x, 1, axis=0))
    nxt = jnp.where(row == DN_ROWS - 1, below, pltpu.roll(x, DN_ROWS - 1, axis=0))
    return _silu(prev * w_ref[0:1, :] + x * w_ref[1:2, :] + nxt * w_ref[2:3, :])


N_UNITS = 4
UW = N_UNITS * CHUNK
N_LEVELS = 6
M_INCL, M_STRICT, M_EYE = N_LEVELS, N_LEVELS + 1, N_LEVELS + 2
PREP_BLOCK = 8


def _dn_kernel(q_ref, k_ref, v_ref, z_ref, ab_ref, wq_ref, wk_ref, wv_ref, pa_ref, pdt_ref, ng_ref, s0_ref,
               o_ref, sout_ref, qs, ks, vs, gbs, pfs, r_s, au_s, m_s, qa_s, sh_s, el_s, st, mc, ex_s, tri_s, *, n):
    nc = n // CHUNK
    nseq = q_ref.shape[0] // n
    pblk = min(PREP_BLOCK, nseq * nc)

    @pl.when((pl.program_id(0) == 0) & (pl.program_id(1) == 0))
    def _():
        as_f32 = lambda m: jnp.where(m, 1.0, 0.0).astype(F32)
        ii = lax.broadcasted_iota(jnp.int32, (CHUNK, UW), 0)
        ll = lax.broadcasted_iota(jnp.int32, (CHUNK, UW), 1)
        unit_of_lane, jj = ll // CHUNK, ll % CHUNK
        fwd, bwd = unit_of_lane < 2, unit_of_lane >= 2
        for k in range(N_LEVELS):
            rk, ck = ii >> k, jj >> k
            lower = (rk - ck == 1) & ((rk & 1) == 1)
            upper = (ck - rk == 1) & ((ck & 1) == 1)
            mc[k] = as_f32((fwd & lower) | (bwd & upper))
        mc[M_INCL] = as_f32((fwd & (ii >= jj)) | (bwd & (ii <= jj)))
        mc[M_STRICT] = as_f32((fwd & (ii > jj)) | (bwd & (ii < jj)))
        mc[M_EYE] = as_f32(ii == jj)
        ec = lax.broadcasted_iota(jnp.int32, (LANES, 2 * UW), 0)
        el_ = lax.broadcasted_iota(jnp.int32, (LANES, 2 * UW), 1)
        ex_s[...] = as_f32(ec == (el_ // UW) * N_UNITS + (el_ % UW) // CHUNK).astype(BF16)
        ri = lax.broadcasted_iota(jnp.int32, (DN_ROWS, DN_ROWS), 0)
        ti = lax.broadcasted_iota(jnp.int32, (DN_ROWS, DN_ROWS), 1)
        tri_s[...] = as_f32((ti <= ri) & (ri // CHUNK == ti // CHUNK)).astype(BF16)

    def prep_rows(b):
        r0 = pl.multiple_of(b * DN_ROWS, DN_ROWS)
        rows = pl.ds(r0, DN_ROWS)
        qs[rows, :] = _head_l2n(_dn_conv(q_ref, wq_ref, r0, n)) * (DN_HEAD_DIM ** -0.5)
        ks[rows, :] = _head_l2n(_dn_conv(k_ref, wk_ref, r0, n))
        vs[rows, :] = _dn_conv(v_ref, wv_ref, r0, n)
        ab = ab_ref[rows, :]
        lane = lax.broadcasted_iota(jnp.int32, ab.shape, 1)
        gb = jnp.where(lane < N_UNITS, -jnp.exp(pa_ref[...]) * _softplus(ab + pdt_ref[...]), _sigmoid(ab))
        gbs[rows, :] = gb
        hi = gb.astype(BF16)
        r1 = gb - hi.astype(F32)
        mid = r1.astype(BF16)
        low = (r1 - mid.astype(F32)).astype(BF16)
        tri = tri_s[...]
        pfs[rows, :] = _dot(tri, hi) + _dot(tri, mid) + _dot(tri, low)

    for q in range(nseq):
        st[q] = jnp.concatenate([s0_ref[q, u // 2, u % 2] for u in range(N_UNITS)], axis=1)
    lo = _lo_mask((CHUNK, LANES))
    lane_c = lax.broadcasted_iota(jnp.int32, (CHUNK, LANES), 1)

    def spread_columns(m):
        hi = m.astype(BF16)
        r1 = m - hi.astype(F32)
        mid = r1.astype(BF16)
        low = (r1 - mid.astype(F32)).astype(BF16)
        e = ex_s[...]
        return _dot(hi, e) + _dot(mid, e) + _dot(low, e)

    def pair_block_diag(x):
        zero = jnp.zeros((CHUNK, LANES), BF16)
        return jnp.concatenate([jnp.where(lo, x, zero), jnp.where(lo, zero, x)], axis=0)

    def unit_dot(a, xb):
        return jnp.concatenate([_dot(a[:, h * LANES:(h + 1) * LANES], pair_block_diag(xb[:, h * LANES:(h + 1) * LANES]))
                                for h in range(2)], axis=1)

    def unit_dot2(a, xb, yb):
        outs = []
        for h in range(2):
            cols = slice(h * LANES, (h + 1) * LANES)
            rhs = jnp.concatenate([pair_block_diag(xb[:, cols]), pair_block_diag(yb[:, cols])], axis=1)
            outs.append(_dot(a[:, cols], rhs))
        return (jnp.concatenate([o[:, :LANES] for o in outs], axis=1),
                jnp.concatenate([o[:, LANES:] for o in outs], axis=1))

    def unit_rows(m):
        t = m.T
        return jnp.concatenate([t[u:u + 1, :] for u in range(N_UNITS)], axis=1)

    dup = lambda x: jnp.concatenate([x, x], axis=1)

    def chunk_prep(cs, between=(), first=()):
        rows = [pl.ds(pl.multiple_of(c * CHUNK, CHUNK), CHUNK) for c in cs]
        each = lambda f, *lists: [f(*xs) for xs in zip(*lists)]
        pending = list(between)
        early = list(first)

        def run_early():
            if early:
                early.pop(0)()

        run_early()

        gx = [gbs[r, :] for r in rows]
        pc = [pfs[r, :] for r in rows]
        tot = [p[CHUNK - 1:CHUNK, :] for p in pc]
        gcm = each(lambda g, p, t: jnp.where(lane_c < 2, p, t - p + g), gx, pc, tot)
        rsm = each(lambda g, p, t: jnp.where(lane_c < 2, t - p, p - g), gx, pc, tot)
        spread = each(lambda m, g: spread_columns(jnp.where(lane_c < N_UNITS, m, g)), gcm, gx)
        gc4 = [x[:, :UW] for x in spread]
        b4 = [x[:, UW:] for x in spread]
        e_gc = each(jnp.exp, gc4)
        dec = each(lambda c, m: jnp.exp(jnp.where(mc[M_INCL] > 0.0, c - unit_rows(m), NEG)), gc4, gcm)
        run_early()
        kt = [ks[r, :].T for r in rows]
        kt_pair = each(lambda t: jnp.concatenate([t[0:CHUNK], t[CHUNK:]], axis=1), kt)
        kt_bd = each(lambda t: pair_block_diag(t.astype(BF16)), kt_pair)
        gram = each(lambda r, b: dup(_dot(ks[r, :].astype(BF16), b)), rows, kt_bd)
        qk = each(lambda r, b: dup(_dot(qs[r, :].astype(BF16), b)), rows, kt_bd)
        run_early()
        lmat = each(lambda g, b, d: g * b * d * mc[M_STRICT], gram, b4, dec)
        amat = each(lambda m, d: (m * d).astype(BF16), qk, dec)
        t = each(lambda l: mc[M_EYE] - l * mc[0], lmat)
        while early:
            run_early()
        for k in range(1, N_LEVELS):
            lb = each(lambda l: (l * mc[k]).astype(BF16), lmat)
            tb = each(lambda x: x.astype(BF16), t)
            x = each(lambda a, b: unit_dot(a, b).astype(BF16), tb, lb)
            t = each(lambda t0, a, b: t0 - unit_dot(a, b), t, x, tb)
            if pending:
                pending.pop(0)()
        tm = each(lambda x: (x - mc[M_EYE]).astype(BF16), t)
        k4 = [dup(ks[r, :]) for r in rows]
        ru = each(lambda r, b: dup(vs[r, :]) * b, rows, b4)
        rw = each(lambda k, b, e: k * b * e, k4, b4, e_gc)
        tuw = each(lambda tmi, u, w: unit_dot2(tmi, u.astype(BF16), w.astype(BF16)), tm, ru, rw)
        ub = each(lambda x, d: (x + d[0]).astype(BF16), ru, tuw)
        wb = each(lambda x, d: (x + d[1]).astype(BF16), rw, tuw)
        kdt = each(lambda t, m: (dup(t) * jnp.exp(unit_rows(m))).astype(BF16), kt_pair, rsm)
        both = each(lambda k, a, w, u: unit_dot2(jnp.concatenate([k, a], axis=0), w, u), kdt, amat, wb, ub)
        mmat, rmat = [x[0][:CHUNK] for x in both], [x[1][:CHUNK] for x in both]
        aw, au = [x[0][CHUNK:] for x in both], [x[1][CHUNK:] for x in both]
        for i, (c, r) in enumerate(zip(cs, rows)):
            m_s[r, :] = mmat[i].astype(BF16)
            r_s[r, :] = rmat[i]
            qa_s[r, :] = (dup(qs[r, :]) * e_gc[i] - aw[i]).astype(BF16)
            au_s[r, :] = au[i]
            el = jnp.concatenate([e_gc[i][CHUNK - 1:CHUNK, :LANES], e_gc[i][0:1, LANES:]], axis=1)
            el_s[c] = jnp.broadcast_to(el, (8, UW))

    def scan_step(i):
        cf = [q * nc + i for q in range(nseq)]
        cb = [q * nc + nc - 1 - i for q in range(nseq)]
        rf = [pl.ds(pl.multiple_of(c * CHUNK, CHUNK), CHUNK) for c in cf]
        rb = [pl.ds(pl.multiple_of(c * CHUNK, CHUNK), CHUNK) for c in cb]
        mixed = lambda ref, q: jnp.concatenate([ref[rf[q], :LANES], ref[rb[q], LANES:]], axis=1)
        s = [st[q] for q in range(nseq)]
        sb = [x.astype(BF16) for x in s]
        for q in range(nseq):
            sh_s[rf[q], :LANES] = sb[q][:, :LANES]
            sh_s[rb[q], LANES:] = sb[q][:, LANES:]
        ms = [unit_dot(mixed(m_s, q), sb[q]) for q in range(nseq)]
        for q in range(nseq):
            el = jnp.concatenate([el_s[cf[q]][0:1, :LANES], el_s[cb[q]][0:1, LANES:]], axis=1)
            st[q] = s[q] * el - ms[q] + mixed(r_s, q)

    def finish_chunks(r0, count):
        parts = []
        for c in range(count):
            rows = pl.ds(pl.multiple_of(r0 + c * CHUNK, CHUNK), CHUNK)
            parts.append(unit_dot(qa_s[rows, :], sh_s[rows, :]) + au_s[rows, :])
        o = jnp.concatenate([p[:, :LANES] + p[:, LANES:] for p in parts], axis=0)
        rows = pl.ds(pl.multiple_of(r0, count * CHUNK), count * CHUNK)
        o_ref[rows, :] = _head_rms(o, ng_ref[...]) * _silu(z_ref[rows, :])

    def finish_rows(b, carry):
        finish_chunks(b * DN_OUT_ROWS, DN_OUT_ROWS // CHUNK)
        return carry

    def scan_only(i, carry):
        scan_step(i)
        return carry

    half = pblk // 2
    if nseq == 1 and nc % pblk == 0 and nc // pblk >= 3 and half <= N_LEVELS - 1 and (half * CHUNK) % DN_ROWS == 0:
        ends = lambda t: [t * half + s for s in range(half)] + [nc - 1 - (t * half + s) for s in range(half)]
        steps = lambda t: [functools.partial(scan_step, t * half + s) for s in range(half)]
        side = half * CHUNK // DN_ROWS
        last_block = n // DN_ROWS - 1
        row_blocks = lambda t: [t * side + s for s in range(side)] + [last_block - (t * side + s) for s in range(side)]
        input_pass = lambda t: [functools.partial(prep_rows, b) for b in row_blocks(t)]
        trips = nc // pblk
        for thunk in input_pass(0):
            thunk()
        chunk_prep(ends(0), first=input_pass(1))

        def prep_and_scan(t, carry):
            chunk_prep(ends(t), steps(t - 1), input_pass(t + 1))
            return carry

        lax.fori_loop(1, trips - 1, prep_and_scan, 0)
        chunk_prep(ends(trips - 1), steps(trips - 2))
        mid = nc // 2
        lax.fori_loop((trips - 1) * half, mid + 2, scan_only, 0)

        def finish_pair(j):
            finish_chunks((mid - 1 - j) * CHUNK, 1)
            finish_chunks((mid + j) * CHUNK, 1)

        def finish_and_scan(i, carry):
            finish_pair(2 * i)
            finish_pair(2 * i + 1)
            scan_step(mid + 2 + 2 * i)
            scan_step(mid + 3 + 2 * i)
            return carry

        lax.fori_loop(0, (mid - 2) // 2, finish_and_scan, 0)
        finish_pair(mid - 2)
        finish_pair(mid - 1)
    else:
        def input_block(b, carry):
            prep_rows(b)
            return carry

        def prep_block(blk, carry):
            chunk_prep([blk * pblk + s for s in range(pblk)])
            return carry

        lax.fori_loop(0, nseq * n // DN_ROWS, input_block, 0)
        lax.fori_loop(0, nseq * nc // pblk, prep_block, 0)
        lax.fori_loop(0, nc, scan_only, 0)
        lax.fori_loop(0, nseq * n // DN_OUT_ROWS, finish_rows, 0)

    for q in range(nseq):
        s = st[q]
        for u in range(N_UNITS):
            sout_ref[q, u // 2, u % 2] = s[:, u * CHUNK:(u + 1) * CHUNK]


def _delta_net(dq, dz, dab, dn_conv_w, pa, pdt, dn_norm_g, s0, layer, s0_layer, n, nseq):
    t = dq.shape[0]
    bsz = t // n
    rows = nseq * n
    nc = n // CHUNK
    conv_spec = lambda off: pl.BlockSpec((None, 3, LANES), lambda b, p: (layer, 0, off + p))
    col_spec = lambda off, bufs=2: pl.BlockSpec((rows, LANES), lambda b, p: (b, off + p),
                                                pipeline_mode=pl.Buffered(bufs))
    state_block = (nseq, 2, 2, DN_HEAD_DIM, DN_HEAD_DIM)
    return pl.pallas_call(
        functools.partial(_dn_kernel, n=n),
        out_shape=[jax.ShapeDtypeStruct((t, DN_DIM), F32),
                   jax.ShapeDtypeStruct((bsz, 2, N_DN_HEADS, DN_HEAD_DIM, DN_HEAD_DIM), F32)],
        grid=(bsz // nseq, N_PAIRS),
        in_specs=[col_spec(0), col_spec(N_PAIRS), col_spec(2 * N_PAIRS), col_spec(0, 1), col_spec(0, 1),
                  conv_spec(0), conv_spec(N_PAIRS), conv_spec(2 * N_PAIRS),
                  pl.BlockSpec((None, None, 1, LANES), lambda b, p: (layer, p, 0, 0)),
                  pl.BlockSpec((None, None, 1, LANES), lambda b, p: (layer, p, 0, 0)),
                  pl.BlockSpec((None, 1, LANES), lambda b, p: (layer, 0, 0)),
                  pl.BlockSpec((nseq, None) + state_block[1:], lambda b, p: (b, s0_layer, 0, p, 0, 0))],
        out_specs=[pl.BlockSpec((rows, LANES), lambda b, p: (b, p)),
                   pl.BlockSpec(state_block, lambda b, p: (b, 0, p, 0, 0))],
        scratch_shapes=[pltpu.VMEM((rows, LANES), F32) for _ in range(5)]
                       + [pltpu.VMEM((rows, UW), F32) for _ in range(2)]
                       + [pltpu.VMEM((rows, UW), BF16) for _ in range(3)]
                       + [pltpu.VMEM((nseq * nc, 8, UW), F32),
                          pltpu.VMEM((nseq, CHUNK, UW), F32),
                          pltpu.VMEM((N_LEVELS + 3, CHUNK, UW), F32),
                          pltpu.VMEM((LANES, 2 * UW), BF16),
                          pltpu.VMEM((DN_ROWS, DN_ROWS), BF16)],
        compiler_params=pltpu.CompilerParams(vmem_limit_bytes=(V7X_VMEM_BYTES * 7) // 8),
        name="delta_net",
    )(dq, dq, dq, dz, dab, dn_conv_w, dn_conv_w, dn_conv_w, pa, pdt, dn_norm_g, s0)


def _out_kernel(x_ref, sc_ref, scp_ref, scn_ref, att_ref, dn_ref, mod_ref, cw_ref, wo_ref, n2_ref,
                wg_ref, wu_ref, wd_ref, o_ref, *, tm, tiles_per_seq):
    t = pl.program_id(0) % tiles_per_seq
    sc = sc_ref[...]
    prod = sc[:, SC_DIM:2 * SC_DIM] * sc[:, 2 * SC_DIM:]
    pv = scp_ref[7:8, :]
    nx = scn_ref[0:1, :]
    prev_row = jnp.where(t > 0, pv[:, SC_DIM:2 * SC_DIM] * pv[:, 2 * SC_DIM:], 0.0)
    next_row = jnp.where(t < tiles_per_seq - 1, nx[:, SC_DIM:2 * SC_DIM] * nx[:, 2 * SC_DIM:], 0.0)
    row = lax.broadcasted_iota(jnp.int32, prod.shape, 0)
    p_prev = jnp.where(row == 0, prev_row, pltpu.roll(prod, 1, axis=0))
    p_next = jnp.where(row == tm - 1, next_row, pltpu.roll(prod, tm - 1, axis=0))
    y_sc = sc[:, :SC_DIM] * (p_prev * cw_ref[0:1, :] + prod * cw_ref[1:2, :] + p_next * cw_ref[2:3, :])
    y = (_dot(y_sc.astype(BF16), wo_ref[0:SC_DIM, :])
         + _dot(att_ref[...].astype(BF16), wo_ref[SC_DIM:SC_DIM + ATT_DIM, :])
         + _dot(dn_ref[...].astype(BF16), wo_ref[SC_DIM + ATT_DIM:, :]))
    x1 = x_ref[...] + mod_ref[2:3, :] * y
    ms = jnp.mean(x1 * x1, axis=-1, keepdims=True)
    h2 = (x1 * lax.rsqrt(ms + EPS) * n2_ref[...]) * (1.0 + mod_ref[4:5, :]) + mod_ref[3:4, :]
    hb = h2.astype(BF16)
    act = _silu(_dot(hb, wg_ref[...])) * _dot(hb, wu_ref[...])
    o_ref[...] = x1 + mod_ref[5:6, :] * _dot(act.astype(BF16), wd_ref[...])


def _output_stage(x, sc, att, dn, mod, sc_conv_w, w_out_b, norm2_g, w_gate_b, w_up_b, w_down_b, layer,
                  tm, seq_len, per_batch_mod):
    t = x.shape[0]
    tiles_per_seq = seq_len // tm
    tiles_per_batch = tiles_per_seq if per_batch_mod else None
    halo = tm // 8
    last = t // 8 - 1
    resident = lambda shape: pl.BlockSpec((None,) + shape, lambda i: (layer, 0, 0), pipeline_mode=pl.Buffered(1))
    return pl.pallas_call(
        functools.partial(_out_kernel, tm=tm, tiles_per_seq=tiles_per_seq),
        out_shape=jax.ShapeDtypeStruct((t, D_MODEL), F32),
        grid=(t // tm,),
        in_specs=[pl.BlockSpec((tm, D_MODEL), lambda i: (i, 0)),
                  pl.BlockSpec((tm, 3 * SC_DIM), lambda i: (i, 0)),
                  pl.BlockSpec((8, 3 * SC_DIM), lambda i: (jnp.maximum(i * halo - 1, 0), 0)),
                  pl.BlockSpec((8, 3 * SC_DIM), lambda i: (jnp.minimum((i + 1) * halo, last), 0)),
                  pl.BlockSpec((tm, ATT_DIM), lambda i: (i, 0)),
                  pl.BlockSpec((tm, DN_DIM), lambda i: (i, 0)),
                  pl.BlockSpec((None, None, 6, D_MODEL), _mod_row_map(layer, tiles_per_batch)),
                  pl.BlockSpec((None, 3, SC_DIM), lambda i: (layer, 0, 0)),
                  resident((MIX_DIM, D_MODEL)),
                  pl.BlockSpec((None, 1, D_MODEL), lambda i: (layer, 0, 0)),
                  resident((D_MODEL, D_FF)), resident((D_MODEL, D_FF)), resident((D_FF, D_MODEL))],
        out_specs=pl.BlockSpec((tm, D_MODEL), lambda i: (i, 0)),
        compiler_params=pltpu.CompilerParams(vmem_limit_bytes=VMEM_LIMIT),
        name="out_stage",
    )(x, sc, sc, sc, att, dn, mod, sc_conv_w, w_out_b, norm2_g, w_gate_b, w_up_b, w_down_b)


def _rope_tables():
    pos = jnp.arange(DEC_SEQ)
    half = HEAD_DIM // 4
    inv = 1.0 / (ROPE_BASE ** (jnp.arange(half, dtype=F32) / half))
    ang_r = (pos // GRID_W).astype(F32)[:, None] * inv
    ang_c = (pos % GRID_W).astype(F32)[:, None] * inv
    cos = jnp.concatenate([jnp.cos(ang_r)] * 2 + [jnp.cos(ang_c)] * 2, axis=-1)
    sin = jnp.concatenate([-jnp.sin(ang_r), jnp.sin(ang_r), -jnp.sin(ang_c), jnp.sin(ang_c)], axis=-1)
    return jnp.tile(cos, (1, 2)), jnp.tile(sin, (1, 2))


def _pack_w_in(w_in):
    a0 = OFF_AB
    b0 = OFF_AB + 2 * N_DN_HEADS
    blocks = []
    for p in range(N_PAIRS):
        idx = [a0 + d * N_DN_HEADS + 2 * p + j for d in range(2) for j in range(2)]
        idx += [b0 + d * N_DN_HEADS + 2 * p + j for d in range(2) for j in range(2)]
        blocks += [w_in[:, :, c:c + 1] for c in idx]
        blocks.append(jnp.zeros((DEPTH, D_MODEL, LANES - len(idx)), w_in.dtype))
    return w_in.astype(BF16), jnp.concatenate(blocks, axis=2).astype(BF16)


def _pair_lanes(p):
    x = p.reshape(DEPTH, 2, N_PAIRS, 2).transpose(0, 2, 1, 3).reshape(DEPTH, N_PAIRS, 1, 4)
    return jnp.pad(x, ((0, 0), (0, 0), (0, 0), (0, LANES - 4)))


def kernel(x_prompt, x_sample, cache_k, cache_v, state_delta, c, c_ctx, w_in, w_out, ada_w, ada_b,
           norm1_g, norm2_g, sc_conv_w, dn_conv_w, q_norm_g, k_norm_g, attn_sink, dn_A_log,
           dn_dt_bias, dn_norm_g, w_gate, w_up, w_down):
    w_main_b, w_ab_b = _pack_w_in(w_in)
    w_out_b, w_gate_b, w_up_b, w_down_b = (w.astype(BF16) for w in (w_out, w_gate, w_up, w_down))
    n1 = norm1_g.reshape(DEPTH, 1, D_MODEL)
    n2 = norm2_g.reshape(DEPTH, 1, D_MODEL)
    qg = jnp.tile(q_norm_g, (1, 2)).reshape(DEPTH, 1, LANES)
    kg = jnp.tile(k_norm_g, (1, 2)).reshape(DEPTH, 1, LANES)
    ng = jnp.tile(dn_norm_g, (1, 2)).reshape(DEPTH, 1, LANES)
    pa, pdt = _pair_lanes(dn_A_log), _pair_lanes(dn_dt_bias)
    cos_t, sin_t = _rope_tables()
    ck = cache_k.reshape(DEC_BATCH, DEPTH, PAST_LEN, KV_DIM)
    cv = cache_v.reshape(DEC_BATCH, DEPTH, PAST_LEN, KV_DIM)
    zero_state = jnp.zeros((BATCH, 1, 2, N_DN_HEADS, DN_HEAD_DIM, DN_HEAD_DIM), F32)

    cvecs = jnp.concatenate([c_ctx[None, :], c, jnp.zeros((MOD_ROWS - 1 - DEC_BATCH, D_MODEL), F32)], axis=0)
    mod = _modulation(cvecs, ada_w, ada_b).reshape(DEPTH, MOD_ROWS, 6, D_MODEL)

    xp = x_prompt.reshape(BATCH * SEQ, D_MODEL)
    xs = x_sample.reshape(DEC_BATCH * DEC_SEQ, D_MODEL)
    new_k, new_v, new_s = [], [], []
    for l in range(DEPTH):
        out_args = (mod, sc_conv_w, w_out_b, n2, w_gate_b, w_up_b, w_down_b, l)
        sc, q, kv, dq, dz, dab = _in_projection(xp, mod, n1, w_main_b, w_ab_b, qg, cos_t, sin_t, l, None)
        att, kn = _attention_ctx(q, kv, kg, attn_sink, l)
        dn, s_out = _delta_net(dq, dz, dab, dn_conv_w, pa, pdt, ng, zero_state, l, 0, SEQ, DN_CTX_SEQS)
        xp = _output_stage(xp, sc, att, dn, *out_args, TM, SEQ, False)
        new_k.append(kn.reshape(BATCH, SEQ, N_KV_HEADS, HEAD_DIM))
        new_v.append(kv[:, KV_DIM:].reshape(BATCH, SEQ, N_KV_HEADS, HEAD_DIM))
        new_s.append(s_out)
        sc, q, kv, dq, dz, dab = _in_projection(xs, mod, n1, w_main_b, w_ab_b, qg, cos_t, sin_t, l, DEC_SEQ // TM_IN)
        att = _attention_lat(q, kv, ck, cv, kg, attn_sink, cos_t, sin_t, l)
        dn, _ = _delta_net(dq, dz, dab, dn_conv_w, pa, pdt, ng, state_delta, l, l, DEC_SEQ, 1)
        xs = _output_stage(xs, sc, att, dn, *out_args, TM_LAT, DEC_SEQ, True)
    return (xp.reshape(BATCH, SEQ, D_MODEL), xs.reshape(DEC_BATCH, DEC_SEQ, D_MODEL),
            jnp.stack(new_k, axis=1), jnp.stack(new_v, axis=1), jnp.stack(new_s, axis=1))
```

```python
import functools

import jax
import jax.numpy as jnp
from jax import lax
from jax.experimental import pallas as pl
from jax.experimental.pallas import tpu as pltpu

F32 = jnp.float32
BF16 = jnp.bfloat16

D_MODEL = 1024
BATCH = 16
SEQ = 256
DEPTH = 2
DEC_BATCH = 4
DEC_SEQ = 4096
PAST_LEN = 512
GRID_W = 64
SC_DIM = 256
N_Q_HEADS = 8
N_KV_HEADS = 2
N_GROUP = N_Q_HEADS // N_KV_HEADS
HEAD_DIM = 64
ATT_DIM = N_Q_HEADS * HEAD_DIM
KV_DIM = N_KV_HEADS * HEAD_DIM
WINDOW = 128
N_DN_HEADS = 4
DN_HEAD_DIM = 64
DN_DIM = N_DN_HEADS * DN_HEAD_DIM
CHUNK = 64
MIX_DIM = SC_DIM + ATT_DIM + DN_DIM
D_FF = -(-8 * D_MODEL // (3 * 256)) * 256
ROPE_BASE = 10000.0
EPS = 1e-6
NEG = -1e30
LOG2E = 1.4426950408889634

LANES = 128
V7X_VMEM_BYTES = 64 * 1024 * 1024
VMEM_LIMIT = (V7X_VMEM_BYTES * 3) // 4

OFF_SC, OFF_Q, OFF_KV, OFF_DQ, OFF_DZ, OFF_AB = 0, 768, 1280, 1536, 2304, 2560
IN_COLS = OFF_AB + 2 * LANES
N_PAIRS = N_DN_HEADS // 2
MOD_ROWS = 8
TM_OUT = 512
TM_IN = 512


def _sigmoid(x):
    return 0.5 * jnp.tanh(0.5 * x) + 0.5


def _silu(x):
    return x * _sigmoid(x)


def _softplus(x):
    return jnp.maximum(x, 0.0) + jnp.log1p(jnp.exp(-jnp.abs(x)))


def _dot(a, b):
    return jnp.dot(a, b, preferred_element_type=F32)


def _dot_nt(a, b):
    return lax.dot_general(a, b, (((1,), (1,)), ((), ())), preferred_element_type=F32)


def _lo_mask(shape):
    return lax.broadcasted_iota(jnp.int32, shape, 1) % LANES < HEAD_DIM


def _half_sums(xx, lo):
    s_lo = jnp.sum(jnp.where(lo, xx, 0.0), axis=-1, keepdims=True)
    s_hi = jnp.sum(jnp.where(lo, 0.0, xx), axis=-1, keepdims=True)
    return jnp.where(lo, s_lo, s_hi)


def _head_rms(x, g_row):
    lo = _lo_mask(x.shape)
    ms = _half_sums(x * x, lo) * (1.0 / HEAD_DIM)
    return x * lax.rsqrt(ms + EPS) * g_row


def _head_l2n(x):
    lo = _lo_mask(x.shape)
    return x * lax.rsqrt(_half_sums(x * x, lo) + EPS)


def _rope(x, cos, sin):
    n = x.shape[1]
    lane = lax.broadcasted_iota(jnp.int32, x.shape, 1)
    swapped = jnp.where((lane & 16) == 0, pltpu.roll(x, n - 16, axis=1), pltpu.roll(x, 16, axis=1))
    return x * cos + swapped * sin


def _dup_half(x, half):
    lo = _lo_mask(x.shape)
    r = pltpu.roll(x, HEAD_DIM, axis=1)
    return jnp.where(lo, x, r) if half == 0 else jnp.where(lo, r, x)


def _mod_kernel(c_ref, w_ref, b_ref, o_ref):
    s = _silu(c_ref[...])
    o_ref[...] = _dot(s.astype(BF16), w_ref[...].astype(BF16)) + b_ref[...]


def _modulation(cvecs, ada_w, ada_b):
    nblk = 6
    return pl.pallas_call(
        _mod_kernel,
        out_shape=jax.ShapeDtypeStruct((DEPTH, MOD_ROWS, 6 * D_MODEL), F32),
        grid=(DEPTH, nblk),
        in_specs=[pl.BlockSpec((MOD_ROWS, D_MODEL), lambda l, j: (0, 0)),
                  pl.BlockSpec((None, D_MODEL, D_MODEL), lambda l, j: (l, 0, j)),
                  pl.BlockSpec((None, 1, D_MODEL), lambda l, j: (l, 0, j))],
        out_specs=pl.BlockSpec((None, MOD_ROWS, D_MODEL), lambda l, j: (l, 0, j)),
        compiler_params=pltpu.CompilerParams(vmem_limit_bytes=VMEM_LIMIT),
        name="adaln_mod",
    )(cvecs, ada_w, ada_b.reshape(DEPTH, 1, 6 * D_MODEL))


def _mod_row_map(layer, tiles_per_batch):
    if tiles_per_batch is None:
        return lambda i: (layer, 0, 0, 0)
    return lambda i: (layer, 1 + i // tiles_per_batch, 0, 0)


def _prep_q(q, qg, cos, sin):
    out = []
    for g in range(ATT_DIM // LANES):
        x = _head_rms(q[:, g * LANES:(g + 1) * LANES], qg)
        if cos is not None:
            x = _rope(x, cos, sin)
        out.append((x * (LOG2E * HEAD_DIM ** -0.5)).astype(BF16))
    return jnp.concatenate(out, axis=1)


def _in_kernel(x_ref, mod_ref, g_ref, w_ref, wab_ref, qg_ref, cos_ref, sin_ref,
               sc_ref, q_ref, kv_ref, dq_ref, dz_ref, ab_ref, *, rotary):
    x = x_ref[...]
    ms = jnp.mean(x * x, axis=-1, keepdims=True)
    h = (x * lax.rsqrt(ms + EPS) * g_ref[...]) * (1.0 + mod_ref[1:2, :]) + mod_ref[0:1, :]
    hb = h.astype(BF16)
    q = _dot(hb, w_ref[:, OFF_Q:OFF_KV])
    q_ref[...] = _prep_q(q, qg_ref[...], cos_ref[...] if rotary else None, sin_ref[...] if rotary else None)
    for ref, a, b in ((sc_ref, OFF_SC, OFF_Q), (kv_ref, OFF_KV, OFF_DQ), (dq_ref, OFF_DQ, OFF_DZ),
                      (dz_ref, OFF_DZ, OFF_AB)):
        ref[...] = _dot(hb, w_ref[:, a:b])
    ab_ref[...] = _dot(hb, wab_ref[...])


def _in_projection(x, mod, norm1_g, w_main_b, w_ab_b, q_norm_g, cos_t, sin_t, layer, tiles_per_batch):
    t = x.shape[0]
    widths = (OFF_Q - OFF_SC, OFF_KV - OFF_Q, OFF_DQ - OFF_KV, OFF_DZ - OFF_DQ, OFF_AB - OFF_DZ, IN_COLS - OFF_AB)
    rotary = tiles_per_batch is not None
    pos_map = (lambda i: (i % tiles_per_batch, 0)) if rotary else (lambda i: (0, 0))
    return pl.pallas_call(
        functools.partial(_in_kernel, rotary=rotary),
        out_shape=[jax.ShapeDtypeStruct((t, w), BF16 if k == 1 else F32) for k, w in enumerate(widths)],
        grid=(t // TM_IN,),
        in_specs=[pl.BlockSpec((TM_IN, D_MODEL), lambda i: (i, 0)),
                  pl.BlockSpec((None, None, 6, D_MODEL), _mod_row_map(layer, tiles_per_batch)),
                  pl.BlockSpec((None, 1, D_MODEL), lambda i: (layer, 0, 0)),
                  pl.BlockSpec((None, D_MODEL, OFF_AB), lambda i: (layer, 0, 0),
                               pipeline_mode=pl.Buffered(1)),
                  pl.BlockSpec((None, D_MODEL, IN_COLS - OFF_AB), lambda i: (layer, 0, 0),
                               pipeline_mode=pl.Buffered(1)),
                  pl.BlockSpec((None, 1, LANES), lambda i: (layer, 0, 0)),
                  pl.BlockSpec((TM_IN, LANES), pos_map),
                  pl.BlockSpec((TM_IN, LANES), pos_map)],
        out_specs=[pl.BlockSpec((TM_IN, w), lambda i: (i, 0)) for w in widths],
        compiler_params=pltpu.CompilerParams(vmem_limit_bytes=VMEM_LIMIT),
        name="in_proj",
    )(x, mod, norm1_g, w_main_b, w_ab_b, q_norm_g, cos_t, sin_t)


def _head_logits(qa, half, key_sets):
    lo = _lo_mask(qa.shape)
    qh = jnp.where(lo if half == 0 else jnp.logical_not(lo), qa, jnp.zeros_like(qa))
    logits = []
    for k, _, mask in key_sets:
        s = _dot_nt(qh, k())
        logits.append(s if mask is None else jnp.where(mask, s, NEG))
    return logits


def _head_softmax_pv(logits, key_sets, sink):
    m = jnp.maximum(jnp.max(jnp.concatenate(logits, axis=1), axis=-1, keepdims=True), sink)
    acc = None
    for s, (_, v1, _) in zip(logits, key_sets):
        o = _dot(jnp.exp2(s - m).astype(BF16), v1())
        acc = o if acc is None else acc + o
    return acc[:, :LANES] / (acc[:, LANES:] + jnp.exp2(sink - m))


def _attend_heads(q_ref, blocks, sink_ref, layer, o_ref):
    n_pairs = ATT_DIM // LANES
    items = [(blk, pair, half) for blk in range(len(blocks)) for pair in range(n_pairs) for half in range(2)]
    kvh_of = lambda pair: (2 * pair) // N_GROUP
    keys = {}

    def key_sets(blk, pair):
        if (blk, kvh_of(pair)) not in keys:
            keys[blk, kvh_of(pair)] = blocks[blk][1](kvh_of(pair))
        return keys[blk, kvh_of(pair)]

    def logits_of(blk, pair, half):
        return _head_logits(q_ref[blocks[blk][0], pair * LANES:(pair + 1) * LANES], half, key_sets(blk, pair))

    logits = logits_of(*items[0])
    done = []
    for idx, (blk, pair, half) in enumerate(items):
        nxt = logits_of(*items[idx + 1]) if idx + 1 < len(items) else None
        done.append(_head_softmax_pv(logits, key_sets(blk, pair), sink_ref[layer, 2 * pair + half] * LOG2E))
        logits = nxt
        if half == 1:
            lo = _lo_mask(done[-1].shape)
            o_ref[blocks[blk][0], pair * LANES:(pair + 1) * LANES] = jnp.where(lo, done[-2], done[-1])


def _values_with_ones(v2, kvh):
    return jnp.concatenate([_dup_half(v2, kvh), jnp.ones(v2.shape, F32)], axis=1).astype(BF16)


def _attn_ctx_kernel(sink_ref, q_ref, kv_ref, kg_ref, o_ref, kn_ref, *, layer):
    kv = kv_ref[...]
    kn = _head_rms(kv[:, :LANES], kg_ref[...])
    kn_ref[...] = kn
    v2 = kv[:, LANES:]
    kd = [_dup_half(kn, kvh).astype(BF16) for kvh in range(N_KV_HEADS)]
    vd = [_values_with_ones(v2, kvh) for kvh in range(N_KV_HEADS)]
    key_sets_of = lambda kvh: [(lambda: kd[kvh], lambda: vd[kvh], None)]
    _attend_heads(q_ref, [(slice(0, SEQ), key_sets_of)], sink_ref, layer, o_ref)


def _attention_ctx(q, kv, k_norm_g, attn_sink, layer):
    t = q.shape[0]
    return pl.pallas_call(
        functools.partial(_attn_ctx_kernel, layer=layer),
        out_shape=[jax.ShapeDtypeStruct((t, ATT_DIM), F32), jax.ShapeDtypeStruct((t, KV_DIM), F32)],
        grid=(t // SEQ,),
        in_specs=[pl.BlockSpec(memory_space=pltpu.SMEM),
                  pl.BlockSpec((SEQ, ATT_DIM), lambda b: (b, 0)),
                  pl.BlockSpec((SEQ, 2 * KV_DIM), lambda b: (b, 0)),
                  pl.BlockSpec((None, 1, LANES), lambda b: (layer, 0, 0))],
        out_specs=[pl.BlockSpec((SEQ, ATT_DIM), lambda b: (b, 0)),
                   pl.BlockSpec((SEQ, KV_DIM), lambda b: (b, 0))],
        compiler_params=pltpu.CompilerParams(vmem_limit_bytes=VMEM_LIMIT),
        name="attn_ctx",
    )(attn_sink, q, kv, k_norm_g)


QBLK = 128
QSTEP = 1024
WIN_KEYS = QBLK + 2 * WINDOW
PREP_ROWS = 512


def _attn_lat_kernel(sink_ref, q_ref, kv_ref, kc_ref, vc_ref, kg_ref, cos_ref, sin_ref, o_ref,
                     kd_s, vd_s, kcd_s, vcd_s, *, layer):
    i = pl.program_id(1)
    n = DEC_SEQ

    @pl.when(i == 0)
    def _():
        for kvh in range(N_KV_HEADS):
            for s in (kd_s, vd_s):
                zpad = jnp.zeros((WINDOW, s.shape[-1]), BF16)
                s[kvh, 0:WINDOW, :] = zpad
                s[kvh, WINDOW + n:2 * WINDOW + n, :] = zpad
            kcd_s[kvh] = _dup_half(kc_ref[...], kvh).astype(BF16)
            vcd_s[kvh] = _values_with_ones(vc_ref[...], kvh)

        def prep(r, carry):
            r0 = pl.multiple_of(r * PREP_ROWS, PREP_ROWS)
            kv = kv_ref[pl.ds(r0, PREP_ROWS), :]
            kn = _head_rms(kv[:, :LANES], kg_ref[...])
            kr = _rope(kn, cos_ref[pl.ds(r0, PREP_ROWS), :], sin_ref[pl.ds(r0, PREP_ROWS), :])
            v2 = kv[:, LANES:]
            for kvh in range(N_KV_HEADS):
                kd_s[kvh, pl.ds(WINDOW + r0, PREP_ROWS), :] = _dup_half(kr, kvh).astype(BF16)
                vd_s[kvh, pl.ds(WINDOW + r0, PREP_ROWS), :] = _values_with_ones(v2, kvh)
            return carry

        lax.fori_loop(0, n // PREP_ROWS, prep, 0)

    rr = lax.broadcasted_iota(jnp.int32, (QBLK, WIN_KEYS), 0)
    jj = lax.broadcasted_iota(jnp.int32, (QBLK, WIN_KEYS), 1)
    band = (jj >= rr) & (jj <= rr + 2 * WINDOW)
    blocks = []
    for j in range(QSTEP // QBLK):
        start = pl.multiple_of(i * QSTEP + j * QBLK, QBLK)
        kpos = jj + (start - WINDOW)
        valid = band & (kpos >= 0) & (kpos < n)
        key_sets_of = lambda kvh, start=start, valid=valid: [
            (lambda: kcd_s[kvh], lambda: vcd_s[kvh], None),
            (lambda: kd_s[kvh, pl.ds(start, WIN_KEYS), :], lambda: vd_s[kvh, pl.ds(start, WIN_KEYS), :], valid)]
        blocks.append((slice(j * QBLK, (j + 1) * QBLK), key_sets_of))
    _attend_heads(q_ref, blocks, sink_ref, layer, o_ref)


def _attention_lat(q, kv, cache_k, cache_v, k_norm_g, attn_sink, cos_t, sin_t, layer):
    t = q.shape[0]
    n = DEC_SEQ
    nblk = n // QSTEP
    return pl.pallas_call(
        functools.partial(_attn_lat_kernel, layer=layer),
        out_shape=jax.ShapeDtypeStruct((t, ATT_DIM), F32),
        grid=(DEC_BATCH, nblk),
        in_specs=[pl.BlockSpec(memory_space=pltpu.SMEM),
                  pl.BlockSpec((QSTEP, ATT_DIM), lambda b, i: (b * nblk + i, 0)),
                  pl.BlockSpec((n, 2 * KV_DIM), lambda b, i: (b, 0)),
                  pl.BlockSpec((None, None, PAST_LEN, KV_DIM), lambda b, i: (b, layer, 0, 0)),
                  pl.BlockSpec((None, None, PAST_LEN, KV_DIM), lambda b, i: (b, layer, 0, 0)),
                  pl.BlockSpec((None, 1, LANES), lambda b, i: (layer, 0, 0)),
                  pl.BlockSpec((n, LANES), lambda b, i: (0, 0)),
                  pl.BlockSpec((n, LANES), lambda b, i: (0, 0))],
        out_specs=pl.BlockSpec((QSTEP, ATT_DIM), lambda b, i: (b * nblk + i, 0)),
        scratch_shapes=[pltpu.VMEM((N_KV_HEADS, n + 2 * WINDOW, LANES), BF16),
                        pltpu.VMEM((N_KV_HEADS, n + 2 * WINDOW, 2 * LANES), BF16),
                        pltpu.VMEM((N_KV_HEADS, PAST_LEN, LANES), BF16),
                        pltpu.VMEM((N_KV_HEADS, PAST_LEN, 2 * LANES), BF16)],
        compiler_params=pltpu.CompilerParams(vmem_limit_bytes=VMEM_LIMIT),
        name="attn_lat",
    )(attn_sink, q, kv, cache_k, cache_v, k_norm_g, cos_t, sin_t)


DN_ROWS = 128
DN_OUT_ROWS = 256
DN_CTX_SEQS = 8


def _dn_conv(x_ref, w_ref, r0, n):
    x = x_ref[pl.ds(r0, DN_ROWS), :]
    total = x_ref.shape[0]
    above = jnp.where(r0 % n > 0, x_ref[pl.ds(jnp.maximum(r0 - 1, 0), 1), :], 0.0)
    below = jnp.where(r0 % n + DN_ROWS < n, x_ref[pl.ds(jnp.minimum(r0 + DN_ROWS, total - 1), 1), :], 0.0)
    row = lax.broadcasted_iota(jnp.int32, x.shape, 0)
    prev = jnp.where(row == 0, above, pltpu.roll(x, 1, axis=0))
    nxt = jnp.where(row == DN_ROWS - 1, below, pltpu.roll(x, DN_ROWS - 1, axis=0))
    return _silu(prev * w_ref[0:1, :] + x * w_ref[1:2, :] + nxt * w_ref[2:3, :])


N_UNITS = 4
UW = N_UNITS * CHUNK
N_LEVELS = 6
M_INCL, M_STRICT, M_EYE = N_LEVELS, N_LEVELS + 1, N_LEVELS + 2
PREP_BLOCK = 8


def _dn_kernel(q_ref, k_ref, v_ref, z_ref, ab_ref, wq_ref, wk_ref, wv_ref, pa_ref, pdt_ref, ng_ref, s0_ref,
               o_ref, sout_ref, qs, ks, vs, gbs, pfs, r_s, au_s, m_s, qa_s, sh_s, el_s, st, mc, ex_s, tri_s, *, n):
    nc = n // CHUNK
    nseq = q_ref.shape[0] // n
    pblk = min(PREP_BLOCK, nseq * nc)

    @pl.when((pl.program_id(0) == 0) & (pl.program_id(1) == 0))
    def _():
        as_f32 = lambda m: jnp.where(m, 1.0, 0.0).astype(F32)
        ii = lax.broadcasted_iota(jnp.int32, (CHUNK, UW), 0)
        ll = lax.broadcasted_iota(jnp.int32, (CHUNK, UW), 1)
        unit_of_lane, jj = ll // CHUNK, ll % CHUNK
        fwd, bwd = unit_of_lane < 2, unit_of_lane >= 2
        for k in range(N_LEVELS):
            rk, ck = ii >> k, jj >> k
            lower = (rk - ck == 1) & ((rk & 1) == 1)
            upper = (ck - rk == 1) & ((ck & 1) == 1)
            mc[k] = as_f32((fwd & lower) | (bwd & upper))
        mc[M_INCL] = as_f32((fwd & (ii >= jj)) | (bwd & (ii <= jj)))
        mc[M_STRICT] = as_f32((fwd & (ii > jj)) | (bwd & (ii < jj)))
        mc[M_EYE] = as_f32(ii == jj)
        ec = lax.broadcasted_iota(jnp.int32, (LANES, 2 * UW), 0)
        el_ = lax.broadcasted_iota(jnp.int32, (LANES, 2 * UW), 1)
        ex_s[...] = as_f32(ec == (el_ // UW) * N_UNITS + (el_ % UW) // CHUNK).astype(BF16)
        ri = lax.broadcasted_iota(jnp.int32, (DN_ROWS, DN_ROWS), 0)
        ti = lax.broadcasted_iota(jnp.int32, (DN_ROWS, DN_ROWS), 1)
        tri_s[...] = as_f32((ti <= ri) & (ri // CHUNK == ti // CHUNK)).astype(BF16)

    def prep_rows(b):
        r0 = pl.multiple_of(b * DN_ROWS, DN_ROWS)
        rows = pl.ds(r0, DN_ROWS)
        qs[rows, :] = _head_l2n(_dn_conv(q_ref, wq_ref, r0, n)) * (DN_HEAD_DIM ** -0.5)
        ks[rows, :] = _head_l2n(_dn_conv(k_ref, wk_ref, r0, n))
        vs[rows, :] = _dn_conv(v_ref, wv_ref, r0, n)
        ab = ab_ref[rows, :]
        lane = lax.broadcasted_iota(jnp.int32, ab.shape, 1)
        gb = jnp.where(lane < N_UNITS, -jnp.exp(pa_ref[...]) * _softplus(ab + pdt_ref[...]), _sigmoid(ab))
        gbs[rows, :] = gb
        hi = gb.astype(BF16)
        r1 = gb - hi.astype(F32)
        mid = r1.astype(BF16)
        low = (r1 - mid.astype(F32)).astype(BF16)
        tri = tri_s[...]
        pfs[rows, :] = _dot(tri, hi) + _dot(tri, mid) + _dot(tri, low)

    for q in range(nseq):
        st[q] = jnp.concatenate([s0_ref[q, u // 2, u % 2] for u in range(N_UNITS)], axis=1)
    lo = _lo_mask((CHUNK, LANES))
    lane_c = lax.broadcasted_iota(jnp.int32, (CHUNK, LANES), 1)

    def spread_columns(m):
        hi = m.astype(BF16)
        r1 = m - hi.astype(F32)
        mid = r1.astype(BF16)
        low = (r1 - mid.astype(F32)).astype(BF16)
        e = ex_s[...]
        return _dot(hi, e) + _dot(mid, e) + _dot(low, e)

    def pair_block_diag(x):
        zero = jnp.zeros((CHUNK, LANES), BF16)
        return jnp.concatenate([jnp.where(lo, x, zero), jnp.where(lo, zero, x)], axis=0)

    def unit_dot(a, xb):
        return jnp.concatenate([_dot(a[:, h * LANES:(h + 1) * LANES], pair_block_diag(xb[:, h * LANES:(h + 1) * LANES]))
                                for h in range(2)], axis=1)

    def unit_dot2(a, xb, yb):
        outs = []
        for h in range(2):
            cols = slice(h * LANES, (h + 1) * LANES)
            rhs = jnp.concatenate([pair_block_diag(xb[:, cols]), pair_block_diag(yb[:, cols])], axis=1)
            outs.append(_dot(a[:, cols], rhs))
        return (jnp.concatenate([o[:, :LANES] for o in outs], axis=1),
                jnp.concatenate([o[:, LANES:] for o in outs], axis=1))

    def unit_rows(m):
        t = m.T
        return jnp.concatenate([t[u:u + 1, :] for u in range(N_UNITS)], axis=1)

    dup = lambda x: jnp.concatenate([x, x], axis=1)

    def chunk_prep(cs, between=(), first=()):
        rows = [pl.ds(pl.multiple_of(c * CHUNK, CHUNK), CHUNK) for c in cs]
        each = lambda f, *lists: [f(*xs) for xs in zip(*lists)]
        pending = list(between)
        early = list(first)

        def run_early():
            if early:
                early.pop(0)()

        run_early()

        gx = [gbs[r, :] for r in rows]
        pc = [pfs[r, :] for r in rows]
        tot = [p[CHUNK - 1:CHUNK, :] for p in pc]
        gcm = each(lambda g, p, t: jnp.where(lane_c < 2, p, t - p + g), gx, pc, tot)
        rsm = each(lambda g, p, t: jnp.where(lane_c < 2, t - p, p - g), gx, pc, tot)
        spread = each(lambda m, g: spread_columns(jnp.where(lane_c < N_UNITS, m, g)), gcm, gx)
        gc4 = [x[:, :UW] for x in spread]
        b4 = [x[:, UW:] for x in spread]
        e_gc = each(jnp.exp, gc4)
        dec = each(lambda c, m: jnp.exp(jnp.where(mc[M_INCL] > 0.0, c - unit_rows(m), NEG)), gc4, gcm)
        run_early()
        kt = [ks[r, :].T for r in rows]
        kt_pair = each(lambda t: jnp.concatenate([t[0:CHUNK], t[CHUNK:]], axis=1), kt)
        kt_bd = each(lambda t: pair_block_diag(t.astype(BF16)), kt_pair)
        gram = each(lambda r, b: dup(_dot(ks[r, :].astype(BF16), b)), rows, kt_bd)
        qk = each(lambda r, b: dup(_dot(qs[r, :].astype(BF16), b)), rows, kt_bd)
        run_early()
        lmat = each(lambda g, b, d: g * b * d * mc[M_STRICT], gram, b4, dec)
        amat = each(lambda m, d: (m * d).astype(BF16), qk, dec)
        t = each(lambda l: mc[M_EYE] - l * mc[0], lmat)
        while early:
            run_early()
        for k in range(1, N_LEVELS):
            lb = each(lambda l: (l * mc[k]).astype(BF16), lmat)
            tb = each(lambda x: x.astype(BF16), t)
            x = each(lambda a, b: unit_dot(a, b).astype(BF16), tb, lb)
            t = each(lambda t0, a, b: t0 - unit_dot(a, b), t, x, tb)
            if pending:
                pending.pop(0)()
        tm = each(lambda x: (x - mc[M_EYE]).astype(BF16), t)
        k4 = [dup(ks[r, :]) for r in rows]
        ru = each(lambda r, b: dup(vs[r, :]) * b, rows, b4)
        rw = each(lambda k, b, e: k * b * e, k4, b4, e_gc)
        tuw = each(lambda tmi, u, w: unit_dot2(tmi, u.astype(BF16), w.astype(BF16)), tm, ru, rw)
        ub = each(lambda x, d: (x + d[0]).astype(BF16), ru, tuw)
        wb = each(lambda x, d: (x + d[1]).astype(BF16), rw, tuw)
        kdt = each(lambda t, m: (dup(t) * jnp.exp(unit_rows(m))).astype(BF16), kt_pair, rsm)
        both = each(lambda k, a, w, u: unit_dot2(jnp.concatenate([k, a], axis=0), w, u), kdt, amat, wb, ub)
        mmat, rmat = [x[0][:CHUNK] for x in both], [x[1][:CHUNK] for x in both]
        aw, au = [x[0][CHUNK:] for x in both], [x[1][CHUNK:] for x in both]
        for i, (c, r) in enumerate(zip(cs, rows)):
            m_s[r, :] = mmat[i].astype(BF16)
            r_s[r, :] = rmat[i]
            qa_s[r, :] = (dup(qs[r, :]) * e_gc[i] - aw[i]).astype(BF16)
            au_s[r, :] = au[i]
            el = jnp.concatenate([e_gc[i][CHUNK - 1:CHUNK, :LANES], e_gc[i][0:1, LANES:]], axis=1)
            el_s[c] = jnp.broadcast_to(el, (8, UW))

    def scan_step(i):
        cf = [q * nc + i for q in range(nseq)]
        cb = [q * nc + nc - 1 - i for q in range(nseq)]
        rf = [pl.ds(pl.multiple_of(c * CHUNK, CHUNK), CHUNK) for c in cf]
        rb = [pl.ds(pl.multiple_of(c * CHUNK, CHUNK), CHUNK) for c in cb]
        mixed = lambda ref, q: jnp.concatenate([ref[rf[q], :LANES], ref[rb[q], LANES:]], axis=1)
        s = [st[q] for q in range(nseq)]
        sb = [x.astype(BF16) for x in s]
        for q in range(nseq):
            sh_s[rf[q], :LANES] = sb[q][:, :LANES]
            sh_s[rb[q], LANES:] = sb[q][:, LANES:]
        ms = [unit_dot(mixed(m_s, q), sb[q]) for q in range(nseq)]
        for q in range(nseq):
            el = jnp.concatenate([el_s[cf[q]][0:1, :LANES], el_s[cb[q]][0:1, LANES:]], axis=1)
            st[q] = s[q] * el - ms[q] + mixed(r_s, q)

    def finish_chunks(r0, count):
        parts = []
        for c in range(count):
            rows = pl.ds(pl.multiple_of(r0 + c * CHUNK, CHUNK), CHUNK)
            parts.append(unit_dot(qa_s[rows, :], sh_s[rows, :]) + au_s[rows, :])
        o = jnp.concatenate([p[:, :LANES] + p[:, LANES:] for p in parts], axis=0)
        rows = pl.ds(pl.multiple_of(r0, count * CHUNK), count * CHUNK)
        o_ref[rows, :] = _head_rms(o, ng_ref[...]) * _silu(z_ref[rows, :])

    def finish_rows(b, carry):
        finish_chunks(b * DN_OUT_ROWS, DN_OUT_ROWS // CHUNK)
        return carry

    def scan_only(i, carry):
        scan_step(i)
        return carry

    half = pblk // 2
    if nseq == 1 and nc % pblk == 0 and nc // pblk >= 3 and half <= N_LEVELS - 1 and (half * CHUNK) % DN_ROWS == 0:
        ends = lambda t: [t * half + s for s in range(half)] + [nc - 1 - (t * half + s) for s in range(half)]
        steps = lambda t: [functools.partial(scan_step, t * half + s) for s in range(half)]
        side = half * CHUNK // DN_ROWS
        last_block = n // DN_ROWS - 1
        row_blocks = lambda t: [t * side + s for s in range(side)] + [last_block - (t * side + s) for s in range(side)]
        input_pass = lambda t: [functools.partial(prep_rows, b) for b in row_blocks(t)]
        trips = nc // pblk
        for thunk in input_pass(0):
            thunk()
        chunk_prep(ends(0), first=input_pass(1))

        def prep_and_scan(t, carry):
            chunk_prep(ends(t), steps(t - 1), input_pass(t + 1))
            return carry

        lax.fori_loop(1, trips - 1, prep_and_scan, 0)
        chunk_prep(ends(trips - 1), steps(trips - 2))
        mid = nc // 2
        lax.fori_loop((trips - 1) * half, mid + 2, scan_only, 0)

        def finish_pair(j):
            finish_chunks((mid - 1 - j) * CHUNK, 1)
            finish_chunks((mid + j) * CHUNK, 1)

        def finish_and_scan(i, carry):
            finish_pair(2 * i)
            finish_pair(2 * i + 1)
            scan_step(mid + 2 + 2 * i)
            scan_step(mid + 3 + 2 * i)
            return carry

        lax.fori_loop(0, (mid - 2) // 2, finish_and_scan, 0)
        finish_pair(mid - 2)
        finish_pair(mid - 1)
    else:
        def input_block(b, carry):
            prep_rows(b)
            return carry

        def prep_block(blk, carry):
            chunk_prep([blk * pblk + s for s in range(pblk)])
            return carry

        lax.fori_loop(0, nseq * n // DN_ROWS, input_block, 0)
        lax.fori_loop(0, nseq * nc // pblk, prep_block, 0)
        lax.fori_loop(0, nc, scan_only, 0)
        lax.fori_loop(0, nseq * n // DN_OUT_ROWS, finish_rows, 0)

    for q in range(nseq):
        s = st[q]
        for u in range(N_UNITS):
            sout_ref[q, u // 2, u % 2] = s[:, u * CHUNK:(u + 1) * CHUNK]


def _delta_net(dq, dz, dab, dn_conv_w, pa, pdt, dn_norm_g, s0, layer, s0_layer, n, nseq):
    t = dq.shape[0]
    bsz = t // n
    rows = nseq * n
    nc = n // CHUNK
    conv_spec = lambda off: pl.BlockSpec((None, 3, LANES), lambda b, p: (layer, 0, off + p))
    col_spec = lambda off, bufs=2: pl.BlockSpec((rows, LANES), lambda b, p: (b, off + p),
                                                pipeline_mode=pl.Buffered(bufs))
    state_block = (nseq, 2, 2, DN_HEAD_DIM, DN_HEAD_DIM)
    return pl.pallas_call(
        functools.partial(_dn_kernel, n=n),
        out_shape=[jax.ShapeDtypeStruct((t, DN_DIM), F32),
                   jax.ShapeDtypeStruct((bsz, 2, N_DN_HEADS, DN_HEAD_DIM, DN_HEAD_DIM), F32)],
        grid=(bsz // nseq, N_PAIRS),
        in_specs=[col_spec(0), col_spec(N_PAIRS), col_spec(2 * N_PAIRS), col_spec(0, 1), col_spec(0, 1),
                  conv_spec(0), conv_spec(N_PAIRS), conv_spec(2 * N_PAIRS),
                  pl.BlockSpec((None, None, 1, LANES), lambda b, p: (layer, p, 0, 0)),
                  pl.BlockSpec((None, None, 1, LANES), lambda b, p: (layer, p, 0, 0)),
                  pl.BlockSpec((None, 1, LANES), lambda b, p: (layer, 0, 0)),
                  pl.BlockSpec((nseq, None) + state_block[1:], lambda b, p: (b, s0_layer, 0, p, 0, 0))],
        out_specs=[pl.BlockSpec((rows, LANES), lambda b, p: (b, p)),
                   pl.BlockSpec(state_block, lambda b, p: (b, 0, p, 0, 0))],
        scratch_shapes=[pltpu.VMEM((rows, LANES), F32) for _ in range(5)]
                       + [pltpu.VMEM((rows, UW), F32) for _ in range(2)]
                       + [pltpu.VMEM((rows, UW), BF16) for _ in range(3)]
                       + [pltpu.VMEM((nseq * nc, 8, UW), F32),
                          pltpu.VMEM((nseq, CHUNK, UW), F32),
                          pltpu.VMEM((N_LEVELS + 3, CHUNK, UW), F32),
                          pltpu.VMEM((LANES, 2 * UW), BF16),
                          pltpu.VMEM((DN_ROWS, DN_ROWS), BF16)],
        compiler_params=pltpu.CompilerParams(vmem_limit_bytes=(V7X_VMEM_BYTES * 7) // 8),
        name="delta_net",
    )(dq, dq, dq, dz, dab, dn_conv_w, dn_conv_w, dn_conv_w, pa, pdt, dn_norm_g, s0)


def _out_kernel(x_ref, sc_ref, scp_ref, scn_ref, att_ref, dn_ref, mod_ref, cw_ref, wo_ref, n2_ref,
                wg_ref, wu_ref, wd_ref, o_ref, *, tm, seq_rows, tiles_per_seq):
    t = pl.program_id(0) % tiles_per_seq
    sc = sc_ref[...]
    prod = sc[:, SC_DIM:2 * SC_DIM] * sc[:, 2 * SC_DIM:]
    pv = scp_ref[7:8, :]
    nx = scn_ref[0:1, :]
    prev_row = jnp.where(t > 0, pv[:, SC_DIM:2 * SC_DIM] * pv[:, 2 * SC_DIM:], 0.0)
    next_row = jnp.where(t < tiles_per_seq - 1, nx[:, SC_DIM:2 * SC_DIM] * nx[:, 2 * SC_DIM:], 0.0)
    row = lax.broadcasted_iota(jnp.int32, prod.shape, 0)
    row_in_seq = row % seq_rows
    p_prev = jnp.where(row_in_seq == 0, jnp.where(row == 0, prev_row, 0.0), pltpu.roll(prod, 1, axis=0))
    p_next = jnp.where(row_in_seq == seq_rows - 1, jnp.where(row == tm - 1, next_row, 0.0),
                       pltpu.roll(prod, tm - 1, axis=0))
    y_sc = sc[:, :SC_DIM] * (p_prev * cw_ref[0:1, :] + prod * cw_ref[1:2, :] + p_next * cw_ref[2:3, :])
    y = (_dot(y_sc.astype(BF16), wo_ref[0:SC_DIM, :])
         + _dot(att_ref[...].astype(BF16), wo_ref[SC_DIM:SC_DIM + ATT_DIM, :])
         + _dot(dn_ref[...].astype(BF16), wo_ref[SC_DIM + ATT_DIM:, :]))
    x1 = x_ref[...] + mod_ref[2:3, :] * y
    ms = jnp.mean(x1 * x1, axis=-1, keepdims=True)
    h2 = (x1 * lax.rsqrt(ms + EPS) * n2_ref[...]) * (1.0 + mod_ref[4:5, :]) + mod_ref[3:4, :]
    hb = h2.astype(BF16)
    act = _silu(_dot(hb, wg_ref[...])) * _dot(hb, wu_ref[...])
    o_ref[...] = x1 + mod_ref[5:6, :] * _dot(act.astype(BF16), wd_ref[...])


def _output_stage(x, sc, att, dn, mod, sc_conv_w, w_out_b, norm2_g, w_gate_b, w_up_b, w_down_b, layer,
                  tm, seq_len, per_batch_mod):
    t = x.shape[0]
    tiles_per_seq = max(seq_len // tm, 1)
    tiles_per_batch = tiles_per_seq if per_batch_mod else None
    halo = tm // 8
    last = t // 8 - 1
    resident = lambda shape: pl.BlockSpec((None,) + shape, lambda i: (layer, 0, 0), pipeline_mode=pl.Buffered(1))
    return pl.pallas_call(
        functools.partial(_out_kernel, tm=tm, seq_rows=min(tm, seq_len), tiles_per_seq=tiles_per_seq),
        out_shape=jax.ShapeDtypeStruct((t, D_MODEL), F32),
        grid=(t // tm,),
        in_specs=[pl.BlockSpec((tm, D_MODEL), lambda i: (i, 0)),
                  pl.BlockSpec((tm, 3 * SC_DIM), lambda i: (i, 0)),
                  pl.BlockSpec((8, 3 * SC_DIM), lambda i: (jnp.maximum(i * halo - 1, 0), 0)),
                  pl.BlockSpec((8, 3 * SC_DIM), lambda i: (jnp.minimum((i + 1) * halo, last), 0)),
                  pl.BlockSpec((tm, ATT_DIM), lambda i: (i, 0)),
                  pl.BlockSpec((tm, DN_DIM), lambda i: (i, 0)),
                  pl.BlockSpec((None, None, 6, D_MODEL), _mod_row_map(layer, tiles_per_batch)),
                  pl.BlockSpec((None, 3, SC_DIM), lambda i: (layer, 0, 0)),
                  resident((MIX_DIM, D_MODEL)),
                  pl.BlockSpec((None, 1, D_MODEL), lambda i: (layer, 0, 0)),
                  resident((D_MODEL, D_FF)), resident((D_MODEL, D_FF)), resident((D_FF, D_MODEL))],
        out_specs=pl.BlockSpec((tm, D_MODEL), lambda i: (i, 0)),
        compiler_params=pltpu.CompilerParams(vmem_limit_bytes=VMEM_LIMIT),
        name="out_stage",
    )(x, sc, sc, sc, att, dn, mod, sc_conv_w, w_out_b, norm2_g, w_gate_b, w_up_b, w_down_b)


def _rope_tables():
    pos = jnp.arange(DEC_SEQ)
    half = HEAD_DIM // 4
    inv = 1.0 / (ROPE_BASE ** (jnp.arange(half, dtype=F32) / half))
    ang_r = (pos // GRID_W).astype(F32)[:, None] * inv
    ang_c = (pos % GRID_W).astype(F32)[:, None] * inv
    cos = jnp.concatenate([jnp.cos(ang_r)] * 2 + [jnp.cos(ang_c)] * 2, axis=-1)
    sin = jnp.concatenate([-jnp.sin(ang_r), jnp.sin(ang_r), -jnp.sin(ang_c), jnp.sin(ang_c)], axis=-1)
    return jnp.tile(cos, (1, 2)), jnp.tile(sin, (1, 2))


def _pack_w_in(w_in):
    a0 = OFF_AB
    b0 = OFF_AB + 2 * N_DN_HEADS
    blocks = []
    for p in range(N_PAIRS):
        idx = [a0 + d * N_DN_HEADS + 2 * p + j for d in range(2) for j in range(2)]
        idx += [b0 + d * N_DN_HEADS + 2 * p + j for d in range(2) for j in range(2)]
        blocks += [w_in[:, :, c:c + 1] for c in idx]
        blocks.append(jnp.zeros((DEPTH, D_MODEL, LANES - len(idx)), w_in.dtype))
    return w_in.astype(BF16), jnp.concatenate(blocks, axis=2).astype(BF16)


def _pair_lanes(p):
    x = p.reshape(DEPTH, 2, N_PAIRS, 2).transpose(0, 2, 1, 3).reshape(DEPTH, N_PAIRS, 1, 4)
    return jnp.pad(x, ((0, 0), (0, 0), (0, 0), (0, LANES - 4)))


def kernel(x_prompt, x_sample, cache_k, cache_v, state_delta, c, c_ctx, w_in, w_out, ada_w, ada_b,
           norm1_g, norm2_g, sc_conv_w, dn_conv_w, q_norm_g, k_norm_g, attn_sink, dn_A_log,
           dn_dt_bias, dn_norm_g, w_gate, w_up, w_down):
    w_main_b, w_ab_b = _pack_w_in(w_in)
    w_out_b, w_gate_b, w_up_b, w_down_b = (w.astype(BF16) for w in (w_out, w_gate, w_up, w_down))
    n1 = norm1_g.reshape(DEPTH, 1, D_MODEL)
    n2 = norm2_g.reshape(DEPTH, 1, D_MODEL)
    qg = jnp.tile(q_norm_g, (1, 2)).reshape(DEPTH, 1, LANES)
    kg = jnp.tile(k_norm_g, (1, 2)).reshape(DEPTH, 1, LANES)
    ng = jnp.tile(dn_norm_g, (1, 2)).reshape(DEPTH, 1, LANES)
    pa, pdt = _pair_lanes(dn_A_log), _pair_lanes(dn_dt_bias)
    cos_t, sin_t = _rope_tables()
    ck = cache_k.reshape(DEC_BATCH, DEPTH, PAST_LEN, KV_DIM)
    cv = cache_v.reshape(DEC_BATCH, DEPTH, PAST_LEN, KV_DIM)
    zero_state = jnp.zeros((BATCH, 1, 2, N_DN_HEADS, DN_HEAD_DIM, DN_HEAD_DIM), F32)

    cvecs = jnp.concatenate([c_ctx[None, :], c, jnp.zeros((MOD_ROWS - 1 - DEC_BATCH, D_MODEL), F32)], axis=0)
    mod = _modulation(cvecs, ada_w, ada_b).reshape(DEPTH, MOD_ROWS, 6, D_MODEL)

    xp = x_prompt.reshape(BATCH * SEQ, D_MODEL)
    xs = x_sample.reshape(DEC_BATCH * DEC_SEQ, D_MODEL)
    new_k, new_v, new_s = [], [], []
    for l in range(DEPTH):
        out_args = (mod, sc_conv_w, w_out_b, n2, w_gate_b, w_up_b, w_down_b, l)
        sc, q, kv, dq, dz, dab = _in_projection(xp, mod, n1, w_main_b, w_ab_b, qg, cos_t, sin_t, l, None)
        att, kn = _attention_ctx(q, kv, kg, attn_sink, l)
        dn, s_out = _delta_net(dq, dz, dab, dn_conv_w, pa, pdt, ng, zero_state, l, 0, SEQ, DN_CTX_SEQS)
        xp = _output_stage(xp, sc, att, dn, *out_args, TM_OUT, SEQ, False)
        new_k.append(kn.reshape(BATCH, SEQ, N_KV_HEADS, HEAD_DIM))
        new_v.append(kv[:, KV_DIM:].reshape(BATCH, SEQ, N_KV_HEADS, HEAD_DIM))
        new_s.append(s_out)
        sc, q, kv, dq, dz, dab = _in_projection(xs, mod, n1, w_main_b, w_ab_b, qg, cos_t, sin_t, l, DEC_SEQ // TM_IN)
        att = _attention_lat(q, kv, ck, cv, kg, attn_sink, cos_t, sin_t, l)
        dn, _ = _delta_net(dq, dz, dab, dn_conv_w, pa, pdt, ng, state_delta, l, l, DEC_SEQ, 1)
        xs = _output_stage(xs, sc, att, dn, *out_args, TM_OUT, DEC_SEQ, True)
    return (xp.reshape(BATCH, SEQ, D_MODEL), xs.reshape(DEC_BATCH, DEC_SEQ, D_MODEL),
            jnp.stack(new_k, axis=1), jnp.stack(new_v, axis=1), jnp.stack(new_s, axis=1))
```
